```python
import jax, jax.numpy as jnp
from jax import lax
import numpy as np

D_MODEL = 1024
BATCH = 2
SEQ = 8192
DEPTH = 2

HEAD_DIM = 64
N_MIXERS = 4
GROUP_WIDTH = D_MODEL // N_MIXERS
N_GROUP_HEADS = GROUP_WIDTH // HEAD_DIM
MIX_WIDTH = N_MIXERS * GROUP_WIDTH
CMP_LEN = 32
CMP_STRIDE = 16
CMP_HIDDEN = 2 * HEAD_DIM
SEL_BLOCK = 64
SEL_TOPK = 16
WINDOW = 512
Q_BLOCK = 128
N_BRANCH = 3
ROPE_THETA = 500000.0
ROPE_DIM = HEAD_DIM // 4
SGU_CHUNK = 128
POOL_WINDOWS = (2, 4, 8, 16)
POOL_GROUP = GROUP_WIDTH // len(POOL_WINDOWS)
CONV_WIDTH = 3
D_FF = 3584
N_EXPERTS = 8
TOP_K = 2
N_DENSE = (DEPTH + 1) // 2
N_MOE = DEPTH // 2
EPS = 1e-6
NEG_INF = -1e30
IN_SIZES = (GROUP_WIDTH,) + (HEAD_DIM,) * 6 + (N_BRANCH * N_GROUP_HEADS,) + (GROUP_WIDTH,) * 6
D_IN = sum(IN_SIZES)

kernel_name = "hybrid_nsa_sgu_pool_conv_moe"


def rms_norm(x, g):
    xf = x.astype(jnp.float32)
    y = xf * lax.rsqrt(jnp.mean(xf * xf, axis=-1, keepdims=True) + EPS)
    return (y * g.astype(jnp.float32)).astype(x.dtype)


def apply_partial_rope(x, pos):
    half = ROPE_DIM // 2
    inv_freq = ROPE_THETA ** (-jnp.arange(half, dtype=jnp.float32) / half)
    ang = pos.astype(jnp.float32)[..., None] * inv_freq
    cos, sin = jnp.cos(ang), jnp.sin(ang)
    xr = x[..., :ROPE_DIM].astype(jnp.float32)
    x1, x2 = xr[..., :half], xr[..., half:]
    rot = jnp.concatenate([x1 * cos - x2 * sin, x1 * sin + x2 * cos], axis=-1).astype(x.dtype)
    return jnp.concatenate([rot, x[..., ROPE_DIM:]], axis=-1)


def compress_blocks(t, pe, w1, w2):
    b, s, d = t.shape
    c = t.reshape(b, s // CMP_STRIDE, CMP_STRIDE, d)
    blocks = jnp.concatenate([c[:, :-1], c[:, 1:]], axis=2) + pe
    hid = jax.nn.gelu(blocks.reshape(b, -1, CMP_LEN * d) @ w1)
    return hid @ w2


def nsa_mixer(q, k_cmp, v_cmp, k_sel, v_sel, k_win, v_win, gates, q_g, k_g, cmp_pe, cmp_w1, cmp_w2):
    b, s, _ = q.shape
    h, d = N_GROUP_HEADS, HEAD_DIM
    scale = d ** -0.5
    pos = jnp.arange(s)
    q = apply_partial_rope(rms_norm(q.reshape(b, s, h, d), q_g), pos[:, None])

    nc = s // CMP_STRIDE - 1
    cmp_end = jnp.arange(nc) * CMP_STRIDE + CMP_LEN - 1
    kc = apply_partial_rope(rms_norm(compress_blocks(k_cmp, cmp_pe[0], cmp_w1[0], cmp_w2[0]), k_g[0]), cmp_end)
    vc = compress_blocks(v_cmp, cmp_pe[1], cmp_w1[1], cmp_w2[1])
    s_c = jnp.einsum("bshd,bnd->bhsn", q, kc, preferred_element_type=jnp.float32) * scale
    valid_c = cmp_end[None, :] <= pos[:, None]
    p_c = jax.nn.softmax(jnp.where(valid_c, s_c, NEG_INF), axis=-1) * valid_c
    o_c = jnp.einsum("bhsn,bnd->bshd", p_c.astype(vc.dtype), vc)

    nb = s // SEL_BLOCK
    cs = jnp.arange(nc) * CMP_STRIDE
    bs = jnp.arange(nb) * SEL_BLOCK
    overlap = jnp.clip(jnp.minimum(cs[:, None] + CMP_LEN, bs[None, :] + SEL_BLOCK)
                       - jnp.maximum(cs[:, None], bs[None, :]), 0, None).astype(jnp.float32) / CMP_LEN
    imp = jnp.einsum("bhsn,nj->bsj", p_c, overlap)
    cur = pos // SEL_BLOCK
    blk = jnp.arange(nb)
    valid_b = blk[None, :] <= cur[:, None]
    forced = (blk[None, :] == 0) | (blk[None, :] == cur[:, None]) | (blk[None, :] == cur[:, None] - 1)
    score = jnp.where(valid_b & forced, 1e4, jnp.where(valid_b, imp, -1.0))
    n_sel = min(SEL_TOPK, nb)
    top_val, sel_idx = lax.top_k(score, n_sel)
    sel_ok = top_val >= 0.0

    k_sel = apply_partial_rope(rms_norm(k_sel, k_g[1]), pos)
    k_win = apply_partial_rope(rms_norm(k_win, k_g[2]), pos)
    kb_sel = k_sel.reshape(b, nb, SEL_BLOCK, d)
    vb_sel = v_sel.reshape(b, nb, SEL_BLOCK, d)
    kpad = jnp.pad(k_win, ((0, 0), (WINDOW, 0), (0, 0)))
    vpad = jnp.pad(v_win, ((0, 0), (WINDOW, 0), (0, 0)))
    nq = s // Q_BLOCK

    def to_blocks(a):
        return a.reshape((b, nq, Q_BLOCK) + a.shape[2:]).swapaxes(0, 1)

    def block_fn(args):
        qb, idx, ok, qi = args
        s0 = qi * Q_BLOCK
        tq = s0 + jnp.arange(Q_BLOCK)
        ks = jax.vmap(lambda kb, ix: kb[ix])(kb_sel, idx).reshape(b, Q_BLOCK, n_sel * SEL_BLOCK, d)
        vs = jax.vmap(lambda vb, ix: vb[ix])(vb_sel, idx).reshape(b, Q_BLOCK, n_sel * SEL_BLOCK, d)
        kpos = idx[..., None] * SEL_BLOCK + jnp.arange(SEL_BLOCK)
        m_s = ((kpos <= tq[None, :, None, None]) & ok[..., None]).reshape(b, Q_BLOCK, n_sel * SEL_BLOCK)
        ss = jnp.einsum("bqhd,bqkd->bhqk", qb, ks, preferred_element_type=jnp.float32) * scale
        ps = jax.nn.softmax(jnp.where(m_s[:, None], ss, NEG_INF), axis=-1)
        o_s = jnp.einsum("bhqk,bqkd->bqhd", ps.astype(vs.dtype), vs)
        kw = lax.dynamic_slice_in_dim(kpad, s0, WINDOW + Q_BLOCK, axis=1)
        vw = lax.dynamic_slice_in_dim(vpad, s0, WINDOW + Q_BLOCK, axis=1)
        kp = s0 - WINDOW + jnp.arange(WINDOW + Q_BLOCK)
        m_w = (kp[None, :] <= tq[:, None]) & (kp[None, :] > tq[:, None] - WINDOW) & (kp[None, :] >= 0)
        sw = jnp.einsum("bqhd,bkd->bhqk", qb, kw, preferred_element_type=jnp.float32) * scale
        pw = jax.nn.softmax(jnp.where(m_w, sw, NEG_INF), axis=-1)
        o_w = jnp.einsum("bhqk,bkd->bqhd", pw.astype(vw.dtype), vw)
        return o_s, o_w

    o_s, o_w = lax.map(block_fn, (to_blocks(q), to_blocks(sel_idx), to_blocks(sel_ok), jnp.arange(nq)))
    o_s = o_s.swapaxes(0, 1).reshape(b, s, h, d)
    o_w = o_w.swapaxes(0, 1).reshape(b, s, h, d)
    o = gates[..., 0:1] * o_c + gates[..., 1:2] * o_s + gates[..., 2:3] * o_w
    return o.reshape(b, s, GROUP_WIDTH)


def sgu_mixer(u, v, g, w_s, b_s):
    b, s, _ = u.shape
    vn = rms_norm(v, g).reshape(b, s // SGU_CHUNK, SGU_CHUNK, N_GROUP_HEADS, HEAD_DIM)
    w_m = w_s * jnp.tril(jnp.ones((SGU_CHUNK, SGU_CHUNK), w_s.dtype))
    mixed = jnp.einsum("gts,bcsgd->bctgd", w_m, vn) + b_s.T[:, :, None]
    return u * mixed.reshape(b, s, GROUP_WIDTH)


def pool_mixer(p, w, scale):
    b, s, _ = p.shape
    pf = p.astype(jnp.float32)
    csum = jnp.pad(jnp.cumsum(pf, axis=1), ((0, 0), (1, 0), (0, 0)))
    t = jnp.arange(s)
    outs = []
    for gi, win in enumerate(POOL_WINDOWS):
        c0, c1 = gi * POOL_GROUP, (gi + 1) * POOL_GROUP
        lo = jnp.maximum(t + 1 - win, 0)
        cnt = jnp.minimum(t + 1, win).astype(jnp.float32)
        mean = (csum[:, 1:, c0:c1] - csum[:, lo, c0:c1]) / cnt[None, :, None]
        outs.append(mean - pf[..., c0:c1])
    dlt = jnp.stack(outs, axis=2).astype(p.dtype)
    y = jnp.einsum("bsgc,gcd->bsgd", dlt, w).reshape(b, s, GROUP_WIDTH)
    return y * scale


def shortconv_mixer(bg, cg, hin, w):
    z = cg * hin
    y = lax.conv_general_dilated(z, w[:, None, :].astype(z.dtype), window_strides=(1,),
                                 padding=[(CONV_WIDTH - 1, 0)],
                                 dimension_numbers=("NWC", "WIO", "NWC"),
                                 feature_group_count=z.shape[-1])
    return bg * y


def hybrid_mixer(hn, w_in, q_g, k_g, cmp_pe, cmp_w1, cmp_w2, sgu_g, sgu_w, sgu_b,
                 pool_w, pool_s, conv_w, out_g, w_out):
    b, s, _ = hn.shape
    offs, acc = [], 0
    for sz in IN_SIZES[:-1]:
        acc += sz
        offs.append(acc)
    proj = hn @ w_in
    (q, kc, vc, ks, vs, kw, vw, gt, u, v, pin, bg, cg, hin) = jnp.split(proj, offs, axis=-1)
    gates = jax.nn.sigmoid(gt.astype(jnp.float32)).reshape(b, s, N_GROUP_HEADS, N_BRANCH, 1).astype(hn.dtype)
    gates = gates[..., 0]
    ya = nsa_mixer(q, kc, vc, ks, vs, kw, vw, gates, q_g, k_g, cmp_pe, cmp_w1, cmp_w2)
    yb = sgu_mixer(u, v, sgu_g, sgu_w, sgu_b)
    yc = pool_mixer(pin, pool_w, pool_s)
    yd = shortconv_mixer(bg, cg, hin, conv_w)
    ys = jnp.stack([ya, yb, yc, yd], axis=2)
    y = rms_norm(ys, out_g.reshape(N_MIXERS, GROUP_WIDTH)).reshape(b, s, MIX_WIDTH)
    return y @ w_out


def swiglu(x, w1, w3, w2):
    return (jax.nn.silu(x @ w1) * (x @ w3)) @ w2


def moe_swiglu(x, router, w1, w3, w2):
    logits = (x @ router).astype(jnp.float32)
    top_v, top_i = lax.top_k(logits, TOP_K)
    top_w = jax.nn.softmax(top_v, axis=-1)
    gate = jnp.sum(jax.nn.one_hot(top_i, N_EXPERTS, dtype=jnp.float32) * top_w[..., None], axis=-2).astype(x.dtype)
    out = jnp.zeros_like(x)
    for e in range(N_EXPERTS):
        out = out + gate[..., e:e + 1] * swiglu(x, w1[e], w3[e], w2[e])
    return out


def setup_inputs(seed: int = 0) -> dict:
    key = jax.random.key(seed)
    ks = jax.random.split(key, 32)
    f32 = jnp.float32
    nrm = lambda k, shape, sc: jax.random.normal(k, shape, f32) * sc
    gain = lambda k, shape: 1.0 + 0.05 * jax.random.normal(k, shape, f32)
    return {
        "x": jax.random.normal(ks[0], (BATCH, SEQ, D_MODEL), f32),
        "attn_norm_g": gain(ks[1], (DEPTH, D_MODEL)),
        "w_in": nrm(ks[2], (DEPTH, D_MODEL, D_IN), D_MODEL ** -0.5),
        "q_norm_g": gain(ks[3], (DEPTH, HEAD_DIM)),
        "k_norm_g": gain(ks[4], (DEPTH, N_BRANCH, HEAD_DIM)),
        "cmp_pe": nrm(ks[5], (DEPTH, 2, CMP_LEN, HEAD_DIM), 0.1),
        "cmp_w1": nrm(ks[6], (DEPTH, 2, CMP_LEN * HEAD_DIM, CMP_HIDDEN), (CMP_LEN * HEAD_DIM) ** -0.5),
        "cmp_w2": nrm(ks[7], (DEPTH, 2, CMP_HIDDEN, HEAD_DIM), CMP_HIDDEN ** -0.5),
        "sgu_norm_g": gain(ks[8], (DEPTH, GROUP_WIDTH)),
        "sgu_w": nrm(ks[9], (DEPTH, N_GROUP_HEADS, SGU_CHUNK, SGU_CHUNK), 0.5 * SGU_CHUNK ** -0.5),
        "sgu_b": gain(ks[10], (DEPTH, N_GROUP_HEADS, SGU_CHUNK)),
        "pool_w": nrm(ks[11], (DEPTH, len(POOL_WINDOWS), POOL_GROUP, POOL_GROUP), POOL_GROUP ** -0.5),
        "pool_scale": 1.0 + 0.1 * jax.random.normal(ks[12], (DEPTH, GROUP_WIDTH), f32),
        "conv_w": nrm(ks[13], (DEPTH, CONV_WIDTH, GROUP_WIDTH), CONV_WIDTH ** -0.5),
        "mix_out_norm_g": gain(ks[14], (DEPTH, MIX_WIDTH)),
        "w_out": nrm(ks[15], (DEPTH, MIX_WIDTH, D_MODEL), MIX_WIDTH ** -0.5),
        "ffn_norm_g": gain(ks[16], (DEPTH, D_MODEL)),
        "ffn_w1": nrm(ks[17], (N_DENSE, D_MODEL, D_FF), D_MODEL ** -0.5),
        "ffn_w3": nrm(ks[18], (N_DENSE, D_MODEL, D_FF), D_MODEL ** -0.5),
        "ffn_w2": nrm(ks[19], (N_DENSE, D_FF, D_MODEL), D_FF ** -0.5),
        "router_w": nrm(ks[20], (N_MOE, D_MODEL, N_EXPERTS), D_MODEL ** -0.5),
        "expert_w1": nrm(ks[21], (N_MOE, N_EXPERTS, D_MODEL, D_FF), D_MODEL ** -0.5),
        "expert_w3": nrm(ks[22], (N_MOE, N_EXPERTS, D_MODEL, D_FF), D_MODEL ** -0.5),
        "expert_w2": nrm(ks[23], (N_MOE, N_EXPERTS, D_FF, D_MODEL), D_FF ** -0.5),
    }


def reference(x, attn_norm_g, w_in, q_norm_g, k_norm_g, cmp_pe, cmp_w1, cmp_w2, sgu_norm_g, sgu_w,
              sgu_b, pool_w, pool_scale, conv_w, mix_out_norm_g, w_out, ffn_norm_g, ffn_w1, ffn_w3,
              ffn_w2, router_w, expert_w1, expert_w3, expert_w2):
    for layer in range(DEPTH):
        hn = rms_norm(x, attn_norm_g[layer])
        x = x + hybrid_mixer(hn, w_in[layer], q_norm_g[layer], k_norm_g[layer], cmp_pe[layer],
                             cmp_w1[layer], cmp_w2[layer], sgu_norm_g[layer], sgu_w[layer],
                             sgu_b[layer], pool_w[layer], pool_scale[layer], conv_w[layer],
                             mix_out_norm_g[layer], w_out[layer])
        hn = rms_norm(x, ffn_norm_g[layer])
        i = layer // 2
        if layer % 2 == 0:
            x = x + swiglu(hn, ffn_w1[i], ffn_w3[i], ffn_w2[i])
        else:
            x = x + moe_swiglu(hn, router_w[i], expert_w1[i], expert_w3[i], expert_w2[i])
    return x
```

```python
import functools

import jax
import jax.numpy as jnp
from jax import lax
from jax.experimental import pallas as pl
from jax.experimental.pallas import tpu as pltpu

D_MODEL = 1024
HEAD_DIM = 64
N_HEADS = 4
GROUP_WIDTH = 256
CMP_LEN = 32
CMP_STRIDE = 16
CMP_HIDDEN = 128
SEL_BLOCK = 64
SEL_TOPK = 16
WINDOW = 512
N_BRANCH = 3
ROPE_THETA = 500000.0
ROPE_DIM = 16
SGU_CHUNK = 128
POOL_WINDOWS = (2, 4, 8, 16)
CONV_WIDTH = 3
D_FF = 3584
N_EXPERTS = 8
EPS = 1e-6
NEG_INF = -1e30

LANES = 128
HIST = 16
D_IN_PAD = 2304
VMEM_LIMIT = 52 * 1024 * 1024

F32 = jnp.float32
BF16 = jnp.bfloat16


def _dot(a, b):
    return jnp.dot(a, b, preferred_element_type=F32)


def _dot_nt(a, b):
    return lax.dot_general(a, b, (((1,), (1,)), ((), ())), preferred_element_type=F32)


def _split(a):
    hi = a.astype(BF16)
    lo = (a - hi.astype(F32)).astype(BF16)
    return hi, lo


def _rms(y, g):
    return y * lax.rsqrt(jnp.mean(y * y, axis=-1, keepdims=True) + EPS) * g


def _head_norm_rope(c, g, cos, sa, sb, bd):
    hi, lo = _split(c * c)
    msq = _dot(hi, bd) + _dot(lo, bd)
    cn = c * lax.rsqrt(msq + EPS) * g
    return cn * cos + pltpu.roll(cn, LANES - ROPE_DIM // 2, 1) * sa + pltpu.roll(cn, ROPE_DIM // 2, 1) * sb


def _proj_kernel(x_ref, g_ref, w_ref, bd_ref, cos_ref, sa_ref, sb_ref, qg_ref, kg_ref,
                 sgug_ref, sguw_ref, sgub_ref, poolw_ref, pools_ref, convw_ref, outg_ref,
                 q_out, kvc_out, kvs_out, kvw_out, gate_out, y_out, pext, zext, *, tm, tiles_per_seq):
    i = pl.program_id(0)
    x = x_ref[...]
    hn = _rms(x, g_ref[...]).astype(BF16)
    bd = bd_ref[...]
    cos, sa, sb = cos_ref[...], sa_ref[...], sb_ref[...]
    lane = lax.broadcasted_iota(jnp.int32, (1, LANES), 1)
    first_half = lane < HEAD_DIM

    def proj(c0, c1):
        return _dot(hn, w_ref[:, c0:c1])

    pq = proj(0, 256)
    for c in range(2):
        qc = _head_norm_rope(pq[:, c * LANES:(c + 1) * LANES], qg_ref[...], cos, sa, sb, bd) * (HEAD_DIM ** -0.5)
        q_out[:, (2 * c) * LANES:(2 * c + 1) * LANES] = jnp.where(first_half, qc, 0.0).astype(BF16)
        q_out[:, (2 * c + 1) * LANES:(2 * c + 2) * LANES] = jnp.where(
            first_half, pltpu.roll(qc, HEAD_DIM, 1), 0.0).astype(BF16)

    pk = proj(256, 512)
    kvc_out[...] = pk[:, 0:LANES]
    ks = pk[:, LANES:2 * LANES]
    kvs_out[...] = jnp.where(first_half, _head_norm_rope(ks, kg_ref[0:1, :], cos, sa, sb, bd), ks).astype(BF16)
    pk = proj(512, 768)
    kw = pk[:, 0:LANES]
    kvw_out[...] = jnp.where(first_half, _head_norm_rope(kw, kg_ref[1:2, :], cos, sa, sb, bd), kw).astype(BF16)
    gate_out[...] = jax.nn.sigmoid(pk[:, LANES:2 * LANES])

    outg = outg_ref[...]
    lane2 = lax.broadcasted_iota(jnp.int32, (1, GROUP_WIDTH), 1)
    grp = lane2 // HEAD_DIM

    u = proj(768, 1024)
    vn = _rms(proj(1024, 1280), sgug_ref[...]).astype(BF16)
    r_i = lax.broadcasted_iota(jnp.int32, (N_HEADS * SGU_CHUNK, SGU_CHUNK), 0) & (SGU_CHUNK - 1)
    c_i = lax.broadcasted_iota(jnp.int32, (N_HEADS * SGU_CHUNK, SGU_CHUNK), 1)
    wm = jnp.where(c_i <= r_i, sguw_ref[...], 0.0).astype(BF16)
    for c in range(tm // SGU_CHUNK):
        rows = slice(c * SGU_CHUNK, (c + 1) * SGU_CHUNK)
        r = _dot(wm, vn[rows, :])
        mixed = sgub_ref[...]
        for gi in range(N_HEADS):
            mixed = mixed + jnp.where(grp == gi, r[gi * SGU_CHUNK:(gi + 1) * SGU_CHUNK, :], 0.0)
        y_out[rows, 0:GROUP_WIDTH] = _rms(u[rows, :] * mixed, outg[0:1, :]).astype(BF16)

    first = (i % tiles_per_seq) == 0

    @pl.when(first)
    def _():
        pext[0:HIST, :] = jnp.zeros((HIST, GROUP_WIDTH), F32)
        zext[0:HIST, :] = jnp.zeros((HIST, GROUP_WIDTH), F32)

    @pl.when(jnp.logical_not(first))
    def _():
        pext[0:HIST, :] = pext[tm:tm + HIST, :]
        zext[0:HIST, :] = zext[tm:tm + HIST, :]

    pin = proj(1280, 1536)
    pext[HIST:, :] = pin
    pos = (i % tiles_per_seq) * tm + lax.broadcasted_iota(jnp.int32, (tm, 1), 0)
    acc = pin
    sums = []
    for k in range(1, POOL_WINDOWS[-1]):
        acc = acc + pext[HIST - k:HIST - k + tm, :]
        if k + 1 in POOL_WINDOWS:
            sums.append(acc)
    wsum = jnp.where(grp == 0, sums[0], jnp.where(grp == 1, sums[1], jnp.where(grp == 2, sums[2], sums[3])))
    win = jnp.where(grp == 0, POOL_WINDOWS[0], jnp.where(grp == 1, POOL_WINDOWS[1],
                                                         jnp.where(grp == 2, POOL_WINDOWS[2], POOL_WINDOWS[3])))
    cnt = jnp.minimum(pos + 1, win).astype(F32)
    dlt = (wsum / cnt - pin).astype(BF16)
    yc = _dot(dlt, poolw_ref[...]) * pools_ref[...]
    y_out[:, GROUP_WIDTH:2 * GROUP_WIDTH] = _rms(yc, outg[1:2, :]).astype(BF16)

    bg = proj(1536, 1792)
    z = proj(1792, 2048) * proj(2048, 2304)
    zext[HIST:, :] = z
    cw = convw_ref[...]
    conv = cw[2:3, :] * z + cw[1:2, :] * zext[HIST - 1:HIST - 1 + tm, :] + cw[0:1, :] * zext[HIST - 2:HIST - 2 + tm, :]
    y_out[:, 2 * GROUP_WIDTH:3 * GROUP_WIDTH] = _rms(bg * conv, outg[2:3, :]).astype(BF16)


def _proj_call(x, g, w, bd, cos, sa, sb, qg, kg, sgug, sguw, sgub, poolw, pools, convw, outg, *, seq, tm):
    n = x.shape[0]
    tps = seq // tm
    const = lambda shape: pl.BlockSpec(shape, lambda i: (0, 0))
    rows = lambda width: pl.BlockSpec((tm, width), lambda i: (i, 0))
    tab = pl.BlockSpec((tm, LANES), lambda i: (i % tps, 0))
    return pl.pallas_call(
        functools.partial(_proj_kernel, tm=tm, tiles_per_seq=tps),
        grid=(n // tm,),
        in_specs=[rows(D_MODEL), const((1, D_MODEL)), const((D_MODEL, D_IN_PAD)), const((LANES, LANES)),
                  tab, tab, tab, const((1, LANES)), const((2, LANES)),
                  const((1, GROUP_WIDTH)), const((N_HEADS * SGU_CHUNK, SGU_CHUNK)), const((SGU_CHUNK, GROUP_WIDTH)),
                  const((GROUP_WIDTH, GROUP_WIDTH)), const((1, GROUP_WIDTH)), const((CONV_WIDTH, GROUP_WIDTH)),
                  const((3, GROUP_WIDTH))],
        out_specs=[rows(4 * LANES), rows(LANES), rows(LANES), rows(LANES), rows(LANES), rows(3 * GROUP_WIDTH)],
        out_shape=[jax.ShapeDtypeStruct((n, 4 * LANES), BF16), jax.ShapeDtypeStruct((n, LANES), F32),
                   jax.ShapeDtypeStruct((n, LANES), BF16), jax.ShapeDtypeStruct((n, LANES), BF16),
                   jax.ShapeDtypeStruct((n, LANES), F32), jax.ShapeDtypeStruct((n, 3 * GROUP_WIDTH), BF16)],
        scratch_shapes=[pltpu.VMEM((tm + HIST, GROUP_WIDTH), F32), pltpu.VMEM((tm + HIST, GROUP_WIDTH), F32)],
        compiler_params=pltpu.CompilerParams(dimension_semantics=("arbitrary",), vmem_limit_bytes=VMEM_LIMIT),
        name="proj_mixers",
    )(x, g, w, bd, cos, sa, sb, qg, kg, sgug, sguw, sgub, poolw, pools, convw, outg)


def _compress_kernel(t_ref, pe_ref, w1a_ref, w1b_ref, w2_ref, kg_ref, bd_ref, cos_ref, sa_ref, sb_ref, o_ref):
    t = t_ref[0]
    ncp = t.shape[0]
    a = _dot((t + pe_ref[0:1, :]).astype(BF16), w1a_ref[...])
    b = _dot((t + pe_ref[1:2, :]).astype(BF16), w1b_ref[...])
    hid = jax.nn.gelu(a + pltpu.roll(b, ncp - 1, 0), approximate=True)
    kv = _dot(hid.astype(BF16), w2_ref[...])
    lane = lax.broadcasted_iota(jnp.int32, (1, LANES), 1)
    kn = _head_norm_rope(kv, kg_ref[...], cos_ref[...], sa_ref[...], sb_ref[...], bd_ref[...])
    o_ref[0] = jnp.where(lane < HEAD_DIM, kn, kv).astype(BF16)


def _compress_call(t2, pe, w1a, w1b, w2, kg, bd, cos, sa, sb):
    b, ncp, width = t2.shape
    const = lambda shape: pl.BlockSpec(shape, lambda i: (0,) * len(shape))
    return pl.pallas_call(
        _compress_kernel,
        grid=(b,),
        in_specs=[pl.BlockSpec((1, ncp, width), lambda i: (i, 0, 0)), const(pe.shape), const(w1a.shape),
                  const(w1b.shape), const(w2.shape), const(kg.shape), const(bd.shape),
                  const(cos.shape), const(sa.shape), const(sb.shape)],
        out_specs=pl.BlockSpec((1, ncp, LANES), lambda i: (i, 0, 0)),
        out_shape=jax.ShapeDtypeStruct((b, ncp, LANES), BF16),
        compiler_params=pltpu.CompilerParams(dimension_semantics=("arbitrary",), vmem_limit_bytes=VMEM_LIMIT),
        name="nsa_compress",
    )(t2, pe, w1a, w1b, w2, kg, bd, cos, sa, sb)


def _attn_kernel(q_ref, gate_ref, kvc_ref, kvs_ref, kvw_ref, ov_ref, og_ref, o_ref, *, tq, tk, n_sel):
    s0 = pl.program_id(1) * tq
    nr = N_HEADS * tq
    q = q_ref[...]
    qs = jnp.concatenate([q[:, h * LANES:(h + 1) * LANES] for h in range(N_HEADS)], axis=0)
    t_row = s0 + (lax.broadcasted_iota(jnp.int32, (nr, 1), 0) & (tq - 1))
    t_q = s0 + lax.broadcasted_iota(jnp.int32, (tq, 1), 0)

    kvc = kvc_ref[0]
    ncp = kvc.shape[0]
    cmp_end = lax.broadcasted_iota(jnp.int32, (1, ncp), 1) * CMP_STRIDE + (CMP_LEN - 1)
    valid = cmp_end <= t_row
    sc = jnp.where(valid, _dot_nt(qs, kvc), NEG_INF)
    e = jnp.where(valid, jnp.exp(sc - jnp.max(sc, axis=-1, keepdims=True)), 0.0)
    l = jnp.sum(e, axis=-1, keepdims=True)
    p = e / jnp.where(l > 0.0, l, 1.0)
    o_c = _dot(p.astype(BF16), kvc)
    psum = p[0:tq] + p[tq:2 * tq] + p[2 * tq:3 * tq] + p[3 * tq:4 * tq]
    hi, lo = _split(psum)
    imp = _dot(hi, ov_ref[...]) + _dot(lo, ov_ref[...])

    nbp = imp.shape[1]
    blk = lax.broadcasted_iota(jnp.int32, (1, nbp), 1)
    blk_f = blk.astype(F32)
    cur = t_q // SEL_BLOCK
    valid_b = blk <= cur
    forced = (blk == 0) | (blk == cur) | (blk == cur - 1)
    score = jnp.where(valid_b & forced, 1e4, jnp.where(valid_b, imp, -1.0))
    sel = jnp.zeros((tq, nbp), F32)
    for _ in range(n_sel):
        top = jnp.max(score, axis=-1, keepdims=True)
        idx = jnp.min(jnp.where(score == top, blk_f, float(nbp)), axis=-1, keepdims=True)
        hit = blk_f == idx
        sel = jnp.where(hit & (top >= 0.0), 1.0, sel)
        score = jnp.where(hit, -2.0, score)
    selb = sel.astype(BF16)

    def sel_step(j, carry):
        m, l, acc = carry
        k0 = pl.multiple_of(j * tk, tk)
        kv = kvs_ref[0, pl.ds(k0, tk), :]
        s = _dot_nt(qs, kv)
        kpos = k0 + lax.broadcasted_iota(jnp.int32, (1, tk), 1)
        expand = jnp.where(lax.broadcasted_iota(jnp.int32, (nbp, tk), 0) == kpos // SEL_BLOCK, 1.0, 0.0).astype(BF16)
        ok = (_dot(selb, expand) > 0.5) & (kpos <= t_q)
        bias = jnp.where(ok, 0.0, NEG_INF)
        s = (s.reshape(N_HEADS, tq, tk) + bias[None]).reshape(nr, tk)
        m_new = jnp.maximum(m, jnp.max(s, axis=-1, keepdims=True))
        alpha = jnp.exp(m - m_new)
        pe = jnp.exp(s - m_new)
        l = alpha * l + jnp.sum(pe, axis=-1, keepdims=True)
        acc = alpha * acc + _dot(pe.astype(BF16), kv)
        return m_new, l, acc

    n_kv = (s0 + tq + tk - 1) // tk
    init = (jnp.full((nr, 1), NEG_INF, F32), jnp.zeros((nr, 1), F32), jnp.zeros((nr, LANES), F32))
    _, l_s, acc_s = lax.fori_loop(0, n_kv, sel_step, init)
    o_s = acc_s / l_s

    wk = WINDOW + tq
    w0 = pl.multiple_of(jnp.maximum(s0 - WINDOW, 0), tq)
    kvw = kvw_ref[0, pl.ds(w0, wk), :]
    kp = w0 + lax.broadcasted_iota(jnp.int32, (1, wk), 1)
    ok = (kp <= t_row) & (kp > t_row - WINDOW)
    sw = jnp.where(ok, _dot_nt(qs, kvw), NEG_INF)
    ew = jnp.exp(sw - jnp.max(sw, axis=-1, keepdims=True))
    o_w = _dot(ew.astype(BF16), kvw) / jnp.sum(ew, axis=-1, keepdims=True)

    g = gate_ref[...]
    heads = []
    for h in range(N_HEADS):
        r = slice(h * tq, (h + 1) * tq)
        c = h * N_BRANCH
        heads.append(g[:, c:c + 1] * o_c[r] + g[:, c + 1:c + 2] * o_s[r] + g[:, c + 2:c + 3] * o_w[r])
    lane = lax.broadcasted_iota(jnp.int32, (1, LANES), 1)
    ya = jnp.concatenate([jnp.where(lane < HEAD_DIM, pltpu.roll(heads[0], HEAD_DIM, 1), heads[1]),
                          jnp.where(lane < HEAD_DIM, pltpu.roll(heads[2], HEAD_DIM, 1), heads[3])], axis=1)
    o_ref[...] = _rms(ya, og_ref[...]).astype(BF16)


def _attn_call(q, gates, kvc, kvs, kvw, overlap, og, *, seq, tq, tk):
    n = q.shape[0]
    b = n // seq
    nq = seq // tq
    ncp = kvc.shape[1]
    n_sel = min(SEL_TOPK, seq // SEL_BLOCK)
    rows = lambda width: pl.BlockSpec((tq, width), lambda bi, qi: (bi * nq + qi, 0))
    per_b = lambda r: pl.BlockSpec((1, r, LANES), lambda bi, qi: (bi, 0, 0))
    const = lambda shape: pl.BlockSpec(shape, lambda bi, qi: (0, 0))
    return pl.pallas_call(
        functools.partial(_attn_kernel, tq=tq, tk=tk, n_sel=n_sel),
        grid=(b, nq),
        in_specs=[rows(4 * LANES), rows(LANES), per_b(ncp), per_b(seq), per_b(seq),
                  const(overlap.shape), const((1, GROUP_WIDTH))],
        out_specs=rows(GROUP_WIDTH),
        out_shape=jax.ShapeDtypeStruct((n, GROUP_WIDTH), BF16),
        compiler_params=pltpu.CompilerParams(dimension_semantics=("arbitrary", "arbitrary"),
                                             vmem_limit_bytes=VMEM_LIMIT),
        name="nsa_attention",
    )(q, gates, kvc, kvs.reshape(b, seq, LANES), kvw.reshape(b, seq, LANES), overlap, og)


def _outproj_kernel(x_ref, ya_ref, yb_ref, w_ref, g_ref, *rest, moe):
    if moe:
        rw_ref, x1_ref, hn_ref, gate_ref = rest
    else:
        x1_ref, hn_ref = rest
    x1 = x_ref[...] + _dot(ya_ref[...], w_ref[0:GROUP_WIDTH, :]) + _dot(yb_ref[...], w_ref[GROUP_WIDTH:, :])
    x1_ref[...] = x1
    hn = _rms(x1, g_ref[...])
    hn_ref[...] = hn.astype(BF16)
    if moe:
        hi, lo = _split(hn)
        logits = _dot(hi, rw_ref[0]) + _dot(lo, rw_ref[0]) + _dot(hi, rw_ref[1])
        lane = lax.broadcasted_iota(jnp.int32, (1, LANES), 1)
        lane_f = lane.astype(F32)
        logits = jnp.where(lane < N_EXPERTS, logits, NEG_INF)
        m1 = jnp.max(logits, axis=-1, keepdims=True)
        i1 = jnp.min(jnp.where(logits == m1, lane_f, float(LANES)), axis=-1, keepdims=True)
        rest_l = jnp.where(lane_f == i1, 2 * NEG_INF, logits)
        m2 = jnp.max(rest_l, axis=-1, keepdims=True)
        i2 = jnp.min(jnp.where(rest_l == m2, lane_f, float(LANES)), axis=-1, keepdims=True)
        e2 = jnp.exp(m2 - m1)
        den = 1.0 + e2
        gate_ref[...] = jnp.where(lane_f == i1, 1.0 / den, jnp.where(lane_f == i2, e2 / den, 0.0))


def _outproj_call(x, ya, yb, w, g, rw, *, tm):
    n = x.shape[0]
    moe = rw is not None
    rows = lambda width: pl.BlockSpec((tm, width), lambda i: (i, 0))
    const = lambda shape: pl.BlockSpec(shape, lambda i: (0,) * len(shape))
    in_specs = [rows(D_MODEL), rows(GROUP_WIDTH), rows(3 * GROUP_WIDTH), const((D_MODEL, D_MODEL)), const((1, D_MODEL))]
    out_specs = [rows(D_MODEL), rows(D_MODEL)]
    out_shape = [jax.ShapeDtypeStruct((n, D_MODEL), F32), jax.ShapeDtypeStruct((n, D_MODEL), BF16)]
    args = [x, ya, yb, w, g]
    if moe:
        in_specs.append(const(rw.shape))
        out_specs.append(rows(LANES))
        out_shape.append(jax.ShapeDtypeStruct((n, LANES), F32))
        args.append(rw)
    return pl.pallas_call(
        functools.partial(_outproj_kernel, moe=moe),
        grid=(n // tm,),
        in_specs=in_specs, out_specs=out_specs, out_shape=out_shape,
        compiler_params=pltpu.CompilerParams(dimension_semantics=("arbitrary",), vmem_limit_bytes=VMEM_LIMIT),
        name="outproj_router" if moe else "outproj",
    )(*args)


def _ffn_kernel(h_ref, x_ref, w1_ref, w3_ref, w2_ref, o_ref, acc_ref):
    f = pl.program_id(1)

    @pl.when(f == 0)
    def _():
        acc_ref[...] = jnp.zeros_like(acc_ref)

    h = h_ref[...]
    a = _dot(h, w1_ref[...])
    t = (a * jax.nn.sigmoid(a) * _dot(h, w3_ref[...])).astype(BF16)
    acc_ref[...] += _dot(t, w2_ref[...])

    @pl.when(f == pl.num_programs(1) - 1)
    def _():
        o_ref[...] = x_ref[...] + acc_ref[...]


def _ffn_call(h, x, w1, w3, w2, *, tm, tf):
    n = h.shape[0]
    rows = pl.BlockSpec((tm, D_MODEL), lambda i, f: (i, 0))
    return pl.pallas_call(
        _ffn_kernel,
        grid=(n // tm, D_FF // tf),
        in_specs=[rows, rows, pl.BlockSpec((D_MODEL, tf), lambda i, f: (0, f)),
                  pl.BlockSpec((D_MODEL, tf), lambda i, f: (0, f)), pl.BlockSpec((tf, D_MODEL), lambda i, f: (f, 0))],
        out_specs=rows,
        out_shape=jax.ShapeDtypeStruct((n, D_MODEL), F32),
        scratch_shapes=[pltpu.VMEM((tm, D_MODEL), F32)],
        compiler_params=pltpu.CompilerParams(dimension_semantics=("arbitrary", "arbitrary"),
                                             vmem_limit_bytes=VMEM_LIMIT),
        name="ffn_swiglu",
    )(h, x, w1, w3, w2)


def _moe_kernel(h_ref, x_ref, gate_ref, w1_ref, w3_ref, w2_ref, o_ref, acc_ref):
    e = pl.program_id(1)
    f = pl.program_id(2)

    @pl.when((e == 0) & (f == 0))
    def _():
        acc_ref[...] = jnp.zeros_like(acc_ref)

    lane = lax.broadcasted_iota(jnp.int32, (1, LANES), 1)
    g_e = jnp.sum(jnp.where(lane == e, gate_ref[...], 0.0), axis=-1, keepdims=True)
    h = h_ref[...]
    a = _dot(h, w1_ref[...])
    t = (g_e * (a * jax.nn.sigmoid(a) * _dot(h, w3_ref[...]))).astype(BF16)
    acc_ref[...] += _dot(t, w2_ref[...])

    @pl.when((e == pl.num_programs(1) - 1) & (f == pl.num_programs(2) - 1))
    def _():
        o_ref[...] = x_ref[...] + acc_ref[...]


def _moe_call(h, x, gate, w1, w3, w2, *, tm, tf):
    n = h.shape[0]
    rows = lambda width: pl.BlockSpec((tm, width), lambda i, e, f: (i, 0))
    return pl.pallas_call(
        _moe_kernel,
        grid=(n // tm, N_EXPERTS, D_FF // tf),
        in_specs=[rows(D_MODEL), rows(D_MODEL), rows(LANES),
                  pl.BlockSpec((None, D_MODEL, tf), lambda i, e, f: (e, 0, f)),
                  pl.BlockSpec((None, D_MODEL, tf), lambda i, e, f: (e, 0, f)),
                  pl.BlockSpec((None, tf, D_MODEL), lambda i, e, f: (e, f, 0))],
        out_specs=rows(D_MODEL),
        out_shape=jax.ShapeDtypeStruct((n, D_MODEL), F32),
        scratch_shapes=[pltpu.VMEM((tm, D_MODEL), F32)],
        compiler_params=pltpu.CompilerParams(dimension_semantics=("arbitrary", "arbitrary", "arbitrary"),
                                             vmem_limit_bytes=VMEM_LIMIT),
        name="moe_swiglu",
    )(h, x, gate, w1, w3, w2)


def _rope_tables(pos):
    half = ROPE_DIM // 2
    inv_freq = ROPE_THETA ** (-jnp.arange(half, dtype=F32) / half)
    ang = pos.astype(F32)[..., None] * inv_freq
    cos, sin = jnp.cos(ang), jnp.sin(ang)
    n = pos.shape[0]
    rest = HEAD_DIM - ROPE_DIM
    cos64 = jnp.concatenate([cos, cos, jnp.ones((n, rest), F32)], axis=-1)
    sa64 = jnp.concatenate([-sin, jnp.zeros((n, rest + half), F32)], axis=-1)
    sb64 = jnp.concatenate([jnp.zeros((n, half), F32), sin, jnp.zeros((n, rest), F32)], axis=-1)
    return tuple(jnp.tile(t, (1, 2)) for t in (cos64, sa64, sb64))


def _block_diag(blocks):
    n = len(blocks)
    rows = []
    for i, blk in enumerate(blocks):
        rows.append(jnp.concatenate([blk if j == i else jnp.zeros((blk.shape[0], blocks[j].shape[1]), blk.dtype)
                                     for j in range(n)], axis=1))
    return jnp.concatenate(rows, axis=0)


def _layout_w_in(w):
    n_gate = N_BRANCH * N_HEADS
    g0 = GROUP_WIDTH + 6 * HEAD_DIM
    pad = jnp.zeros((w.shape[0], LANES - n_gate), w.dtype)
    return jnp.concatenate([w[:, :g0], w[:, g0:g0 + n_gate], pad, w[:, g0 + n_gate:]], axis=1)


def _layout_compress(pe, w1, w2):
    half = CMP_LEN // 2
    pe2 = jnp.transpose(pe, (1, 0, 2)).reshape(2, half * LANES)
    w1r = w1.reshape(2, 2, half, HEAD_DIM, CMP_HIDDEN)
    zeros = jnp.zeros((half, HEAD_DIM, CMP_HIDDEN), w1.dtype)

    def big(part):
        k_rows = jnp.concatenate([w1r[0, part], zeros], axis=-1)
        v_rows = jnp.concatenate([zeros, w1r[1, part]], axis=-1)
        return jnp.concatenate([k_rows, v_rows], axis=1).reshape(half * LANES, 2 * CMP_HIDDEN)

    return pe2, big(0), big(1), _block_diag([w2[0], w2[1]])


def _overlap_matrix(seq, ncp, nbp):
    cs = jnp.arange(ncp) * CMP_STRIDE
    bs = jnp.arange(nbp) * SEL_BLOCK
    ov = jnp.clip(jnp.minimum(cs[:, None] + CMP_LEN, bs[None, :] + SEL_BLOCK)
                  - jnp.maximum(cs[:, None], bs[None, :]), 0, None).astype(F32) / CMP_LEN
    keep = (jnp.arange(ncp)[:, None] < seq // CMP_STRIDE - 1) & (jnp.arange(nbp)[None, :] < seq // SEL_BLOCK)
    return jnp.where(keep, ov, 0.0)


def kernel(x, attn_norm_g, w_in, q_norm_g, k_norm_g, cmp_pe, cmp_w1, cmp_w2, sgu_norm_g, sgu_w, sgu_b, pool_w,
           pool_scale, conv_w, mix_out_norm_g, w_out, ffn_norm_g, ffn_w1, ffn_w3, ffn_w2, router_w, expert_w1,
           expert_w3, expert_w2):
    b, seq, _ = x.shape
    n = b * seq
    depth = w_in.shape[0]
    tm = 512
    ncp = seq // CMP_STRIDE
    nbp = max(LANES, seq // SEL_BLOCK)

    cos, sa, sb = _rope_tables(jnp.arange(seq))
    cos_c, sa_c, sb_c = _rope_tables(jnp.arange(ncp) * CMP_STRIDE + CMP_LEN - 1)
    bd = _block_diag([jnp.full((HEAD_DIM, HEAD_DIM), 1.0 / HEAD_DIM, F32)] * 2).astype(BF16)
    overlap = _overlap_matrix(seq, ncp, nbp).astype(BF16)
    row = lambda v: v.reshape(1, -1)
    two = lambda v: jnp.tile(v, 2).reshape(1, LANES)

    xf = x.reshape(n, D_MODEL)
    for layer in range(depth):
        kg = k_norm_g[layer]
        sgub = jnp.repeat(sgu_b[layer].T, HEAD_DIM, axis=1)
        q, kvc, kvs, kvw, gates, ybcd = _proj_call(
            xf, row(attn_norm_g[layer]), _layout_w_in(w_in[layer]).astype(BF16), bd, cos, sa, sb,
            two(q_norm_g[layer]), jnp.stack([jnp.tile(kg[1], 2), jnp.tile(kg[2], 2)]),
            row(sgu_norm_g[layer]), sgu_w[layer].reshape(N_HEADS * SGU_CHUNK, SGU_CHUNK), sgub,
            _block_diag([pool_w[layer, i] for i in range(len(POOL_WINDOWS))]).astype(BF16),
            row(pool_scale[layer]), conv_w[layer], mix_out_norm_g[layer].reshape(4, GROUP_WIDTH)[1:],
            seq=seq, tm=tm)
        pe2, w1a, w1b, w2c = _layout_compress(cmp_pe[layer], cmp_w1[layer], cmp_w2[layer])
        kvcc = _compress_call(kvc.reshape(b, ncp, CMP_STRIDE * LANES), pe2, w1a.astype(BF16), w1b.astype(BF16),
                              w2c.astype(BF16), two(kg[0]), bd, cos_c, sa_c, sb_c)
        ya = _attn_call(q, gates, kvcc, kvs, kvw, overlap, row(mix_out_norm_g[layer, :GROUP_WIDTH]),
                        seq=seq, tq=128, tk=512)
        i = layer // 2
        if layer % 2 == 0:
            x1, hn = _outproj_call(xf, ya, ybcd, w_out[layer].astype(BF16), row(ffn_norm_g[layer]), None, tm=tm)
            xf = _ffn_call(hn, x1, ffn_w1[i].astype(BF16), ffn_w3[i].astype(BF16), ffn_w2[i].astype(BF16),
                           tm=1024, tf=512)
        else:
            r = jnp.pad(router_w[i], ((0, 0), (0, LANES - N_EXPERTS)))
            r_hi = r.astype(BF16)
            rw = jnp.stack([r_hi, (r - r_hi.astype(F32)).astype(BF16)])
            x1, hn, gate = _outproj_call(xf, ya, ybcd, w_out[layer].astype(BF16), row(ffn_norm_g[layer]), rw, tm=tm)
            xf = _moe_call(hn, x1, gate, expert_w1[i].astype(BF16), expert_w3[i].astype(BF16),
                           expert_w2[i].astype(BF16), tm=1024, tf=512)
    return xf.reshape(b, seq, D_MODEL)
```

```python
import functools

import jax
import jax.numpy as jnp
from jax import lax
from jax.experimental import pallas as pl
from jax.experimental.pallas import tpu as pltpu

D_MODEL = 1024
HEAD_DIM = 64
N_HEADS = 4
GROUP_WIDTH = 256
CMP_LEN = 32
CMP_STRIDE = 16
CMP_HIDDEN = 128
SEL_BLOCK = 64
SEL_TOPK = 16
WINDOW = 512
N_BRANCH = 3
ROPE_THETA = 500000.0
ROPE_DIM = 16
SGU_CHUNK = 128
POOL_WINDOWS = (2, 4, 8, 16)
CONV_WIDTH = 3
D_FF = 3584
N_EXPERTS = 8
EPS = 1e-6
NEG_INF = -1e30

LANES = 128
MOE_ROWS = 128
HIST = 16
D_IN_PAD = 2304
VMEM_LIMIT = 52 * 1024 * 1024

F32 = jnp.float32
BF16 = jnp.bfloat16


def _dot(a, b):
    return jnp.dot(a, b, preferred_element_type=F32)


def _dot_nt(a, b):
    return lax.dot_general(a, b, (((1,), (1,)), ((), ())), preferred_element_type=F32)


def _split(a):
    hi = a.astype(BF16)
    lo = (a - hi.astype(F32)).astype(BF16)
    return hi, lo


def _rms(y, g):
    return y * lax.rsqrt(jnp.mean(y * y, axis=-1, keepdims=True) + EPS) * g


def _head_norm_rope(c, g, cos, sa, sb, bd):
    hi, lo = _split(c * c)
    msq = _dot(hi, bd) + _dot(lo, bd)
    cn = c * lax.rsqrt(msq + EPS) * g
    return cn * cos + pltpu.roll(cn, LANES - ROPE_DIM // 2, 1) * sa + pltpu.roll(cn, ROPE_DIM // 2, 1) * sb


def _proj_kernel(x_ref, g_ref, w_ref, bd_ref, cos_ref, sa_ref, sb_ref, qg_ref, kg_ref,
                 sgug_ref, sguw_ref, sgub_ref, poolw_ref, pools_ref, convw_ref, outg_ref,
                 q_out, kvc_out, kvs_out, kvw_out, gate_out, y_out, pext, zext, *, tm, tiles_per_seq):
    i = pl.program_id(0)
    x = x_ref[...]
    hn = _rms(x, g_ref[...]).astype(BF16)
    bd = bd_ref[...]
    cos, sa, sb = cos_ref[...], sa_ref[...], sb_ref[...]
    lane = lax.broadcasted_iota(jnp.int32, (1, LANES), 1)
    first_half = lane < HEAD_DIM

    def proj(c0, c1):
        return _dot(hn, w_ref[:, c0:c1])

    pq = proj(0, 256)
    for c in range(2):
        qc = _head_norm_rope(pq[:, c * LANES:(c + 1) * LANES], qg_ref[...], cos, sa, sb, bd) * (HEAD_DIM ** -0.5)
        q_out[:, (2 * c) * LANES:(2 * c + 1) * LANES] = jnp.where(first_half, qc, 0.0).astype(BF16)
        q_out[:, (2 * c + 1) * LANES:(2 * c + 2) * LANES] = jnp.where(
            first_half, pltpu.roll(qc, HEAD_DIM, 1), 0.0).astype(BF16)

    pk = proj(256, 512)
    kvc_out[...] = pk[:, 0:LANES]
    ks = pk[:, LANES:2 * LANES]
    kvs_out[...] = jnp.where(first_half, _head_norm_rope(ks, kg_ref[0:1, :], cos, sa, sb, bd), ks).astype(BF16)
    pk = proj(512, 768)
    kw = pk[:, 0:LANES]
    kvw_out[...] = jnp.where(first_half, _head_norm_rope(kw, kg_ref[1:2, :], cos, sa, sb, bd), kw).astype(BF16)
    gate_out[...] = jax.nn.sigmoid(pk[:, LANES:2 * LANES])

    outg = outg_ref[...]
    lane2 = lax.broadcasted_iota(jnp.int32, (1, GROUP_WIDTH), 1)
    grp = lane2 // HEAD_DIM

    u = proj(768, 1024)
    vn = _rms(proj(1024, 1280), sgug_ref[...]).astype(BF16)
    r_i = lax.broadcasted_iota(jnp.int32, (N_HEADS * SGU_CHUNK, SGU_CHUNK), 0) & (SGU_CHUNK - 1)
    c_i = lax.broadcasted_iota(jnp.int32, (N_HEADS * SGU_CHUNK, SGU_CHUNK), 1)
    wm = jnp.where(c_i <= r_i, sguw_ref[...], 0.0).astype(BF16)
    for c in range(tm // SGU_CHUNK):
        rows = slice(c * SGU_CHUNK, (c + 1) * SGU_CHUNK)
        r = _dot(wm, vn[rows, :])
        mixed = sgub_ref[...]
        for gi in range(N_HEADS):
            mixed = mixed + jnp.where(grp == gi, r[gi * SGU_CHUNK:(gi + 1) * SGU_CHUNK, :], 0.0)
        y_out[rows, 0:GROUP_WIDTH] = _rms(u[rows, :] * mixed, outg[0:1, :]).astype(BF16)

    first = (i % tiles_per_seq) == 0

    @pl.when(first)
    def _():
        pext[0:HIST, :] = jnp.zeros((HIST, GROUP_WIDTH), F32)
        zext[0:HIST, :] = jnp.zeros((HIST, GROUP_WIDTH), F32)

    @pl.when(jnp.logical_not(first))
    def _():
        pext[0:HIST, :] = pext[tm:tm + HIST, :]
        zext[0:HIST, :] = zext[tm:tm + HIST, :]

    pin = proj(1280, 1536)
    pext[HIST:, :] = pin
    pos = (i % tiles_per_seq) * tm + lax.broadcasted_iota(jnp.int32, (tm, 1), 0)
    acc = pin
    sums = []
    for k in range(1, POOL_WINDOWS[-1]):
        acc = acc + pext[HIST - k:HIST - k + tm, :]
        if k + 1 in POOL_WINDOWS:
            sums.append(acc)
    wsum = jnp.where(grp == 0, sums[0], jnp.where(grp == 1, sums[1], jnp.where(grp == 2, sums[2], sums[3])))
    win = jnp.where(grp == 0, POOL_WINDOWS[0], jnp.where(grp == 1, POOL_WINDOWS[1],
                                                         jnp.where(grp == 2, POOL_WINDOWS[2], POOL_WINDOWS[3])))
    cnt = jnp.minimum(pos + 1, win).astype(F32)
    dlt = (wsum / cnt - pin).astype(BF16)
    yc = _dot(dlt, poolw_ref[...]) * pools_ref[...]
    y_out[:, GROUP_WIDTH:2 * GROUP_WIDTH] = _rms(yc, outg[1:2, :]).astype(BF16)

    bg = proj(1536, 1792)
    z = proj(1792, 2048) * proj(2048, 2304)
    zext[HIST:, :] = z
    cw = convw_ref[...]
    conv = cw[2:3, :] * z + cw[1:2, :] * zext[HIST - 1:HIST - 1 + tm, :] + cw[0:1, :] * zext[HIST - 2:HIST - 2 + tm, :]
    y_out[:, 2 * GROUP_WIDTH:3 * GROUP_WIDTH] = _rms(bg * conv, outg[2:3, :]).astype(BF16)


def _proj_call(x, g, w, bd, cos, sa, sb, qg, kg, sgug, sguw, sgub, poolw, pools, convw, outg, *, seq, tm):
    n = x.shape[0]
    tps = seq // tm
    const = lambda shape: pl.BlockSpec(shape, lambda i: (0, 0))
    rows = lambda width: pl.BlockSpec((tm, width), lambda i: (i, 0))
    tab = pl.BlockSpec((tm, LANES), lambda i: (i % tps, 0))
    return pl.pallas_call(
        functools.partial(_proj_kernel, tm=tm, tiles_per_seq=tps),
        grid=(n // tm,),
        in_specs=[rows(D_MODEL), const((1, D_MODEL)), const((D_MODEL, D_IN_PAD)), const((LANES, LANES)),
                  tab, tab, tab, const((1, LANES)), const((2, LANES)),
                  const((1, GROUP_WIDTH)), const((N_HEADS * SGU_CHUNK, SGU_CHUNK)), const((SGU_CHUNK, GROUP_WIDTH)),
                  const((GROUP_WIDTH, GROUP_WIDTH)), const((1, GROUP_WIDTH)), const((CONV_WIDTH, GROUP_WIDTH)),
                  const((3, GROUP_WIDTH))],
        out_specs=[rows(4 * LANES), rows(LANES), rows(LANES), rows(LANES), rows(LANES), rows(3 * GROUP_WIDTH)],
        out_shape=[jax.ShapeDtypeStruct((n, 4 * LANES), BF16), jax.ShapeDtypeStruct((n, LANES), F32),
                   jax.ShapeDtypeStruct((n, LANES), BF16), jax.ShapeDtypeStruct((n, LANES), BF16),
                   jax.ShapeDtypeStruct((n, LANES), F32), jax.ShapeDtypeStruct((n, 3 * GROUP_WIDTH), BF16)],
        scratch_shapes=[pltpu.VMEM((tm + HIST, GROUP_WIDTH), F32), pltpu.VMEM((tm + HIST, GROUP_WIDTH), F32)],
        compiler_params=pltpu.CompilerParams(dimension_semantics=("arbitrary",), vmem_limit_bytes=VMEM_LIMIT),
        name="proj_mixers",
    )(x, g, w, bd, cos, sa, sb, qg, kg, sgug, sguw, sgub, poolw, pools, convw, outg)


def _compress_kernel(t_ref, pe_ref, w1a_ref, w1b_ref, w2_ref, kg_ref, bd_ref, cos_ref, sa_ref, sb_ref, o_ref):
    t = t_ref[0]
    ncp = t.shape[0]
    a = _dot((t + pe_ref[0:1, :]).astype(BF16), w1a_ref[...])
    b = _dot((t + pe_ref[1:2, :]).astype(BF16), w1b_ref[...])
    hid = jax.nn.gelu(a + pltpu.roll(b, ncp - 1, 0), approximate=True)
    kv = _dot(hid.astype(BF16), w2_ref[...])
    lane = lax.broadcasted_iota(jnp.int32, (1, LANES), 1)
    kn = _head_norm_rope(kv, kg_ref[...], cos_ref[...], sa_ref[...], sb_ref[...], bd_ref[...])
    o_ref[0] = jnp.where(lane < HEAD_DIM, kn, kv).astype(BF16)


def _compress_call(t2, pe, w1a, w1b, w2, kg, bd, cos, sa, sb):
    b, ncp, width = t2.shape
    const = lambda shape: pl.BlockSpec(shape, lambda i: (0,) * len(shape))
    return pl.pallas_call(
        _compress_kernel,
        grid=(b,),
        in_specs=[pl.BlockSpec((1, ncp, width), lambda i: (i, 0, 0)), const(pe.shape), const(w1a.shape),
                  const(w1b.shape), const(w2.shape), const(kg.shape), const(bd.shape),
                  const(cos.shape), const(sa.shape), const(sb.shape)],
        out_specs=pl.BlockSpec((1, ncp, LANES), lambda i: (i, 0, 0)),
        out_shape=jax.ShapeDtypeStruct((b, ncp, LANES), BF16),
        compiler_params=pltpu.CompilerParams(dimension_semantics=("arbitrary",), vmem_limit_bytes=VMEM_LIMIT),
        name="nsa_compress",
    )(t2, pe, w1a, w1b, w2, kg, bd, cos, sa, sb)


def _attn_kernel(q_ref, gate_ref, kvc_ref, kvs_ref, kvw_ref, ov_ref, og_ref, o_ref, *, tq, tk, n_sel):
    s0 = pl.program_id(1) * tq
    nr = N_HEADS * tq
    q = q_ref[...]
    qs = jnp.concatenate([q[:, h * LANES:(h + 1) * LANES] for h in range(N_HEADS)], axis=0)
    t_row = s0 + (lax.broadcasted_iota(jnp.int32, (nr, 1), 0) & (tq - 1))
    t_q = s0 + lax.broadcasted_iota(jnp.int32, (tq, 1), 0)

    kvc = kvc_ref[0]
    ncp = kvc.shape[0]
    cmp_end = lax.broadcasted_iota(jnp.int32, (1, ncp), 1) * CMP_STRIDE + (CMP_LEN - 1)
    valid = cmp_end <= t_row
    sc = jnp.where(valid, _dot_nt(qs, kvc), NEG_INF)
    e = jnp.where(valid, jnp.exp(sc - jnp.max(sc, axis=-1, keepdims=True)), 0.0)
    l = jnp.sum(e, axis=-1, keepdims=True)
    p = e / jnp.where(l > 0.0, l, 1.0)
    o_c = _dot(p.astype(BF16), kvc)
    psum = p[0:tq] + p[tq:2 * tq] + p[2 * tq:3 * tq] + p[3 * tq:4 * tq]
    hi, lo = _split(psum)
    imp = _dot(hi, ov_ref[...]) + _dot(lo, ov_ref[...])

    nbp = imp.shape[1]
    blk = lax.broadcasted_iota(jnp.int32, (1, nbp), 1)
    blk_f = blk.astype(F32)
    cur = t_q // SEL_BLOCK
    valid_b = blk <= cur
    forced = (blk == 0) | (blk == cur) | (blk == cur - 1)
    score = jnp.where(valid_b & forced, 1e4, jnp.where(valid_b, imp, -1.0))
    sel = jnp.zeros((tq, nbp), F32)
    for _ in range(n_sel):
        top = jnp.max(score, axis=-1, keepdims=True)
        idx = jnp.min(jnp.where(score == top, blk_f, float(nbp)), axis=-1, keepdims=True)
        hit = blk_f == idx
        sel = jnp.where(hit & (top >= 0.0), 1.0, sel)
        score = jnp.where(hit, -2.0, score)
    selb = sel.astype(BF16)

    def sel_step(j, carry):
        m, l, acc = carry
        k0 = pl.multiple_of(j * tk, tk)
        kv = kvs_ref[0, pl.ds(k0, tk), :]
        s = _dot_nt(qs, kv)
        kpos = k0 + lax.broadcasted_iota(jnp.int32, (1, tk), 1)
        expand = jnp.where(lax.broadcasted_iota(jnp.int32, (nbp, tk), 0) == kpos // SEL_BLOCK, 1.0, 0.0).astype(BF16)
        ok = (_dot(selb, expand) > 0.5) & (kpos <= t_q)
        bias = jnp.where(ok, 0.0, NEG_INF)
        s = (s.reshape(N_HEADS, tq, tk) + bias[None]).reshape(nr, tk)
        m_new = jnp.maximum(m, jnp.max(s, axis=-1, keepdims=True))
        alpha = jnp.exp(m - m_new)
        pe = jnp.exp(s - m_new)
        l = alpha * l + jnp.sum(pe, axis=-1, keepdims=True)
        acc = alpha * acc + _dot(pe.astype(BF16), kv)
        return m_new, l, acc

    n_kv = (s0 + tq + tk - 1) // tk
    init = (jnp.full((nr, 1), NEG_INF, F32), jnp.zeros((nr, 1), F32), jnp.zeros((nr, LANES), F32))
    _, l_s, acc_s = lax.fori_loop(0, n_kv, sel_step, init)
    o_s = acc_s / l_s

    wk = WINDOW + tq
    w0 = pl.multiple_of(jnp.maximum(s0 - WINDOW, 0), tq)
    kvw = kvw_ref[0, pl.ds(w0, wk), :]
    kp = w0 + lax.broadcasted_iota(jnp.int32, (1, wk), 1)
    ok = (kp <= t_row) & (kp > t_row - WINDOW)
    sw = jnp.where(ok, _dot_nt(qs, kvw), NEG_INF)
    ew = jnp.exp(sw - jnp.max(sw, axis=-1, keepdims=True))
    o_w = _dot(ew.astype(BF16), kvw) / jnp.sum(ew, axis=-1, keepdims=True)

    g = gate_ref[...]
    heads = []
    for h in range(N_HEADS):
        r = slice(h * tq, (h + 1) * tq)
        c = h * N_BRANCH
        heads.append(g[:, c:c + 1] * o_c[r] + g[:, c + 1:c + 2] * o_s[r] + g[:, c + 2:c + 3] * o_w[r])
    lane = lax.broadcasted_iota(jnp.int32, (1, LANES), 1)
    ya = jnp.concatenate([jnp.where(lane < HEAD_DIM, pltpu.roll(heads[0], HEAD_DIM, 1), heads[1]),
                          jnp.where(lane < HEAD_DIM, pltpu.roll(heads[2], HEAD_DIM, 1), heads[3])], axis=1)
    o_ref[...] = _rms(ya, og_ref[...]).astype(BF16)


def _attn_call(q, gates, kvc, kvs, kvw, overlap, og, *, seq, tq, tk):
    n = q.shape[0]
    b = n // seq
    nq = seq // tq
    ncp = kvc.shape[1]
    n_sel = min(SEL_TOPK, seq // SEL_BLOCK)
    rows = lambda width: pl.BlockSpec((tq, width), lambda bi, qi: (bi * nq + qi, 0))
    per_b = lambda r: pl.BlockSpec((1, r, LANES), lambda bi, qi: (bi, 0, 0))
    const = lambda shape: pl.BlockSpec(shape, lambda bi, qi: (0, 0))
    return pl.pallas_call(
        functools.partial(_attn_kernel, tq=tq, tk=tk, n_sel=n_sel),
        grid=(b, nq),
        in_specs=[rows(4 * LANES), rows(LANES), per_b(ncp), per_b(seq), per_b(seq),
                  const(overlap.shape), const((1, GROUP_WIDTH))],
        out_specs=rows(GROUP_WIDTH),
        out_shape=jax.ShapeDtypeStruct((n, GROUP_WIDTH), BF16),
        compiler_params=pltpu.CompilerParams(dimension_semantics=("arbitrary", "arbitrary"),
                                             vmem_limit_bytes=VMEM_LIMIT),
        name="nsa_attention",
    )(q, gates, kvc, kvs.reshape(b, seq, LANES), kvw.reshape(b, seq, LANES), overlap, og)


def _outproj_kernel(x_ref, ya_ref, yb_ref, w_ref, g_ref, *rest, moe):
    if moe:
        rw_ref, x1_ref, hn_ref, gate_ref = rest
    else:
        x1_ref, hn_ref = rest
    x1 = x_ref[...] + _dot(ya_ref[...], w_ref[0:GROUP_WIDTH, :]) + _dot(yb_ref[...], w_ref[GROUP_WIDTH:, :])
    x1_ref[...] = x1
    hn = _rms(x1, g_ref[...])
    hn_ref[...] = hn.astype(BF16)
    if moe:
        hi, lo = _split(hn)
        logits = _dot(hi, rw_ref[0]) + _dot(lo, rw_ref[0]) + _dot(hi, rw_ref[1])
        lane = lax.broadcasted_iota(jnp.int32, (1, LANES), 1)
        lane_f = lane.astype(F32)
        logits = jnp.where(lane < N_EXPERTS, logits, NEG_INF)
        m1 = jnp.max(logits, axis=-1, keepdims=True)
        i1 = jnp.min(jnp.where(logits == m1, lane_f, float(LANES)), axis=-1, keepdims=True)
        rest_l = jnp.where(lane_f == i1, 2 * NEG_INF, logits)
        m2 = jnp.max(rest_l, axis=-1, keepdims=True)
        i2 = jnp.min(jnp.where(rest_l == m2, lane_f, float(LANES)), axis=-1, keepdims=True)
        e2 = jnp.exp(m2 - m1)
        den = 1.0 + e2
        gate_ref[...] = jnp.where(lane_f == i1, 1.0 / den, jnp.where(lane_f == i2, e2 / den, 0.0))


def _outproj_call(x, ya, yb, w, g, rw, *, tm):
    n = x.shape[0]
    moe = rw is not None
    rows = lambda width: pl.BlockSpec((tm, width), lambda i: (i, 0))
    const = lambda shape: pl.BlockSpec(shape, lambda i: (0,) * len(shape))
    in_specs = [rows(D_MODEL), rows(GROUP_WIDTH), rows(3 * GROUP_WIDTH), const((D_MODEL, D_MODEL)), const((1, D_MODEL))]
    out_specs = [rows(D_MODEL), rows(D_MODEL)]
    out_shape = [jax.ShapeDtypeStruct((n, D_MODEL), F32), jax.ShapeDtypeStruct((n, D_MODEL), BF16)]
    args = [x, ya, yb, w, g]
    if moe:
        in_specs.append(const(rw.shape))
        out_specs.append(rows(LANES))
        out_shape.append(jax.ShapeDtypeStruct((n, LANES), F32))
        args.append(rw)
    return pl.pallas_call(
        functools.partial(_outproj_kernel, moe=moe),
        grid=(n // tm,),
        in_specs=in_specs, out_specs=out_specs, out_shape=out_shape,
        compiler_params=pltpu.CompilerParams(dimension_semantics=("arbitrary",), vmem_limit_bytes=VMEM_LIMIT),
        name="outproj_router" if moe else "outproj",
    )(*args)


def _ffn_kernel(h_ref, x_ref, w1_ref, w3_ref, w2_ref, o_ref, acc_ref):
    f = pl.program_id(1)

    @pl.when(f == 0)
    def _():
        acc_ref[...] = jnp.zeros_like(acc_ref)

    h = h_ref[...]
    a = _dot(h, w1_ref[...])
    t = (a * jax.nn.sigmoid(a) * _dot(h, w3_ref[...])).astype(BF16)
    acc_ref[...] += _dot(t, w2_ref[...])

    @pl.when(f == pl.num_programs(1) - 1)
    def _():
        o_ref[...] = x_ref[...] + acc_ref[...]


def _ffn_call(h, x, w1, w3, w2, *, tm, tf):
    n = h.shape[0]
    rows = pl.BlockSpec((tm, D_MODEL), lambda i, f: (i, 0))
    return pl.pallas_call(
        _ffn_kernel,
        grid=(n // tm, D_FF // tf),
        in_specs=[rows, rows, pl.BlockSpec((D_MODEL, tf), lambda i, f: (0, f)),
                  pl.BlockSpec((D_MODEL, tf), lambda i, f: (0, f)), pl.BlockSpec((tf, D_MODEL), lambda i, f: (f, 0))],
        out_specs=rows,
        out_shape=jax.ShapeDtypeStruct((n, D_MODEL), F32),
        scratch_shapes=[pltpu.VMEM((tm, D_MODEL), F32)],
        compiler_params=pltpu.CompilerParams(dimension_semantics=("arbitrary", "arbitrary"),
                                             vmem_limit_bytes=VMEM_LIMIT),
        name="ffn_swiglu",
    )(h, x, w1, w3, w2)


def _moe_kernel(h_ref, x_ref, gate_ref, w1_ref, w3_ref, w2_ref, o_ref,
                xc_ref, yc_ref, rank_ref, gate_t_ref, rank_t_ref, nch_ref, *, tm):
    e = pl.program_id(1)
    f = pl.program_id(2)
    rb = MOE_ROWS

    @pl.when((e == 0) & (f == 0))
    def _():
        o_ref[...] = x_ref[...]
        gate = gate_ref[...]
        routed = jnp.where(gate > 0.0, 1.0, 0.0).astype(BF16)
        before = jnp.where(lax.broadcasted_iota(jnp.int32, (tm, tm), 1) < lax.broadcasted_iota(jnp.int32, (tm, tm), 0),
                           1.0, 0.0).astype(BF16)
        rank = _dot(before, routed)
        rank_ref[...] = rank
        gate_t_ref[...] = gate.T[0:N_EXPERTS]
        rank_t_ref[...] = rank.T[0:N_EXPERTS]

    @pl.when(f == 0)
    def _():
        g_row = gate_t_ref[pl.ds(e, 1), :]
        n_rows = jnp.sum(jnp.where(g_row > 0.0, 1.0, 0.0)).astype(jnp.int32)
        n_chunks = (n_rows + rb - 1) // rb
        nch_ref[0] = n_chunks
        key = jnp.where(g_row > 0.0, rank_t_ref[pl.ds(e, 1), :], -1.0)

        def gather(c, carry):
            r0 = pl.multiple_of(c * rb, rb)
            slot = (r0 + lax.broadcasted_iota(jnp.int32, (rb, 1), 0)).astype(F32)
            onehot = jnp.where(key == slot, 1.0, 0.0).astype(BF16)
            xc_ref[pl.ds(r0, rb), :] = _dot(onehot, h_ref[...]).astype(BF16)
            yc_ref[pl.ds(r0, rb), :] = jnp.zeros((rb, D_MODEL), F32)
            return carry

        lax.fori_loop(0, n_chunks, gather, 0)

    n_chunks = nch_ref[0]

    def swiglu_rows(r0, rows):
        xc = xc_ref[pl.ds(r0, rows), :]
        a = _dot(xc, w1_ref[...])
        t = (a * jax.nn.sigmoid(a) * _dot(xc, w3_ref[...])).astype(BF16)
        yc_ref[pl.ds(r0, rows), :] += _dot(t, w2_ref[...])

    def pair(c, carry):
        swiglu_rows(pl.multiple_of(c * 2 * rb, 2 * rb), 2 * rb)
        return carry

    lax.fori_loop(0, n_chunks // 2, pair, 0)

    @pl.when(n_chunks % 2 == 1)
    def _():
        swiglu_rows(pl.multiple_of((n_chunks - 1) * rb, rb), rb)

    @pl.when(f == pl.num_programs(2) - 1)
    def _():
        lane = lax.broadcasted_iota(jnp.int32, (1, LANES), 1)
        g_col = jnp.sum(jnp.where(lane == e, gate_ref[...], 0.0), axis=-1, keepdims=True)
        r_col = jnp.sum(jnp.where(lane == e, rank_ref[...], 0.0), axis=-1, keepdims=True)
        key = jnp.where(g_col > 0.0, r_col, -1.0)

        def scatter(c, carry):
            r0 = pl.multiple_of(c * rb, rb)
            slot = (r0 + lax.broadcasted_iota(jnp.int32, (1, rb), 1)).astype(F32)
            onehot = jnp.where(key == slot, 1.0, 0.0).astype(BF16)
            o_ref[...] += g_col * _dot(onehot, yc_ref[pl.ds(r0, rb), :].astype(BF16))
            return carry

        lax.fori_loop(0, n_chunks, scatter, 0)


def _moe_call(h, x, gate, w1, w3, w2, *, tm, tf):
    n = h.shape[0]
    rows = lambda width: pl.BlockSpec((tm, width), lambda i, e, f: (i, 0))
    return pl.pallas_call(
        functools.partial(_moe_kernel, tm=tm),
        grid=(n // tm, N_EXPERTS, D_FF // tf),
        in_specs=[rows(D_MODEL), rows(D_MODEL), rows(LANES),
                  pl.BlockSpec((None, D_MODEL, tf), lambda i, e, f: (e, 0, f)),
                  pl.BlockSpec((None, D_MODEL, tf), lambda i, e, f: (e, 0, f)),
                  pl.BlockSpec((None, tf, D_MODEL), lambda i, e, f: (e, f, 0))],
        out_specs=rows(D_MODEL),
        out_shape=jax.ShapeDtypeStruct((n, D_MODEL), F32),
        scratch_shapes=[pltpu.VMEM((tm, D_MODEL), BF16), pltpu.VMEM((tm, D_MODEL), F32), pltpu.VMEM((tm, LANES), F32),
                        pltpu.VMEM((N_EXPERTS, tm), F32), pltpu.VMEM((N_EXPERTS, tm), F32),
                        pltpu.SMEM((1,), jnp.int32)],
        compiler_params=pltpu.CompilerParams(dimension_semantics=("arbitrary", "arbitrary", "arbitrary"),
                                             vmem_limit_bytes=VMEM_LIMIT),
        name="moe_swiglu",
    )(h, x, gate, w1, w3, w2)


def _rope_tables(pos):
    half = ROPE_DIM // 2
    inv_freq = ROPE_THETA ** (-jnp.arange(half, dtype=F32) / half)
    ang = pos.astype(F32)[..., None] * inv_freq
    cos, sin = jnp.cos(ang), jnp.sin(ang)
    n = pos.shape[0]
    rest = HEAD_DIM - ROPE_DIM
    cos64 = jnp.concatenate([cos, cos, jnp.ones((n, rest), F32)], axis=-1)
    sa64 = jnp.concatenate([-sin, jnp.zeros((n, rest + half), F32)], axis=-1)
    sb64 = jnp.concatenate([jnp.zeros((n, half), F32), sin, jnp.zeros((n, rest), F32)], axis=-1)
    return tuple(jnp.tile(t, (1, 2)) for t in (cos64, sa64, sb64))


def _block_diag(blocks):
    n = len(blocks)
    rows = []
    for i, blk in enumerate(blocks):
        rows.append(jnp.concatenate([blk if j == i else jnp.zeros((blk.shape[0], blocks[j].shape[1]), blk.dtype)
                                     for j in range(n)], axis=1))
    return jnp.concatenate(rows, axis=0)


def _layout_w_in(w):
    n_gate = N_BRANCH * N_HEADS
    g0 = GROUP_WIDTH + 6 * HEAD_DIM
    pad = jnp.zeros((w.shape[0], LANES - n_gate), w.dtype)
    return jnp.concatenate([w[:, :g0], w[:, g0:g0 + n_gate], pad, w[:, g0 + n_gate:]], axis=1)


def _layout_compress(pe, w1, w2):
    half = CMP_LEN // 2
    pe2 = jnp.transpose(pe, (1, 0, 2)).reshape(2, half * LANES)
    w1r = w1.reshape(2, 2, half, HEAD_DIM, CMP_HIDDEN)
    zeros = jnp.zeros((half, HEAD_DIM, CMP_HIDDEN), w1.dtype)

    def big(part):
        k_rows = jnp.concatenate([w1r[0, part], zeros], axis=-1)
        v_rows = jnp.concatenate([zeros, w1r[1, part]], axis=-1)
        return jnp.concatenate([k_rows, v_rows], axis=1).reshape(half * LANES, 2 * CMP_HIDDEN)

    return pe2, big(0), big(1), _block_diag([w2[0], w2[1]])


def _overlap_matrix(seq, ncp, nbp):
    cs = jnp.arange(ncp) * CMP_STRIDE
    bs = jnp.arange(nbp) * SEL_BLOCK
    ov = jnp.clip(jnp.minimum(cs[:, None] + CMP_LEN, bs[None, :] + SEL_BLOCK)
                  - jnp.maximum(cs[:, None], bs[None, :]), 0, None).astype(F32) / CMP_LEN
    keep = (jnp.arange(ncp)[:, None] < seq // CMP_STRIDE - 1) & (jnp.arange(nbp)[None, :] < seq // SEL_BLOCK)
    return jnp.where(keep, ov, 0.0)


def kernel(x, attn_norm_g, w_in, q_norm_g, k_norm_g, cmp_pe, cmp_w1, cmp_w2, sgu_norm_g, sgu_w, sgu_b, pool_w,
           pool_scale, conv_w, mix_out_norm_g, w_out, ffn_norm_g, ffn_w1, ffn_w3, ffn_w2, router_w, expert_w1,
           expert_w3, expert_w2):
    b, seq, _ = x.shape
    n = b * seq
    depth = w_in.shape[0]
    tm = 512
    ncp = seq // CMP_STRIDE
    nbp = max(LANES, seq // SEL_BLOCK)

    cos, sa, sb = _rope_tables(jnp.arange(seq))
    cos_c, sa_c, sb_c = _rope_tables(jnp.arange(ncp) * CMP_STRIDE + CMP_LEN - 1)
    bd = _block_diag([jnp.full((HEAD_DIM, HEAD_DIM), 1.0 / HEAD_DIM, F32)] * 2).astype(BF16)
    overlap = _overlap_matrix(seq, ncp, nbp).astype(BF16)
    row = lambda v: v.reshape(1, -1)
    two = lambda v: jnp.tile(v, 2).reshape(1, LANES)

    xf = x.reshape(n, D_MODEL)
    for layer in range(depth):
        kg = k_norm_g[layer]
        sgub = jnp.repeat(sgu_b[layer].T, HEAD_DIM, axis=1)
        q, kvc, kvs, kvw, gates, ybcd = _proj_call(
            xf, row(attn_norm_g[layer]), _layout_w_in(w_in[layer]).astype(BF16), bd, cos, sa, sb,
            two(q_norm_g[layer]), jnp.stack([jnp.tile(kg[1], 2), jnp.tile(kg[2], 2)]),
            row(sgu_norm_g[layer]), sgu_w[layer].reshape(N_HEADS * SGU_CHUNK, SGU_CHUNK), sgub,
            _block_diag([pool_w[layer, i] for i in range(len(POOL_WINDOWS))]).astype(BF16),
            row(pool_scale[layer]), conv_w[layer], mix_out_norm_g[layer].reshape(4, GROUP_WIDTH)[1:],
            seq=seq, tm=tm)
        pe2, w1a, w1b, w2c = _layout_compress(cmp_pe[layer], cmp_w1[layer], cmp_w2[layer])
        kvcc = _compress_call(kvc.reshape(b, ncp, CMP_STRIDE * LANES), pe2, w1a.astype(BF16), w1b.astype(BF16),
                              w2c.astype(BF16), two(kg[0]), bd, cos_c, sa_c, sb_c)
        ya = _attn_call(q, gates, kvcc, kvs, kvw, overlap, row(mix_out_norm_g[layer, :GROUP_WIDTH]),
                        seq=seq, tq=128, tk=512)
        i = layer // 2
        if layer % 2 == 0:
            x1, hn = _outproj_call(xf, ya, ybcd, w_out[layer].astype(BF16), row(ffn_norm_g[layer]), None, tm=tm)
            xf = _ffn_call(hn, x1, ffn_w1[i].astype(BF16), ffn_w3[i].astype(BF16), ffn_w2[i].astype(BF16),
                           tm=1024, tf=512)
        else:
            r = jnp.pad(router_w[i], ((0, 0), (0, LANES - N_EXPERTS)))
            r_hi = r.astype(BF16)
            rw = jnp.stack([r_hi, (r - r_hi.astype(F32)).astype(BF16)])
            x1, hn, gate = _outproj_call(xf, ya, ybcd, w_out[layer].astype(BF16), row(ffn_norm_g[layer]), rw, tm=tm)
            xf = _moe_call(hn, x1, gate, expert_w1[i].astype(BF16), expert_w3[i].astype(BF16),
                           expert_w2[i].astype(BF16), tm=1024, tf=512)
    return xf.reshape(b, seq, D_MODEL)
```

```python
import functools

import jax
import jax.numpy as jnp
from jax import lax
from jax.experimental import pallas as pl
from jax.experimental.pallas import tpu as pltpu

D_MODEL = 1024
HEAD_DIM = 64
N_HEADS = 4
GROUP_WIDTH = 256
CMP_LEN = 32
CMP_STRIDE = 16
CMP_HIDDEN = 128
SEL_BLOCK = 64
SEL_TOPK = 16
WINDOW = 512
N_BRANCH = 3
ROPE_THETA = 500000.0
ROPE_DIM = 16
SGU_CHUNK = 128
POOL_WINDOWS = (2, 4, 8, 16)
CONV_WIDTH = 3
D_FF = 3584
N_EXPERTS = 8
EPS = 1e-6
NEG_INF = -1e30

LANES = 128
MOE_ROWS = 128
MOE_SUB = 1024
HIST = 16
D_IN_PAD = 2304
VMEM_LIMIT = 52 * 1024 * 1024

F32 = jnp.float32
BF16 = jnp.bfloat16


def _dot(a, b):
    return jnp.dot(a, b, preferred_element_type=F32)


def _dot_nt(a, b):
    return lax.dot_general(a, b, (((1,), (1,)), ((), ())), preferred_element_type=F32)


def _split(a):
    hi = a.astype(BF16)
    lo = (a - hi.astype(F32)).astype(BF16)
    return hi, lo


def _rms(y, g):
    return y * lax.rsqrt(jnp.mean(y * y, axis=-1, keepdims=True) + EPS) * g


def _head_norm_rope(c, g, cos, sa, sb, bd):
    hi, lo = _split(c * c)
    msq = _dot(hi, bd) + _dot(lo, bd)
    cn = c * lax.rsqrt(msq + EPS) * g
    return cn * cos + pltpu.roll(cn, LANES - ROPE_DIM // 2, 1) * sa + pltpu.roll(cn, ROPE_DIM // 2, 1) * sb


def _proj_kernel(x_ref, g_ref, w_ref, bd_ref, cos_ref, sa_ref, sb_ref, qg_ref, kg_ref,
                 sgug_ref, sguw_ref, sgub_ref, poolw_ref, pools_ref, convw_ref, outg_ref,
                 q_out, kvc_out, kvs_out, kvw_out, gate_out, y_out, pext, zext, *, tm, tiles_per_seq):
    i = pl.program_id(0)
    x = x_ref[...]
    hn = _rms(x, g_ref[...]).astype(BF16)
    bd = bd_ref[...]
    cos, sa, sb = cos_ref[...], sa_ref[...], sb_ref[...]
    lane = lax.broadcasted_iota(jnp.int32, (1, LANES), 1)
    first_half = lane < HEAD_DIM

    def proj(c0, c1):
        return _dot(hn, w_ref[:, c0:c1])

    pq = proj(0, 256)
    for c in range(2):
        qc = _head_norm_rope(pq[:, c * LANES:(c + 1) * LANES], qg_ref[...], cos, sa, sb, bd) * (HEAD_DIM ** -0.5)
        q_out[:, (2 * c) * LANES:(2 * c + 1) * LANES] = jnp.where(first_half, qc, 0.0).astype(BF16)
        q_out[:, (2 * c + 1) * LANES:(2 * c + 2) * LANES] = jnp.where(
            first_half, pltpu.roll(qc, HEAD_DIM, 1), 0.0).astype(BF16)

    pk = proj(256, 512)
    kvc_out[...] = pk[:, 0:LANES]
    ks = pk[:, LANES:2 * LANES]
    kvs_out[...] = jnp.where(first_half, _head_norm_rope(ks, kg_ref[0:1, :], cos, sa, sb, bd), ks).astype(BF16)
    pk = proj(512, 768)
    kw = pk[:, 0:LANES]
    kvw_out[...] = jnp.where(first_half, _head_norm_rope(kw, kg_ref[1:2, :], cos, sa, sb, bd), kw).astype(BF16)
    gate_out[...] = jax.nn.sigmoid(pk[:, LANES:2 * LANES])

    outg = outg_ref[...]
    lane2 = lax.broadcasted_iota(jnp.int32, (1, GROUP_WIDTH), 1)
    grp = lane2 // HEAD_DIM

    u = proj(768, 1024)
    vn = _rms(proj(1024, 1280), sgug_ref[...]).astype(BF16)
    r_i = lax.broadcasted_iota(jnp.int32, (N_HEADS * SGU_CHUNK, SGU_CHUNK), 0) & (SGU_CHUNK - 1)
    c_i = lax.broadcasted_iota(jnp.int32, (N_HEADS * SGU_CHUNK, SGU_CHUNK), 1)
    wm = jnp.where(c_i <= r_i, sguw_ref[...], 0.0).astype(BF16)
    for c in range(tm // SGU_CHUNK):
        rows = slice(c * SGU_CHUNK, (c + 1) * SGU_CHUNK)
        r = _dot(wm, vn[rows, :])
        mixed = sgub_ref[...]
        for gi in range(N_HEADS):
            mixed = mixed + jnp.where(grp == gi, r[gi * SGU_CHUNK:(gi + 1) * SGU_CHUNK, :], 0.0)
        y_out[rows, 0:GROUP_WIDTH] = _rms(u[rows, :] * mixed, outg[0:1, :]).astype(BF16)

    first = (i % tiles_per_seq) == 0

    @pl.when(first)
    def _():
        pext[0:HIST, :] = jnp.zeros((HIST, GROUP_WIDTH), F32)
        zext[0:HIST, :] = jnp.zeros((HIST, GROUP_WIDTH), F32)

    @pl.when(jnp.logical_not(first))
    def _():
        pext[0:HIST, :] = pext[tm:tm + HIST, :]
        zext[0:HIST, :] = zext[tm:tm + HIST, :]

    pin = proj(1280, 1536)
    pext[HIST:, :] = pin
    pos = (i % tiles_per_seq) * tm + lax.broadcasted_iota(jnp.int32, (tm, 1), 0)
    acc = pin
    sums = []
    for k in range(1, POOL_WINDOWS[-1]):
        acc = acc + pext[HIST - k:HIST - k + tm, :]
        if k + 1 in POOL_WINDOWS:
            sums.append(acc)
    wsum = jnp.where(grp == 0, sums[0], jnp.where(grp == 1, sums[1], jnp.where(grp == 2, sums[2], sums[3])))
    win = jnp.where(grp == 0, POOL_WINDOWS[0], jnp.where(grp == 1, POOL_WINDOWS[1],
                                                         jnp.where(grp == 2, POOL_WINDOWS[2], POOL_WINDOWS[3])))
    cnt = jnp.minimum(pos + 1, win).astype(F32)
    dlt = (wsum / cnt - pin).astype(BF16)
    yc = _dot(dlt, poolw_ref[...]) * pools_ref[...]
    y_out[:, GROUP_WIDTH:2 * GROUP_WIDTH] = _rms(yc, outg[1:2, :]).astype(BF16)

    bg = proj(1536, 1792)
    z = proj(1792, 2048) * proj(2048, 2304)
    zext[HIST:, :] = z
    cw = convw_ref[...]
    conv = cw[2:3, :] * z + cw[1:2, :] * zext[HIST - 1:HIST - 1 + tm, :] + cw[0:1, :] * zext[HIST - 2:HIST - 2 + tm, :]
    y_out[:, 2 * GROUP_WIDTH:3 * GROUP_WIDTH] = _rms(bg * conv, outg[2:3, :]).astype(BF16)


def _proj_call(x, g, w, bd, cos, sa, sb, qg, kg, sgug, sguw, sgub, poolw, pools, convw, outg, *, seq, tm):
    n = x.shape[0]
    tps = seq // tm
    const = lambda shape: pl.BlockSpec(shape, lambda i: (0, 0))
    rows = lambda width: pl.BlockSpec((tm, width), lambda i: (i, 0))
    tab = pl.BlockSpec((tm, LANES), lambda i: (i % tps, 0))
    return pl.pallas_call(
        functools.partial(_proj_kernel, tm=tm, tiles_per_seq=tps),
        grid=(n // tm,),
        in_specs=[rows(D_MODEL), const((1, D_MODEL)), const((D_MODEL, D_IN_PAD)), const((LANES, LANES)),
                  tab, tab, tab, const((1, LANES)), const((2, LANES)),
                  const((1, GROUP_WIDTH)), const((N_HEADS * SGU_CHUNK, SGU_CHUNK)), const((SGU_CHUNK, GROUP_WIDTH)),
                  const((GROUP_WIDTH, GROUP_WIDTH)), const((1, GROUP_WIDTH)), const((CONV_WIDTH, GROUP_WIDTH)),
                  const((3, GROUP_WIDTH))],
        out_specs=[rows(4 * LANES), rows(LANES), rows(LANES), rows(LANES), rows(LANES), rows(3 * GROUP_WIDTH)],
        out_shape=[jax.ShapeDtypeStruct((n, 4 * LANES), BF16), jax.ShapeDtypeStruct((n, LANES), F32),
                   jax.ShapeDtypeStruct((n, LANES), BF16), jax.ShapeDtypeStruct((n, LANES), BF16),
                   jax.ShapeDtypeStruct((n, LANES), F32), jax.ShapeDtypeStruct((n, 3 * GROUP_WIDTH), BF16)],
        scratch_shapes=[pltpu.VMEM((tm + HIST, GROUP_WIDTH), F32), pltpu.VMEM((tm + HIST, GROUP_WIDTH), F32)],
        compiler_params=pltpu.CompilerParams(dimension_semantics=("arbitrary",), vmem_limit_bytes=VMEM_LIMIT),
        name="proj_mixers",
    )(x, g, w, bd, cos, sa, sb, qg, kg, sgug, sguw, sgub, poolw, pools, convw, outg)


def _compress_kernel(t_ref, pe_ref, w1a_ref, w1b_ref, w2_ref, kg_ref, bd_ref, cos_ref, sa_ref, sb_ref, o_ref):
    t = t_ref[0]
    ncp = t.shape[0]
    a = _dot((t + pe_ref[0:1, :]).astype(BF16), w1a_ref[...])
    b = _dot((t + pe_ref[1:2, :]).astype(BF16), w1b_ref[...])
    hid = jax.nn.gelu(a + pltpu.roll(b, ncp - 1, 0), approximate=True)
    kv = _dot(hid.astype(BF16), w2_ref[...])
    lane = lax.broadcasted_iota(jnp.int32, (1, LANES), 1)
    kn = _head_norm_rope(kv, kg_ref[...], cos_ref[...], sa_ref[...], sb_ref[...], bd_ref[...])
    o_ref[0] = jnp.where(lane < HEAD_DIM, kn, kv).astype(BF16)


def _compress_call(t2, pe, w1a, w1b, w2, kg, bd, cos, sa, sb):
    b, ncp, width = t2.shape
    const = lambda shape: pl.BlockSpec(shape, lambda i: (0,) * len(shape))
    return pl.pallas_call(
        _compress_kernel,
        grid=(b,),
        in_specs=[pl.BlockSpec((1, ncp, width), lambda i: (i, 0, 0)), const(pe.shape), const(w1a.shape),
                  const(w1b.shape), const(w2.shape), const(kg.shape), const(bd.shape),
                  const(cos.shape), const(sa.shape), const(sb.shape)],
        out_specs=pl.BlockSpec((1, ncp, LANES), lambda i: (i, 0, 0)),
        out_shape=jax.ShapeDtypeStruct((b, ncp, LANES), BF16),
        compiler_params=pltpu.CompilerParams(dimension_semantics=("arbitrary",), vmem_limit_bytes=VMEM_LIMIT),
        name="nsa_compress",
    )(t2, pe, w1a, w1b, w2, kg, bd, cos, sa, sb)


def _attn_kernel(q_ref, gate_ref, kvc_ref, kvs_ref, kvw_ref, ov_ref, og_ref, o_ref, *, tq, tk, n_sel):
    s0 = pl.program_id(1) * tq
    nr = N_HEADS * tq
    q = q_ref[...]
    qs = jnp.concatenate([q[:, h * LANES:(h + 1) * LANES] for h in range(N_HEADS)], axis=0)
    t_row = s0 + (lax.broadcasted_iota(jnp.int32, (nr, 1), 0) & (tq - 1))
    t_q = s0 + lax.broadcasted_iota(jnp.int32, (tq, 1), 0)

    kvc = kvc_ref[0]
    ncp = kvc.shape[0]
    cmp_end = lax.broadcasted_iota(jnp.int32, (1, ncp), 1) * CMP_STRIDE + (CMP_LEN - 1)
    valid = cmp_end <= t_row
    sc = jnp.where(valid, _dot_nt(qs, kvc), NEG_INF)
    e = jnp.where(valid, jnp.exp(sc - jnp.max(sc, axis=-1, keepdims=True)), 0.0)
    l = jnp.sum(e, axis=-1, keepdims=True)
    p = e / jnp.where(l > 0.0, l, 1.0)
    o_c = _dot(p.astype(BF16), kvc)
    psum = p[0:tq] + p[tq:2 * tq] + p[2 * tq:3 * tq] + p[3 * tq:4 * tq]
    hi, lo = _split(psum)
    imp = _dot(hi, ov_ref[...]) + _dot(lo, ov_ref[...])

    nbp = imp.shape[1]
    blk = lax.broadcasted_iota(jnp.int32, (1, nbp), 1)
    blk_f = blk.astype(F32)
    cur = t_q // SEL_BLOCK
    valid_b = blk <= cur
    forced = (blk == 0) | (blk == cur) | (blk == cur - 1)
    score = jnp.where(valid_b & forced, 1e4, jnp.where(valid_b, imp, -1.0))
    sel = jnp.zeros((tq, nbp), F32)
    for _ in range(n_sel):
        top = jnp.max(score, axis=-1, keepdims=True)
        idx = jnp.min(jnp.where(score == top, blk_f, float(nbp)), axis=-1, keepdims=True)
        hit = blk_f == idx
        sel = jnp.where(hit & (top >= 0.0), 1.0, sel)
        score = jnp.where(hit, -2.0, score)
    selb = sel.astype(BF16)

    def sel_step(j, carry):
        m, l, acc = carry
        k0 = pl.multiple_of(j * tk, tk)
        kv = kvs_ref[0, pl.ds(k0, tk), :]
        s = _dot_nt(qs, kv)
        kpos = k0 + lax.broadcasted_iota(jnp.int32, (1, tk), 1)
        expand = jnp.where(lax.broadcasted_iota(jnp.int32, (nbp, tk), 0) == kpos // SEL_BLOCK, 1.0, 0.0).astype(BF16)
        ok = (_dot(selb, expand) > 0.5) & (kpos <= t_q)
        bias = jnp.where(ok, 0.0, NEG_INF)
        s = (s.reshape(N_HEADS, tq, tk) + bias[None]).reshape(nr, tk)
        m_new = jnp.maximum(m, jnp.max(s, axis=-1, keepdims=True))
        alpha = jnp.exp(m - m_new)
        pe = jnp.exp(s - m_new)
        l = alpha * l + jnp.sum(pe, axis=-1, keepdims=True)
        acc = alpha * acc + _dot(pe.astype(BF16), kv)
        return m_new, l, acc

    n_kv = (s0 + tq + tk - 1) // tk
    init = (jnp.full((nr, 1), NEG_INF, F32), jnp.zeros((nr, 1), F32), jnp.zeros((nr, LANES), F32))
    _, l_s, acc_s = lax.fori_loop(0, n_kv, sel_step, init)
    o_s = acc_s / l_s

    wk = WINDOW + tq
    w0 = pl.multiple_of(jnp.maximum(s0 - WINDOW, 0), tq)
    kvw = kvw_ref[0, pl.ds(w0, wk), :]
    kp = w0 + lax.broadcasted_iota(jnp.int32, (1, wk), 1)
    ok = (kp <= t_row) & (kp > t_row - WINDOW)
    sw = jnp.where(ok, _dot_nt(qs, kvw), NEG_INF)
    ew = jnp.exp(sw - jnp.max(sw, axis=-1, keepdims=True))
    o_w = _dot(ew.astype(BF16), kvw) / jnp.sum(ew, axis=-1, keepdims=True)

    g = gate_ref[...]
    heads = []
    for h in range(N_HEADS):
        r = slice(h * tq, (h + 1) * tq)
        c = h * N_BRANCH
        heads.append(g[:, c:c + 1] * o_c[r] + g[:, c + 1:c + 2] * o_s[r] + g[:, c + 2:c + 3] * o_w[r])
    lane = lax.broadcasted_iota(jnp.int32, (1, LANES), 1)
    ya = jnp.concatenate([jnp.where(lane < HEAD_DIM, pltpu.roll(heads[0], HEAD_DIM, 1), heads[1]),
                          jnp.where(lane < HEAD_DIM, pltpu.roll(heads[2], HEAD_DIM, 1), heads[3])], axis=1)
    o_ref[...] = _rms(ya, og_ref[...]).astype(BF16)


def _attn_call(q, gates, kvc, kvs, kvw, overlap, og, *, seq, tq, tk):
    n = q.shape[0]
    b = n // seq
    nq = seq // tq
    ncp = kvc.shape[1]
    n_sel = min(SEL_TOPK, seq // SEL_BLOCK)
    rows = lambda width: pl.BlockSpec((tq, width), lambda bi, qi: (bi * nq + qi, 0))
    per_b = lambda r: pl.BlockSpec((1, r, LANES), lambda bi, qi: (bi, 0, 0))
    const = lambda shape: pl.BlockSpec(shape, lambda bi, qi: (0, 0))
    return pl.pallas_call(
        functools.partial(_attn_kernel, tq=tq, tk=tk, n_sel=n_sel),
        grid=(b, nq),
        in_specs=[rows(4 * LANES), rows(LANES), per_b(ncp), per_b(seq), per_b(seq),
                  const(overlap.shape), const((1, GROUP_WIDTH))],
        out_specs=rows(GROUP_WIDTH),
        out_shape=jax.ShapeDtypeStruct((n, GROUP_WIDTH), BF16),
        compiler_params=pltpu.CompilerParams(dimension_semantics=("arbitrary", "arbitrary"),
                                             vmem_limit_bytes=VMEM_LIMIT),
        name="nsa_attention",
    )(q, gates, kvc, kvs.reshape(b, seq, LANES), kvw.reshape(b, seq, LANES), overlap, og)


def _outproj_kernel(x_ref, ya_ref, yb_ref, w_ref, g_ref, *rest, moe):
    if moe:
        rw_ref, x1_ref, hn_ref, gate_ref = rest
    else:
        x1_ref, hn_ref = rest
    x1 = x_ref[...] + _dot(ya_ref[...], w_ref[0:GROUP_WIDTH, :]) + _dot(yb_ref[...], w_ref[GROUP_WIDTH:, :])
    x1_ref[...] = x1
    hn = _rms(x1, g_ref[...])
    hn_ref[...] = hn.astype(BF16)
    if moe:
        hi, lo = _split(hn)
        logits = _dot(hi, rw_ref[0]) + _dot(lo, rw_ref[0]) + _dot(hi, rw_ref[1])
        lane = lax.broadcasted_iota(jnp.int32, (1, LANES), 1)
        lane_f = lane.astype(F32)
        logits = jnp.where(lane < N_EXPERTS, logits, NEG_INF)
        m1 = jnp.max(logits, axis=-1, keepdims=True)
        i1 = jnp.min(jnp.where(logits == m1, lane_f, float(LANES)), axis=-1, keepdims=True)
        rest_l = jnp.where(lane_f == i1, 2 * NEG_INF, logits)
        m2 = jnp.max(rest_l, axis=-1, keepdims=True)
        i2 = jnp.min(jnp.where(rest_l == m2, lane_f, float(LANES)), axis=-1, keepdims=True)
        e2 = jnp.exp(m2 - m1)
        den = 1.0 + e2
        gate_ref[...] = jnp.where(lane_f == i1, 1.0 / den, jnp.where(lane_f == i2, e2 / den, 0.0))


def _outproj_call(x, ya, yb, w, g, rw, *, tm):
    n = x.shape[0]
    moe = rw is not None
    rows = lambda width: pl.BlockSpec((tm, width), lambda i: (i, 0))
    const = lambda shape: pl.BlockSpec(shape, lambda i: (0,) * len(shape))
    in_specs = [rows(D_MODEL), rows(GROUP_WIDTH), rows(3 * GROUP_WIDTH), const((D_MODEL, D_MODEL)), const((1, D_MODEL))]
    out_specs = [rows(D_MODEL), rows(D_MODEL)]
    out_shape = [jax.ShapeDtypeStruct((n, D_MODEL), F32), jax.ShapeDtypeStruct((n, D_MODEL), BF16)]
    args = [x, ya, yb, w, g]
    if moe:
        in_specs.append(const(rw.shape))
        out_specs.append(rows(LANES))
        out_shape.append(jax.ShapeDtypeStruct((n, LANES), F32))
        args.append(rw)
    return pl.pallas_call(
        functools.partial(_outproj_kernel, moe=moe),
        grid=(n // tm,),
        in_specs=in_specs, out_specs=out_specs, out_shape=out_shape,
        compiler_params=pltpu.CompilerParams(dimension_semantics=("arbitrary",), vmem_limit_bytes=VMEM_LIMIT),
        name="outproj_router" if moe else "outproj",
    )(*args)


def _ffn_kernel(h_ref, x_ref, w1_ref, w3_ref, w2_ref, o_ref, acc_ref):
    f = pl.program_id(1)

    @pl.when(f == 0)
    def _():
        acc_ref[...] = jnp.zeros_like(acc_ref)

    h = h_ref[...]
    a = _dot(h, w1_ref[...])
    t = (a * jax.nn.sigmoid(a) * _dot(h, w3_ref[...])).astype(BF16)
    acc_ref[...] += _dot(t, w2_ref[...])

    @pl.when(f == pl.num_programs(1) - 1)
    def _():
        o_ref[...] = x_ref[...] + acc_ref[...]


def _ffn_call(h, x, w1, w3, w2, *, tm, tf):
    n = h.shape[0]
    rows = pl.BlockSpec((tm, D_MODEL), lambda i, f: (i, 0))
    return pl.pallas_call(
        _ffn_kernel,
        grid=(n // tm, D_FF // tf),
        in_specs=[rows, rows, pl.BlockSpec((D_MODEL, tf), lambda i, f: (0, f)),
                  pl.BlockSpec((D_MODEL, tf), lambda i, f: (0, f)), pl.BlockSpec((tf, D_MODEL), lambda i, f: (f, 0))],
        out_specs=rows,
        out_shape=jax.ShapeDtypeStruct((n, D_MODEL), F32),
        scratch_shapes=[pltpu.VMEM((tm, D_MODEL), F32)],
        compiler_params=pltpu.CompilerParams(dimension_semantics=("arbitrary", "arbitrary"),
                                             vmem_limit_bytes=VMEM_LIMIT),
        name="ffn_swiglu",
    )(h, x, w1, w3, w2)


def _moe_kernel(h_ref, x_ref, gate_ref, w1_ref, w3_ref, w2_ref, o_ref,
                xc_ref, yc_ref, rank_ref, gate_t_ref, rank_t_ref, nch_ref, *, n_sub):
    e = pl.program_id(1)
    f = pl.program_id(2)
    rb = MOE_ROWS
    sub = MOE_SUB

    def for_chunks(n_chunks, fn):
        def pair(c, carry):
            fn(c * 2 * rb, 2 * rb)
            return carry

        lax.fori_loop(0, n_chunks // 2, pair, 0)

        @pl.when(n_chunks % 2 == 1)
        def _():
            fn((n_chunks - 1) * rb, rb)

    @pl.when((e == 0) & (f == 0))
    def _():
        o_ref[...] = x_ref[...]
        before = jnp.where(lax.broadcasted_iota(jnp.int32, (sub, sub), 1)
                           < lax.broadcasted_iota(jnp.int32, (sub, sub), 0), 1.0, 0.0).astype(BF16)
        for s in range(n_sub):
            span = slice(s * sub, (s + 1) * sub)
            gate = gate_ref[span, :]
            rank = _dot(before, jnp.where(gate > 0.0, 1.0, 0.0).astype(BF16))
            rank_ref[span, :] = rank
            gate_t_ref[:, span] = gate.T[0:N_EXPERTS]
            rank_t_ref[:, span] = rank.T[0:N_EXPERTS]

    @pl.when(f == 0)
    def _():
        for s in range(n_sub):
            span = slice(s * sub, (s + 1) * sub)
            g_row = gate_t_ref[pl.ds(e, 1), span]
            n_rows = jnp.sum(jnp.where(g_row > 0.0, 1.0, 0.0)).astype(jnp.int32)
            n_chunks = (n_rows + rb - 1) // rb
            nch_ref[s] = n_chunks
            key = jnp.where(g_row > 0.0, rank_t_ref[pl.ds(e, 1), span], -1.0)

            def gather(r0, rows, s=s, span=span, key=key):
                slot = (r0 + lax.broadcasted_iota(jnp.int32, (rows, 1), 0)).astype(F32)
                onehot = jnp.where(key == slot, 1.0, 0.0).astype(BF16)
                dst = pl.ds(pl.multiple_of(s * sub + r0, rb), rows)
                xc_ref[dst, :] = _dot(onehot, h_ref[span, :]).astype(BF16)
                yc_ref[dst, :] = jnp.zeros((rows, D_MODEL), F32)

            for_chunks(n_chunks, gather)

    for s in range(n_sub):
        def swiglu(r0, rows, s=s):
            src = pl.ds(pl.multiple_of(s * sub + r0, rb), rows)
            xc = xc_ref[src, :]
            a = _dot(xc, w1_ref[...])
            t = (a * jax.nn.sigmoid(a) * _dot(xc, w3_ref[...])).astype(BF16)
            yc_ref[src, :] += _dot(t, w2_ref[...])

        for_chunks(nch_ref[s], swiglu)

    @pl.when(f == pl.num_programs(2) - 1)
    def _():
        lane = lax.broadcasted_iota(jnp.int32, (1, LANES), 1)
        for s in range(n_sub):
            span = slice(s * sub, (s + 1) * sub)
            g_col = jnp.sum(jnp.where(lane == e, gate_ref[span, :], 0.0), axis=-1, keepdims=True)
            r_col = jnp.sum(jnp.where(lane == e, rank_ref[span, :], 0.0), axis=-1, keepdims=True)
            key = jnp.where(g_col > 0.0, r_col, -1.0)

            def scatter(r0, rows, s=s, span=span, key=key, g_col=g_col):
                slot = (r0 + lax.broadcasted_iota(jnp.int32, (1, rows), 1)).astype(F32)
                onehot = jnp.where(key == slot, 1.0, 0.0).astype(BF16)
                src = pl.ds(pl.multiple_of(s * sub + r0, rb), rows)
                o_ref[span, :] += g_col * _dot(onehot, yc_ref[src, :].astype(BF16))

            for_chunks(nch_ref[s], scatter)


def _moe_call(h, x, gate, w1, w3, w2, *, tm, tf):
    n = h.shape[0]
    n_sub = tm // MOE_SUB
    once = pl.Buffered(1)
    rows = lambda width, mode=None: pl.BlockSpec((tm, width), lambda i, e, f: (i, 0), pipeline_mode=mode)
    return pl.pallas_call(
        functools.partial(_moe_kernel, n_sub=n_sub),
        grid=(n // tm, N_EXPERTS, D_FF // tf),
        in_specs=[rows(D_MODEL, once), rows(D_MODEL, once), rows(LANES),
                  pl.BlockSpec((None, D_MODEL, tf), lambda i, e, f: (e, 0, f)),
                  pl.BlockSpec((None, D_MODEL, tf), lambda i, e, f: (e, 0, f)),
                  pl.BlockSpec((None, tf, D_MODEL), lambda i, e, f: (e, f, 0))],
        out_specs=rows(D_MODEL),
        out_shape=jax.ShapeDtypeStruct((n, D_MODEL), F32),
        scratch_shapes=[pltpu.VMEM((tm, D_MODEL), BF16), pltpu.VMEM((tm, D_MODEL), F32), pltpu.VMEM((tm, LANES), F32),
                        pltpu.VMEM((N_EXPERTS, tm), F32), pltpu.VMEM((N_EXPERTS, tm), F32),
                        pltpu.SMEM((n_sub,), jnp.int32)],
        compiler_params=pltpu.CompilerParams(dimension_semantics=("arbitrary", "arbitrary", "arbitrary"),
                                             vmem_limit_bytes=VMEM_LIMIT),
        name="moe_swiglu",
    )(h, x, gate, w1, w3, w2)


def _rope_tables(pos):
    half = ROPE_DIM // 2
    inv_freq = ROPE_THETA ** (-jnp.arange(half, dtype=F32) / half)
    ang = pos.astype(F32)[..., None] * inv_freq
    cos, sin = jnp.cos(ang), jnp.sin(ang)
    n = pos.shape[0]
    rest = HEAD_DIM - ROPE_DIM
    cos64 = jnp.concatenate([cos, cos, jnp.ones((n, rest), F32)], axis=-1)
    sa64 = jnp.concatenate([-sin, jnp.zeros((n, rest + half), F32)], axis=-1)
    sb64 = jnp.concatenate([jnp.zeros((n, half), F32), sin, jnp.zeros((n, rest), F32)], axis=-1)
    return tuple(jnp.tile(t, (1, 2)) for t in (cos64, sa64, sb64))


def _block_diag(blocks):
    n = len(blocks)
    rows = []
    for i, blk in enumerate(blocks):
        rows.append(jnp.concatenate([blk if j == i else jnp.zeros((blk.shape[0], blocks[j].shape[1]), blk.dtype)
                                     for j in range(n)], axis=1))
    return jnp.concatenate(rows, axis=0)


def _layout_w_in(w):
    n_gate = N_BRANCH * N_HEADS
    g0 = GROUP_WIDTH + 6 * HEAD_DIM
    pad = jnp.zeros((w.shape[0], LANES - n_gate), w.dtype)
    return jnp.concatenate([w[:, :g0], w[:, g0:g0 + n_gate], pad, w[:, g0 + n_gate:]], axis=1)


def _layout_compress(pe, w1, w2):
    half = CMP_LEN // 2
    pe2 = jnp.transpose(pe, (1, 0, 2)).reshape(2, half * LANES)
    w1r = w1.reshape(2, 2, half, HEAD_DIM, CMP_HIDDEN)
    zeros = jnp.zeros((half, HEAD_DIM, CMP_HIDDEN), w1.dtype)

    def big(part):
        k_rows = jnp.concatenate([w1r[0, part], zeros], axis=-1)
        v_rows = jnp.concatenate([zeros, w1r[1, part]], axis=-1)
        return jnp.concatenate([k_rows, v_rows], axis=1).reshape(half * LANES, 2 * CMP_HIDDEN)

    return pe2, big(0), big(1), _block_diag([w2[0], w2[1]])


def _overlap_matrix(seq, ncp, nbp):
    cs = jnp.arange(ncp) * CMP_STRIDE
    bs = jnp.arange(nbp) * SEL_BLOCK
    ov = jnp.clip(jnp.minimum(cs[:, None] + CMP_LEN, bs[None, :] + SEL_BLOCK)
                  - jnp.maximum(cs[:, None], bs[None, :]), 0, None).astype(F32) / CMP_LEN
    keep = (jnp.arange(ncp)[:, None] < seq // CMP_STRIDE - 1) & (jnp.arange(nbp)[None, :] < seq // SEL_BLOCK)
    return jnp.where(keep, ov, 0.0)


def kernel(x, attn_norm_g, w_in, q_norm_g, k_norm_g, cmp_pe, cmp_w1, cmp_w2, sgu_norm_g, sgu_w, sgu_b, pool_w,
           pool_scale, conv_w, mix_out_norm_g, w_out, ffn_norm_g, ffn_w1, ffn_w3, ffn_w2, router_w, expert_w1,
           expert_w3, expert_w2):
    b, seq, _ = x.shape
    n = b * seq
    depth = w_in.shape[0]
    tm = 512
    ncp = seq // CMP_STRIDE
    nbp = max(LANES, seq // SEL_BLOCK)

    cos, sa, sb = _rope_tables(jnp.arange(seq))
    cos_c, sa_c, sb_c = _rope_tables(jnp.arange(ncp) * CMP_STRIDE + CMP_LEN - 1)
    bd = _block_diag([jnp.full((HEAD_DIM, HEAD_DIM), 1.0 / HEAD_DIM, F32)] * 2).astype(BF16)
    overlap = _overlap_matrix(seq, ncp, nbp).astype(BF16)
    row = lambda v: v.reshape(1, -1)
    two = lambda v: jnp.tile(v, 2).reshape(1, LANES)

    xf = x.reshape(n, D_MODEL)
    for layer in range(depth):
        kg = k_norm_g[layer]
        sgub = jnp.repeat(sgu_b[layer].T, HEAD_DIM, axis=1)
        q, kvc, kvs, kvw, gates, ybcd = _proj_call(
            xf, row(attn_norm_g[layer]), _layout_w_in(w_in[layer]).astype(BF16), bd, cos, sa, sb,
            two(q_norm_g[layer]), jnp.stack([jnp.tile(kg[1], 2), jnp.tile(kg[2], 2)]),
            row(sgu_norm_g[layer]), sgu_w[layer].reshape(N_HEADS * SGU_CHUNK, SGU_CHUNK), sgub,
            _block_diag([pool_w[layer, i] for i in range(len(POOL_WINDOWS))]).astype(BF16),
            row(pool_scale[layer]), conv_w[layer], mix_out_norm_g[layer].reshape(4, GROUP_WIDTH)[1:],
            seq=seq, tm=tm)
        pe2, w1a, w1b, w2c = _layout_compress(cmp_pe[layer], cmp_w1[layer], cmp_w2[layer])
        kvcc = _compress_call(kvc.reshape(b, ncp, CMP_STRIDE * LANES), pe2, w1a.astype(BF16), w1b.astype(BF16),
                              w2c.astype(BF16), two(kg[0]), bd, cos_c, sa_c, sb_c)
        ya = _attn_call(q, gates, kvcc, kvs, kvw, overlap, row(mix_out_norm_g[layer, :GROUP_WIDTH]),
                        seq=seq, tq=128, tk=512)
        i = layer // 2
        if layer % 2 == 0:
            x1, hn = _outproj_call(xf, ya, ybcd, w_out[layer].astype(BF16), row(ffn_norm_g[layer]), None, tm=tm)
            xf = _ffn_call(hn, x1, ffn_w1[i].astype(BF16), ffn_w3[i].astype(BF16), ffn_w2[i].astype(BF16),
                           tm=1024, tf=512)
        else:
            r = jnp.pad(router_w[i], ((0, 0), (0, LANES - N_EXPERTS)))
            r_hi = r.astype(BF16)
            rw = jnp.stack([r_hi, (r - r_hi.astype(F32)).astype(BF16)])
            x1, hn, gate = _outproj_call(xf, ya, ybcd, w_out[layer].astype(BF16), row(ffn_norm_g[layer]), rw, tm=tm)
            xf = _moe_call(hn, x1, gate, expert_w1[i].astype(BF16), expert_w3[i].astype(BF16),
                           expert_w2[i].astype(BF16), tm=2 * MOE_SUB, tf=512)
    return xf.reshape(b, seq, D_MODEL)
```

```python
import functools

import jax
import jax.numpy as jnp
from jax import lax
from jax.experimental import pallas as pl
from jax.experimental.pallas import tpu as pltpu

D_MODEL = 1024
HEAD_DIM = 64
N_HEADS = 4
GROUP_WIDTH = 256
CMP_LEN = 32
CMP_STRIDE = 16
CMP_HIDDEN = 128
SEL_BLOCK = 64
SEL_TOPK = 16
WINDOW = 512
N_BRANCH = 3
ROPE_THETA = 500000.0
ROPE_DIM = 16
SGU_CHUNK = 128
POOL_WINDOWS = (2, 4, 8, 16)
CONV_WIDTH = 3
D_FF = 3584
N_EXPERTS = 8
EPS = 1e-6
NEG_INF = -1e30
Q_SCALE = HEAD_DIM ** -0.5 * 1.4426950408889634
GATE_ROWS = 16

LANES = 128
MOE_ROWS = 128
MOE_SUB = 1024
HIST = 16
D_IN_PAD = 2304
VMEM_LIMIT = 52 * 1024 * 1024

F32 = jnp.float32
BF16 = jnp.bfloat16


def _dot(a, b):
    return jnp.dot(a, b, preferred_element_type=F32)


def _split(a):
    hi = a.astype(BF16)
    lo = (a - hi.astype(F32)).astype(BF16)
    return hi, lo


def _rms(y, g):
    return y * lax.rsqrt(jnp.mean(y * y, axis=-1, keepdims=True) + EPS) * g


def _head_norm_rope(c, g, cos, sa, sb, bd):
    hi, lo = _split(c * c)
    msq = _dot(hi, bd) + _dot(lo, bd)
    cn = c * lax.rsqrt(msq + EPS) * g
    return cn * cos + pltpu.roll(cn, LANES - ROPE_DIM // 2, 1) * sa + pltpu.roll(cn, ROPE_DIM // 2, 1) * sb


def _proj_kernel(x_ref, g_ref, w_ref, bd_ref, cos_ref, sa_ref, sb_ref, qg_ref, kg_ref,
                 sgug_ref, sguw_ref, sgub_ref, poolw_ref, pools_ref, convw_ref, outg_ref,
                 qt_out, kvc_out, kvs_out, vts_out, kvw_out, vtw_out, gatet_out, y_out, pext, zext,
                 *, tm, tiles_per_seq):
    i = pl.program_id(0)
    x = x_ref[...]
    hn = _rms(x, g_ref[...]).astype(BF16)
    bd = bd_ref[...]
    cos, sa, sb = cos_ref[...], sa_ref[...], sb_ref[...]
    lane = lax.broadcasted_iota(jnp.int32, (1, LANES), 1)
    first_half = lane < HEAD_DIM

    def proj(c0, c1):
        return _dot(hn, w_ref[:, c0:c1])

    pq = proj(0, 256)
    for c in range(2):
        qc = _head_norm_rope(pq[:, c * LANES:(c + 1) * LANES], qg_ref[...], cos, sa, sb, bd) * Q_SCALE
        qt_out[(2 * c) * LANES:(2 * c + 1) * LANES, :] = jnp.where(first_half, qc, 0.0).T.astype(BF16)
        qt_out[(2 * c + 1) * LANES:(2 * c + 2) * LANES, :] = jnp.where(
            first_half, pltpu.roll(qc, HEAD_DIM, 1), 0.0).T.astype(BF16)

    pk = proj(256, 512)
    kvc_out[...] = pk[:, 0:LANES]
    ks = pk[:, LANES:2 * LANES]
    kvs_out[...] = jnp.where(first_half, _head_norm_rope(ks, kg_ref[0:1, :], cos, sa, sb, bd), ks).astype(BF16)
    vts_out[...] = jnp.where(first_half, 1.0, ks).T.astype(BF16)
    pk = proj(512, 768)
    kw = pk[:, 0:LANES]
    kvw_out[...] = jnp.where(first_half, _head_norm_rope(kw, kg_ref[1:2, :], cos, sa, sb, bd), kw).astype(BF16)
    vtw_out[...] = jnp.where(first_half, 1.0, kw).T.astype(BF16)
    gatet_out[...] = jax.nn.sigmoid(pk[:, LANES:2 * LANES]).T[0:GATE_ROWS, :]

    outg = outg_ref[...]
    lane2 = lax.broadcasted_iota(jnp.int32, (1, GROUP_WIDTH), 1)
    grp = lane2 // HEAD_DIM

    u = proj(768, 1024)
    vn = _rms(proj(1024, 1280), sgug_ref[...]).astype(BF16)
    r_i = lax.broadcasted_iota(jnp.int32, (N_HEADS * SGU_CHUNK, SGU_CHUNK), 0) & (SGU_CHUNK - 1)
    c_i = lax.broadcasted_iota(jnp.int32, (N_HEADS * SGU_CHUNK, SGU_CHUNK), 1)
    wm = jnp.where(c_i <= r_i, sguw_ref[...], 0.0).astype(BF16)
    for c in range(tm // SGU_CHUNK):
        rows = slice(c * SGU_CHUNK, (c + 1) * SGU_CHUNK)
        r = _dot(wm, vn[rows, :])
        mixed = sgub_ref[...]
        for gi in range(N_HEADS):
            mixed = mixed + jnp.where(grp == gi, r[gi * SGU_CHUNK:(gi + 1) * SGU_CHUNK, :], 0.0)
        y_out[rows, 0:GROUP_WIDTH] = _rms(u[rows, :] * mixed, outg[0:1, :]).astype(BF16)

    first = (i % tiles_per_seq) == 0

    @pl.when(first)
    def _():
        pext[0:HIST, :] = jnp.zeros((HIST, GROUP_WIDTH), F32)
        zext[0:HIST, :] = jnp.zeros((HIST, GROUP_WIDTH), F32)

    @pl.when(jnp.logical_not(first))
    def _():
        pext[0:HIST, :] = pext[tm:tm + HIST, :]
        zext[0:HIST, :] = zext[tm:tm + HIST, :]

    pin = proj(1280, 1536)
    pext[HIST:, :] = pin
    pos = (i % tiles_per_seq) * tm + lax.broadcasted_iota(jnp.int32, (tm, 1), 0)
    acc = pin
    sums = []
    for k in range(1, POOL_WINDOWS[-1]):
        acc = acc + pext[HIST - k:HIST - k + tm, :]
        if k + 1 in POOL_WINDOWS:
            sums.append(acc)
    wsum = jnp.where(grp == 0, sums[0], jnp.where(grp == 1, sums[1], jnp.where(grp == 2, sums[2], sums[3])))
    win = jnp.where(grp == 0, POOL_WINDOWS[0], jnp.where(grp == 1, POOL_WINDOWS[1],
                                                         jnp.where(grp == 2, POOL_WINDOWS[2], POOL_WINDOWS[3])))
    cnt = jnp.minimum(pos + 1, win).astype(F32)
    dlt = (wsum / cnt - pin).astype(BF16)
    yc = _dot(dlt, poolw_ref[...]) * pools_ref[...]
    y_out[:, GROUP_WIDTH:2 * GROUP_WIDTH] = _rms(yc, outg[1:2, :]).astype(BF16)

    bg = proj(1536, 1792)
    z = proj(1792, 2048) * proj(2048, 2304)
    zext[HIST:, :] = z
    cw = convw_ref[...]
    conv = cw[2:3, :] * z + cw[1:2, :] * zext[HIST - 1:HIST - 1 + tm, :] + cw[0:1, :] * zext[HIST - 2:HIST - 2 + tm, :]
    y_out[:, 2 * GROUP_WIDTH:3 * GROUP_WIDTH] = _rms(bg * conv, outg[2:3, :]).astype(BF16)


def _proj_call(x, g, w, bd, cos, sa, sb, qg, kg, sgug, sguw, sgub, poolw, pools, convw, outg, *, seq, tm):
    n = x.shape[0]
    tps = seq // tm
    const = lambda shape: pl.BlockSpec(shape, lambda i: (0, 0))
    rows = lambda width: pl.BlockSpec((tm, width), lambda i: (i, 0))
    cols = lambda height: pl.BlockSpec((height, tm), lambda i: (0, i))
    tab = pl.BlockSpec((tm, LANES), lambda i: (i % tps, 0))
    return pl.pallas_call(
        functools.partial(_proj_kernel, tm=tm, tiles_per_seq=tps),
        grid=(n // tm,),
        in_specs=[rows(D_MODEL), const((1, D_MODEL)), const((D_MODEL, D_IN_PAD)), const((LANES, LANES)),
                  tab, tab, tab, const((1, LANES)), const((2, LANES)),
                  const((1, GROUP_WIDTH)), const((N_HEADS * SGU_CHUNK, SGU_CHUNK)), const((SGU_CHUNK, GROUP_WIDTH)),
                  const((GROUP_WIDTH, GROUP_WIDTH)), const((1, GROUP_WIDTH)), const((CONV_WIDTH, GROUP_WIDTH)),
                  const((3, GROUP_WIDTH))],
        out_specs=[cols(N_HEADS * LANES), rows(LANES), rows(LANES), cols(LANES), rows(LANES), cols(LANES),
                   cols(GATE_ROWS), rows(3 * GROUP_WIDTH)],
        out_shape=[jax.ShapeDtypeStruct((N_HEADS * LANES, n), BF16), jax.ShapeDtypeStruct((n, LANES), F32),
                   jax.ShapeDtypeStruct((n, LANES), BF16), jax.ShapeDtypeStruct((LANES, n), BF16),
                   jax.ShapeDtypeStruct((n, LANES), BF16), jax.ShapeDtypeStruct((LANES, n), BF16),
                   jax.ShapeDtypeStruct((GATE_ROWS, n), F32), jax.ShapeDtypeStruct((n, 3 * GROUP_WIDTH), BF16)],
        scratch_shapes=[pltpu.VMEM((tm + HIST, GROUP_WIDTH), F32), pltpu.VMEM((tm + HIST, GROUP_WIDTH), F32)],
        compiler_params=pltpu.CompilerParams(dimension_semantics=("arbitrary",), vmem_limit_bytes=VMEM_LIMIT),
        name="proj_mixers",
    )(x, g, w, bd, cos, sa, sb, qg, kg, sgug, sguw, sgub, poolw, pools, convw, outg)


def _compress_kernel(t_ref, pe_ref, w1a_ref, w1b_ref, w2_ref, kg_ref, bd_ref, cos_ref, sa_ref, sb_ref, kv_ref, vt_ref):
    t = t_ref[0]
    ncp = t.shape[0]
    a = _dot((t + pe_ref[0:1, :]).astype(BF16), w1a_ref[...])
    b = _dot((t + pe_ref[1:2, :]).astype(BF16), w1b_ref[...])
    hid = jax.nn.gelu(a + pltpu.roll(b, ncp - 1, 0), approximate=True)
    kv = _dot(hid.astype(BF16), w2_ref[...])
    first_half = lax.broadcasted_iota(jnp.int32, (1, LANES), 1) < HEAD_DIM
    kn = _head_norm_rope(kv, kg_ref[...], cos_ref[...], sa_ref[...], sb_ref[...], bd_ref[...])
    kv_ref[0] = jnp.where(first_half, kn, kv).astype(BF16)
    vt_ref[0] = jnp.where(first_half, 1.0, kv).T.astype(BF16)


def _compress_call(t2, pe, w1a, w1b, w2, kg, bd, cos, sa, sb):
    b, ncp, width = t2.shape
    const = lambda shape: pl.BlockSpec(shape, lambda i: (0,) * len(shape))
    return pl.pallas_call(
        _compress_kernel,
        grid=(b,),
        in_specs=[pl.BlockSpec((1, ncp, width), lambda i: (i, 0, 0)), const(pe.shape), const(w1a.shape),
                  const(w1b.shape), const(w2.shape), const(kg.shape), const(bd.shape),
                  const(cos.shape), const(sa.shape), const(sb.shape)],
        out_specs=[pl.BlockSpec((1, ncp, LANES), lambda i: (i, 0, 0)), pl.BlockSpec((1, LANES, ncp), lambda i: (i, 0, 0))],
        out_shape=[jax.ShapeDtypeStruct((b, ncp, LANES), BF16), jax.ShapeDtypeStruct((b, LANES, ncp), BF16)],
        compiler_params=pltpu.CompilerParams(dimension_semantics=("arbitrary",), vmem_limit_bytes=VMEM_LIMIT),
        name="nsa_compress",
    )(t2, pe, w1a, w1b, w2, kg, bd, cos, sa, sb)


def _attn_kernel(qt_ref, gatet_ref, kvc_ref, vtc_ref, kvs_ref, vts_ref, kvw_ref, vtw_ref, ovt_ref, og_ref, o_ref,
                 bias_ref, sa_ref, sb_ref, m_ref, acc_ref, *, tq, tk, n_sel):
    s0 = pl.program_id(1) * tq
    nc = N_HEADS * tq
    qs = jnp.concatenate([qt_ref[h * LANES:(h + 1) * LANES, :] for h in range(N_HEADS)], axis=1)
    t_col = s0 + (lax.broadcasted_iota(jnp.int32, (1, nc), 1) & (tq - 1))
    t_q = s0 + lax.broadcasted_iota(jnp.int32, (1, tq), 1)

    kvc = kvc_ref[0]
    ncp = kvc.shape[0]
    cmp_end = lax.broadcasted_iota(jnp.int32, (ncp, 1), 0) * CMP_STRIDE + (CMP_LEN - 1)
    valid = cmp_end <= t_col
    sc = jnp.where(valid, _dot(kvc, qs), NEG_INF)
    e = jnp.exp2(sc - jnp.max(sc, axis=0, keepdims=True))
    p = e * jnp.where(t_col >= CMP_LEN - 1, 1.0 / jnp.sum(e, axis=0, keepdims=True), 0.0)
    o_c = _dot(vtc_ref[0], p.astype(BF16))[HEAD_DIM:, :]
    psum = p[:, 0:tq] + p[:, tq:2 * tq] + p[:, 2 * tq:3 * tq] + p[:, 3 * tq:4 * tq]
    hi, lo = _split(psum)
    imp = _dot(ovt_ref[...], hi) + _dot(ovt_ref[...], lo)

    nbp = imp.shape[0]
    blk = lax.broadcasted_iota(jnp.int32, (nbp, 1), 0)
    blk_f = blk.astype(F32)
    cur = t_q // SEL_BLOCK
    valid_b = blk <= cur
    forced = (blk == 0) | (blk == cur) | (blk == cur - 1)
    score = jnp.where(valid_b & forced, 1e4, jnp.where(valid_b, imp, -1.0))
    sel = jnp.zeros((nbp, tq), F32)
    for _ in range(n_sel):
        top = jnp.max(score, axis=0, keepdims=True)
        idx = jnp.min(jnp.where(score == top, blk_f, float(nbp)), axis=0, keepdims=True)
        hit = blk_f == idx
        sel = jnp.where(hit & (top >= 0.0), 1.0, sel)
        score = jnp.where(hit, -2.0, score)
    bias = jnp.where(sel > 0.5, 0.0, NEG_INF)
    bias_ref[...] = jnp.concatenate([bias] * N_HEADS, axis=1)

    def scores(j):
        return _dot(kvs_ref[pl.ds(pl.multiple_of(j * tk, tk), tk), :], qs)

    def consume(s_ref, j, causal):
        k0 = pl.multiple_of(j * tk, tk)
        b0 = j * (tk // SEL_BLOCK)
        s = jnp.concatenate([s_ref[b * SEL_BLOCK:(b + 1) * SEL_BLOCK, :] + bias_ref[pl.ds(b0 + b, 1), :]
                             for b in range(tk // SEL_BLOCK)], axis=0)
        if causal:
            kpos = k0 + lax.broadcasted_iota(jnp.int32, (tk, 1), 0)
            s = jnp.where(kpos <= t_col, s, NEG_INF)
        m = m_ref[...]
        m_new = jnp.maximum(m, jnp.max(s, axis=0, keepdims=True))
        pe = jnp.exp2(s - m_new).astype(BF16)
        acc_ref[...] = jnp.exp2(m - m_new) * acc_ref[...] + _dot(vts_ref[:, pl.ds(k0, tk)], pe)
        m_ref[...] = m_new

    n_full = (s0 + tq + tk - 1) // tk - 1
    m_ref[...] = jnp.full((1, nc), NEG_INF, F32)
    acc_ref[...] = jnp.zeros((LANES, nc), F32)
    sa_ref[...] = scores(0)

    def pair(i, carry):
        j = 2 * i
        sb_ref[...] = scores(j + 1)
        consume(sa_ref, j, False)
        sa_ref[...] = scores(jnp.minimum(j + 2, n_full))
        consume(sb_ref, j + 1, False)
        return carry

    lax.fori_loop(0, n_full // 2, pair, 0)

    @pl.when(n_full % 2 == 1)
    def _():
        sb_ref[...] = scores(n_full)
        consume(sa_ref, n_full - 1, False)
        consume(sb_ref, n_full, True)

    @pl.when(n_full % 2 == 0)
    def _():
        consume(sa_ref, n_full, True)

    acc_s = acc_ref[...]
    o_s = acc_s[HEAD_DIM:, :] / acc_s[0:1, :]

    wk = WINDOW + tq
    w0 = pl.multiple_of(jnp.maximum(s0 - WINDOW, 0), tq)
    kp = w0 + lax.broadcasted_iota(jnp.int32, (wk, 1), 0)
    ok = (kp <= t_col) & (kp > t_col - WINDOW)
    sw = jnp.where(ok, _dot(kvw_ref[pl.ds(w0, wk), :], qs), NEG_INF)
    pw = jnp.exp2(sw - jnp.max(sw, axis=0, keepdims=True)).astype(BF16)
    acc_w = _dot(vtw_ref[:, pl.ds(w0, wk)], pw)
    o_w = acc_w[HEAD_DIM:, :] / acc_w[0:1, :]

    g = gatet_ref[...]
    heads = []
    for h in range(N_HEADS):
        c = slice(h * tq, (h + 1) * tq)
        r = h * N_BRANCH
        heads.append(g[r:r + 1, :] * o_c[:, c] + g[r + 1:r + 2, :] * o_s[:, c] + g[r + 2:r + 3, :] * o_w[:, c])
    ya = jnp.concatenate(heads, axis=0).T
    o_ref[...] = _rms(ya, og_ref[...]).astype(BF16)


def _attn_call(qt, gatet, kvc, vtc, kvs, vts, kvw, vtw, ovt, og, *, seq, tq, tk):
    n = qt.shape[1]
    b = n // seq
    nq = seq // tq
    ncp = kvc.shape[1]
    nbp = ovt.shape[0]
    n_sel = min(SEL_TOPK, seq // SEL_BLOCK)
    cols = lambda height: pl.BlockSpec((height, tq), lambda bi, qi: (0, bi * nq + qi))
    per_b = lambda shape: pl.BlockSpec((1,) + shape, lambda bi, qi: (bi, 0, 0))
    const = lambda shape: pl.BlockSpec(shape, lambda bi, qi: (0, 0))
    seq_rows = pl.BlockSpec((seq, LANES), lambda bi, qi: (bi, 0))
    seq_cols = pl.BlockSpec((LANES, seq), lambda bi, qi: (0, bi))
    return pl.pallas_call(
        functools.partial(_attn_kernel, tq=tq, tk=tk, n_sel=n_sel),
        grid=(b, nq),
        in_specs=[cols(N_HEADS * LANES), cols(GATE_ROWS), per_b((ncp, LANES)), per_b((LANES, ncp)),
                  seq_rows, seq_cols, seq_rows, seq_cols, const(ovt.shape), const((1, GROUP_WIDTH))],
        out_specs=pl.BlockSpec((tq, GROUP_WIDTH), lambda bi, qi: (bi * nq + qi, 0)),
        out_shape=jax.ShapeDtypeStruct((n, GROUP_WIDTH), BF16),
        scratch_shapes=[pltpu.VMEM((nbp, N_HEADS * tq), F32), pltpu.VMEM((tk, N_HEADS * tq), F32),
                        pltpu.VMEM((tk, N_HEADS * tq), F32), pltpu.VMEM((1, N_HEADS * tq), F32),
                        pltpu.VMEM((LANES, N_HEADS * tq), F32)],
        compiler_params=pltpu.CompilerParams(dimension_semantics=("arbitrary", "arbitrary"),
                                             vmem_limit_bytes=VMEM_LIMIT),
        name="nsa_attention",
    )(qt, gatet, kvc, vtc, kvs, vts, kvw, vtw, ovt, og)


def _outproj_kernel(x_ref, ya_ref, yb_ref, w_ref, g_ref, *rest, moe):
    if moe:
        rw_ref, x1_ref, hn_ref, gate_ref = rest
    else:
        x1_ref, hn_ref = rest
    x1 = x_ref[...] + _dot(ya_ref[...], w_ref[0:GROUP_WIDTH, :]) + _dot(yb_ref[...], w_ref[GROUP_WIDTH:, :])
    x1_ref[...] = x1
    hn = _rms(x1, g_ref[...])
    hn_ref[...] = hn.astype(BF16)
    if moe:
        hi, lo = _split(hn)
        logits = _dot(hi, rw_ref[0]) + _dot(lo, rw_ref[0]) + _dot(hi, rw_ref[1])
        lane = lax.broadcasted_iota(jnp.int32, (1, LANES), 1)
        lane_f = lane.astype(F32)
        logits = jnp.where(lane < N_EXPERTS, logits, NEG_INF)
        m1 = jnp.max(logits, axis=-1, keepdims=True)
        i1 = jnp.min(jnp.where(logits == m1, lane_f, float(LANES)), axis=-1, keepdims=True)
        rest_l = jnp.where(lane_f == i1, 2 * NEG_INF, logits)
        m2 = jnp.max(rest_l, axis=-1, keepdims=True)
        i2 = jnp.min(jnp.where(rest_l == m2, lane_f, float(LANES)), axis=-1, keepdims=True)
        e2 = jnp.exp(m2 - m1)
        den = 1.0 + e2
        gate_ref[...] = jnp.where(lane_f == i1, 1.0 / den, jnp.where(lane_f == i2, e2 / den, 0.0))


def _outproj_call(x, ya, yb, w, g, rw, *, tm):
    n = x.shape[0]
    moe = rw is not None
    rows = lambda width: pl.BlockSpec((tm, width), lambda i: (i, 0))
    const = lambda shape: pl.BlockSpec(shape, lambda i: (0,) * len(shape))
    in_specs = [rows(D_MODEL), rows(GROUP_WIDTH), rows(3 * GROUP_WIDTH), const((D_MODEL, D_MODEL)), const((1, D_MODEL))]
    out_specs = [rows(D_MODEL), rows(D_MODEL)]
    out_shape = [jax.ShapeDtypeStruct((n, D_MODEL), F32), jax.ShapeDtypeStruct((n, D_MODEL), BF16)]
    args = [x, ya, yb, w, g]
    if moe:
        in_specs.append(const(rw.shape))
        out_specs.append(rows(LANES))
        out_shape.append(jax.ShapeDtypeStruct((n, LANES), F32))
        args.append(rw)
    return pl.pallas_call(
        functools.partial(_outproj_kernel, moe=moe),
        grid=(n // tm,),
        in_specs=in_specs, out_specs=out_specs, out_shape=out_shape,
        compiler_params=pltpu.CompilerParams(dimension_semantics=("arbitrary",), vmem_limit_bytes=VMEM_LIMIT),
        name="outproj_router" if moe else "outproj",
    )(*args)


def _ffn_kernel(h_ref, x_ref, w1_ref, w3_ref, w2_ref, o_ref, acc_ref):
    f = pl.program_id(1)

    @pl.when(f == 0)
    def _():
        acc_ref[...] = jnp.zeros_like(acc_ref)

    h = h_ref[...]
    a = _dot(h, w1_ref[...])
    t = (a * jax.nn.sigmoid(a) * _dot(h, w3_ref[...])).astype(BF16)
    acc_ref[...] += _dot(t, w2_ref[...])

    @pl.when(f == pl.num_programs(1) - 1)
    def _():
        o_ref[...] = x_ref[...] + acc_ref[...]


def _ffn_call(h, x, w1, w3, w2, *, tm, tf):
    n = h.shape[0]
    rows = pl.BlockSpec((tm, D_MODEL), lambda i, f: (i, 0))
    return pl.pallas_call(
        _ffn_kernel,
        grid=(n // tm, D_FF // tf),
        in_specs=[rows, rows, pl.BlockSpec((D_MODEL, tf), lambda i, f: (0, f)),
                  pl.BlockSpec((D_MODEL, tf), lambda i, f: (0, f)), pl.BlockSpec((tf, D_MODEL), lambda i, f: (f, 0))],
        out_specs=rows,
        out_shape=jax.ShapeDtypeStruct((n, D_MODEL), F32),
        scratch_shapes=[pltpu.VMEM((tm, D_MODEL), F32)],
        compiler_params=pltpu.CompilerParams(dimension_semantics=("arbitrary", "arbitrary"),
                                             vmem_limit_bytes=VMEM_LIMIT),
        name="ffn_swiglu",
    )(h, x, w1, w3, w2)


def _moe_kernel(h_ref, x_ref, gate_ref, w1_ref, w3_ref, w2_ref, o_ref,
                xc_ref, yc_ref, rank_ref, gate_t_ref, rank_t_ref, nch_ref, *, n_sub):
    e = pl.program_id(1)
    f = pl.program_id(2)
    rb = MOE_ROWS
    sub = MOE_SUB

    def for_chunks(n_chunks, fn):
        def pair(c, carry):
            fn(c * 2 * rb, 2 * rb)
            return carry

        lax.fori_loop(0, n_chunks // 2, pair, 0)

        @pl.when(n_chunks % 2 == 1)
        def _():
            fn((n_chunks - 1) * rb, rb)

    @pl.when((e == 0) & (f == 0))
    def _():
        o_ref[...] = x_ref[...]
        before = jnp.where(lax.broadcasted_iota(jnp.int32, (sub, sub), 1)
                           < lax.broadcasted_iota(jnp.int32, (sub, sub), 0), 1.0, 0.0).astype(BF16)
        for s in range(n_sub):
            span = slice(s * sub, (s + 1) * sub)
            gate = gate_ref[span, :]
            rank = _dot(before, jnp.where(gate > 0.0, 1.0, 0.0).astype(BF16))
            rank_ref[span, :] = rank
            gate_t_ref[:, span] = gate.T[0:N_EXPERTS]
            rank_t_ref[:, span] = rank.T[0:N_EXPERTS]

    @pl.when(f == 0)
    def _():
        for s in range(n_sub):
            span = slice(s * sub, (s + 1) * sub)
            g_row = gate_t_ref[pl.ds(e, 1), span]
            n_rows = jnp.sum(jnp.where(g_row > 0.0, 1.0, 0.0)).astype(jnp.int32)
            n_chunks = (n_rows + rb - 1) // rb
            nch_ref[s] = n_chunks
            key = jnp.where(g_row > 0.0, rank_t_ref[pl.ds(e, 1), span], -1.0)

            def gather(r0, rows, s=s, span=span, key=key):
                slot = (r0 + lax.broadcasted_iota(jnp.int32, (rows, 1), 0)).astype(F32)
                onehot = jnp.where(key == slot, 1.0, 0.0).astype(BF16)
                dst = pl.ds(pl.multiple_of(s * sub + r0, rb), rows)
                xc_ref[dst, :] = _dot(onehot, h_ref[span, :]).astype(BF16)
                yc_ref[dst, :] = jnp.zeros((rows, D_MODEL), F32)

            for_chunks(n_chunks, gather)

    for s in range(n_sub):
        def swiglu(r0, rows, s=s):
            src = pl.ds(pl.multiple_of(s * sub + r0, rb), rows)
            xc = xc_ref[src, :]
            a = _dot(xc, w1_ref[...])
            t = (a * jax.nn.sigmoid(a) * _dot(xc, w3_ref[...])).astype(BF16)
            yc_ref[src, :] += _dot(t, w2_ref[...])

        for_chunks(nch_ref[s], swiglu)

    @pl.when(f == pl.num_programs(2) - 1)
    def _():
        lane = lax.broadcasted_iota(jnp.int32, (1, LANES), 1)
        for s in range(n_sub):
            span = slice(s * sub, (s + 1) * sub)
            g_col = jnp.sum(jnp.where(lane == e, gate_ref[span, :], 0.0), axis=-1, keepdims=True)
            r_col = jnp.sum(jnp.where(lane == e, rank_ref[span, :], 0.0), axis=-1, keepdims=True)
            key = jnp.where(g_col > 0.0, r_col, -1.0)

            def scatter(r0, rows, s=s, span=span, key=key, g_col=g_col):
                slot = (r0 + lax.broadcasted_iota(jnp.int32, (1, rows), 1)).astype(F32)
                onehot = jnp.where(key == slot, 1.0, 0.0).astype(BF16)
                src = pl.ds(pl.multiple_of(s * sub + r0, rb), rows)
                o_ref[span, :] += g_col * _dot(onehot, yc_ref[src, :].astype(BF16))

            for_chunks(nch_ref[s], scatter)


def _moe_call(h, x, gate, w1, w3, w2, *, tm, tf):
    n = h.shape[0]
    n_sub = tm // MOE_SUB
    once = pl.Buffered(1)
    rows = lambda width, mode=None: pl.BlockSpec((tm, width), lambda i, e, f: (i, 0), pipeline_mode=mode)
    return pl.pallas_call(
        functools.partial(_moe_kernel, n_sub=n_sub),
        grid=(n // tm, N_EXPERTS, D_FF // tf),
        in_specs=[rows(D_MODEL, once), rows(D_MODEL, once), rows(LANES),
                  pl.BlockSpec((None, D_MODEL, tf), lambda i, e, f: (e, 0, f)),
                  pl.BlockSpec((None, D_MODEL, tf), lambda i, e, f: (e, 0, f)),
                  pl.BlockSpec((None, tf, D_MODEL), lambda i, e, f: (e, f, 0))],
        out_specs=rows(D_MODEL),
        out_shape=jax.ShapeDtypeStruct((n, D_MODEL), F32),
        scratch_shapes=[pltpu.VMEM((tm, D_MODEL), BF16), pltpu.VMEM((tm, D_MODEL), F32), pltpu.VMEM((tm, LANES), F32),
                        pltpu.VMEM((N_EXPERTS, tm), F32), pltpu.VMEM((N_EXPERTS, tm), F32),
                        pltpu.SMEM((n_sub,), jnp.int32)],
        compiler_params=pltpu.CompilerParams(dimension_semantics=("arbitrary", "arbitrary", "arbitrary"),
                                             vmem_limit_bytes=VMEM_LIMIT),
        name="moe_swiglu",
    )(h, x, gate, w1, w3, w2)


def _rope_tables(pos):
    half = ROPE_DIM // 2
    inv_freq = ROPE_THETA ** (-jnp.arange(half, dtype=F32) / half)
    ang = pos.astype(F32)[..., None] * inv_freq
    cos, sin = jnp.cos(ang), jnp.sin(ang)
    n = pos.shape[0]
    rest = HEAD_DIM - ROPE_DIM
    cos64 = jnp.concatenate([cos, cos, jnp.ones((n, rest), F32)], axis=-1)
    sa64 = jnp.concatenate([-sin, jnp.zeros((n, rest + half), F32)], axis=-1)
    sb64 = jnp.concatenate([jnp.zeros((n, half), F32), sin, jnp.zeros((n, rest), F32)], axis=-1)
    return tuple(jnp.tile(t, (1, 2)) for t in (cos64, sa64, sb64))


def _block_diag(blocks):
    n = len(blocks)
    rows = []
    for i, blk in enumerate(blocks):
        rows.append(jnp.concatenate([blk if j == i else jnp.zeros((blk.shape[0], blocks[j].shape[1]), blk.dtype)
                                     for j in range(n)], axis=1))
    return jnp.concatenate(rows, axis=0)


def _layout_w_in(w):
    n_gate = N_BRANCH * N_HEADS
    g0 = GROUP_WIDTH + 6 * HEAD_DIM
    pad = jnp.zeros((w.shape[0], LANES - n_gate), w.dtype)
    return jnp.concatenate([w[:, :g0], w[:, g0:g0 + n_gate], pad, w[:, g0 + n_gate:]], axis=1)


def _layout_compress(pe, w1, w2):
    half = CMP_LEN // 2
    pe2 = jnp.transpose(pe, (1, 0, 2)).reshape(2, half * LANES)
    w1r = w1.reshape(2, 2, half, HEAD_DIM, CMP_HIDDEN)
    zeros = jnp.zeros((half, HEAD_DIM, CMP_HIDDEN), w1.dtype)

    def big(part):
        k_rows = jnp.concatenate([w1r[0, part], zeros], axis=-1)
        v_rows = jnp.concatenate([zeros, w1r[1, part]], axis=-1)
        return jnp.concatenate([k_rows, v_rows], axis=1).reshape(half * LANES, 2 * CMP_HIDDEN)

    return pe2, big(0), big(1), _block_diag([w2[0], w2[1]])


def _overlap_matrix(seq, ncp, nbp):
    cs = jnp.arange(ncp) * CMP_STRIDE
    bs = jnp.arange(nbp) * SEL_BLOCK
    ov = jnp.clip(jnp.minimum(cs[:, None] + CMP_LEN, bs[None, :] + SEL_BLOCK)
                  - jnp.maximum(cs[:, None], bs[None, :]), 0, None).astype(F32) / CMP_LEN
    keep = (jnp.arange(ncp)[:, None] < seq // CMP_STRIDE - 1) & (jnp.arange(nbp)[None, :] < seq // SEL_BLOCK)
    return jnp.where(keep, ov, 0.0)


def kernel(x, attn_norm_g, w_in, q_norm_g, k_norm_g, cmp_pe, cmp_w1, cmp_w2, sgu_norm_g, sgu_w, sgu_b, pool_w,
           pool_scale, conv_w, mix_out_norm_g, w_out, ffn_norm_g, ffn_w1, ffn_w3, ffn_w2, router_w, expert_w1,
           expert_w3, expert_w2):
    b, seq, _ = x.shape
    n = b * seq
    depth = w_in.shape[0]
    tm = 512
    ncp = seq // CMP_STRIDE
    nbp = max(LANES, seq // SEL_BLOCK)

    cos, sa, sb = _rope_tables(jnp.arange(seq))
    cos_c, sa_c, sb_c = _rope_tables(jnp.arange(ncp) * CMP_STRIDE + CMP_LEN - 1)
    bd = _block_diag([jnp.full((HEAD_DIM, HEAD_DIM), 1.0 / HEAD_DIM, F32)] * 2).astype(BF16)
    overlap_t = _overlap_matrix(seq, ncp, nbp).T.astype(BF16)
    row = lambda v: v.reshape(1, -1)
    two = lambda v: jnp.tile(v, 2).reshape(1, LANES)

    xf = x.reshape(n, D_MODEL)
    for layer in range(depth):
        kg = k_norm_g[layer]
        sgub = jnp.repeat(sgu_b[layer].T, HEAD_DIM, axis=1)
        qt, kvc, kvs, vts, kvw, vtw, gatet, ybcd = _proj_call(
            xf, row(attn_norm_g[layer]), _layout_w_in(w_in[layer]).astype(BF16), bd, cos, sa, sb,
            two(q_norm_g[layer]), jnp.stack([jnp.tile(kg[1], 2), jnp.tile(kg[2], 2)]),
            row(sgu_norm_g[layer]), sgu_w[layer].reshape(N_HEADS * SGU_CHUNK, SGU_CHUNK), sgub,
            _block_diag([pool_w[layer, i] for i in range(len(POOL_WINDOWS))]).astype(BF16),
            row(pool_scale[layer]), conv_w[layer], mix_out_norm_g[layer].reshape(4, GROUP_WIDTH)[1:],
            seq=seq, tm=tm)
        pe2, w1a, w1b, w2c = _layout_compress(cmp_pe[layer], cmp_w1[layer], cmp_w2[layer])
        kvcc, vtc = _compress_call(kvc.reshape(b, ncp, CMP_STRIDE * LANES), pe2, w1a.astype(BF16), w1b.astype(BF16),
                                   w2c.astype(BF16), two(kg[0]), bd, cos_c, sa_c, sb_c)
        ya = _attn_call(qt, gatet, kvcc, vtc, kvs, vts, kvw, vtw, overlap_t,
                        row(mix_out_norm_g[layer, :GROUP_WIDTH]), seq=seq, tq=256, tk=512)
        i = layer // 2
        if layer % 2 == 0:
            x1, hn = _outproj_call(xf, ya, ybcd, w_out[layer].astype(BF16), row(ffn_norm_g[layer]), None, tm=tm)
            xf = _ffn_call(hn, x1, ffn_w1[i].astype(BF16), ffn_w3[i].astype(BF16), ffn_w2[i].astype(BF16),
                           tm=1024, tf=512)
        else:
            r = jnp.pad(router_w[i], ((0, 0), (0, LANES - N_EXPERTS)))
            r_hi = r.astype(BF16)
            rw = jnp.stack([r_hi, (r - r_hi.astype(F32)).astype(BF16)])
            x1, hn, gate = _outproj_call(xf, ya, ybcd, w_out[layer].astype(BF16), row(ffn_norm_g[layer]), rw, tm=tm)
            xf = _moe_call(hn, x1, gate, expert_w1[i].astype(BF16), expert_w3[i].astype(BF16),
                           expert_w2[i].astype(BF16), tm=2 * MOE_SUB, tf=512)
    return xf.reshape(b, seq, D_MODEL)
```

```python
import functools

import jax
import jax.numpy as jnp
from jax import lax
from jax.experimental import pallas as pl
from jax.experimental.pallas import tpu as pltpu

D_MODEL = 1024
HEAD_DIM = 64
N_HEADS = 4
GROUP_WIDTH = 256
CMP_LEN = 32
CMP_STRIDE = 16
CMP_HIDDEN = 128
SEL_BLOCK = 64
SEL_TOPK = 16
WINDOW = 512
N_BRANCH = 3
ROPE_THETA = 500000.0
ROPE_DIM = 16
SGU_CHUNK = 128
POOL_WINDOWS = (2, 4, 8, 16)
CONV_WIDTH = 3
D_FF = 3584
N_EXPERTS = 8
EPS = 1e-6
NEG_INF = -1e30
Q_SCALE = HEAD_DIM ** -0.5 * 1.4426950408889634
GATE_ROWS = 16

LANES = 128
MOE_ROWS = 128
FF_TILE = 896
MOE_SUB = 1024
HIST = 16
D_IN_PAD = 2304
VMEM_LIMIT = 52 * 1024 * 1024

F32 = jnp.float32
BF16 = jnp.bfloat16


def _dot(a, b):
    return jnp.dot(a, b, preferred_element_type=F32)


def _split(a):
    hi = a.astype(BF16)
    lo = (a - hi.astype(F32)).astype(BF16)
    return hi, lo


def _rms(y, g):
    return y * lax.rsqrt(jnp.mean(y * y, axis=-1, keepdims=True) + EPS) * g


def _head_norm_rope(c, g, cos, sa, sb, bd):
    hi, lo = _split(c * c)
    msq = _dot(hi, bd) + _dot(lo, bd)
    cn = c * lax.rsqrt(msq + EPS) * g
    return cn * cos + pltpu.roll(cn, LANES - ROPE_DIM // 2, 1) * sa + pltpu.roll(cn, ROPE_DIM // 2, 1) * sb


def _proj_kernel(x_ref, g_ref, w_ref, bd_ref, cos_ref, sa_ref, sb_ref, qg_ref, kg_ref,
                 sgug_ref, sguw_ref, sgub_ref, poolw_ref, pools_ref, convw_ref, outg_ref,
                 qt_out, kvc_out, kvs_out, vts_out, kvw_out, vtw_out, gatet_out, y_out, pext, zext,
                 *, tm, tiles_per_seq):
    i = pl.program_id(0)
    x = x_ref[...]
    hn = _rms(x, g_ref[...]).astype(BF16)
    bd = bd_ref[...]
    cos, sa, sb = cos_ref[...], sa_ref[...], sb_ref[...]
    lane = lax.broadcasted_iota(jnp.int32, (1, LANES), 1)
    first_half = lane < HEAD_DIM

    def proj(c0, c1):
        return _dot(hn, w_ref[:, c0:c1])

    pq = proj(0, 256)
    for c in range(2):
        qc = _head_norm_rope(pq[:, c * LANES:(c + 1) * LANES], qg_ref[...], cos, sa, sb, bd) * Q_SCALE
        qt_out[(2 * c) * LANES:(2 * c + 1) * LANES, :] = jnp.where(first_half, qc, 0.0).T.astype(BF16)
        qt_out[(2 * c + 1) * LANES:(2 * c + 2) * LANES, :] = jnp.where(
            first_half, pltpu.roll(qc, HEAD_DIM, 1), 0.0).T.astype(BF16)

    pk = proj(256, 512)
    kvc_out[...] = pk[:, 0:LANES]
    ks = pk[:, LANES:2 * LANES]
    kvs_out[...] = jnp.where(first_half, _head_norm_rope(ks, kg_ref[0:1, :], cos, sa, sb, bd), ks).astype(BF16)
    vts_out[...] = jnp.where(first_half, 1.0, ks).T.astype(BF16)
    pk = proj(512, 768)
    kw = pk[:, 0:LANES]
    kvw_out[...] = jnp.where(first_half, _head_norm_rope(kw, kg_ref[1:2, :], cos, sa, sb, bd), kw).astype(BF16)
    vtw_out[...] = jnp.where(first_half, 1.0, kw).T.astype(BF16)
    gatet_out[...] = jax.nn.sigmoid(pk[:, LANES:2 * LANES]).T[0:GATE_ROWS, :]

    outg = outg_ref[...]
    lane2 = lax.broadcasted_iota(jnp.int32, (1, GROUP_WIDTH), 1)
    grp = lane2 // HEAD_DIM

    u = proj(768, 1024)
    vn = _rms(proj(1024, 1280), sgug_ref[...]).astype(BF16)
    r_i = lax.broadcasted_iota(jnp.int32, (N_HEADS * SGU_CHUNK, SGU_CHUNK), 0) & (SGU_CHUNK - 1)
    c_i = lax.broadcasted_iota(jnp.int32, (N_HEADS * SGU_CHUNK, SGU_CHUNK), 1)
    wm = jnp.where(c_i <= r_i, sguw_ref[...], 0.0).astype(BF16)
    for c in range(tm // SGU_CHUNK):
        rows = slice(c * SGU_CHUNK, (c + 1) * SGU_CHUNK)
        r = _dot(wm, vn[rows, :])
        mixed = sgub_ref[...]
        for gi in range(N_HEADS):
            mixed = mixed + jnp.where(grp == gi, r[gi * SGU_CHUNK:(gi + 1) * SGU_CHUNK, :], 0.0)
        y_out[rows, 0:GROUP_WIDTH] = _rms(u[rows, :] * mixed, outg[0:1, :]).astype(BF16)

    first = (i % tiles_per_seq) == 0

    @pl.when(first)
    def _():
        pext[0:HIST, :] = jnp.zeros((HIST, GROUP_WIDTH), F32)
        zext[0:HIST, :] = jnp.zeros((HIST, GROUP_WIDTH), F32)

    @pl.when(jnp.logical_not(first))
    def _():
        pext[0:HIST, :] = pext[tm:tm + HIST, :]
        zext[0:HIST, :] = zext[tm:tm + HIST, :]

    pin = proj(1280, 1536)
    pext[HIST:, :] = pin
    pos = (i % tiles_per_seq) * tm + lax.broadcasted_iota(jnp.int32, (tm, 1), 0)
    acc = pin
    sums = []
    for k in range(1, POOL_WINDOWS[-1]):
        acc = acc + pext[HIST - k:HIST - k + tm, :]
        if k + 1 in POOL_WINDOWS:
            sums.append(acc)
    wsum = jnp.where(grp == 0, sums[0], jnp.where(grp == 1, sums[1], jnp.where(grp == 2, sums[2], sums[3])))
    win = jnp.where(grp == 0, POOL_WINDOWS[0], jnp.where(grp == 1, POOL_WINDOWS[1],
                                                         jnp.where(grp == 2, POOL_WINDOWS[2], POOL_WINDOWS[3])))
    cnt = jnp.minimum(pos + 1, win).astype(F32)
    dlt = (wsum / cnt - pin).astype(BF16)
    yc = _dot(dlt, poolw_ref[...]) * pools_ref[...]
    y_out[:, GROUP_WIDTH:2 * GROUP_WIDTH] = _rms(yc, outg[1:2, :]).astype(BF16)

    bg = proj(1536, 1792)
    z = proj(1792, 2048) * proj(2048, 2304)
    zext[HIST:, :] = z
    cw = convw_ref[...]
    conv = cw[2:3, :] * z + cw[1:2, :] * zext[HIST - 1:HIST - 1 + tm, :] + cw[0:1, :] * zext[HIST - 2:HIST - 2 + tm, :]
    y_out[:, 2 * GROUP_WIDTH:3 * GROUP_WIDTH] = _rms(bg * conv, outg[2:3, :]).astype(BF16)


def _proj_call(x, g, w, bd, cos, sa, sb, qg, kg, sgug, sguw, sgub, poolw, pools, convw, outg, *, seq, tm):
    n = x.shape[0]
    tps = seq // tm
    const = lambda shape: pl.BlockSpec(shape, lambda i: (0, 0))
    rows = lambda width: pl.BlockSpec((tm, width), lambda i: (i, 0))
    cols = lambda height: pl.BlockSpec((height, tm), lambda i: (0, i))
    tab = pl.BlockSpec((tm, LANES), lambda i: (i % tps, 0))
    return pl.pallas_call(
        functools.partial(_proj_kernel, tm=tm, tiles_per_seq=tps),
        grid=(n // tm,),
        in_specs=[rows(D_MODEL), const((1, D_MODEL)), const((D_MODEL, D_IN_PAD)), const((LANES, LANES)),
                  tab, tab, tab, const((1, LANES)), const((2, LANES)),
                  const((1, GROUP_WIDTH)), const((N_HEADS * SGU_CHUNK, SGU_CHUNK)), const((SGU_CHUNK, GROUP_WIDTH)),
                  const((GROUP_WIDTH, GROUP_WIDTH)), const((1, GROUP_WIDTH)), const((CONV_WIDTH, GROUP_WIDTH)),
                  const((3, GROUP_WIDTH))],
        out_specs=[cols(N_HEADS * LANES), rows(LANES), rows(LANES), cols(LANES), rows(LANES), cols(LANES),
                   cols(GATE_ROWS), rows(3 * GROUP_WIDTH)],
        out_shape=[jax.ShapeDtypeStruct((N_HEADS * LANES, n), BF16), jax.ShapeDtypeStruct((n, LANES), F32),
                   jax.ShapeDtypeStruct((n, LANES), BF16), jax.ShapeDtypeStruct((LANES, n), BF16),
                   jax.ShapeDtypeStruct((n, LANES), BF16), jax.ShapeDtypeStruct((LANES, n), BF16),
                   jax.ShapeDtypeStruct((GATE_ROWS, n), F32), jax.ShapeDtypeStruct((n, 3 * GROUP_WIDTH), BF16)],
        scratch_shapes=[pltpu.VMEM((tm + HIST, GROUP_WIDTH), F32), pltpu.VMEM((tm + HIST, GROUP_WIDTH), F32)],
        compiler_params=pltpu.CompilerParams(dimension_semantics=("arbitrary",), vmem_limit_bytes=VMEM_LIMIT),
        name="proj_mixers",
    )(x, g, w, bd, cos, sa, sb, qg, kg, sgug, sguw, sgub, poolw, pools, convw, outg)


def _compress_kernel(t_ref, pe_ref, w1a_ref, w1b_ref, w2_ref, kg_ref, bd_ref, cos_ref, sa_ref, sb_ref, kv_ref, vt_ref):
    t = t_ref[0]
    ncp = t.shape[0]
    a = _dot((t + pe_ref[0:1, :]).astype(BF16), w1a_ref[...])
    b = _dot((t + pe_ref[1:2, :]).astype(BF16), w1b_ref[...])
    hid = jax.nn.gelu(a + pltpu.roll(b, ncp - 1, 0), approximate=True)
    kv = _dot(hid.astype(BF16), w2_ref[...])
    first_half = lax.broadcasted_iota(jnp.int32, (1, LANES), 1) < HEAD_DIM
    kn = _head_norm_rope(kv, kg_ref[...], cos_ref[...], sa_ref[...], sb_ref[...], bd_ref[...])
    kv_ref[0] = jnp.where(first_half, kn, kv).astype(BF16)
    vt_ref[0] = jnp.where(first_half, 1.0, kv).T.astype(BF16)


def _compress_call(t2, pe, w1a, w1b, w2, kg, bd, cos, sa, sb):
    b, ncp, width = t2.shape
    const = lambda shape: pl.BlockSpec(shape, lambda i: (0,) * len(shape))
    return pl.pallas_call(
        _compress_kernel,
        grid=(b,),
        in_specs=[pl.BlockSpec((1, ncp, width), lambda i: (i, 0, 0)), const(pe.shape), const(w1a.shape),
                  const(w1b.shape), const(w2.shape), const(kg.shape), const(bd.shape),
                  const(cos.shape), const(sa.shape), const(sb.shape)],
        out_specs=[pl.BlockSpec((1, ncp, LANES), lambda i: (i, 0, 0)), pl.BlockSpec((1, LANES, ncp), lambda i: (i, 0, 0))],
        out_shape=[jax.ShapeDtypeStruct((b, ncp, LANES), BF16), jax.ShapeDtypeStruct((b, LANES, ncp), BF16)],
        compiler_params=pltpu.CompilerParams(dimension_semantics=("arbitrary",), vmem_limit_bytes=VMEM_LIMIT),
        name="nsa_compress",
    )(t2, pe, w1a, w1b, w2, kg, bd, cos, sa, sb)


def _attn_kernel(qt_ref, gatet_ref, kvc_ref, vtc_ref, kvs_ref, vts_ref, kvw_ref, vtw_ref, ovt_ref, og_ref, o_ref,
                 bias_ref, sa_ref, sb_ref, m_ref, acc_ref, *, tq, tk, n_sel):
    s0 = pl.program_id(1) * tq
    nc = N_HEADS * tq
    qs = jnp.concatenate([qt_ref[h * LANES:(h + 1) * LANES, :] for h in range(N_HEADS)], axis=1)
    t_col = s0 + (lax.broadcasted_iota(jnp.int32, (1, nc), 1) & (tq - 1))
    t_q = s0 + lax.broadcasted_iota(jnp.int32, (1, tq), 1)

    kvc = kvc_ref[0]
    ncp = kvc.shape[0]
    cmp_end = lax.broadcasted_iota(jnp.int32, (ncp, 1), 0) * CMP_STRIDE + (CMP_LEN - 1)
    valid = cmp_end <= t_col
    sc = jnp.where(valid, _dot(kvc, qs), NEG_INF)
    e = jnp.exp2(sc - jnp.max(sc, axis=0, keepdims=True))
    p = e * jnp.where(t_col >= CMP_LEN - 1, 1.0 / jnp.sum(e, axis=0, keepdims=True), 0.0)
    o_c = _dot(vtc_ref[0], p.astype(BF16))[HEAD_DIM:, :]
    psum = p[:, 0:tq] + p[:, tq:2 * tq] + p[:, 2 * tq:3 * tq] + p[:, 3 * tq:4 * tq]
    hi, lo = _split(psum)
    imp = _dot(ovt_ref[...], hi) + _dot(ovt_ref[...], lo)

    nbp = imp.shape[0]
    blk = lax.broadcasted_iota(jnp.int32, (nbp, 1), 0)
    blk_f = blk.astype(F32)
    cur = t_q // SEL_BLOCK
    valid_b = blk <= cur
    forced = (blk == 0) | (blk == cur) | (blk == cur - 1)
    score = jnp.where(valid_b & forced, 1e4, jnp.where(valid_b, imp, -1.0))
    sel = jnp.zeros((nbp, tq), F32)
    for _ in range(n_sel):
        top = jnp.max(score, axis=0, keepdims=True)
        idx = jnp.min(jnp.where(score == top, blk_f, float(nbp)), axis=0, keepdims=True)
        hit = blk_f == idx
        sel = jnp.where(hit & (top >= 0.0), 1.0, sel)
        score = jnp.where(hit, -2.0, score)
    bias = jnp.where(sel > 0.5, 0.0, NEG_INF)
    bias_ref[...] = jnp.concatenate([bias] * N_HEADS, axis=1)

    def scores(j):
        return _dot(kvs_ref[pl.ds(pl.multiple_of(j * tk, tk), tk), :], qs)

    def consume(s_ref, j, causal):
        k0 = pl.multiple_of(j * tk, tk)
        b0 = j * (tk // SEL_BLOCK)
        s = jnp.concatenate([s_ref[b * SEL_BLOCK:(b + 1) * SEL_BLOCK, :] + bias_ref[pl.ds(b0 + b, 1), :]
                             for b in range(tk // SEL_BLOCK)], axis=0)
        if causal:
            kpos = k0 + lax.broadcasted_iota(jnp.int32, (tk, 1), 0)
            s = jnp.where(kpos <= t_col, s, NEG_INF)
        m = m_ref[...]
        m_new = jnp.maximum(m, jnp.max(s, axis=0, keepdims=True))
        pe = jnp.exp2(s - m_new).astype(BF16)
        acc_ref[...] = jnp.exp2(m - m_new) * acc_ref[...] + _dot(vts_ref[:, pl.ds(k0, tk)], pe)
        m_ref[...] = m_new

    n_full = (s0 + tq + tk - 1) // tk - 1
    m_ref[...] = jnp.full((1, nc), NEG_INF, F32)
    acc_ref[...] = jnp.zeros((LANES, nc), F32)
    sa_ref[...] = scores(0)

    def pair(i, carry):
        j = 2 * i
        sb_ref[...] = scores(j + 1)
        consume(sa_ref, j, False)
        sa_ref[...] = scores(jnp.minimum(j + 2, n_full))
        consume(sb_ref, j + 1, False)
        return carry

    lax.fori_loop(0, n_full // 2, pair, 0)

    @pl.when(n_full % 2 == 1)
    def _():
        sb_ref[...] = scores(n_full)
        consume(sa_ref, n_full - 1, False)
        consume(sb_ref, n_full, True)

    @pl.when(n_full % 2 == 0)
    def _():
        consume(sa_ref, n_full, True)

    acc_s = acc_ref[...]
    o_s = acc_s[HEAD_DIM:, :] / acc_s[0:1, :]

    wk = WINDOW + tq
    w0 = pl.multiple_of(jnp.maximum(s0 - WINDOW, 0), tq)
    kp = w0 + lax.broadcasted_iota(jnp.int32, (wk, 1), 0)
    ok = (kp <= t_col) & (kp > t_col - WINDOW)
    sw = jnp.where(ok, _dot(kvw_ref[pl.ds(w0, wk), :], qs), NEG_INF)
    pw = jnp.exp2(sw - jnp.max(sw, axis=0, keepdims=True)).astype(BF16)
    acc_w = _dot(vtw_ref[:, pl.ds(w0, wk)], pw)
    o_w = acc_w[HEAD_DIM:, :] / acc_w[0:1, :]

    g = gatet_ref[...]
    heads = []
    for h in range(N_HEADS):
        c = slice(h * tq, (h + 1) * tq)
        r = h * N_BRANCH
        heads.append(g[r:r + 1, :] * o_c[:, c] + g[r + 1:r + 2, :] * o_s[:, c] + g[r + 2:r + 3, :] * o_w[:, c])
    ya = jnp.concatenate(heads, axis=0).T
    o_ref[...] = _rms(ya, og_ref[...]).astype(BF16)


def _attn_call(qt, gatet, kvc, vtc, kvs, vts, kvw, vtw, ovt, og, *, seq, tq, tk):
    n = qt.shape[1]
    b = n // seq
    nq = seq // tq
    ncp = kvc.shape[1]
    nbp = ovt.shape[0]
    n_sel = min(SEL_TOPK, seq // SEL_BLOCK)
    cols = lambda height: pl.BlockSpec((height, tq), lambda bi, qi: (0, bi * nq + qi))
    per_b = lambda shape: pl.BlockSpec((1,) + shape, lambda bi, qi: (bi, 0, 0))
    const = lambda shape: pl.BlockSpec(shape, lambda bi, qi: (0, 0))
    seq_rows = pl.BlockSpec((seq, LANES), lambda bi, qi: (bi, 0))
    seq_cols = pl.BlockSpec((LANES, seq), lambda bi, qi: (0, bi))
    return pl.pallas_call(
        functools.partial(_attn_kernel, tq=tq, tk=tk, n_sel=n_sel),
        grid=(b, nq),
        in_specs=[cols(N_HEADS * LANES), cols(GATE_ROWS), per_b((ncp, LANES)), per_b((LANES, ncp)),
                  seq_rows, seq_cols, seq_rows, seq_cols, const(ovt.shape), const((1, GROUP_WIDTH))],
        out_specs=pl.BlockSpec((tq, GROUP_WIDTH), lambda bi, qi: (bi * nq + qi, 0)),
        out_shape=jax.ShapeDtypeStruct((n, GROUP_WIDTH), BF16),
        scratch_shapes=[pltpu.VMEM((nbp, N_HEADS * tq), F32), pltpu.VMEM((tk, N_HEADS * tq), F32),
                        pltpu.VMEM((tk, N_HEADS * tq), F32), pltpu.VMEM((1, N_HEADS * tq), F32),
                        pltpu.VMEM((LANES, N_HEADS * tq), F32)],
        compiler_params=pltpu.CompilerParams(dimension_semantics=("arbitrary", "arbitrary"),
                                             vmem_limit_bytes=VMEM_LIMIT),
        name="nsa_attention",
    )(qt, gatet, kvc, vtc, kvs, vts, kvw, vtw, ovt, og)


def _outproj_kernel(x_ref, ya_ref, yb_ref, w_ref, g_ref, *rest, moe):
    if moe:
        rw_ref, x1_ref, hn_ref, gate_ref = rest
    else:
        x1_ref, hn_ref = rest
    x1 = x_ref[...] + _dot(ya_ref[...], w_ref[0:GROUP_WIDTH, :]) + _dot(yb_ref[...], w_ref[GROUP_WIDTH:, :])
    x1_ref[...] = x1
    hn = _rms(x1, g_ref[...])
    hn_ref[...] = hn.astype(BF16)
    if moe:
        hi, lo = _split(hn)
        logits = _dot(hi, rw_ref[0]) + _dot(lo, rw_ref[0]) + _dot(hi, rw_ref[1])
        lane = lax.broadcasted_iota(jnp.int32, (1, LANES), 1)
        lane_f = lane.astype(F32)
        logits = jnp.where(lane < N_EXPERTS, logits, NEG_INF)
        m1 = jnp.max(logits, axis=-1, keepdims=True)
        i1 = jnp.min(jnp.where(logits == m1, lane_f, float(LANES)), axis=-1, keepdims=True)
        rest_l = jnp.where(lane_f == i1, 2 * NEG_INF, logits)
        m2 = jnp.max(rest_l, axis=-1, keepdims=True)
        i2 = jnp.min(jnp.where(rest_l == m2, lane_f, float(LANES)), axis=-1, keepdims=True)
        e2 = jnp.exp(m2 - m1)
        den = 1.0 + e2
        gate_ref[...] = jnp.where(lane_f == i1, 1.0 / den, jnp.where(lane_f == i2, e2 / den, 0.0))


def _outproj_call(x, ya, yb, w, g, rw, *, tm):
    n = x.shape[0]
    moe = rw is not None
    rows = lambda width: pl.BlockSpec((tm, width), lambda i: (i, 0))
    const = lambda shape: pl.BlockSpec(shape, lambda i: (0,) * len(shape))
    in_specs = [rows(D_MODEL), rows(GROUP_WIDTH), rows(3 * GROUP_WIDTH), const((D_MODEL, D_MODEL)), const((1, D_MODEL))]
    out_specs = [rows(D_MODEL), rows(D_MODEL)]
    out_shape = [jax.ShapeDtypeStruct((n, D_MODEL), F32), jax.ShapeDtypeStruct((n, D_MODEL), BF16)]
    args = [x, ya, yb, w, g]
    if moe:
        in_specs.append(const(rw.shape))
        out_specs.append(rows(LANES))
        out_shape.append(jax.ShapeDtypeStruct((n, LANES), F32))
        args.append(rw)
    return pl.pallas_call(
        functools.partial(_outproj_kernel, moe=moe),
        grid=(n // tm,),
        in_specs=in_specs, out_specs=out_specs, out_shape=out_shape,
        compiler_params=pltpu.CompilerParams(dimension_semantics=("arbitrary",), vmem_limit_bytes=VMEM_LIMIT),
        name="outproj_router" if moe else "outproj",
    )(*args)


def _ffn_kernel(h_ref, x_ref, w1_ref, w3_ref, w2_ref, o_ref, acc_ref):
    f = pl.program_id(1)

    @pl.when(f == 0)
    def _():
        acc_ref[...] = jnp.zeros_like(acc_ref)

    h = h_ref[...]
    a = _dot(h, w1_ref[...])
    t = (a * jax.nn.sigmoid(a) * _dot(h, w3_ref[...])).astype(BF16)
    acc_ref[...] += _dot(t, w2_ref[...])

    @pl.when(f == pl.num_programs(1) - 1)
    def _():
        o_ref[...] = x_ref[...] + acc_ref[...]


def _ffn_call(h, x, w1, w3, w2, *, tm, tf):
    n = h.shape[0]
    rows = pl.BlockSpec((tm, D_MODEL), lambda i, f: (i, 0))
    up = pl.BlockSpec((None, D_MODEL, tf), lambda i, f: (f, 0, 0))
    return pl.pallas_call(
        _ffn_kernel,
        grid=(n // tm, w1.shape[0]),
        in_specs=[rows, rows, up, up, pl.BlockSpec((tf, D_MODEL), lambda i, f: (f, 0))],
        out_specs=rows,
        out_shape=jax.ShapeDtypeStruct((n, D_MODEL), F32),
        scratch_shapes=[pltpu.VMEM((tm, D_MODEL), F32)],
        compiler_params=pltpu.CompilerParams(dimension_semantics=("arbitrary", "arbitrary"),
                                             vmem_limit_bytes=VMEM_LIMIT),
        name="ffn_swiglu",
    )(h, x, w1, w3, w2)


def _moe_kernel(h_ref, x_ref, gate_ref, w1_ref, w3_ref, w2_ref, o_ref,
                xc_ref, yc_ref, rank_ref, gate_t_ref, rank_t_ref, nch_ref, *, n_sub):
    e = pl.program_id(1)
    f = pl.program_id(2)
    rb = MOE_ROWS
    sub = MOE_SUB

    def for_chunks(n_chunks, fn):
        def pair(c, carry):
            fn(c * 2 * rb, 2 * rb)
            return carry

        lax.fori_loop(0, n_chunks // 2, pair, 0)

        @pl.when(n_chunks % 2 == 1)
        def _():
            fn((n_chunks - 1) * rb, rb)

    @pl.when((e == 0) & (f == 0))
    def _():
        o_ref[...] = x_ref[...]
        before = jnp.where(lax.broadcasted_iota(jnp.int32, (sub, sub), 1)
                           < lax.broadcasted_iota(jnp.int32, (sub, sub), 0), 1.0, 0.0).astype(BF16)
        for s in range(n_sub):
            span = slice(s * sub, (s + 1) * sub)
            gate = gate_ref[span, :]
            rank = _dot(before, jnp.where(gate > 0.0, 1.0, 0.0).astype(BF16))
            rank_ref[span, :] = rank
            gate_t_ref[:, span] = gate.T[0:N_EXPERTS]
            rank_t_ref[:, span] = rank.T[0:N_EXPERTS]

    @pl.when(f == 0)
    def _():
        for s in range(n_sub):
            span = slice(s * sub, (s + 1) * sub)
            g_row = gate_t_ref[pl.ds(e, 1), span]
            n_rows = jnp.sum(jnp.where(g_row > 0.0, 1.0, 0.0)).astype(jnp.int32)
            n_chunks = (n_rows + rb - 1) // rb
            nch_ref[s] = n_chunks
            key = jnp.where(g_row > 0.0, rank_t_ref[pl.ds(e, 1), span], -1.0)

            def gather(r0, rows, s=s, span=span, key=key):
                slot = (r0 + lax.broadcasted_iota(jnp.int32, (rows, 1), 0)).astype(F32)
                onehot = jnp.where(key == slot, 1.0, 0.0).astype(BF16)
                dst = pl.ds(pl.multiple_of(s * sub + r0, rb), rows)
                xc_ref[dst, :] = _dot(onehot, h_ref[span, :]).astype(BF16)
                yc_ref[dst, :] = jnp.zeros((rows, D_MODEL), F32)

            for_chunks(n_chunks, gather)

    for s in range(n_sub):
        def swiglu(r0, rows, s=s):
            src = pl.ds(pl.multiple_of(s * sub + r0, rb), rows)
            xc = xc_ref[src, :]
            a = _dot(xc, w1_ref[...])
            t = (a * jax.nn.sigmoid(a) * _dot(xc, w3_ref[...])).astype(BF16)
            yc_ref[src, :] += _dot(t, w2_ref[...])

        for_chunks(nch_ref[s], swiglu)

    @pl.when(f == pl.num_programs(2) - 1)
    def _():
        lane = lax.broadcasted_iota(jnp.int32, (1, LANES), 1)
        for s in range(n_sub):
            span = slice(s * sub, (s + 1) * sub)
            g_col = jnp.sum(jnp.where(lane == e, gate_ref[span, :], 0.0), axis=-1, keepdims=True)
            r_col = jnp.sum(jnp.where(lane == e, rank_ref[span, :], 0.0), axis=-1, keepdims=True)
            key = jnp.where(g_col > 0.0, r_col, -1.0)

            def scatter(r0, rows, s=s, span=span, key=key, g_col=g_col):
                slot = (r0 + lax.broadcasted_iota(jnp.int32, (1, rows), 1)).astype(F32)
                onehot = jnp.where(key == slot, 1.0, 0.0).astype(BF16)
                src = pl.ds(pl.multiple_of(s * sub + r0, rb), rows)
                o_ref[span, :] += g_col * _dot(onehot, yc_ref[src, :].astype(BF16))

            for_chunks(nch_ref[s], scatter)


def _moe_call(h, x, gate, w1, w3, w2, *, tm, tf):
    n = h.shape[0]
    n_sub = tm // MOE_SUB
    once = pl.Buffered(1)
    rows = lambda width, mode=None: pl.BlockSpec((tm, width), lambda i, e, f: (i, 0), pipeline_mode=mode)
    return pl.pallas_call(
        functools.partial(_moe_kernel, n_sub=n_sub),
        grid=(n // tm, N_EXPERTS, w1.shape[1]),
        in_specs=[rows(D_MODEL, once), rows(D_MODEL, once), rows(LANES),
                  pl.BlockSpec((None, None, D_MODEL, tf), lambda i, e, f: (e, f, 0, 0)),
                  pl.BlockSpec((None, None, D_MODEL, tf), lambda i, e, f: (e, f, 0, 0)),
                  pl.BlockSpec((None, tf, D_MODEL), lambda i, e, f: (e, f, 0))],
        out_specs=rows(D_MODEL, once),
        out_shape=jax.ShapeDtypeStruct((n, D_MODEL), F32),
        scratch_shapes=[pltpu.VMEM((tm, D_MODEL), BF16), pltpu.VMEM((tm, D_MODEL), F32), pltpu.VMEM((tm, LANES), F32),
                        pltpu.VMEM((N_EXPERTS, tm), F32), pltpu.VMEM((N_EXPERTS, tm), F32),
                        pltpu.SMEM((n_sub,), jnp.int32)],
        compiler_params=pltpu.CompilerParams(dimension_semantics=("arbitrary", "arbitrary", "arbitrary"),
                                             vmem_limit_bytes=VMEM_LIMIT),
        name="moe_swiglu",
    )(h, x, gate, w1, w3, w2)


def _rope_tables(pos):
    half = ROPE_DIM // 2
    inv_freq = ROPE_THETA ** (-jnp.arange(half, dtype=F32) / half)
    ang = pos.astype(F32)[..., None] * inv_freq
    cos, sin = jnp.cos(ang), jnp.sin(ang)
    n = pos.shape[0]
    rest = HEAD_DIM - ROPE_DIM
    cos64 = jnp.concatenate([cos, cos, jnp.ones((n, rest), F32)], axis=-1)
    sa64 = jnp.concatenate([-sin, jnp.zeros((n, rest + half), F32)], axis=-1)
    sb64 = jnp.concatenate([jnp.zeros((n, half), F32), sin, jnp.zeros((n, rest), F32)], axis=-1)
    return tuple(jnp.tile(t, (1, 2)) for t in (cos64, sa64, sb64))


def _block_diag(blocks):
    n = len(blocks)
    rows = []
    for i, blk in enumerate(blocks):
        rows.append(jnp.concatenate([blk if j == i else jnp.zeros((blk.shape[0], blocks[j].shape[1]), blk.dtype)
                                     for j in range(n)], axis=1))
    return jnp.concatenate(rows, axis=0)


def _layout_w_in(w):
    n_gate = N_BRANCH * N_HEADS
    g0 = GROUP_WIDTH + 6 * HEAD_DIM
    pad = jnp.zeros((w.shape[0], LANES - n_gate), w.dtype)
    return jnp.concatenate([w[:, :g0], w[:, g0:g0 + n_gate], pad, w[:, g0 + n_gate:]], axis=1)


def _layout_compress(pe, w1, w2):
    half = CMP_LEN // 2
    pe2 = jnp.transpose(pe, (1, 0, 2)).reshape(2, half * LANES)
    w1r = w1.reshape(2, 2, half, HEAD_DIM, CMP_HIDDEN)
    zeros = jnp.zeros((half, HEAD_DIM, CMP_HIDDEN), w1.dtype)

    def big(part):
        k_rows = jnp.concatenate([w1r[0, part], zeros], axis=-1)
        v_rows = jnp.concatenate([zeros, w1r[1, part]], axis=-1)
        return jnp.concatenate([k_rows, v_rows], axis=1).reshape(half * LANES, 2 * CMP_HIDDEN)

    return pe2, big(0), big(1), _block_diag([w2[0], w2[1]])


def _overlap_matrix(seq, ncp, nbp):
    cs = jnp.arange(ncp) * CMP_STRIDE
    bs = jnp.arange(nbp) * SEL_BLOCK
    ov = jnp.clip(jnp.minimum(cs[:, None] + CMP_LEN, bs[None, :] + SEL_BLOCK)
                  - jnp.maximum(cs[:, None], bs[None, :]), 0, None).astype(F32) / CMP_LEN
    keep = (jnp.arange(ncp)[:, None] < seq // CMP_STRIDE - 1) & (jnp.arange(nbp)[None, :] < seq // SEL_BLOCK)
    return jnp.where(keep, ov, 0.0)


def kernel(x, attn_norm_g, w_in, q_norm_g, k_norm_g, cmp_pe, cmp_w1, cmp_w2, sgu_norm_g, sgu_w, sgu_b, pool_w,
           pool_scale, conv_w, mix_out_norm_g, w_out, ffn_norm_g, ffn_w1, ffn_w3, ffn_w2, router_w, expert_w1,
           expert_w3, expert_w2):
    b, seq, _ = x.shape
    n = b * seq
    depth = w_in.shape[0]
    tm = 512
    ncp = seq // CMP_STRIDE
    nbp = max(LANES, seq // SEL_BLOCK)

    cos, sa, sb = _rope_tables(jnp.arange(seq))
    cos_c, sa_c, sb_c = _rope_tables(jnp.arange(ncp) * CMP_STRIDE + CMP_LEN - 1)
    bd = _block_diag([jnp.full((HEAD_DIM, HEAD_DIM), 1.0 / HEAD_DIM, F32)] * 2).astype(BF16)
    overlap_t = _overlap_matrix(seq, ncp, nbp).T.astype(BF16)
    row = lambda v: v.reshape(1, -1)
    two = lambda v: jnp.tile(v, 2).reshape(1, LANES)

    xf = x.reshape(n, D_MODEL)
    for layer in range(depth):
        kg = k_norm_g[layer]
        sgub = jnp.repeat(sgu_b[layer].T, HEAD_DIM, axis=1)
        qt, kvc, kvs, vts, kvw, vtw, gatet, ybcd = _proj_call(
            xf, row(attn_norm_g[layer]), _layout_w_in(w_in[layer]).astype(BF16), bd, cos, sa, sb,
            two(q_norm_g[layer]), jnp.stack([jnp.tile(kg[1], 2), jnp.tile(kg[2], 2)]),
            row(sgu_norm_g[layer]), sgu_w[layer].reshape(N_HEADS * SGU_CHUNK, SGU_CHUNK), sgub,
            _block_diag([pool_w[layer, i] for i in range(len(POOL_WINDOWS))]).astype(BF16),
            row(pool_scale[layer]), conv_w[layer], mix_out_norm_g[layer].reshape(4, GROUP_WIDTH)[1:],
            seq=seq, tm=tm)
        pe2, w1a, w1b, w2c = _layout_compress(cmp_pe[layer], cmp_w1[layer], cmp_w2[layer])
        kvcc, vtc = _compress_call(kvc.reshape(b, ncp, CMP_STRIDE * LANES), pe2, w1a.astype(BF16), w1b.astype(BF16),
                                   w2c.astype(BF16), two(kg[0]), bd, cos_c, sa_c, sb_c)
        ya = _attn_call(qt, gatet, kvcc, vtc, kvs, vts, kvw, vtw, overlap_t,
                        row(mix_out_norm_g[layer, :GROUP_WIDTH]), seq=seq, tq=256, tk=512)
        i = layer // 2
        if layer % 2 == 0:
            x1, hn = _outproj_call(xf, ya, ybcd, w_out[layer].astype(BF16), row(ffn_norm_g[layer]), None, tm=tm)
            tf = FF_TILE
            up = lambda w: w.astype(BF16).reshape(D_MODEL, D_FF // tf, tf).swapaxes(0, 1)
            xf = _ffn_call(hn, x1, up(ffn_w1[i]), up(ffn_w3[i]), ffn_w2[i].astype(BF16), tm=1024, tf=tf)
        else:
            r = jnp.pad(router_w[i], ((0, 0), (0, LANES - N_EXPERTS)))
            r_hi = r.astype(BF16)
            rw = jnp.stack([r_hi, (r - r_hi.astype(F32)).astype(BF16)])
            x1, hn, gate = _outproj_call(xf, ya, ybcd, w_out[layer].astype(BF16), row(ffn_norm_g[layer]), rw, tm=tm)
            tf = FF_TILE
            up = lambda w: w.astype(BF16).reshape(N_EXPERTS, D_MODEL, D_FF // tf, tf).transpose(0, 2, 1, 3)
            xf = _moe_call(hn, x1, gate, up(expert_w1[i]), up(expert_w3[i]), expert_w2[i].astype(BF16),
                           tm=2 * MOE_SUB, tf=tf)
    return xf.reshape(b, seq, D_MODEL)
```

```python
import functools

import jax
import jax.numpy as jnp
from jax import lax
from jax.experimental import pallas as pl
from jax.experimental.pallas import tpu as pltpu

D_MODEL = 1024
HEAD_DIM = 64
N_HEADS = 4
GROUP_WIDTH = 256
CMP_LEN = 32
CMP_STRIDE = 16
CMP_HIDDEN = 128
SEL_BLOCK = 64
SEL_TOPK = 16
WINDOW = 512
N_BRANCH = 3
ROPE_THETA = 500000.0
ROPE_DIM = 16
SGU_CHUNK = 128
POOL_WINDOWS = (2, 4, 8, 16)
CONV_WIDTH = 3
D_FF = 3584
N_EXPERTS = 8
EPS = 1e-6
NEG_INF = -1e30
Q_SCALE = HEAD_DIM ** -0.5 * 1.4426950408889634
CMP_PART = 128
N_FORCED = 3
SEL_KEY_TILE = 512
BIAS_PAD = 16
GATE_ROWS = 16

LANES = 128
MOE_ROWS = 128
MOE_FF_TILE = 896
MOE_SUB = 1024
HIST = 16
D_IN_PAD = 2304
TOKEN_TILE = 512
ATTN_Q_TILE = 256
FFN_TOKEN_TILE = 1024
FFN_FF_TILE = 512
VMEM_LIMIT = 52 * 1024 * 1024

F32 = jnp.float32
BF16 = jnp.bfloat16


def _dot(a, b):
    return jnp.dot(a, b, preferred_element_type=F32)


def _split(a):
    hi = a.astype(BF16)
    lo = (a - hi.astype(F32)).astype(BF16)
    return hi, lo


def _rms(y, g):
    return y * lax.rsqrt(jnp.mean(y * y, axis=-1, keepdims=True) + EPS) * g


def _head_norm_rope(c, g, cos, sa, sb, bd):
    hi, lo = _split(c * c)
    msq = _dot(hi, bd) + _dot(lo, bd)
    cn = c * lax.rsqrt(msq + EPS) * g
    return cn * cos + pltpu.roll(cn, LANES - ROPE_DIM // 2, 1) * sa + pltpu.roll(cn, ROPE_DIM // 2, 1) * sb


def _proj_kernel(x_ref, g_ref, w_ref, bd_ref, cos_ref, sa_ref, sb_ref, qg_ref, kg_ref,
                 sgug_ref, sguw_ref, sgub_ref, poolw_ref, pools_ref, convw_ref, outg_ref,
                 qt_out, kvc_out, kvs_out, vts_out, kvw_out, vtw_out, gatet_out, y_out, pext, zext, p_ref,
                 *, tm, tiles_per_seq):
    i = pl.program_id(0)
    x = x_ref[...]
    hn = _rms(x, g_ref[...]).astype(BF16)
    bd = bd_ref[...]
    cos, sa, sb = cos_ref[...], sa_ref[...], sb_ref[...]
    lane = lax.broadcasted_iota(jnp.int32, (1, LANES), 1)
    first_half = lane < HEAD_DIM

    for c0 in range(0, D_IN_PAD, 2 * LANES):
        p_ref[:, c0:c0 + 2 * LANES] = _dot(hn, w_ref[:, c0:c0 + 2 * LANES])

    def proj(c0, c1):
        return p_ref[:, c0:c1]

    pq = proj(0, 256)
    for c in range(2):
        qc = _head_norm_rope(pq[:, c * LANES:(c + 1) * LANES], qg_ref[...], cos, sa, sb, bd) * Q_SCALE
        qt_out[(2 * c) * LANES:(2 * c + 1) * LANES, :] = jnp.where(first_half, qc, 0.0).T.astype(BF16)
        qt_out[(2 * c + 1) * LANES:(2 * c + 2) * LANES, :] = jnp.where(
            first_half, pltpu.roll(qc, HEAD_DIM, 1), 0.0).T.astype(BF16)

    pk = proj(256, 512)
    kvc_out[...] = pk[:, 0:LANES]
    ks = pk[:, LANES:2 * LANES]
    tok = (i % tiles_per_seq) * tm + lax.broadcasted_iota(jnp.int32, (tm, 1), 0)
    blk_in_tile = (tok // SEL_BLOCK) % (SEL_KEY_TILE // SEL_BLOCK)
    kvs_out[...] = jnp.where(first_half, _head_norm_rope(ks, kg_ref[0:1, :], cos, sa, sb, bd),
                             jnp.where(lane == HEAD_DIM + blk_in_tile, 1.0, 0.0)).astype(BF16)
    vts_out[...] = jnp.where(first_half, 1.0, ks).T.astype(BF16)
    pk = proj(512, 768)
    kw = pk[:, 0:LANES]
    kvw_out[...] = jnp.where(first_half, _head_norm_rope(kw, kg_ref[1:2, :], cos, sa, sb, bd), kw).astype(BF16)
    vtw_out[...] = jnp.where(first_half, 1.0, kw).T.astype(BF16)
    gatet_out[...] = jax.nn.sigmoid(pk[:, LANES:2 * LANES]).T[0:GATE_ROWS, :]

    outg = outg_ref[...]
    lane2 = lax.broadcasted_iota(jnp.int32, (1, GROUP_WIDTH), 1)
    grp = lane2 // HEAD_DIM

    u = proj(768, 1024)
    vn = _rms(proj(1024, 1280), sgug_ref[...]).astype(BF16)
    r_i = lax.broadcasted_iota(jnp.int32, (N_HEADS * SGU_CHUNK, SGU_CHUNK), 0) & (SGU_CHUNK - 1)
    c_i = lax.broadcasted_iota(jnp.int32, (N_HEADS * SGU_CHUNK, SGU_CHUNK), 1)
    wm = jnp.where(c_i <= r_i, sguw_ref[...], 0.0).astype(BF16)
    for c in range(tm // SGU_CHUNK):
        rows = slice(c * SGU_CHUNK, (c + 1) * SGU_CHUNK)
        r = _dot(wm, vn[rows, :])
        mixed = sgub_ref[...]
        for gi in range(N_HEADS):
            mixed = mixed + jnp.where(grp == gi, r[gi * SGU_CHUNK:(gi + 1) * SGU_CHUNK, :], 0.0)
        y_out[rows, 0:GROUP_WIDTH] = _rms(u[rows, :] * mixed, outg[0:1, :]).astype(BF16)

    first = (i % tiles_per_seq) == 0

    @pl.when(first)
    def _():
        pext[0:HIST, :] = jnp.zeros((HIST, GROUP_WIDTH), F32)
        zext[0:HIST, :] = jnp.zeros((HIST, GROUP_WIDTH), F32)

    @pl.when(jnp.logical_not(first))
    def _():
        pext[0:HIST, :] = pext[tm:tm + HIST, :]
        zext[0:HIST, :] = zext[tm:tm + HIST, :]

    pin = proj(1280, 1536)
    pext[HIST:, :] = pin
    pos = (i % tiles_per_seq) * tm + lax.broadcasted_iota(jnp.int32, (tm, 1), 0)
    acc = pin
    sums = []
    for k in range(1, POOL_WINDOWS[-1]):
        acc = acc + pext[HIST - k:HIST - k + tm, :]
        if k + 1 in POOL_WINDOWS:
            sums.append(acc)
    wsum = jnp.where(grp == 0, sums[0], jnp.where(grp == 1, sums[1], jnp.where(grp == 2, sums[2], sums[3])))
    win = jnp.where(grp == 0, POOL_WINDOWS[0], jnp.where(grp == 1, POOL_WINDOWS[1],
                                                         jnp.where(grp == 2, POOL_WINDOWS[2], POOL_WINDOWS[3])))
    cnt = jnp.minimum(pos + 1, win).astype(F32)
    dlt = (wsum / cnt - pin).astype(BF16)
    yc = _dot(dlt, poolw_ref[...]) * pools_ref[...]
    y_out[:, GROUP_WIDTH:2 * GROUP_WIDTH] = _rms(yc, outg[1:2, :]).astype(BF16)

    bg = proj(1536, 1792)
    z = proj(1792, 2048) * proj(2048, 2304)
    zext[HIST:, :] = z
    cw = convw_ref[...]
    conv = cw[2:3, :] * z + cw[1:2, :] * zext[HIST - 1:HIST - 1 + tm, :] + cw[0:1, :] * zext[HIST - 2:HIST - 2 + tm, :]
    y_out[:, 2 * GROUP_WIDTH:3 * GROUP_WIDTH] = _rms(bg * conv, outg[2:3, :]).astype(BF16)


def _proj_call(x, g, w, bd, cos, sa, sb, qg, kg, sgug, sguw, sgub, poolw, pools, convw, outg, *, seq, tm):
    n = x.shape[0]
    tps = seq // tm
    const = lambda shape: pl.BlockSpec(shape, lambda i: (0, 0))
    rows = lambda width: pl.BlockSpec((tm, width), lambda i: (i, 0))
    cols = lambda height: pl.BlockSpec((height, tm), lambda i: (0, i))
    tab = pl.BlockSpec((tm, LANES), lambda i: (i % tps, 0))
    return pl.pallas_call(
        functools.partial(_proj_kernel, tm=tm, tiles_per_seq=tps),
        grid=(n // tm,),
        in_specs=[rows(D_MODEL), const((1, D_MODEL)), const((D_MODEL, D_IN_PAD)), const((LANES, LANES)),
                  tab, tab, tab, const((1, LANES)), const((2, LANES)),
                  const((1, GROUP_WIDTH)), const((N_HEADS * SGU_CHUNK, SGU_CHUNK)), const((SGU_CHUNK, GROUP_WIDTH)),
                  const((GROUP_WIDTH, GROUP_WIDTH)), const((1, GROUP_WIDTH)), const((CONV_WIDTH, GROUP_WIDTH)),
                  const((3, GROUP_WIDTH))],
        out_specs=[cols(N_HEADS * LANES), rows(LANES), rows(LANES), cols(LANES), rows(LANES), cols(LANES),
                   cols(GATE_ROWS), rows(3 * GROUP_WIDTH)],
        out_shape=[jax.ShapeDtypeStruct((N_HEADS * LANES, n), BF16), jax.ShapeDtypeStruct((n, LANES), F32),
                   jax.ShapeDtypeStruct((n, LANES), BF16), jax.ShapeDtypeStruct((LANES, n), BF16),
                   jax.ShapeDtypeStruct((n, LANES), BF16), jax.ShapeDtypeStruct((LANES, n), BF16),
                   jax.ShapeDtypeStruct((GATE_ROWS, n), F32), jax.ShapeDtypeStruct((n, 3 * GROUP_WIDTH), BF16)],
        scratch_shapes=[pltpu.VMEM((tm + HIST, GROUP_WIDTH), F32), pltpu.VMEM((tm + HIST, GROUP_WIDTH), F32),
                        pltpu.VMEM((tm, D_IN_PAD), F32)],
        compiler_params=pltpu.CompilerParams(dimension_semantics=("arbitrary",), vmem_limit_bytes=VMEM_LIMIT),
        name="proj_mixers",
    )(x, g, w, bd, cos, sa, sb, qg, kg, sgug, sguw, sgub, poolw, pools, convw, outg)


def _compress_kernel(t_ref, pe_ref, w1a_ref, w1b_ref, w2_ref, kg_ref, bd_ref, cos_ref, sa_ref, sb_ref, kv_ref, vt_ref):
    t = t_ref[0]
    ncp = t.shape[0]
    a = _dot((t + pe_ref[0:1, :]).astype(BF16), w1a_ref[...])
    b = _dot((t + pe_ref[1:2, :]).astype(BF16), w1b_ref[...])
    hid = jax.nn.gelu(a + pltpu.roll(b, ncp - 1, 0), approximate=True)
    kv = _dot(hid.astype(BF16), w2_ref[...])
    first_half = lax.broadcasted_iota(jnp.int32, (1, LANES), 1) < HEAD_DIM
    kn = _head_norm_rope(kv, kg_ref[...], cos_ref[...], sa_ref[...], sb_ref[...], bd_ref[...])
    kv_ref[0] = jnp.where(first_half, kn, kv).astype(BF16)
    vt_ref[0] = jnp.where(first_half, 1.0, kv).T.astype(BF16)


def _compress_call(t2, pe, w1a, w1b, w2, kg, bd, cos, sa, sb):
    b, ncp, width = t2.shape
    const = lambda shape: pl.BlockSpec(shape, lambda i: (0,) * len(shape))
    return pl.pallas_call(
        _compress_kernel,
        grid=(b,),
        in_specs=[pl.BlockSpec((1, ncp, width), lambda i: (i, 0, 0)), const(pe.shape), const(w1a.shape),
                  const(w1b.shape), const(w2.shape), const(kg.shape), const(bd.shape),
                  const(cos.shape), const(sa.shape), const(sb.shape)],
        out_specs=[pl.BlockSpec((1, ncp, LANES), lambda i: (i, 0, 0)), pl.BlockSpec((1, LANES, ncp), lambda i: (i, 0, 0))],
        out_shape=[jax.ShapeDtypeStruct((b, ncp, LANES), BF16), jax.ShapeDtypeStruct((b, LANES, ncp), BF16)],
        compiler_params=pltpu.CompilerParams(dimension_semantics=("arbitrary",), vmem_limit_bytes=VMEM_LIMIT),
        name="nsa_compress",
    )(t2, pe, w1a, w1b, w2, kg, bd, cos, sa, sb)


def _attn_kernel(qt_ref, gatet_ref, kvc_ref, vtc_ref, kvs_ref, vts_ref, kvw_ref, vtw_ref, ovt_ref, og_ref, o_ref,
                 bias_ref, sa_ref, sb_ref, m_ref, acc_ref, oc_ref, imp_ref, sw_ref, ow_ref,
                 *, tq, tk, n_sel):
    s0 = pl.program_id(1) * tq
    nc = N_HEADS * tq
    qs = jnp.concatenate([qt_ref[h * LANES:(h + 1) * LANES, :] for h in range(N_HEADS)], axis=1)
    t_col = s0 + (lax.broadcasted_iota(jnp.int32, (1, nc), 1) & (tq - 1))
    t_q = s0 + lax.broadcasted_iota(jnp.int32, (1, tq), 1)

    ncp = kvc_ref.shape[1]
    part = CMP_PART if ncp % CMP_PART == 0 else ncp

    def cmp_branch(rows):
        cmp_end = lax.broadcasted_iota(jnp.int32, (rows, 1), 0) * CMP_STRIDE + (CMP_LEN - 1)
        sc = jnp.where(cmp_end <= t_col, _dot(kvc_ref[0, 0:rows, :], qs), NEG_INF)
        e = jnp.exp2(sc - jnp.max(sc, axis=0, keepdims=True))
        p = e * jnp.where(t_col >= CMP_LEN - 1, 1.0 / jnp.sum(e, axis=0, keepdims=True), 0.0)
        oc_ref[...] = _dot(vtc_ref[0, :, 0:rows], p.astype(BF16))[HEAD_DIM:, :]
        psum = p[:, 0:tq] + p[:, tq:2 * tq] + p[:, 2 * tq:3 * tq] + p[:, 3 * tq:4 * tq]
        hi, lo = _split(psum)
        imp_ref[...] = _dot(ovt_ref[:, 0:rows], hi) + _dot(ovt_ref[:, 0:rows], lo)

    n_parts = (s0 + tq + CMP_STRIDE * part - 1) // (CMP_STRIDE * part)
    for k in range(1, ncp // part + 1):
        pl.when(n_parts == k)(functools.partial(cmp_branch, k * part))
    o_c = oc_ref[...]
    imp = imp_ref[...]

    wk = WINDOW + tq
    w0 = pl.multiple_of(jnp.maximum(s0 - WINDOW, 0), tq)
    sw_ref[...] = _dot(kvw_ref[pl.ds(w0, wk), :], qs)

    nbp = imp.shape[0]
    blk = lax.broadcasted_iota(jnp.int32, (nbp, 1), 0)
    blk_f = blk.astype(F32)
    cur = t_q // SEL_BLOCK
    valid_b = blk <= cur
    forced = valid_b & ((blk == 0) | (blk == cur) | (blk == cur - 1))
    score = jnp.where(valid_b & jnp.logical_not(forced), imp, -1.0)
    sel = jnp.where(forced, 1.0, 0.0)
    for _ in range(n_sel - N_FORCED):
        top = jnp.max(score, axis=0, keepdims=True)
        idx = jnp.min(jnp.where(score == top, blk_f, float(nbp)), axis=0, keepdims=True)
        hit = blk_f == idx
        sel = jnp.where(hit & (top >= 0.0), 1.0, sel)
        score = jnp.where(hit, -2.0, score)
    bias = jnp.where(sel > 0.5, 0.0, NEG_INF)
    bias_ref[0:nbp, :] = jnp.concatenate([bias] * N_HEADS, axis=1)
    bias_ref[nbp:, :] = jnp.zeros((BIAS_PAD, nc), F32)

    kp = w0 + lax.broadcasted_iota(jnp.int32, (wk, 1), 0)
    sw = jnp.where((kp <= t_col) & (kp > t_col - WINDOW), sw_ref[...], NEG_INF)
    pw = jnp.exp2(sw - jnp.max(sw, axis=0, keepdims=True)).astype(BF16)
    acc_w = _dot(vtw_ref[:, pl.ds(w0, wk)], pw)
    ow_ref[...] = acc_w[HEAD_DIM:, :] / acc_w[0:1, :]

    n_blk = tk // SEL_BLOCK
    q_top = qs[0:HEAD_DIM, :]
    q_rest = jnp.zeros((LANES - HEAD_DIM - BIAS_PAD, nc), BF16)

    def scores(j):
        tile_bias = bias_ref[pl.ds(pl.multiple_of(j * n_blk, n_blk), BIAS_PAD), :].astype(BF16)
        qb = jnp.concatenate([q_top, tile_bias, q_rest], axis=0)
        return _dot(kvs_ref[pl.ds(pl.multiple_of(j * tk, tk), tk), :], qb)

    def consume(s_ref, j, causal):
        k0 = pl.multiple_of(j * tk, tk)
        s = s_ref[...]
        if causal:
            kpos = k0 + lax.broadcasted_iota(jnp.int32, (tk, 1), 0)
            s = jnp.where(kpos <= t_col, s, NEG_INF)
        m = m_ref[...]
        m_new = jnp.maximum(m, jnp.max(s, axis=0, keepdims=True))
        pe = jnp.exp2(s - m_new).astype(BF16)
        acc_ref[...] = jnp.exp2(m - m_new) * acc_ref[...] + _dot(vts_ref[:, pl.ds(k0, tk)], pe)
        m_ref[...] = m_new

    n_full = (s0 + tq + tk - 1) // tk - 1
    m_ref[...] = jnp.full((1, nc), NEG_INF, F32)
    acc_ref[...] = jnp.zeros((LANES, nc), F32)
    sa_ref[...] = scores(0)

    def pair(i, carry):
        j = 2 * i
        sb_ref[...] = scores(j + 1)
        consume(sa_ref, j, False)
        sa_ref[...] = scores(jnp.minimum(j + 2, n_full))
        consume(sb_ref, j + 1, False)
        return carry

    lax.fori_loop(0, n_full // 2, pair, 0)

    @pl.when(n_full % 2 == 1)
    def _():
        sb_ref[...] = scores(n_full)
        consume(sa_ref, n_full - 1, False)
        consume(sb_ref, n_full, True)

    @pl.when(n_full % 2 == 0)
    def _():
        consume(sa_ref, n_full, True)

    acc_s = acc_ref[...]
    o_s = acc_s[HEAD_DIM:, :] / acc_s[0:1, :]

    g = gatet_ref[...]
    heads = []
    for h in range(N_HEADS):
        c = slice(h * tq, (h + 1) * tq)
        r = h * N_BRANCH
        heads.append(g[r:r + 1, :] * o_c[:, c] + g[r + 1:r + 2, :] * o_s[:, c] + g[r + 2:r + 3, :] * ow_ref[:, c])
    ya = jnp.concatenate(heads, axis=0).T
    o_ref[...] = _rms(ya, og_ref[...]).astype(BF16)


def _attn_call(qt, gatet, kvc, vtc, kvs, vts, kvw, vtw, ovt, og, *, seq, tq, tk):
    n = qt.shape[1]
    b = n // seq
    nq = seq // tq
    ncp = kvc.shape[1]
    nbp = ovt.shape[0]
    n_sel = min(SEL_TOPK, seq // SEL_BLOCK)
    cols = lambda height: pl.BlockSpec((height, tq), lambda bi, qi: (0, bi * nq + qi))
    per_b = lambda shape: pl.BlockSpec((1,) + shape, lambda bi, qi: (bi, 0, 0))
    const = lambda shape: pl.BlockSpec(shape, lambda bi, qi: (0, 0))
    seq_rows = pl.BlockSpec((seq, LANES), lambda bi, qi: (bi, 0))
    seq_cols = pl.BlockSpec((LANES, seq), lambda bi, qi: (0, bi))
    return pl.pallas_call(
        functools.partial(_attn_kernel, tq=tq, tk=tk, n_sel=n_sel),
        grid=(b, nq),
        in_specs=[cols(N_HEADS * LANES), cols(GATE_ROWS), per_b((ncp, LANES)), per_b((LANES, ncp)),
                  seq_rows, seq_cols, seq_rows, seq_cols, const(ovt.shape), const((1, GROUP_WIDTH))],
        out_specs=pl.BlockSpec((tq, GROUP_WIDTH), lambda bi, qi: (bi * nq + qi, 0)),
        out_shape=jax.ShapeDtypeStruct((n, GROUP_WIDTH), BF16),
        scratch_shapes=[pltpu.VMEM((nbp + BIAS_PAD, N_HEADS * tq), F32), pltpu.VMEM((tk, N_HEADS * tq), F32),
                        pltpu.VMEM((tk, N_HEADS * tq), F32), pltpu.VMEM((1, N_HEADS * tq), F32),
                        pltpu.VMEM((LANES, N_HEADS * tq), F32), pltpu.VMEM((HEAD_DIM, N_HEADS * tq), F32),
                        pltpu.VMEM((nbp, tq), F32), pltpu.VMEM((WINDOW + tq, N_HEADS * tq), F32),
                        pltpu.VMEM((HEAD_DIM, N_HEADS * tq), F32)],
        compiler_params=pltpu.CompilerParams(dimension_semantics=("arbitrary", "arbitrary"),
                                             vmem_limit_bytes=VMEM_LIMIT),
        name="nsa_attention",
    )(qt, gatet, kvc, vtc, kvs, vts, kvw, vtw, ovt, og)


def _outproj_kernel(x_ref, ya_ref, yb_ref, w_ref, g_ref, *rest, moe):
    if moe:
        rw_ref, x1_ref, hn_ref, gate_ref = rest
    else:
        x1_ref, hn_ref = rest
    x1 = x_ref[...] + _dot(ya_ref[...], w_ref[0:GROUP_WIDTH, :]) + _dot(yb_ref[...], w_ref[GROUP_WIDTH:, :])
    x1_ref[...] = x1
    hn = _rms(x1, g_ref[...])
    hn_ref[...] = hn.astype(BF16)
    if moe:
        hi, lo = _split(hn)
        logits = _dot(hi, rw_ref[0]) + _dot(lo, rw_ref[0]) + _dot(hi, rw_ref[1])
        lane = lax.broadcasted_iota(jnp.int32, (1, LANES), 1)
        lane_f = lane.astype(F32)
        logits = jnp.where(lane < N_EXPERTS, logits, NEG_INF)
        m1 = jnp.max(logits, axis=-1, keepdims=True)
        i1 = jnp.min(jnp.where(logits == m1, lane_f, float(LANES)), axis=-1, keepdims=True)
        rest_l = jnp.where(lane_f == i1, 2 * NEG_INF, logits)
        m2 = jnp.max(rest_l, axis=-1, keepdims=True)
        i2 = jnp.min(jnp.where(rest_l == m2, lane_f, float(LANES)), axis=-1, keepdims=True)
        e2 = jnp.exp(m2 - m1)
        den = 1.0 + e2
        gate_ref[...] = jnp.where(lane_f == i1, 1.0 / den, jnp.where(lane_f == i2, e2 / den, 0.0))


def _outproj_call(x, ya, yb, w, g, rw, *, tm):
    n = x.shape[0]
    moe = rw is not None
    rows = lambda width: pl.BlockSpec((tm, width), lambda i: (i, 0))
    const = lambda shape: pl.BlockSpec(shape, lambda i: (0,) * len(shape))
    in_specs = [rows(D_MODEL), rows(GROUP_WIDTH), rows(3 * GROUP_WIDTH), const((D_MODEL, D_MODEL)), const((1, D_MODEL))]
    out_specs = [rows(D_MODEL), rows(D_MODEL)]
    out_shape = [jax.ShapeDtypeStruct((n, D_MODEL), F32), jax.ShapeDtypeStruct((n, D_MODEL), BF16)]
    args = [x, ya, yb, w, g]
    if moe:
        in_specs.append(const(rw.shape))
        out_specs.append(rows(LANES))
        out_shape.append(jax.ShapeDtypeStruct((n, LANES), F32))
        args.append(rw)
    return pl.pallas_call(
        functools.partial(_outproj_kernel, moe=moe),
        grid=(n // tm,),
        in_specs=in_specs, out_specs=out_specs, out_shape=out_shape,
        compiler_params=pltpu.CompilerParams(dimension_semantics=("arbitrary",), vmem_limit_bytes=VMEM_LIMIT),
        name="outproj_router" if moe else "outproj",
    )(*args)


def _ffn_kernel(h_ref, x_ref, w1_ref, w3_ref, w2_ref, o_ref, acc_ref):
    f = pl.program_id(1)

    @pl.when(f == 0)
    def _():
        acc_ref[...] = jnp.zeros_like(acc_ref)

    h = h_ref[...]
    a = _dot(h, w1_ref[...])
    t = (a * jax.nn.sigmoid(a) * _dot(h, w3_ref[...])).astype(BF16)
    acc_ref[...] += _dot(t, w2_ref[...])

    @pl.when(f == pl.num_programs(1) - 1)
    def _():
        o_ref[...] = x_ref[...] + acc_ref[...]


def _ffn_call(h, x, w1, w3, w2, *, tm, tf):
    n = h.shape[0]
    rows = pl.BlockSpec((tm, D_MODEL), lambda i, f: (i, 0))
    up = pl.BlockSpec((D_MODEL, tf), lambda i, f: (0, f))
    return pl.pallas_call(
        _ffn_kernel,
        grid=(n // tm, D_FF // tf),
        in_specs=[rows, rows, up, up, pl.BlockSpec((tf, D_MODEL), lambda i, f: (f, 0))],
        out_specs=rows,
        out_shape=jax.ShapeDtypeStruct((n, D_MODEL), F32),
        scratch_shapes=[pltpu.VMEM((tm, D_MODEL), F32)],
        compiler_params=pltpu.CompilerParams(dimension_semantics=("arbitrary", "arbitrary"),
                                             vmem_limit_bytes=VMEM_LIMIT),
        name="ffn_swiglu",
    )(h, x, w1, w3, w2)


def _moe_kernel(h_ref, x_ref, gate_ref, w1_ref, w3_ref, w2_ref, o_ref,
                xc_ref, yc_ref, rank_ref, gate_t_ref, rank_t_ref, nch_ref, *, n_sub):
    e = pl.program_id(1)
    f = pl.program_id(2)
    rb = MOE_ROWS
    sub = MOE_SUB

    def for_chunks(n_chunks, fn):
        def pair(c, carry):
            fn(c * 2 * rb, 2 * rb)
            return carry

        lax.fori_loop(0, n_chunks // 2, pair, 0)

        @pl.when(n_chunks % 2 == 1)
        def _():
            fn((n_chunks - 1) * rb, rb)

    @pl.when((e == 0) & (f == 0))
    def _():
        o_ref[...] = x_ref[...]
        before = jnp.where(lax.broadcasted_iota(jnp.int32, (sub, sub), 1)
                           < lax.broadcasted_iota(jnp.int32, (sub, sub), 0), 1.0, 0.0).astype(BF16)
        for s in range(n_sub):
            span = slice(s * sub, (s + 1) * sub)
            gate = gate_ref[span, :]
            rank = _dot(before, jnp.where(gate > 0.0, 1.0, 0.0).astype(BF16))
            rank_ref[span, :] = rank
            gate_t_ref[:, span] = gate.T[0:N_EXPERTS]
            rank_t_ref[:, span] = rank.T[0:N_EXPERTS]

    @pl.when(f == 0)
    def _():
        for s in range(n_sub):
            span = slice(s * sub, (s + 1) * sub)
            g_row = gate_t_ref[pl.ds(e, 1), span]
            n_rows = jnp.sum(jnp.where(g_row > 0.0, 1.0, 0.0)).astype(jnp.int32)
            n_chunks = (n_rows + rb - 1) // rb
            nch_ref[s] = n_chunks
            key = jnp.where(g_row > 0.0, rank_t_ref[pl.ds(e, 1), span], -1.0)

            def gather(r0, rows, s=s, span=span, key=key):
                slot = (r0 + lax.broadcasted_iota(jnp.int32, (rows, 1), 0)).astype(F32)
                onehot = jnp.where(key == slot, 1.0, 0.0).astype(BF16)
                dst = pl.ds(pl.multiple_of(s * sub + r0, rb), rows)
                xc_ref[dst, :] = _dot(onehot, h_ref[span, :]).astype(BF16)
                yc_ref[dst, :] = jnp.zeros((rows, D_MODEL), F32)

            for_chunks(n_chunks, gather)

    for s in range(n_sub):
        def swiglu(r0, rows, s=s):
            src = pl.ds(pl.multiple_of(s * sub + r0, rb), rows)
            xc = xc_ref[src, :]
            a = _dot(xc, w1_ref[...])
            t = (a * jax.nn.sigmoid(a) * _dot(xc, w3_ref[...])).astype(BF16)
            yc_ref[src, :] += _dot(t, w2_ref[...])

        for_chunks(nch_ref[s], swiglu)

    @pl.when(f == pl.num_programs(2) - 1)
    def _():
        lane = lax.broadcasted_iota(jnp.int32, (1, LANES), 1)
        for s in range(n_sub):
            span = slice(s * sub, (s + 1) * sub)
            g_col = jnp.sum(jnp.where(lane == e, gate_ref[span, :], 0.0), axis=-1, keepdims=True)
            r_col = jnp.sum(jnp.where(lane == e, rank_ref[span, :], 0.0), axis=-1, keepdims=True)
            key = jnp.where(g_col > 0.0, r_col, -1.0)

            def scatter(r0, rows, s=s, span=span, key=key, g_col=g_col):
                slot = (r0 + lax.broadcasted_iota(jnp.int32, (1, rows), 1)).astype(F32)
                onehot = jnp.where(key == slot, 1.0, 0.0).astype(BF16)
                src = pl.ds(pl.multiple_of(s * sub + r0, rb), rows)
                o_ref[span, :] += g_col * _dot(onehot, yc_ref[src, :].astype(BF16))

            for_chunks(nch_ref[s], scatter)


def _moe_call(h, x, gate, w1, w3, w2, *, tm, tf):
    n = h.shape[0]
    n_sub = tm // MOE_SUB
    once = pl.Buffered(1)
    rows = lambda width, mode=None: pl.BlockSpec((tm, width), lambda i, e, f: (i, 0), pipeline_mode=mode)
    return pl.pallas_call(
        functools.partial(_moe_kernel, n_sub=n_sub),
        grid=(n // tm, N_EXPERTS, D_FF // tf),
        in_specs=[rows(D_MODEL, once), rows(D_MODEL, once), rows(LANES),
                  pl.BlockSpec((None, D_MODEL, tf), lambda i, e, f: (e, 0, f)),
                  pl.BlockSpec((None, D_MODEL, tf), lambda i, e, f: (e, 0, f)),
                  pl.BlockSpec((None, tf, D_MODEL), lambda i, e, f: (e, f, 0))],
        out_specs=rows(D_MODEL, once),
        out_shape=jax.ShapeDtypeStruct((n, D_MODEL), F32),
        scratch_shapes=[pltpu.VMEM((tm, D_MODEL), BF16), pltpu.VMEM((tm, D_MODEL), F32), pltpu.VMEM((tm, LANES), F32),
                        pltpu.VMEM((N_EXPERTS, tm), F32), pltpu.VMEM((N_EXPERTS, tm), F32),
                        pltpu.SMEM((n_sub,), jnp.int32)],
        compiler_params=pltpu.CompilerParams(dimension_semantics=("arbitrary", "arbitrary", "arbitrary"),
                                             vmem_limit_bytes=VMEM_LIMIT),
        name="moe_swiglu",
    )(h, x, gate, w1, w3, w2)


def _rope_tables(pos):
    half = ROPE_DIM // 2
    inv_freq = ROPE_THETA ** (-jnp.arange(half, dtype=F32) / half)
    ang = pos.astype(F32)[..., None] * inv_freq
    cos, sin = jnp.cos(ang), jnp.sin(ang)
    n = pos.shape[0]
    rest = HEAD_DIM - ROPE_DIM
    cos64 = jnp.concatenate([cos, cos, jnp.ones((n, rest), F32)], axis=-1)
    sa64 = jnp.concatenate([-sin, jnp.zeros((n, rest + half), F32)], axis=-1)
    sb64 = jnp.concatenate([jnp.zeros((n, half), F32), sin, jnp.zeros((n, rest), F32)], axis=-1)
    return tuple(jnp.tile(t, (1, 2)) for t in (cos64, sa64, sb64))


def _block_diag(blocks):
    n = len(blocks)
    rows = []
    for i, blk in enumerate(blocks):
        rows.append(jnp.concatenate([blk if j == i else jnp.zeros((blk.shape[0], blocks[j].shape[1]), blk.dtype)
                                     for j in range(n)], axis=1))
    return jnp.concatenate(rows, axis=0)


def _layout_w_in(w):
    n_gate = N_BRANCH * N_HEADS
    g0 = GROUP_WIDTH + 6 * HEAD_DIM
    pad = jnp.zeros((w.shape[0], LANES - n_gate), w.dtype)
    return jnp.concatenate([w[:, :g0], w[:, g0:g0 + n_gate], pad, w[:, g0 + n_gate:]], axis=1)


def _layout_compress(pe, w1, w2):
    half = CMP_LEN // 2
    pe2 = jnp.transpose(pe, (1, 0, 2)).reshape(2, half * LANES)
    w1r = w1.reshape(2, 2, half, HEAD_DIM, CMP_HIDDEN)
    zeros = jnp.zeros((half, HEAD_DIM, CMP_HIDDEN), w1.dtype)

    def big(part):
        k_rows = jnp.concatenate([w1r[0, part], zeros], axis=-1)
        v_rows = jnp.concatenate([zeros, w1r[1, part]], axis=-1)
        return jnp.concatenate([k_rows, v_rows], axis=1).reshape(half * LANES, 2 * CMP_HIDDEN)

    return pe2, big(0), big(1), _block_diag([w2[0], w2[1]])


def _overlap_matrix(seq, ncp, nbp):
    cs = jnp.arange(ncp) * CMP_STRIDE
    bs = jnp.arange(nbp) * SEL_BLOCK
    ov = jnp.clip(jnp.minimum(cs[:, None] + CMP_LEN, bs[None, :] + SEL_BLOCK)
                  - jnp.maximum(cs[:, None], bs[None, :]), 0, None).astype(F32) / CMP_LEN
    keep = (jnp.arange(ncp)[:, None] < seq // CMP_STRIDE - 1) & (jnp.arange(nbp)[None, :] < seq // SEL_BLOCK)
    return jnp.where(keep, ov, 0.0)


def kernel(x, attn_norm_g, w_in, q_norm_g, k_norm_g, cmp_pe, cmp_w1, cmp_w2, sgu_norm_g, sgu_w, sgu_b, pool_w,
           pool_scale, conv_w, mix_out_norm_g, w_out, ffn_norm_g, ffn_w1, ffn_w3, ffn_w2, router_w, expert_w1,
           expert_w3, expert_w2):
    b, seq, _ = x.shape
    n = b * seq
    depth = w_in.shape[0]
    tm = TOKEN_TILE
    ncp = seq // CMP_STRIDE
    nbp = max(LANES, seq // SEL_BLOCK)

    cos, sa, sb = _rope_tables(jnp.arange(seq))
    cos_c, sa_c, sb_c = _rope_tables(jnp.arange(ncp) * CMP_STRIDE + CMP_LEN - 1)
    bd = _block_diag([jnp.full((HEAD_DIM, HEAD_DIM), 1.0 / HEAD_DIM, F32)] * 2).astype(BF16)
    overlap_t = _overlap_matrix(seq, ncp, nbp).T.astype(BF16)
    row = lambda v: v.reshape(1, -1)
    two = lambda v: jnp.tile(v, 2).reshape(1, LANES)

    xf = x.reshape(n, D_MODEL)
    for layer in range(depth):
        kg = k_norm_g[layer]
        sgub = jnp.repeat(sgu_b[layer].T, HEAD_DIM, axis=1)
        qt, kvc, kvs, vts, kvw, vtw, gatet, ybcd = _proj_call(
            xf, row(attn_norm_g[layer]), _layout_w_in(w_in[layer]).astype(BF16), bd, cos, sa, sb,
            two(q_norm_g[layer]), jnp.stack([jnp.tile(kg[1], 2), jnp.tile(kg[2], 2)]),
            row(sgu_norm_g[layer]), sgu_w[layer].reshape(N_HEADS * SGU_CHUNK, SGU_CHUNK), sgub,
            _block_diag([pool_w[layer, i] for i in range(len(POOL_WINDOWS))]).astype(BF16),
            row(pool_scale[layer]), conv_w[layer], mix_out_norm_g[layer].reshape(4, GROUP_WIDTH)[1:],
            seq=seq, tm=tm)
        pe2, w1a, w1b, w2c = _layout_compress(cmp_pe[layer], cmp_w1[layer], cmp_w2[layer])
        kvcc, vtc = _compress_call(kvc.reshape(b, ncp, CMP_STRIDE * LANES), pe2, w1a.astype(BF16), w1b.astype(BF16),
                                   w2c.astype(BF16), two(kg[0]), bd, cos_c, sa_c, sb_c)
        ya = _attn_call(qt, gatet, kvcc, vtc, kvs, vts, kvw, vtw, overlap_t,
                        row(mix_out_norm_g[layer, :GROUP_WIDTH]), seq=seq, tq=ATTN_Q_TILE, tk=SEL_KEY_TILE)
        i = layer // 2
        if layer % 2 == 0:
            x1, hn = _outproj_call(xf, ya, ybcd, w_out[layer].astype(BF16), row(ffn_norm_g[layer]), None, tm=tm)
            xf = _ffn_call(hn, x1, ffn_w1[i].astype(BF16), ffn_w3[i].astype(BF16), ffn_w2[i].astype(BF16),
                           tm=FFN_TOKEN_TILE, tf=FFN_FF_TILE)
        else:
            r = jnp.pad(router_w[i], ((0, 0), (0, LANES - N_EXPERTS)))
            r_hi = r.astype(BF16)
            rw = jnp.stack([r_hi, (r - r_hi.astype(F32)).astype(BF16)])
            x1, hn, gate = _outproj_call(xf, ya, ybcd, w_out[layer].astype(BF16), row(ffn_norm_g[layer]), rw, tm=tm)
            xf = _moe_call(hn, x1, gate, expert_w1[i].astype(BF16), expert_w3[i].astype(BF16),
                           expert_w2[i].astype(BF16), tm=2 * MOE_SUB, tf=MOE_FF_TILE)
    return xf.reshape(b, seq, D_MODEL)
```

```python
import functools

import jax
import jax.numpy as jnp
from jax import lax
from jax.experimental import pallas as pl
from jax.experimental.pallas import tpu as pltpu

D_MODEL = 1024
HEAD_DIM = 64
N_HEADS = 4
GROUP_WIDTH = 256
CMP_LEN = 32
CMP_STRIDE = 16
CMP_HIDDEN = 128
SEL_BLOCK = 64
SEL_TOPK = 16
WINDOW = 512
N_BRANCH = 3
ROPE_THETA = 500000.0
ROPE_DIM = 16
SGU_CHUNK = 128
POOL_WINDOWS = (2, 4, 8, 16)
CONV_WIDTH = 3
D_FF = 3584
N_EXPERTS = 8
EPS = 1e-6
NEG_INF = -1e30
Q_SCALE = HEAD_DIM ** -0.5 * 1.4426950408889634
CMP_PART = 128
N_FORCED = 3
SEL_KEY_TILE = 512
BIAS_PAD = 16
GATE_ROWS = 16

LANES = 128
MOE_ROWS = 128
MOE_FF_TILE = 1792
MOE_SUB = 1024
HIST = 16
D_IN_PAD = 2304
TOKEN_TILE = 512
ATTN_Q_TILE = 256
FFN_TOKEN_TILE = 512
FFN_FF_TILE = 1792
VMEM_LIMIT = 56 * 1024 * 1024

F32 = jnp.float32
BF16 = jnp.bfloat16


def _dot(a, b):
    return jnp.dot(a, b, preferred_element_type=F32)


def _split(a):
    hi = a.astype(BF16)
    lo = (a - hi.astype(F32)).astype(BF16)
    return hi, lo


def _rms(y, g):
    return y * lax.rsqrt(jnp.mean(y * y, axis=-1, keepdims=True) + EPS) * g


def _head_norm_rope(c, g, cos, sa, sb, bd):
    hi, lo = _split(c * c)
    msq = _dot(hi, bd) + _dot(lo, bd)
    cn = c * lax.rsqrt(msq + EPS) * g
    return cn * cos + pltpu.roll(cn, LANES - ROPE_DIM // 2, 1) * sa + pltpu.roll(cn, ROPE_DIM // 2, 1) * sb


def _proj_kernel(x_ref, g_ref, w_ref, bd_ref, cos_ref, sa_ref, sb_ref, qg_ref, kg_ref,
                 sgug_ref, sguw_ref, sgub_ref, poolw_ref, pools_ref, convw_ref, outg_ref,
                 qt_out, kvc_out, kvs_out, vts_out, kvw_out, vtw_out, gatet_out, y_out, pext, zext, p_ref,
                 *, tm, tiles_per_seq):
    i = pl.program_id(0)
    x = x_ref[...]
    hn = _rms(x, g_ref[...]).astype(BF16)
    bd = bd_ref[...]
    cos, sa, sb = cos_ref[...], sa_ref[...], sb_ref[...]
    lane = lax.broadcasted_iota(jnp.int32, (1, LANES), 1)
    first_half = lane < HEAD_DIM

    for c0 in range(0, D_IN_PAD, 2 * LANES):
        p_ref[:, c0:c0 + 2 * LANES] = _dot(hn, w_ref[:, c0:c0 + 2 * LANES])

    def proj(c0, c1):
        return p_ref[:, c0:c1]

    pq = proj(0, 256)
    for c in range(2):
        qc = _head_norm_rope(pq[:, c * LANES:(c + 1) * LANES], qg_ref[...], cos, sa, sb, bd) * Q_SCALE
        qt_out[(2 * c) * LANES:(2 * c + 1) * LANES, :] = jnp.where(first_half, qc, 0.0).T.astype(BF16)
        qt_out[(2 * c + 1) * LANES:(2 * c + 2) * LANES, :] = jnp.where(
            first_half, pltpu.roll(qc, HEAD_DIM, 1), 0.0).T.astype(BF16)

    pk = proj(256, 512)
    kvc_out[...] = pk[:, 0:LANES]
    ks = pk[:, LANES:2 * LANES]
    tok = (i % tiles_per_seq) * tm + lax.broadcasted_iota(jnp.int32, (tm, 1), 0)
    blk_in_tile = (tok // SEL_BLOCK) % (SEL_KEY_TILE // SEL_BLOCK)
    kvs_out[...] = jnp.where(first_half, _head_norm_rope(ks, kg_ref[0:1, :], cos, sa, sb, bd),
                             jnp.where(lane == HEAD_DIM + blk_in_tile, 1.0, 0.0)).astype(BF16)
    vts_out[...] = jnp.where(first_half, 1.0, ks).T.astype(BF16)
    pk = proj(512, 768)
    kw = pk[:, 0:LANES]
    kvw_out[...] = jnp.where(first_half, _head_norm_rope(kw, kg_ref[1:2, :], cos, sa, sb, bd), kw).astype(BF16)
    vtw_out[...] = jnp.where(first_half, 1.0, kw).T.astype(BF16)
    gatet_out[...] = jax.nn.sigmoid(pk[:, LANES:2 * LANES]).T[0:GATE_ROWS, :]

    outg = outg_ref[...]
    lane2 = lax.broadcasted_iota(jnp.int32, (1, GROUP_WIDTH), 1)
    grp = lane2 // HEAD_DIM

    u = proj(768, 1024)
    vn = _rms(proj(1024, 1280), sgug_ref[...]).astype(BF16)
    r_i = lax.broadcasted_iota(jnp.int32, (N_HEADS * SGU_CHUNK, SGU_CHUNK), 0) & (SGU_CHUNK - 1)
    c_i = lax.broadcasted_iota(jnp.int32, (N_HEADS * SGU_CHUNK, SGU_CHUNK), 1)
    wm = jnp.where(c_i <= r_i, sguw_ref[...], 0.0).astype(BF16)
    for c in range(tm // SGU_CHUNK):
        rows = slice(c * SGU_CHUNK, (c + 1) * SGU_CHUNK)
        r = _dot(wm, vn[rows, :])
        mixed = sgub_ref[...]
        for gi in range(N_HEADS):
            mixed = mixed + jnp.where(grp == gi, r[gi * SGU_CHUNK:(gi + 1) * SGU_CHUNK, :], 0.0)
        y_out[rows, 0:GROUP_WIDTH] = _rms(u[rows, :] * mixed, outg[0:1, :]).astype(BF16)

    first = (i % tiles_per_seq) == 0

    @pl.when(first)
    def _():
        pext[0:HIST, :] = jnp.zeros((HIST, GROUP_WIDTH), F32)
        zext[0:HIST, :] = jnp.zeros((HIST, GROUP_WIDTH), F32)

    @pl.when(jnp.logical_not(first))
    def _():
        pext[0:HIST, :] = pext[tm:tm + HIST, :]
        zext[0:HIST, :] = zext[tm:tm + HIST, :]

    pin = proj(1280, 1536)
    pext[HIST:, :] = pin
    pos = (i % tiles_per_seq) * tm + lax.broadcasted_iota(jnp.int32, (tm, 1), 0)
    acc = pin
    sums = []
    for k in range(1, POOL_WINDOWS[-1]):
        acc = acc + pext[HIST - k:HIST - k + tm, :]
        if k + 1 in POOL_WINDOWS:
            sums.append(acc)
    wsum = jnp.where(grp == 0, sums[0], jnp.where(grp == 1, sums[1], jnp.where(grp == 2, sums[2], sums[3])))
    win = jnp.where(grp == 0, POOL_WINDOWS[0], jnp.where(grp == 1, POOL_WINDOWS[1],
                                                         jnp.where(grp == 2, POOL_WINDOWS[2], POOL_WINDOWS[3])))
    cnt = jnp.minimum(pos + 1, win).astype(F32)
    dlt = (wsum / cnt - pin).astype(BF16)
    yc = _dot(dlt, poolw_ref[...]) * pools_ref[...]
    y_out[:, GROUP_WIDTH:2 * GROUP_WIDTH] = _rms(yc, outg[1:2, :]).astype(BF16)

    bg = proj(1536, 1792)
    z = proj(1792, 2048) * proj(2048, 2304)
    zext[HIST:, :] = z
    cw = convw_ref[...]
    conv = cw[2:3, :] * z + cw[1:2, :] * zext[HIST - 1:HIST - 1 + tm, :] + cw[0:1, :] * zext[HIST - 2:HIST - 2 + tm, :]
    y_out[:, 2 * GROUP_WIDTH:3 * GROUP_WIDTH] = _rms(bg * conv, outg[2:3, :]).astype(BF16)


def _proj_call(x, g, w, bd, cos, sa, sb, qg, kg, sgug, sguw, sgub, poolw, pools, convw, outg, *, seq, tm):
    n = x.shape[0]
    tps = seq // tm
    const = lambda shape: pl.BlockSpec(shape, lambda i: (0, 0))
    rows = lambda width: pl.BlockSpec((tm, width), lambda i: (i, 0))
    cols = lambda height: pl.BlockSpec((height, tm), lambda i: (0, i))
    tab = pl.BlockSpec((tm, LANES), lambda i: (i % tps, 0))
    return pl.pallas_call(
        functools.partial(_proj_kernel, tm=tm, tiles_per_seq=tps),
        grid=(n // tm,),
        in_specs=[rows(D_MODEL), const((1, D_MODEL)), const((D_MODEL, D_IN_PAD)), const((LANES, LANES)),
                  tab, tab, tab, const((1, LANES)), const((2, LANES)),
                  const((1, GROUP_WIDTH)), const((N_HEADS * SGU_CHUNK, SGU_CHUNK)), const((SGU_CHUNK, GROUP_WIDTH)),
                  const((GROUP_WIDTH, GROUP_WIDTH)), const((1, GROUP_WIDTH)), const((CONV_WIDTH, GROUP_WIDTH)),
                  const((3, GROUP_WIDTH))],
        out_specs=[cols(N_HEADS * LANES), rows(LANES), rows(LANES), cols(LANES), rows(LANES), cols(LANES),
                   cols(GATE_ROWS), rows(3 * GROUP_WIDTH)],
        out_shape=[jax.ShapeDtypeStruct((N_HEADS * LANES, n), BF16), jax.ShapeDtypeStruct((n, LANES), F32),
                   jax.ShapeDtypeStruct((n, LANES), BF16), jax.ShapeDtypeStruct((LANES, n), BF16),
                   jax.ShapeDtypeStruct((n, LANES), BF16), jax.ShapeDtypeStruct((LANES, n), BF16),
                   jax.ShapeDtypeStruct((GATE_ROWS, n), F32), jax.ShapeDtypeStruct((n, 3 * GROUP_WIDTH), BF16)],
        scratch_shapes=[pltpu.VMEM((tm + HIST, GROUP_WIDTH), F32), pltpu.VMEM((tm + HIST, GROUP_WIDTH), F32),
                        pltpu.VMEM((tm, D_IN_PAD), F32)],
        compiler_params=pltpu.CompilerParams(dimension_semantics=("arbitrary",), vmem_limit_bytes=VMEM_LIMIT),
        name="proj_mixers",
    )(x, g, w, bd, cos, sa, sb, qg, kg, sgug, sguw, sgub, poolw, pools, convw, outg)


def _compress_kernel(t_ref, pe_ref, w1a_ref, w1b_ref, w2_ref, kg_ref, bd_ref, cos_ref, sa_ref, sb_ref, kv_ref, vt_ref):
    t = t_ref[0]
    ncp = t.shape[0]
    a = _dot((t + pe_ref[0:1, :]).astype(BF16), w1a_ref[...])
    b = _dot((t + pe_ref[1:2, :]).astype(BF16), w1b_ref[...])
    hid = jax.nn.gelu(a + pltpu.roll(b, ncp - 1, 0), approximate=True)
    kv = _dot(hid.astype(BF16), w2_ref[...])
    first_half = lax.broadcasted_iota(jnp.int32, (1, LANES), 1) < HEAD_DIM
    kn = _head_norm_rope(kv, kg_ref[...], cos_ref[...], sa_ref[...], sb_ref[...], bd_ref[...])
    kv_ref[0] = jnp.where(first_half, kn, kv).astype(BF16)
    vt_ref[0] = jnp.where(first_half, 1.0, kv).T.astype(BF16)


def _compress_call(t2, pe, w1a, w1b, w2, kg, bd, cos, sa, sb):
    b, ncp, width = t2.shape
    const = lambda shape: pl.BlockSpec(shape, lambda i: (0,) * len(shape))
    return pl.pallas_call(
        _compress_kernel,
        grid=(b,),
        in_specs=[pl.BlockSpec((1, ncp, width), lambda i: (i, 0, 0)), const(pe.shape), const(w1a.shape),
                  const(w1b.shape), const(w2.shape), const(kg.shape), const(bd.shape),
                  const(cos.shape), const(sa.shape), const(sb.shape)],
        out_specs=[pl.BlockSpec((1, ncp, LANES), lambda i: (i, 0, 0)), pl.BlockSpec((1, LANES, ncp), lambda i: (i, 0, 0))],
        out_shape=[jax.ShapeDtypeStruct((b, ncp, LANES), BF16), jax.ShapeDtypeStruct((b, LANES, ncp), BF16)],
        compiler_params=pltpu.CompilerParams(dimension_semantics=("arbitrary",), vmem_limit_bytes=VMEM_LIMIT),
        name="nsa_compress",
    )(t2, pe, w1a, w1b, w2, kg, bd, cos, sa, sb)


def _attn_kernel(qt_ref, gatet_ref, kvc_ref, vtc_ref, kvs_ref, vts_ref, kvw_ref, vtw_ref, ovt_ref, og_ref, o_ref,
                 bias_ref, sa_ref, sb_ref, m_ref, acc_ref, oc_ref, imp_ref, sw_ref, ow_ref,
                 *, tq, tk, n_sel):
    s0 = pl.program_id(1) * tq
    nc = N_HEADS * tq
    qs = jnp.concatenate([qt_ref[h * LANES:(h + 1) * LANES, :] for h in range(N_HEADS)], axis=1)
    t_col = s0 + (lax.broadcasted_iota(jnp.int32, (1, nc), 1) & (tq - 1))
    t_q = s0 + lax.broadcasted_iota(jnp.int32, (1, tq), 1)

    ncp = kvc_ref.shape[1]
    part = CMP_PART if ncp % CMP_PART == 0 else ncp

    def cmp_branch(rows):
        cmp_end = lax.broadcasted_iota(jnp.int32, (rows, 1), 0) * CMP_STRIDE + (CMP_LEN - 1)
        sc = jnp.where(cmp_end <= t_col, _dot(kvc_ref[0, 0:rows, :], qs), NEG_INF)
        e = jnp.exp2(sc - jnp.max(sc, axis=0, keepdims=True))
        p = e * jnp.where(t_col >= CMP_LEN - 1, 1.0 / jnp.sum(e, axis=0, keepdims=True), 0.0)
        oc_ref[...] = _dot(vtc_ref[0, :, 0:rows], p.astype(BF16))[HEAD_DIM:, :]
        psum = p[:, 0:tq] + p[:, tq:2 * tq] + p[:, 2 * tq:3 * tq] + p[:, 3 * tq:4 * tq]
        hi, lo = _split(psum)
        imp_ref[...] = _dot(ovt_ref[:, 0:rows], hi) + _dot(ovt_ref[:, 0:rows], lo)

    n_parts = (s0 + tq + CMP_STRIDE * part - 1) // (CMP_STRIDE * part)
    for k in range(1, ncp // part + 1):
        pl.when(n_parts == k)(functools.partial(cmp_branch, k * part))
    o_c = oc_ref[...]
    imp = imp_ref[...]

    wk = WINDOW + tq
    w0 = pl.multiple_of(jnp.maximum(s0 - WINDOW, 0), tq)
    sw_ref[...] = _dot(kvw_ref[pl.ds(w0, wk), :], qs)

    nbp = imp.shape[0]
    blk = lax.broadcasted_iota(jnp.int32, (nbp, 1), 0)
    blk_f = blk.astype(F32)
    cur = t_q // SEL_BLOCK
    valid_b = blk <= cur
    forced = valid_b & ((blk == 0) | (blk == cur) | (blk == cur - 1))
    score = jnp.where(valid_b & jnp.logical_not(forced), imp, -1.0)
    sel = jnp.where(forced, 1.0, 0.0)
    for _ in range(n_sel - N_FORCED):
        top = jnp.max(score, axis=0, keepdims=True)
        idx = jnp.min(jnp.where(score == top, blk_f, float(nbp)), axis=0, keepdims=True)
        hit = blk_f == idx
        sel = jnp.where(hit & (top >= 0.0), 1.0, sel)
        score = jnp.where(hit, -2.0, score)
    bias = jnp.where(sel > 0.5, 0.0, NEG_INF)
    bias_ref[0:nbp, :] = jnp.concatenate([bias] * N_HEADS, axis=1)
    bias_ref[nbp:, :] = jnp.zeros((BIAS_PAD, nc), F32)

    kp = w0 + lax.broadcasted_iota(jnp.int32, (wk, 1), 0)
    sw = jnp.where((kp <= t_col) & (kp > t_col - WINDOW), sw_ref[...], NEG_INF)
    pw = jnp.exp2(sw - jnp.max(sw, axis=0, keepdims=True)).astype(BF16)
    acc_w = _dot(vtw_ref[:, pl.ds(w0, wk)], pw)
    ow_ref[...] = acc_w[HEAD_DIM:, :] / acc_w[0:1, :]

    n_blk = tk // SEL_BLOCK
    q_top = qs[0:HEAD_DIM, :]
    q_rest = jnp.zeros((LANES - HEAD_DIM - BIAS_PAD, nc), BF16)

    def scores(j):
        tile_bias = bias_ref[pl.ds(pl.multiple_of(j * n_blk, n_blk), BIAS_PAD), :].astype(BF16)
        qb = jnp.concatenate([q_top, tile_bias, q_rest], axis=0)
        return _dot(kvs_ref[pl.ds(pl.multiple_of(j * tk, tk), tk), :], qb)

    def consume(s_ref, j, causal):
        k0 = pl.multiple_of(j * tk, tk)
        s = s_ref[...]
        if causal:
            kpos = k0 + lax.broadcasted_iota(jnp.int32, (tk, 1), 0)
            s = jnp.where(kpos <= t_col, s, NEG_INF)
        m = m_ref[...]
        m_new = jnp.maximum(m, jnp.max(s, axis=0, keepdims=True))
        pe = jnp.exp2(s - m_new).astype(BF16)
        acc_ref[...] = jnp.exp2(m - m_new) * acc_ref[...] + _dot(vts_ref[:, pl.ds(k0, tk)], pe)
        m_ref[...] = m_new

    n_full = (s0 + tq + tk - 1) // tk - 1
    m_ref[...] = jnp.full((1, nc), NEG_INF, F32)
    acc_ref[...] = jnp.zeros((LANES, nc), F32)
    sa_ref[...] = scores(0)

    def pair(i, carry):
        j = 2 * i
        sb_ref[...] = scores(j + 1)
        consume(sa_ref, j, False)
        sa_ref[...] = scores(jnp.minimum(j + 2, n_full))
        consume(sb_ref, j + 1, False)
        return carry

    lax.fori_loop(0, n_full // 2, pair, 0)

    @pl.when(n_full % 2 == 1)
    def _():
        sb_ref[...] = scores(n_full)
        consume(sa_ref, n_full - 1, False)
        consume(sb_ref, n_full, True)

    @pl.when(n_full % 2 == 0)
    def _():
        consume(sa_ref, n_full, True)

    acc_s = acc_ref[...]
    o_s = acc_s[HEAD_DIM:, :] / acc_s[0:1, :]

    g = gatet_ref[...]
    heads = []
    for h in range(N_HEADS):
        c = slice(h * tq, (h + 1) * tq)
        r = h * N_BRANCH
        heads.append(g[r:r + 1, :] * o_c[:, c] + g[r + 1:r + 2, :] * o_s[:, c] + g[r + 2:r + 3, :] * ow_ref[:, c])
    ya = jnp.concatenate(heads, axis=0).T
    o_ref[...] = _rms(ya, og_ref[...]).astype(BF16)


def _attn_call(qt, gatet, kvc, vtc, kvs, vts, kvw, vtw, ovt, og, *, seq, tq, tk):
    n = qt.shape[1]
    b = n // seq
    nq = seq // tq
    ncp = kvc.shape[1]
    nbp = ovt.shape[0]
    n_sel = min(SEL_TOPK, seq // SEL_BLOCK)
    cols = lambda height: pl.BlockSpec((height, tq), lambda bi, qi: (0, bi * nq + qi))
    per_b = lambda shape: pl.BlockSpec((1,) + shape, lambda bi, qi: (bi, 0, 0))
    const = lambda shape: pl.BlockSpec(shape, lambda bi, qi: (0, 0))
    seq_rows = pl.BlockSpec((seq, LANES), lambda bi, qi: (bi, 0))
    seq_cols = pl.BlockSpec((LANES, seq), lambda bi, qi: (0, bi))
    return pl.pallas_call(
        functools.partial(_attn_kernel, tq=tq, tk=tk, n_sel=n_sel),
        grid=(b, nq),
        in_specs=[cols(N_HEADS * LANES), cols(GATE_ROWS), per_b((ncp, LANES)), per_b((LANES, ncp)),
                  seq_rows, seq_cols, seq_rows, seq_cols, const(ovt.shape), const((1, GROUP_WIDTH))],
        out_specs=pl.BlockSpec((tq, GROUP_WIDTH), lambda bi, qi: (bi * nq + qi, 0)),
        out_shape=jax.ShapeDtypeStruct((n, GROUP_WIDTH), BF16),
        scratch_shapes=[pltpu.VMEM((nbp + BIAS_PAD, N_HEADS * tq), F32), pltpu.VMEM((tk, N_HEADS * tq), F32),
                        pltpu.VMEM((tk, N_HEADS * tq), F32), pltpu.VMEM((1, N_HEADS * tq), F32),
                        pltpu.VMEM((LANES, N_HEADS * tq), F32), pltpu.VMEM((HEAD_DIM, N_HEADS * tq), F32),
                        pltpu.VMEM((nbp, tq), F32), pltpu.VMEM((WINDOW + tq, N_HEADS * tq), F32),
                        pltpu.VMEM((HEAD_DIM, N_HEADS * tq), F32)],
        compiler_params=pltpu.CompilerParams(dimension_semantics=("arbitrary", "arbitrary"),
                                             vmem_limit_bytes=VMEM_LIMIT),
        name="nsa_attention",
    )(qt, gatet, kvc, vtc, kvs, vts, kvw, vtw, ovt, og)


def _outproj_kernel(x_ref, ya_ref, yb_ref, w_ref, g_ref, *rest, moe):
    if moe:
        rw_ref, x1_ref, hn_ref, gate_ref = rest
    else:
        x1_ref, hn_ref = rest
    x1 = x_ref[...] + _dot(ya_ref[...], w_ref[0:GROUP_WIDTH, :]) + _dot(yb_ref[...], w_ref[GROUP_WIDTH:, :])
    x1_ref[...] = x1
    hn = _rms(x1, g_ref[...])
    hn_ref[...] = hn.astype(BF16)
    if moe:
        hi, lo = _split(hn)
        logits = _dot(hi, rw_ref[0]) + _dot(lo, rw_ref[0]) + _dot(hi, rw_ref[1])
        lane = lax.broadcasted_iota(jnp.int32, (1, LANES), 1)
        lane_f = lane.astype(F32)
        logits = jnp.where(lane < N_EXPERTS, logits, NEG_INF)
        m1 = jnp.max(logits, axis=-1, keepdims=True)
        i1 = jnp.min(jnp.where(logits == m1, lane_f, float(LANES)), axis=-1, keepdims=True)
        rest_l = jnp.where(lane_f == i1, 2 * NEG_INF, logits)
        m2 = jnp.max(rest_l, axis=-1, keepdims=True)
        i2 = jnp.min(jnp.where(rest_l == m2, lane_f, float(LANES)), axis=-1, keepdims=True)
        e2 = jnp.exp(m2 - m1)
        den = 1.0 + e2
        gate_ref[...] = jnp.where(lane_f == i1, 1.0 / den, jnp.where(lane_f == i2, e2 / den, 0.0))


def _outproj_call(x, ya, yb, w, g, rw, *, tm):
    n = x.shape[0]
    moe = rw is not None
    rows = lambda width: pl.BlockSpec((tm, width), lambda i: (i, 0))
    const = lambda shape: pl.BlockSpec(shape, lambda i: (0,) * len(shape))
    in_specs = [rows(D_MODEL), rows(GROUP_WIDTH), rows(3 * GROUP_WIDTH), const((D_MODEL, D_MODEL)), const((1, D_MODEL))]
    out_specs = [rows(D_MODEL), rows(D_MODEL)]
    out_shape = [jax.ShapeDtypeStruct((n, D_MODEL), F32), jax.ShapeDtypeStruct((n, D_MODEL), BF16)]
    args = [x, ya, yb, w, g]
    if moe:
        in_specs.append(const(rw.shape))
        out_specs.append(rows(LANES))
        out_shape.append(jax.ShapeDtypeStruct((n, LANES), F32))
        args.append(rw)
    return pl.pallas_call(
        functools.partial(_outproj_kernel, moe=moe),
        grid=(n // tm,),
        in_specs=in_specs, out_specs=out_specs, out_shape=out_shape,
        compiler_params=pltpu.CompilerParams(dimension_semantics=("arbitrary",), vmem_limit_bytes=VMEM_LIMIT),
        name="outproj_router" if moe else "outproj",
    )(*args)


def _ffn_kernel(h_ref, x_ref, w1_ref, w3_ref, w2_ref, o_ref, acc_ref):
    f = pl.program_id(1)

    @pl.when(f == 0)
    def _():
        acc_ref[...] = jnp.zeros_like(acc_ref)

    h = h_ref[...]
    a = _dot(h, w1_ref[...])
    t = (a * jax.nn.sigmoid(a) * _dot(h, w3_ref[...])).astype(BF16)
    acc_ref[...] += _dot(t, w2_ref[...])

    @pl.when(f == pl.num_programs(1) - 1)
    def _():
        o_ref[...] = x_ref[...] + acc_ref[...]


def _ffn_call(h, x, w1, w3, w2, *, tm, tf):
    n = h.shape[0]
    rows = pl.BlockSpec((tm, D_MODEL), lambda i, f: (i, 0))
    up = pl.BlockSpec((D_MODEL, tf), lambda i, f: (0, f))
    return pl.pallas_call(
        _ffn_kernel,
        grid=(n // tm, D_FF // tf),
        in_specs=[rows, rows, up, up, pl.BlockSpec((tf, D_MODEL), lambda i, f: (f, 0))],
        out_specs=rows,
        out_shape=jax.ShapeDtypeStruct((n, D_MODEL), F32),
        scratch_shapes=[pltpu.VMEM((tm, D_MODEL), F32)],
        compiler_params=pltpu.CompilerParams(dimension_semantics=("arbitrary", "arbitrary"),
                                             vmem_limit_bytes=VMEM_LIMIT),
        name="ffn_swiglu",
    )(h, x, w1, w3, w2)


def _moe_kernel(h_ref, x_hbm, gate_ref, w1_ref, w3_ref, w2_ref, o_ref,
                xc_ref, yc_ref, rank_ref, gate_t_ref, rank_t_ref, start_ref, x_sem, *, n_sub):
    e = pl.program_id(1)
    f = pl.program_id(2)
    rb = MOE_ROWS
    sub = MOE_SUB

    def for_chunks(c_lo, c_hi, fn):
        def pair(c, carry):
            fn(pl.multiple_of((c_lo + 2 * c) * rb, rb), 2 * rb)
            return carry

        lax.fori_loop(0, (c_hi - c_lo) // 2, pair, 0)

        @pl.when((c_hi - c_lo) % 2 == 1)
        def _():
            fn(pl.multiple_of((c_hi - 1) * rb, rb), rb)

    def sub_chunks(s):
        return start_ref[s] // rb, (start_ref[s + 1] + rb - 1) // rb

    @pl.when((e == 0) & (f == 0))
    def _():
        tm = n_sub * sub
        x_copy = pltpu.make_async_copy(x_hbm.at[pl.ds(pl.multiple_of(pl.program_id(0) * tm, tm), tm), :], o_ref, x_sem)
        x_copy.start()
        before = jnp.where(lax.broadcasted_iota(jnp.int32, (sub, sub), 1)
                           < lax.broadcasted_iota(jnp.int32, (sub, sub), 0), 1.0, 0.0).astype(BF16)
        routed_before = jnp.zeros((1, LANES), F32)
        for s in range(n_sub):
            span = slice(s * sub, (s + 1) * sub)
            gate = gate_ref[span, :]
            routed = jnp.where(gate > 0.0, 1.0, 0.0)
            rank = _dot(before, routed.astype(BF16)) + routed_before
            rank_ref[span, :] = rank
            gate_t_ref[:, span] = gate.T[0:N_EXPERTS]
            rank_t_ref[:, span] = rank.T[0:N_EXPERTS]
            routed_before = routed_before + jnp.sum(routed, axis=0, keepdims=True)
        x_copy.wait()

    @pl.when(f == 0)
    def _():
        start_ref[0] = 0
        for s in range(n_sub):
            g_row = gate_t_ref[pl.ds(e, 1), s * sub:(s + 1) * sub]
            start_ref[s + 1] = start_ref[s] + jnp.sum(jnp.where(g_row > 0.0, 1.0, 0.0)).astype(jnp.int32)

        def clear(r0, rows):
            xc_ref[pl.ds(r0, rows), :] = jnp.zeros((rows, D_MODEL), BF16)
            yc_ref[pl.ds(r0, rows), :] = jnp.zeros((rows, D_MODEL), F32)

        for_chunks(0, (start_ref[n_sub] + rb - 1) // rb, clear)

        for s in range(n_sub):
            span = slice(s * sub, (s + 1) * sub)
            g_row = gate_t_ref[pl.ds(e, 1), span]
            key = jnp.where(g_row > 0.0, rank_t_ref[pl.ds(e, 1), span], -1.0)

            def gather(r0, rows, span=span, key=key):
                slot = (r0 + lax.broadcasted_iota(jnp.int32, (rows, 1), 0)).astype(F32)
                onehot = jnp.where(key == slot, 1.0, 0.0).astype(BF16)
                dst = pl.ds(r0, rows)
                xc_ref[dst, :] = (xc_ref[dst, :].astype(F32) + _dot(onehot, h_ref[span, :])).astype(BF16)

            for_chunks(*sub_chunks(s), gather)

    def swiglu(r0, rows):
        src = pl.ds(r0, rows)
        xc = xc_ref[src, :]
        a = _dot(xc, w1_ref[...])
        t = (a * jax.nn.sigmoid(a) * _dot(xc, w3_ref[...])).astype(BF16)
        yc_ref[src, :] += _dot(t, w2_ref[...])

    for_chunks(0, (start_ref[n_sub] + rb - 1) // rb, swiglu)

    @pl.when(f == pl.num_programs(2) - 1)
    def _():
        lane = lax.broadcasted_iota(jnp.int32, (1, LANES), 1)
        for s in range(n_sub):
            span = slice(s * sub, (s + 1) * sub)
            g_col = jnp.sum(jnp.where(lane == e, gate_ref[span, :], 0.0), axis=-1, keepdims=True)
            r_col = jnp.sum(jnp.where(lane == e, rank_ref[span, :], 0.0), axis=-1, keepdims=True)
            key = jnp.where(g_col > 0.0, r_col, -1.0)

            def scatter(r0, rows, span=span, key=key, g_col=g_col):
                slot = (r0 + lax.broadcasted_iota(jnp.int32, (1, rows), 1)).astype(F32)
                onehot = jnp.where(key == slot, 1.0, 0.0).astype(BF16)
                o_ref[span, :] += g_col * _dot(onehot, yc_ref[pl.ds(r0, rows), :].astype(BF16))

            for_chunks(*sub_chunks(s), scatter)


def _moe_call(h, x, gate, w1, w3, w2, *, tm, tf):
    n = h.shape[0]
    n_sub = tm // MOE_SUB
    once = pl.Buffered(1)
    rows = lambda width, mode=None: pl.BlockSpec((tm, width), lambda i, e, f: (i, 0), pipeline_mode=mode)
    return pl.pallas_call(
        functools.partial(_moe_kernel, n_sub=n_sub),
        grid=(n // tm, N_EXPERTS, D_FF // tf),
        in_specs=[rows(D_MODEL, once), pl.BlockSpec(memory_space=pl.ANY), rows(LANES),
                  pl.BlockSpec((None, D_MODEL, tf), lambda i, e, f: (e, 0, f)),
                  pl.BlockSpec((None, D_MODEL, tf), lambda i, e, f: (e, 0, f)),
                  pl.BlockSpec((None, tf, D_MODEL), lambda i, e, f: (e, f, 0))],
        out_specs=rows(D_MODEL, once),
        out_shape=jax.ShapeDtypeStruct((n, D_MODEL), F32),
        scratch_shapes=[pltpu.VMEM((tm, D_MODEL), BF16), pltpu.VMEM((tm, D_MODEL), F32), pltpu.VMEM((tm, LANES), F32),
                        pltpu.VMEM((N_EXPERTS, tm), F32), pltpu.VMEM((N_EXPERTS, tm), F32),
                        pltpu.SMEM((n_sub + 1,), jnp.int32), pltpu.SemaphoreType.DMA(())],
        compiler_params=pltpu.CompilerParams(dimension_semantics=("arbitrary", "arbitrary", "arbitrary"),
                                             vmem_limit_bytes=VMEM_LIMIT),
        name="moe_swiglu",
    )(h, x, gate, w1, w3, w2)


def _rope_tables(pos):
    half = ROPE_DIM // 2
    inv_freq = ROPE_THETA ** (-jnp.arange(half, dtype=F32) / half)
    ang = pos.astype(F32)[..., None] * inv_freq
    cos, sin = jnp.cos(ang), jnp.sin(ang)
    n = pos.shape[0]
    rest = HEAD_DIM - ROPE_DIM
    cos64 = jnp.concatenate([cos, cos, jnp.ones((n, rest), F32)], axis=-1)
    sa64 = jnp.concatenate([-sin, jnp.zeros((n, rest + half), F32)], axis=-1)
    sb64 = jnp.concatenate([jnp.zeros((n, half), F32), sin, jnp.zeros((n, rest), F32)], axis=-1)
    return tuple(jnp.tile(t, (1, 2)) for t in (cos64, sa64, sb64))


def _block_diag(blocks):
    n = len(blocks)
    rows = []
    for i, blk in enumerate(blocks):
        rows.append(jnp.concatenate([blk if j == i else jnp.zeros((blk.shape[0], blocks[j].shape[1]), blk.dtype)
                                     for j in range(n)], axis=1))
    return jnp.concatenate(rows, axis=0)


def _layout_w_in(w):
    n_gate = N_BRANCH * N_HEADS
    g0 = GROUP_WIDTH + 6 * HEAD_DIM
    pad = jnp.zeros((w.shape[0], LANES - n_gate), w.dtype)
    return jnp.concatenate([w[:, :g0], w[:, g0:g0 + n_gate], pad, w[:, g0 + n_gate:]], axis=1)


def _layout_compress(pe, w1, w2):
    half = CMP_LEN // 2
    pe2 = jnp.transpose(pe, (1, 0, 2)).reshape(2, half * LANES)
    w1r = w1.reshape(2, 2, half, HEAD_DIM, CMP_HIDDEN)
    zeros = jnp.zeros((half, HEAD_DIM, CMP_HIDDEN), w1.dtype)

    def big(part):
        k_rows = jnp.concatenate([w1r[0, part], zeros], axis=-1)
        v_rows = jnp.concatenate([zeros, w1r[1, part]], axis=-1)
        return jnp.concatenate([k_rows, v_rows], axis=1).reshape(half * LANES, 2 * CMP_HIDDEN)

    return pe2, big(0), big(1), _block_diag([w2[0], w2[1]])


def _overlap_matrix(seq, ncp, nbp):
    cs = jnp.arange(ncp) * CMP_STRIDE
    bs = jnp.arange(nbp) * SEL_BLOCK
    ov = jnp.clip(jnp.minimum(cs[:, None] + CMP_LEN, bs[None, :] + SEL_BLOCK)
                  - jnp.maximum(cs[:, None], bs[None, :]), 0, None).astype(F32) / CMP_LEN
    keep = (jnp.arange(ncp)[:, None] < seq // CMP_STRIDE - 1) & (jnp.arange(nbp)[None, :] < seq // SEL_BLOCK)
    return jnp.where(keep, ov, 0.0)


def kernel(x, attn_norm_g, w_in, q_norm_g, k_norm_g, cmp_pe, cmp_w1, cmp_w2, sgu_norm_g, sgu_w, sgu_b, pool_w,
           pool_scale, conv_w, mix_out_norm_g, w_out, ffn_norm_g, ffn_w1, ffn_w3, ffn_w2, router_w, expert_w1,
           expert_w3, expert_w2):
    b, seq, _ = x.shape
    n = b * seq
    depth = w_in.shape[0]
    tm = TOKEN_TILE
    ncp = seq // CMP_STRIDE
    nbp = max(LANES, seq // SEL_BLOCK)

    cos, sa, sb = _rope_tables(jnp.arange(seq))
    cos_c, sa_c, sb_c = _rope_tables(jnp.arange(ncp) * CMP_STRIDE + CMP_LEN - 1)
    bd = _block_diag([jnp.full((HEAD_DIM, HEAD_DIM), 1.0 / HEAD_DIM, F32)] * 2).astype(BF16)
    overlap_t = _overlap_matrix(seq, ncp, nbp).T.astype(BF16)
    row = lambda v: v.reshape(1, -1)
    two = lambda v: jnp.tile(v, 2).reshape(1, LANES)

    xf = x.reshape(n, D_MODEL)
    for layer in range(depth):
        kg = k_norm_g[layer]
        sgub = jnp.repeat(sgu_b[layer].T, HEAD_DIM, axis=1)
        qt, kvc, kvs, vts, kvw, vtw, gatet, ybcd = _proj_call(
            xf, row(attn_norm_g[layer]), _layout_w_in(w_in[layer]).astype(BF16), bd, cos, sa, sb,
            two(q_norm_g[layer]), jnp.stack([jnp.tile(kg[1], 2), jnp.tile(kg[2], 2)]),
            row(sgu_norm_g[layer]), sgu_w[layer].reshape(N_HEADS * SGU_CHUNK, SGU_CHUNK), sgub,
            _block_diag([pool_w[layer, i] for i in range(len(POOL_WINDOWS))]).astype(BF16),
            row(pool_scale[layer]), conv_w[layer], mix_out_norm_g[layer].reshape(4, GROUP_WIDTH)[1:],
            seq=seq, tm=tm)
        pe2, w1a, w1b, w2c = _layout_compress(cmp_pe[layer], cmp_w1[layer], cmp_w2[layer])
        kvcc, vtc = _compress_call(kvc.reshape(b, ncp, CMP_STRIDE * LANES), pe2, w1a.astype(BF16), w1b.astype(BF16),
                                   w2c.astype(BF16), two(kg[0]), bd, cos_c, sa_c, sb_c)
        ya = _attn_call(qt, gatet, kvcc, vtc, kvs, vts, kvw, vtw, overlap_t,
                        row(mix_out_norm_g[layer, :GROUP_WIDTH]), seq=seq, tq=ATTN_Q_TILE, tk=SEL_KEY_TILE)
        i = layer // 2
        if layer % 2 == 0:
            x1, hn = _outproj_call(xf, ya, ybcd, w_out[layer].astype(BF16), row(ffn_norm_g[layer]), None, tm=tm)
            xf = _ffn_call(hn, x1, ffn_w1[i].astype(BF16), ffn_w3[i].astype(BF16), ffn_w2[i].astype(BF16),
                           tm=FFN_TOKEN_TILE, tf=FFN_FF_TILE)
        else:
            r = jnp.pad(router_w[i], ((0, 0), (0, LANES - N_EXPERTS)))
            r_hi = r.astype(BF16)
            rw = jnp.stack([r_hi, (r - r_hi.astype(F32)).astype(BF16)])
            x1, hn, gate = _outproj_call(xf, ya, ybcd, w_out[layer].astype(BF16), row(ffn_norm_g[layer]), rw, tm=tm)
            xf = _moe_call(hn, x1, gate, expert_w1[i].astype(BF16), expert_w3[i].astype(BF16),
                           expert_w2[i].astype(BF16), tm=2 * MOE_SUB, tf=MOE_FF_TILE)
    return xf.reshape(b, seq, D_MODEL)
```

```python
import functools

import numpy as np
import jax
import jax.numpy as jnp
from jax import lax
from jax.experimental import pallas as pl
from jax.experimental.pallas import tpu as pltpu

D_MODEL = 1024
HEAD_DIM = 64
N_HEADS = 4
GROUP_WIDTH = 256
CMP_LEN = 32
CMP_STRIDE = 16
CMP_HIDDEN = 128
SEL_BLOCK = 64
SEL_TOPK = 16
WINDOW = 512
N_BRANCH = 3
ROPE_THETA = 500000.0
ROPE_DIM = 16
SGU_CHUNK = 128
POOL_WINDOWS = (2, 4, 8, 16)
CONV_WIDTH = 3
D_FF = 3584
N_EXPERTS = 8
EPS = 1e-6
NEG_INF = -1e30
Q_SCALE = HEAD_DIM ** -0.5 * 1.4426950408889634
CMP_PART = 128
N_FORCED = 3
SEL_KEY_TILE = 512
BIAS_PAD = 16
GATE_ROWS = 16

LANES = 128
MOE_ROWS = 128
MOE_FF_TILE = 1792
MOE_STATIC_CHUNKS = 4
MOE_SUB = 1024
HIST = 16
D_IN_PAD = 2304
TOKEN_TILE = 512
ATTN_Q_TILE = 256
FFN_TOKEN_TILE = 512
FFN_FF_TILE = 1792
VMEM_LIMIT = 56 * 1024 * 1024

F32 = jnp.float32
BF16 = jnp.bfloat16


def _dot(a, b):
    return jnp.dot(a, b, preferred_element_type=F32)


def _split(a):
    hi = a.astype(BF16)
    lo = (a - hi.astype(F32)).astype(BF16)
    return hi, lo


def _rms(y, g):
    return y * lax.rsqrt(jnp.mean(y * y, axis=-1, keepdims=True) + EPS) * g


def _head_norm_rope(c, g, cos, sa, sb, bd):
    hi, lo = _split(c * c)
    msq = _dot(hi, bd) + _dot(lo, bd)
    cn = c * lax.rsqrt(msq + EPS) * g
    return cn * cos + pltpu.roll(cn, LANES - ROPE_DIM // 2, 1) * sa + pltpu.roll(cn, ROPE_DIM // 2, 1) * sb


def _proj_kernel(x_ref, g_ref, w_ref, bd_ref, cos_ref, sa_ref, sb_ref, qg_ref, kg_ref,
                 sgug_ref, sguw_ref, sgub_ref, poolw_ref, pools_ref, convw_ref, outg_ref,
                 qt_out, kvc_out, kvs_out, vts_out, kvw_out, vtw_out, gatet_out, y_out, pext, zext, p_ref, kc_ref,
                 *, tm, tiles_per_seq):
    i = pl.program_id(0)
    x = x_ref[...]
    hn = _rms(x, g_ref[...]).astype(BF16)
    bd = bd_ref[...]
    cos, sa, sb = cos_ref[...], sa_ref[...], sb_ref[...]
    lane = lax.broadcasted_iota(jnp.int32, (1, LANES), 1)
    first_half = lane < HEAD_DIM

    for c0 in range(0, D_IN_PAD, 2 * LANES):
        p_ref[:, c0:c0 + 2 * LANES] = _dot(hn, w_ref[:, c0:c0 + 2 * LANES])

    def proj(c0, c1):
        return p_ref[:, c0:c1]

    pq = proj(0, 256)
    for c in range(2):
        qc = _head_norm_rope(pq[:, c * LANES:(c + 1) * LANES], qg_ref[...], cos, sa, sb, bd) * Q_SCALE
        qt_out[(2 * c) * LANES:(2 * c + 1) * LANES, :] = jnp.where(first_half, qc, 0.0).T.astype(BF16)
        qt_out[(2 * c + 1) * LANES:(2 * c + 2) * LANES, :] = jnp.where(
            first_half, pltpu.roll(qc, HEAD_DIM, 1), 0.0).T.astype(BF16)

    pk = proj(256, 512)
    kc_ref[...] = pk[:, 0:LANES]
    for r in range(CMP_STRIDE):
        kvc_out[:, r * LANES:(r + 1) * LANES] = kc_ref[pl.ds(r, tm // CMP_STRIDE, stride=CMP_STRIDE), :]
    ks = pk[:, LANES:2 * LANES]
    tok = (i % tiles_per_seq) * tm + lax.broadcasted_iota(jnp.int32, (tm, 1), 0)
    blk_in_tile = (tok // SEL_BLOCK) % (SEL_KEY_TILE // SEL_BLOCK)
    kvs_out[...] = jnp.where(first_half, _head_norm_rope(ks, kg_ref[0:1, :], cos, sa, sb, bd),
                             jnp.where(lane == HEAD_DIM + blk_in_tile, 1.0, 0.0)).astype(BF16)
    vts_out[...] = jnp.where(first_half, 1.0, ks).T.astype(BF16)
    pk = proj(512, 768)
    kw = pk[:, 0:LANES]
    kvw_out[...] = jnp.where(first_half, _head_norm_rope(kw, kg_ref[1:2, :], cos, sa, sb, bd), kw).astype(BF16)
    vtw_out[...] = jnp.where(first_half, 1.0, kw).T.astype(BF16)
    gatet_out[...] = jax.nn.sigmoid(pk[:, LANES:2 * LANES]).T[0:GATE_ROWS, :]

    outg = outg_ref[...]
    lane2 = lax.broadcasted_iota(jnp.int32, (1, GROUP_WIDTH), 1)
    grp = lane2 // HEAD_DIM

    u = proj(768, 1024)
    vn = _rms(proj(1024, 1280), sgug_ref[...]).astype(BF16)
    r_i = lax.broadcasted_iota(jnp.int32, (N_HEADS * SGU_CHUNK, SGU_CHUNK), 0) & (SGU_CHUNK - 1)
    c_i = lax.broadcasted_iota(jnp.int32, (N_HEADS * SGU_CHUNK, SGU_CHUNK), 1)
    wm = jnp.where(c_i <= r_i, sguw_ref[...], 0.0).astype(BF16)
    for c in range(tm // SGU_CHUNK):
        rows = slice(c * SGU_CHUNK, (c + 1) * SGU_CHUNK)
        r = _dot(wm, vn[rows, :])
        mixed = sgub_ref[...]
        for gi in range(N_HEADS):
            mixed = mixed + jnp.where(grp == gi, r[gi * SGU_CHUNK:(gi + 1) * SGU_CHUNK, :], 0.0)
        y_out[rows, 0:GROUP_WIDTH] = _rms(u[rows, :] * mixed, outg[0:1, :]).astype(BF16)

    first = (i % tiles_per_seq) == 0

    @pl.when(first)
    def _():
        pext[0:HIST, :] = jnp.zeros((HIST, GROUP_WIDTH), F32)
        zext[0:HIST, :] = jnp.zeros((HIST, GROUP_WIDTH), F32)

    @pl.when(jnp.logical_not(first))
    def _():
        pext[0:HIST, :] = pext[tm:tm + HIST, :]
        zext[0:HIST, :] = zext[tm:tm + HIST, :]

    pin = proj(1280, 1536)
    pext[HIST:, :] = pin
    pos = (i % tiles_per_seq) * tm + lax.broadcasted_iota(jnp.int32, (tm, 1), 0)
    acc = pin
    sums = []
    for k in range(1, POOL_WINDOWS[-1]):
        acc = acc + pext[HIST - k:HIST - k + tm, :]
        if k + 1 in POOL_WINDOWS:
            sums.append(acc)
    wsum = jnp.where(grp == 0, sums[0], jnp.where(grp == 1, sums[1], jnp.where(grp == 2, sums[2], sums[3])))
    win = jnp.where(grp == 0, POOL_WINDOWS[0], jnp.where(grp == 1, POOL_WINDOWS[1],
                                                         jnp.where(grp == 2, POOL_WINDOWS[2], POOL_WINDOWS[3])))
    cnt = jnp.minimum(pos + 1, win).astype(F32)
    dlt = (wsum / cnt - pin).astype(BF16)
    yc = _dot(dlt, poolw_ref[...]) * pools_ref[...]
    y_out[:, GROUP_WIDTH:2 * GROUP_WIDTH] = _rms(yc, outg[1:2, :]).astype(BF16)

    bg = proj(1536, 1792)
    z = proj(1792, 2048) * proj(2048, 2304)
    zext[HIST:, :] = z
    cw = convw_ref[...]
    conv = cw[2:3, :] * z + cw[1:2, :] * zext[HIST - 1:HIST - 1 + tm, :] + cw[0:1, :] * zext[HIST - 2:HIST - 2 + tm, :]
    y_out[:, 2 * GROUP_WIDTH:3 * GROUP_WIDTH] = _rms(bg * conv, outg[2:3, :]).astype(BF16)


def _proj_call(x, g, w, bd, cos, sa, sb, qg, kg, sgug, sguw, sgub, poolw, pools, convw, outg, *, seq, tm):
    n = x.shape[0]
    tps = seq // tm
    const = lambda shape: pl.BlockSpec(shape, lambda i: (0, 0))
    rows = lambda width: pl.BlockSpec((tm, width), lambda i: (i, 0))
    cols = lambda height: pl.BlockSpec((height, tm), lambda i: (0, i))
    tab = pl.BlockSpec((tm, LANES), lambda i: (i % tps, 0))
    return pl.pallas_call(
        functools.partial(_proj_kernel, tm=tm, tiles_per_seq=tps),
        grid=(n // tm,),
        in_specs=[rows(D_MODEL), const((1, D_MODEL)), const((D_MODEL, D_IN_PAD)), const((LANES, LANES)),
                  tab, tab, tab, const((1, LANES)), const((2, LANES)),
                  const((1, GROUP_WIDTH)), const((N_HEADS * SGU_CHUNK, SGU_CHUNK)), const((SGU_CHUNK, GROUP_WIDTH)),
                  const((GROUP_WIDTH, GROUP_WIDTH)), const((1, GROUP_WIDTH)), const((CONV_WIDTH, GROUP_WIDTH)),
                  const((3, GROUP_WIDTH))],
        out_specs=[cols(N_HEADS * LANES), pl.BlockSpec((tm // CMP_STRIDE, CMP_STRIDE * LANES), lambda i: (i, 0)),
                   rows(LANES), cols(LANES), rows(LANES), cols(LANES),
                   cols(GATE_ROWS), rows(3 * GROUP_WIDTH)],
        out_shape=[jax.ShapeDtypeStruct((N_HEADS * LANES, n), BF16),
                   jax.ShapeDtypeStruct((n // CMP_STRIDE, CMP_STRIDE * LANES), F32),
                   jax.ShapeDtypeStruct((n, LANES), BF16), jax.ShapeDtypeStruct((LANES, n), BF16),
                   jax.ShapeDtypeStruct((n, LANES), BF16), jax.ShapeDtypeStruct((LANES, n), BF16),
                   jax.ShapeDtypeStruct((GATE_ROWS, n), F32), jax.ShapeDtypeStruct((n, 3 * GROUP_WIDTH), BF16)],
        scratch_shapes=[pltpu.VMEM((tm + HIST, GROUP_WIDTH), F32), pltpu.VMEM((tm + HIST, GROUP_WIDTH), F32),
                        pltpu.VMEM((tm, D_IN_PAD), F32), pltpu.VMEM((tm, LANES), F32)],
        compiler_params=pltpu.CompilerParams(dimension_semantics=("arbitrary",), vmem_limit_bytes=VMEM_LIMIT),
        name="proj_mixers",
    )(x, g, w, bd, cos, sa, sb, qg, kg, sgug, sguw, sgub, poolw, pools, convw, outg)


def _compress_kernel(t_ref, pe_ref, w1a_ref, w1b_ref, w2_ref, kg_ref, bd_ref, cos_ref, sa_ref, sb_ref, kv_ref, vt_ref):
    t = t_ref[0]
    ncp = t.shape[0]
    a = _dot((t + pe_ref[0:1, :]).astype(BF16), w1a_ref[...])
    b = _dot((t + pe_ref[1:2, :]).astype(BF16), w1b_ref[...])
    hid = jax.nn.gelu(a + pltpu.roll(b, ncp - 1, 0), approximate=True)
    kv = _dot(hid.astype(BF16), w2_ref[...])
    first_half = lax.broadcasted_iota(jnp.int32, (1, LANES), 1) < HEAD_DIM
    kn = _head_norm_rope(kv, kg_ref[...], cos_ref[...], sa_ref[...], sb_ref[...], bd_ref[...])
    kv_ref[0] = jnp.where(first_half, kn, kv).astype(BF16)
    vt_ref[0] = jnp.where(first_half, 1.0, kv).T.astype(BF16)


def _compress_call(t2, pe, w1a, w1b, w2, kg, bd, cos, sa, sb):
    b, ncp, width = t2.shape
    const = lambda shape: pl.BlockSpec(shape, lambda i: (0,) * len(shape))
    return pl.pallas_call(
        _compress_kernel,
        grid=(b,),
        in_specs=[pl.BlockSpec((1, ncp, width), lambda i: (i, 0, 0)), const(pe.shape), const(w1a.shape),
                  const(w1b.shape), const(w2.shape), const(kg.shape), const(bd.shape),
                  const(cos.shape), const(sa.shape), const(sb.shape)],
        out_specs=[pl.BlockSpec((1, ncp, LANES), lambda i: (i, 0, 0)), pl.BlockSpec((1, LANES, ncp), lambda i: (i, 0, 0))],
        out_shape=[jax.ShapeDtypeStruct((b, ncp, LANES), BF16), jax.ShapeDtypeStruct((b, LANES, ncp), BF16)],
        compiler_params=pltpu.CompilerParams(dimension_semantics=("arbitrary",), vmem_limit_bytes=VMEM_LIMIT),
        name="nsa_compress",
    )(t2, pe, w1a, w1b, w2, kg, bd, cos, sa, sb)


def _attn_kernel(qt_ref, gatet_ref, kvc_ref, vtc_ref, kvs_ref, vts_ref, kvw_ref, vtw_ref, ovt_ref, og_ref, o_ref,
                 bias_ref, sa_ref, sb_ref, m_ref, acc_ref, oc_ref, imp_ref, sw_ref, ow_ref,
                 *, tq, tk, n_sel):
    s0 = pl.program_id(1) * tq
    nc = N_HEADS * tq
    qs = jnp.concatenate([qt_ref[h * LANES:(h + 1) * LANES, :] for h in range(N_HEADS)], axis=1)
    t_col = s0 + (lax.broadcasted_iota(jnp.int32, (1, nc), 1) & (tq - 1))
    t_q = s0 + lax.broadcasted_iota(jnp.int32, (1, tq), 1)

    ncp = kvc_ref.shape[1]
    part = CMP_PART if ncp % CMP_PART == 0 else ncp

    def cmp_branch(rows):
        cmp_end = lax.broadcasted_iota(jnp.int32, (rows, 1), 0) * CMP_STRIDE + (CMP_LEN - 1)
        sc = jnp.where(cmp_end <= t_col, _dot(kvc_ref[0, 0:rows, :], qs), NEG_INF)
        e = jnp.exp2(sc - jnp.max(sc, axis=0, keepdims=True))
        p = e * jnp.where(t_col >= CMP_LEN - 1, 1.0 / jnp.sum(e, axis=0, keepdims=True), 0.0)
        oc_ref[...] = _dot(vtc_ref[0, :, 0:rows], p.astype(BF16))[HEAD_DIM:, :]
        psum = p[:, 0:tq] + p[:, tq:2 * tq] + p[:, 2 * tq:3 * tq] + p[:, 3 * tq:4 * tq]
        hi, lo = _split(psum)
        imp_ref[...] = _dot(ovt_ref[:, 0:rows], hi) + _dot(ovt_ref[:, 0:rows], lo)

    n_parts = (s0 + tq + CMP_STRIDE * part - 1) // (CMP_STRIDE * part)
    for k in range(1, ncp // part + 1):
        pl.when(n_parts == k)(functools.partial(cmp_branch, k * part))
    o_c = oc_ref[...]
    imp = imp_ref[...]

    wk = WINDOW + tq
    w0 = pl.multiple_of(jnp.maximum(s0 - WINDOW, 0), tq)
    sw_ref[...] = _dot(kvw_ref[pl.ds(w0, wk), :], qs)

    nbp = imp.shape[0]
    blk = lax.broadcasted_iota(jnp.int32, (nbp, 1), 0)
    blk_f = blk.astype(F32)
    cur = t_q // SEL_BLOCK
    valid_b = blk <= cur
    forced = valid_b & ((blk == 0) | (blk == cur) | (blk == cur - 1))
    score = jnp.where(valid_b & jnp.logical_not(forced), imp, -1.0)
    sel = jnp.where(forced, 1.0, 0.0)
    for _ in range(n_sel - N_FORCED):
        top = jnp.max(score, axis=0, keepdims=True)
        idx = jnp.min(jnp.where(score == top, blk_f, float(nbp)), axis=0, keepdims=True)
        hit = blk_f == idx
        sel = jnp.where(hit & (top >= 0.0), 1.0, sel)
        score = jnp.where(hit, -2.0, score)
    bias = jnp.where(sel > 0.5, 0.0, NEG_INF)
    bias_ref[0:nbp, :] = jnp.concatenate([bias] * N_HEADS, axis=1)
    bias_ref[nbp:, :] = jnp.zeros((BIAS_PAD, nc), F32)

    kp = w0 + lax.broadcasted_iota(jnp.int32, (wk, 1), 0)
    sw = jnp.where((kp <= t_col) & (kp > t_col - WINDOW), sw_ref[...], NEG_INF)
    pw = jnp.exp2(sw - jnp.max(sw, axis=0, keepdims=True)).astype(BF16)
    acc_w = _dot(vtw_ref[:, pl.ds(w0, wk)], pw)
    ow_ref[...] = acc_w[HEAD_DIM:, :] / acc_w[0:1, :]

    n_blk = tk // SEL_BLOCK
    q_top = qs[0:HEAD_DIM, :]
    q_rest = jnp.zeros((LANES - HEAD_DIM - BIAS_PAD, nc), BF16)

    def scores(j):
        tile_bias = bias_ref[pl.ds(pl.multiple_of(j * n_blk, n_blk), BIAS_PAD), :].astype(BF16)
        qb = jnp.concatenate([q_top, tile_bias, q_rest], axis=0)
        return _dot(kvs_ref[pl.ds(pl.multiple_of(j * tk, tk), tk), :], qb)

    def consume(s_ref, j, causal):
        k0 = pl.multiple_of(j * tk, tk)
        s = s_ref[...]
        if causal:
            kpos = k0 + lax.broadcasted_iota(jnp.int32, (tk, 1), 0)
            s = jnp.where(kpos <= t_col, s, NEG_INF)
        m = m_ref[...]
        m_new = jnp.maximum(m, jnp.max(s, axis=0, keepdims=True))
        pe = jnp.exp2(s - m_new).astype(BF16)
        acc_ref[...] = jnp.exp2(m - m_new) * acc_ref[...] + _dot(vts_ref[:, pl.ds(k0, tk)], pe)
        m_ref[...] = m_new

    n_full = (s0 + tq + tk - 1) // tk - 1
    m_ref[...] = jnp.full((1, nc), NEG_INF, F32)
    acc_ref[...] = jnp.zeros((LANES, nc), F32)
    sa_ref[...] = scores(0)

    def pair(i, carry):
        j = 2 * i
        sb_ref[...] = scores(j + 1)
        consume(sa_ref, j, False)
        sa_ref[...] = scores(jnp.minimum(j + 2, n_full))
        consume(sb_ref, j + 1, False)
        return carry

    lax.fori_loop(0, n_full // 2, pair, 0)

    @pl.when(n_full % 2 == 1)
    def _():
        sb_ref[...] = scores(n_full)
        consume(sa_ref, n_full - 1, False)
        consume(sb_ref, n_full, True)

    @pl.when(n_full % 2 == 0)
    def _():
        consume(sa_ref, n_full, True)

    acc_s = acc_ref[...]
    o_s = acc_s[HEAD_DIM:, :] / acc_s[0:1, :]

    g = gatet_ref[...]
    heads = []
    for h in range(N_HEADS):
        c = slice(h * tq, (h + 1) * tq)
        r = h * N_BRANCH
        heads.append(g[r:r + 1, :] * o_c[:, c] + g[r + 1:r + 2, :] * o_s[:, c] + g[r + 2:r + 3, :] * ow_ref[:, c])
    ya = jnp.concatenate(heads, axis=0).T
    o_ref[...] = _rms(ya, og_ref[...]).astype(BF16)


def _attn_call(qt, gatet, kvc, vtc, kvs, vts, kvw, vtw, ovt, og, *, seq, tq, tk):
    n = qt.shape[1]
    b = n // seq
    nq = seq // tq
    ncp = kvc.shape[1]
    nbp = ovt.shape[0]
    n_sel = min(SEL_TOPK, seq // SEL_BLOCK)
    cols = lambda height: pl.BlockSpec((height, tq), lambda bi, qi: (0, bi * nq + qi))
    per_b = lambda shape: pl.BlockSpec((1,) + shape, lambda bi, qi: (bi, 0, 0))
    const = lambda shape: pl.BlockSpec(shape, lambda bi, qi: (0, 0))
    seq_rows = pl.BlockSpec((seq, LANES), lambda bi, qi: (bi, 0))
    seq_cols = pl.BlockSpec((LANES, seq), lambda bi, qi: (0, bi))
    return pl.pallas_call(
        functools.partial(_attn_kernel, tq=tq, tk=tk, n_sel=n_sel),
        grid=(b, nq),
        in_specs=[cols(N_HEADS * LANES), cols(GATE_ROWS), per_b((ncp, LANES)), per_b((LANES, ncp)),
                  seq_rows, seq_cols, seq_rows, seq_cols, const(ovt.shape), const((1, GROUP_WIDTH))],
        out_specs=pl.BlockSpec((tq, GROUP_WIDTH), lambda bi, qi: (bi * nq + qi, 0)),
        out_shape=jax.ShapeDtypeStruct((n, GROUP_WIDTH), BF16),
        scratch_shapes=[pltpu.VMEM((nbp + BIAS_PAD, N_HEADS * tq), F32), pltpu.VMEM((tk, N_HEADS * tq), F32),
                        pltpu.VMEM((tk, N_HEADS * tq), F32), pltpu.VMEM((1, N_HEADS * tq), F32),
                        pltpu.VMEM((LANES, N_HEADS * tq), F32), pltpu.VMEM((HEAD_DIM, N_HEADS * tq), F32),
                        pltpu.VMEM((nbp, tq), F32), pltpu.VMEM((WINDOW + tq, N_HEADS * tq), F32),
                        pltpu.VMEM((HEAD_DIM, N_HEADS * tq), F32)],
        compiler_params=pltpu.CompilerParams(dimension_semantics=("arbitrary", "arbitrary"),
                                             vmem_limit_bytes=VMEM_LIMIT),
        name="nsa_attention",
    )(qt, gatet, kvc, vtc, kvs, vts, kvw, vtw, ovt, og)


def _outproj_kernel(x_ref, ya_ref, yb_ref, w_ref, g_ref, *rest, moe):
    if moe:
        rw_ref, x1_ref, hn_ref, gate_ref = rest
    else:
        x1_ref, hn_ref = rest
    x1 = x_ref[...] + _dot(ya_ref[...], w_ref[0:GROUP_WIDTH, :]) + _dot(yb_ref[...], w_ref[GROUP_WIDTH:, :])
    x1_ref[...] = x1
    hn = _rms(x1, g_ref[...])
    hn_ref[...] = hn.astype(BF16)
    if moe:
        hi, lo = _split(hn)
        logits = _dot(hi, rw_ref[0]) + _dot(lo, rw_ref[0]) + _dot(hi, rw_ref[1])
        lane = lax.broadcasted_iota(jnp.int32, (1, LANES), 1)
        lane_f = lane.astype(F32)
        logits = jnp.where(lane < N_EXPERTS, logits, NEG_INF)
        m1 = jnp.max(logits, axis=-1, keepdims=True)
        i1 = jnp.min(jnp.where(logits == m1, lane_f, float(LANES)), axis=-1, keepdims=True)
        rest_l = jnp.where(lane_f == i1, 2 * NEG_INF, logits)
        m2 = jnp.max(rest_l, axis=-1, keepdims=True)
        i2 = jnp.min(jnp.where(rest_l == m2, lane_f, float(LANES)), axis=-1, keepdims=True)
        e2 = jnp.exp(m2 - m1)
        den = 1.0 + e2
        gate_ref[...] = jnp.where(lane_f == i1, 1.0 / den, jnp.where(lane_f == i2, e2 / den, 0.0))


def _outproj_call(x, ya, yb, w, g, rw, *, tm):
    n = x.shape[0]
    moe = rw is not None
    rows = lambda width: pl.BlockSpec((tm, width), lambda i: (i, 0))
    const = lambda shape: pl.BlockSpec(shape, lambda i: (0,) * len(shape))
    in_specs = [rows(D_MODEL), rows(GROUP_WIDTH), rows(3 * GROUP_WIDTH), const((D_MODEL, D_MODEL)), const((1, D_MODEL))]
    out_specs = [rows(D_MODEL), rows(D_MODEL)]
    out_shape = [jax.ShapeDtypeStruct((n, D_MODEL), F32), jax.ShapeDtypeStruct((n, D_MODEL), BF16)]
    args = [x, ya, yb, w, g]
    if moe:
        in_specs.append(const(rw.shape))
        out_specs.append(rows(LANES))
        out_shape.append(jax.ShapeDtypeStruct((n, LANES), F32))
        args.append(rw)
    return pl.pallas_call(
        functools.partial(_outproj_kernel, moe=moe),
        grid=(n // tm,),
        in_specs=in_specs, out_specs=out_specs, out_shape=out_shape,
        compiler_params=pltpu.CompilerParams(dimension_semantics=("arbitrary",), vmem_limit_bytes=VMEM_LIMIT),
        name="outproj_router" if moe else "outproj",
    )(*args)


def _ffn_kernel(h_ref, x_ref, w1_ref, w3_ref, w2_ref, o_ref, acc_ref):
    f = pl.program_id(1)

    @pl.when(f == 0)
    def _():
        acc_ref[...] = jnp.zeros_like(acc_ref)

    h = h_ref[...]
    a = _dot(h, w1_ref[...])
    t = (a * jax.nn.sigmoid(a) * _dot(h, w3_ref[...])).astype(BF16)
    acc_ref[...] += _dot(t, w2_ref[...])

    @pl.when(f == pl.num_programs(1) - 1)
    def _():
        o_ref[...] = x_ref[...] + acc_ref[...]


def _ffn_call(h, x, w1, w3, w2, *, tm, tf):
    n = h.shape[0]
    rows = pl.BlockSpec((tm, D_MODEL), lambda i, f: (i, 0))
    up = pl.BlockSpec((D_MODEL, tf), lambda i, f: (0, f))
    return pl.pallas_call(
        _ffn_kernel,
        grid=(n // tm, D_FF // tf),
        in_specs=[rows, rows, up, up, pl.BlockSpec((tf, D_MODEL), lambda i, f: (f, 0))],
        out_specs=rows,
        out_shape=jax.ShapeDtypeStruct((n, D_MODEL), F32),
        scratch_shapes=[pltpu.VMEM((tm, D_MODEL), F32)],
        compiler_params=pltpu.CompilerParams(dimension_semantics=("arbitrary", "arbitrary"),
                                             vmem_limit_bytes=VMEM_LIMIT),
        name="ffn_swiglu",
    )(h, x, w1, w3, w2)


def _moe_kernel(h_ref, x_hbm, gate_ref, w1_ref, w3_ref, w2_ref, o_ref,
                xc_ref, yc_ref, rank_ref, gate_t_ref, rank_t_ref, start_ref, x_sem, *, n_sub):
    e = pl.program_id(1)
    f = pl.program_id(2)
    rb = MOE_ROWS
    sub = MOE_SUB

    def for_chunks(c_lo, c_hi, fn):
        def pair(c, carry):
            fn(pl.multiple_of((c_lo + 2 * c) * rb, rb), 2 * rb)
            return carry

        lax.fori_loop(0, (c_hi - c_lo) // 2, pair, 0)

        @pl.when((c_hi - c_lo) % 2 == 1)
        def _():
            fn(pl.multiple_of((c_hi - 1) * rb, rb), rb)

    def sub_chunks(s):
        return start_ref[s] // rb, (start_ref[s + 1] + rb - 1) // rb

    @pl.when((e == 0) & (f == 0))
    def _():
        tm = n_sub * sub
        x_copy = pltpu.make_async_copy(x_hbm.at[pl.ds(pl.multiple_of(pl.program_id(0) * tm, tm), tm), :], o_ref, x_sem)
        x_copy.start()
        before = jnp.where(lax.broadcasted_iota(jnp.int32, (sub, sub), 1)
                           < lax.broadcasted_iota(jnp.int32, (sub, sub), 0), 1.0, 0.0).astype(BF16)
        routed_before = jnp.zeros((1, LANES), F32)
        for s in range(n_sub):
            span = slice(s * sub, (s + 1) * sub)
            gate = gate_ref[span, :]
            routed = jnp.where(gate > 0.0, 1.0, 0.0)
            rank = _dot(before, routed.astype(BF16)) + routed_before
            rank_ref[span, :] = rank
            gate_t_ref[:, span] = gate.T[0:N_EXPERTS]
            rank_t_ref[:, span] = rank.T[0:N_EXPERTS]
            routed_before = routed_before + jnp.sum(routed, axis=0, keepdims=True)
        x_copy.wait()

    @pl.when(f == 0)
    def _():
        start_ref[0] = 0
        for s in range(n_sub):
            g_row = gate_t_ref[pl.ds(e, 1), s * sub:(s + 1) * sub]
            start_ref[s + 1] = start_ref[s] + jnp.sum(jnp.where(g_row > 0.0, 1.0, 0.0)).astype(jnp.int32)

        def clear(r0, rows):
            xc_ref[pl.ds(r0, rows), :] = jnp.zeros((rows, D_MODEL), BF16)
            yc_ref[pl.ds(r0, rows), :] = jnp.zeros((rows, D_MODEL), F32)

        for_chunks(0, jnp.maximum((start_ref[n_sub] + rb - 1) // rb, MOE_STATIC_CHUNKS), clear)

        for s in range(n_sub):
            span = slice(s * sub, (s + 1) * sub)
            g_row = gate_t_ref[pl.ds(e, 1), span]
            key = jnp.where(g_row > 0.0, rank_t_ref[pl.ds(e, 1), span], -1.0)

            def gather(r0, rows, span=span, key=key):
                slot = (r0 + lax.broadcasted_iota(jnp.int32, (rows, 1), 0)).astype(F32)
                onehot = jnp.where(key == slot, 1.0, 0.0).astype(BF16)
                dst = pl.ds(r0, rows)
                xc_ref[dst, :] = (xc_ref[dst, :].astype(F32) + _dot(onehot, h_ref[span, :])).astype(BF16)

            for_chunks(*sub_chunks(s), gather)

    def swiglu(r0, rows):
        src = pl.ds(r0, rows)
        xc = xc_ref[src, :]
        a = _dot(xc, w1_ref[...])
        t = (a * jax.nn.sigmoid(a) * _dot(xc, w3_ref[...])).astype(BF16)
        yc_ref[src, :] += _dot(t, w2_ref[...])

    for c in range(0, MOE_STATIC_CHUNKS, 2):
        swiglu(c * rb, 2 * rb)
    for_chunks(MOE_STATIC_CHUNKS, jnp.maximum((start_ref[n_sub] + rb - 1) // rb, MOE_STATIC_CHUNKS), swiglu)

    @pl.when(f == pl.num_programs(2) - 1)
    def _():
        lane = lax.broadcasted_iota(jnp.int32, (1, LANES), 1)
        for s in range(n_sub):
            span = slice(s * sub, (s + 1) * sub)
            g_col = jnp.sum(jnp.where(lane == e, gate_ref[span, :], 0.0), axis=-1, keepdims=True)
            r_col = jnp.sum(jnp.where(lane == e, rank_ref[span, :], 0.0), axis=-1, keepdims=True)
            key = jnp.where(g_col > 0.0, r_col, -1.0)

            def scatter(r0, rows, span=span, key=key, g_col=g_col):
                slot = (r0 + lax.broadcasted_iota(jnp.int32, (1, rows), 1)).astype(F32)
                onehot = jnp.where(key == slot, 1.0, 0.0).astype(BF16)
                o_ref[span, :] += g_col * _dot(onehot, yc_ref[pl.ds(r0, rows), :].astype(BF16))

            for_chunks(*sub_chunks(s), scatter)


def _moe_call(h, x, gate, w1, w3, w2, *, tm, tf):
    n = h.shape[0]
    n_sub = tm // MOE_SUB
    once = pl.Buffered(1)
    rows = lambda width, mode=None: pl.BlockSpec((tm, width), lambda i, e, f: (i, 0), pipeline_mode=mode)
    return pl.pallas_call(
        functools.partial(_moe_kernel, n_sub=n_sub),
        grid=(n // tm, N_EXPERTS, D_FF // tf),
        in_specs=[rows(D_MODEL, once), pl.BlockSpec(memory_space=pl.ANY), rows(LANES),
                  pl.BlockSpec((None, D_MODEL, tf), lambda i, e, f: (e, 0, f)),
                  pl.BlockSpec((None, D_MODEL, tf), lambda i, e, f: (e, 0, f)),
                  pl.BlockSpec((None, tf, D_MODEL), lambda i, e, f: (e, f, 0))],
        out_specs=rows(D_MODEL, once),
        out_shape=jax.ShapeDtypeStruct((n, D_MODEL), F32),
        scratch_shapes=[pltpu.VMEM((tm, D_MODEL), BF16), pltpu.VMEM((tm, D_MODEL), F32), pltpu.VMEM((tm, LANES), F32),
                        pltpu.VMEM((N_EXPERTS, tm), F32), pltpu.VMEM((N_EXPERTS, tm), F32),
                        pltpu.SMEM((n_sub + 1,), jnp.int32), pltpu.SemaphoreType.DMA(())],
        compiler_params=pltpu.CompilerParams(dimension_semantics=("arbitrary", "arbitrary", "arbitrary"),
                                             vmem_limit_bytes=VMEM_LIMIT),
        name="moe_swiglu",
    )(h, x, gate, w1, w3, w2)


def _rope_tables(pos):
    half = ROPE_DIM // 2
    inv_freq = ROPE_THETA ** (-np.arange(half, dtype=np.float64) / half)
    ang = pos.astype(np.float64)[..., None] * inv_freq
    cos, sin = np.cos(ang).astype(np.float32), np.sin(ang).astype(np.float32)
    n = pos.shape[0]
    rest = HEAD_DIM - ROPE_DIM
    cos64 = np.concatenate([cos, cos, np.ones((n, rest), np.float32)], axis=-1)
    sa64 = np.concatenate([-sin, np.zeros((n, rest + half), np.float32)], axis=-1)
    sb64 = np.concatenate([np.zeros((n, half), np.float32), sin, np.zeros((n, rest), np.float32)], axis=-1)
    return tuple(jnp.asarray(np.tile(t, (1, 2)), F32) for t in (cos64, sa64, sb64))


def _block_diag(blocks):
    n = len(blocks)
    rows = []
    for i, blk in enumerate(blocks):
        rows.append(jnp.concatenate([blk if j == i else jnp.zeros((blk.shape[0], blocks[j].shape[1]), blk.dtype)
                                     for j in range(n)], axis=1))
    return jnp.concatenate(rows, axis=0)


def _layout_w_in(w):
    n_gate = N_BRANCH * N_HEADS
    g0 = GROUP_WIDTH + 6 * HEAD_DIM
    pad = jnp.zeros(w.shape[:-1] + (LANES - n_gate,), w.dtype)
    return jnp.concatenate([w[..., :g0 + n_gate], pad, w[..., g0 + n_gate:]], axis=-1)


def _layout_compress(pe, w1, w2):
    half = CMP_LEN // 2
    pe2 = jnp.transpose(pe, (1, 0, 2)).reshape(2, half * LANES)
    w1r = w1.reshape(2, 2, half, HEAD_DIM, CMP_HIDDEN)
    zeros = jnp.zeros((half, HEAD_DIM, CMP_HIDDEN), w1.dtype)

    def big(part):
        k_rows = jnp.concatenate([w1r[0, part], zeros], axis=-1)
        v_rows = jnp.concatenate([zeros, w1r[1, part]], axis=-1)
        return jnp.concatenate([k_rows, v_rows], axis=1).reshape(half * LANES, 2 * CMP_HIDDEN)

    return pe2, big(0), big(1), _block_diag([w2[0], w2[1]])


def _overlap_matrix(seq, ncp, nbp):
    cs = jnp.arange(ncp) * CMP_STRIDE
    bs = jnp.arange(nbp) * SEL_BLOCK
    ov = jnp.clip(jnp.minimum(cs[:, None] + CMP_LEN, bs[None, :] + SEL_BLOCK)
                  - jnp.maximum(cs[:, None], bs[None, :]), 0, None).astype(F32) / CMP_LEN
    keep = (jnp.arange(ncp)[:, None] < seq // CMP_STRIDE - 1) & (jnp.arange(nbp)[None, :] < seq // SEL_BLOCK)
    return jnp.where(keep, ov, 0.0)


def kernel(x, attn_norm_g, w_in, q_norm_g, k_norm_g, cmp_pe, cmp_w1, cmp_w2, sgu_norm_g, sgu_w, sgu_b, pool_w,
           pool_scale, conv_w, mix_out_norm_g, w_out, ffn_norm_g, ffn_w1, ffn_w3, ffn_w2, router_w, expert_w1,
           expert_w3, expert_w2):
    b, seq, _ = x.shape
    n = b * seq
    depth = w_in.shape[0]
    tm = TOKEN_TILE
    ncp = seq // CMP_STRIDE
    nbp = max(LANES, seq // SEL_BLOCK)

    cos, sa, sb = _rope_tables(np.arange(seq))
    cos_c, sa_c, sb_c = _rope_tables(np.arange(ncp) * CMP_STRIDE + CMP_LEN - 1)
    bd = _block_diag([jnp.full((HEAD_DIM, HEAD_DIM), 1.0 / HEAD_DIM, F32)] * 2).astype(BF16)
    overlap_t = _overlap_matrix(seq, ncp, nbp).T.astype(BF16)
    row = lambda v: v.reshape(1, -1)
    two = lambda v: jnp.tile(v, 2).reshape(1, LANES)

    w_in_b = _layout_w_in(w_in.astype(BF16))
    xf = x.reshape(n, D_MODEL)
    for layer in range(depth):
        kg = k_norm_g[layer]
        sgub = jnp.repeat(sgu_b[layer].T, HEAD_DIM, axis=1)
        qt, kvc, kvs, vts, kvw, vtw, gatet, ybcd = _proj_call(
            xf, row(attn_norm_g[layer]), w_in_b[layer], bd, cos, sa, sb,
            two(q_norm_g[layer]), jnp.stack([jnp.tile(kg[1], 2), jnp.tile(kg[2], 2)]),
            row(sgu_norm_g[layer]), sgu_w[layer].reshape(N_HEADS * SGU_CHUNK, SGU_CHUNK), sgub,
            _block_diag([pool_w[layer, i] for i in range(len(POOL_WINDOWS))]).astype(BF16),
            row(pool_scale[layer]), conv_w[layer], mix_out_norm_g[layer].reshape(4, GROUP_WIDTH)[1:],
            seq=seq, tm=tm)
        pe2, w1a, w1b, w2c = _layout_compress(cmp_pe[layer], cmp_w1[layer], cmp_w2[layer])
        kvcc, vtc = _compress_call(kvc.reshape(b, ncp, CMP_STRIDE * LANES), pe2, w1a.astype(BF16), w1b.astype(BF16),
                                   w2c.astype(BF16), two(kg[0]), bd, cos_c, sa_c, sb_c)
        ya = _attn_call(qt, gatet, kvcc, vtc, kvs, vts, kvw, vtw, overlap_t,
                        row(mix_out_norm_g[layer, :GROUP_WIDTH]), seq=seq, tq=ATTN_Q_TILE, tk=SEL_KEY_TILE)
        i = layer // 2
        if layer % 2 == 0:
            x1, hn = _outproj_call(xf, ya, ybcd, w_out[layer].astype(BF16), row(ffn_norm_g[layer]), None, tm=tm)
            xf = _ffn_call(hn, x1, ffn_w1[i].astype(BF16), ffn_w3[i].astype(BF16), ffn_w2[i].astype(BF16),
                           tm=FFN_TOKEN_TILE, tf=FFN_FF_TILE)
        else:
            r = jnp.pad(router_w[i], ((0, 0), (0, LANES - N_EXPERTS)))
            r_hi = r.astype(BF16)
            rw = jnp.stack([r_hi, (r - r_hi.astype(F32)).astype(BF16)])
            x1, hn, gate = _outproj_call(xf, ya, ybcd, w_out[layer].astype(BF16), row(ffn_norm_g[layer]), rw, tm=tm)
            xf = _moe_call(hn, x1, gate, expert_w1[i].astype(BF16), expert_w3[i].astype(BF16),
                           expert_w2[i].astype(BF16), tm=2 * MOE_SUB, tf=MOE_FF_TILE)
    return xf.reshape(b, seq, D_MODEL)
```

```python
import functools

import numpy as np
import jax
import jax.numpy as jnp
from jax import lax
from jax.experimental import pallas as pl
from jax.experimental.pallas import tpu as pltpu

D_MODEL = 1024
HEAD_DIM = 64
N_HEADS = 4
GROUP_WIDTH = 256
CMP_LEN = 32
CMP_STRIDE = 16
CMP_HIDDEN = 128
SEL_BLOCK = 64
SEL_TOPK = 16
WINDOW = 512
N_BRANCH = 3
ROPE_THETA = 500000.0
ROPE_DIM = 16
SGU_CHUNK = 128
POOL_WINDOWS = (2, 4, 8, 16)
CONV_WIDTH = 3
D_FF = 3584
N_EXPERTS = 8
EPS = 1e-6
NEG_INF = -1e30
Q_SCALE = HEAD_DIM ** -0.5 * 1.4426950408889634
CMP_PART = 128
N_FORCED = 3
SEL_KEY_TILE = 512
BIAS_PAD = 16
GATE_ROWS = 16

LANES = 128
MOE_ROWS = 128
MOE_FF_TILE = 1792
MOE_SUB = 1024
HIST = 16
D_IN_PAD = 2304
TOKEN_TILE = 512
ATTN_Q_TILE = 256
FFN_TOKEN_TILE = 512
FFN_ROWS = 128
FFN_FF_TILE = 1792
VMEM_LIMIT = 56 * 1024 * 1024

F32 = jnp.float32
BF16 = jnp.bfloat16


def _dot(a, b):
    return jnp.dot(a, b, preferred_element_type=F32)


def _split(a):
    hi = a.astype(BF16)
    lo = (a - hi.astype(F32)).astype(BF16)
    return hi, lo


def _rms(y, g):
    return y * lax.rsqrt(jnp.mean(y * y, axis=-1, keepdims=True) + EPS) * g


def _head_norm_rope(c, g, cos, sa, sb, bd):
    hi, lo = _split(c * c)
    msq = _dot(hi, bd) + _dot(lo, bd)
    cn = c * lax.rsqrt(msq + EPS) * g
    return cn * cos + pltpu.roll(cn, LANES - ROPE_DIM // 2, 1) * sa + pltpu.roll(cn, ROPE_DIM // 2, 1) * sb


def _proj_kernel(x_ref, g_ref, w_ref, bd_ref, cos_ref, sa_ref, sb_ref, qg_ref, kg_ref,
                 sgug_ref, sguw_ref, sgub_ref, poolw_ref, pools_ref, convw_ref, outg_ref,
                 qt_out, kvc_out, kvs_out, vts_out, kvw_out, vtw_out, gatet_out, y_out, pext, zext, p_ref, kc_ref,
                 *, tm, tiles_per_seq):
    i = pl.program_id(0)
    x = x_ref[...]
    hn = _rms(x, g_ref[...]).astype(BF16)
    bd = bd_ref[...]
    cos, sa, sb = cos_ref[...], sa_ref[...], sb_ref[...]
    lane = lax.broadcasted_iota(jnp.int32, (1, LANES), 1)
    first_half = lane < HEAD_DIM

    for c0 in range(0, D_IN_PAD, 2 * LANES):
        p_ref[:, c0:c0 + 2 * LANES] = _dot(hn, w_ref[:, c0:c0 + 2 * LANES])

    def proj(c0, c1):
        return p_ref[:, c0:c1]

    pq = proj(0, 256)
    for c in range(2):
        qc = _head_norm_rope(pq[:, c * LANES:(c + 1) * LANES], qg_ref[...], cos, sa, sb, bd) * Q_SCALE
        qt_out[(2 * c) * LANES:(2 * c + 1) * LANES, :] = jnp.where(first_half, qc, 0.0).T.astype(BF16)
        qt_out[(2 * c + 1) * LANES:(2 * c + 2) * LANES, :] = jnp.where(
            first_half, pltpu.roll(qc, HEAD_DIM, 1), 0.0).T.astype(BF16)

    pk = proj(256, 512)
    kc_ref[...] = pk[:, 0:LANES]
    for r in range(CMP_STRIDE):
        kvc_out[:, r * LANES:(r + 1) * LANES] = kc_ref[pl.ds(r, tm // CMP_STRIDE, stride=CMP_STRIDE), :]
    ks = pk[:, LANES:2 * LANES]
    tok = (i % tiles_per_seq) * tm + lax.broadcasted_iota(jnp.int32, (tm, 1), 0)
    blk_in_tile = (tok // SEL_BLOCK) % (SEL_KEY_TILE // SEL_BLOCK)
    kvs_out[...] = jnp.where(first_half, _head_norm_rope(ks, kg_ref[0:1, :], cos, sa, sb, bd),
                             jnp.where(lane == HEAD_DIM + blk_in_tile, 1.0, 0.0)).astype(BF16)
    vts_out[...] = jnp.where(first_half, 1.0, ks).T.astype(BF16)
    pk = proj(512, 768)
    kw = pk[:, 0:LANES]
    kvw_out[...] = jnp.where(first_half, _head_norm_rope(kw, kg_ref[1:2, :], cos, sa, sb, bd), kw).astype(BF16)
    vtw_out[...] = jnp.where(first_half, 1.0, kw).T.astype(BF16)
    gatet_out[...] = jax.nn.sigmoid(pk[:, LANES:2 * LANES]).T[0:GATE_ROWS, :]

    outg = outg_ref[...]
    lane2 = lax.broadcasted_iota(jnp.int32, (1, GROUP_WIDTH), 1)
    grp = lane2 // HEAD_DIM

    u = proj(768, 1024)
    vn = _rms(proj(1024, 1280), sgug_ref[...]).astype(BF16)
    r_i = lax.broadcasted_iota(jnp.int32, (N_HEADS * SGU_CHUNK, SGU_CHUNK), 0) & (SGU_CHUNK - 1)
    c_i = lax.broadcasted_iota(jnp.int32, (N_HEADS * SGU_CHUNK, SGU_CHUNK), 1)
    wm = jnp.where(c_i <= r_i, sguw_ref[...], 0.0).astype(BF16)
    for c in range(tm // SGU_CHUNK):
        rows = slice(c * SGU_CHUNK, (c + 1) * SGU_CHUNK)
        r = _dot(wm, vn[rows, :])
        mixed = sgub_ref[...]
        for gi in range(N_HEADS):
            mixed = mixed + jnp.where(grp == gi, r[gi * SGU_CHUNK:(gi + 1) * SGU_CHUNK, :], 0.0)
        y_out[rows, 0:GROUP_WIDTH] = _rms(u[rows, :] * mixed, outg[0:1, :]).astype(BF16)

    first = (i % tiles_per_seq) == 0

    @pl.when(first)
    def _():
        pext[0:HIST, :] = jnp.zeros((HIST, GROUP_WIDTH), F32)
        zext[0:HIST, :] = jnp.zeros((HIST, GROUP_WIDTH), F32)

    @pl.when(jnp.logical_not(first))
    def _():
        pext[0:HIST, :] = pext[tm:tm + HIST, :]
        zext[0:HIST, :] = zext[tm:tm + HIST, :]

    pin = proj(1280, 1536)
    pext[HIST:, :] = pin
    pos = (i % tiles_per_seq) * tm + lax.broadcasted_iota(jnp.int32, (tm, 1), 0)
    acc = pin
    sums = []
    for k in range(1, POOL_WINDOWS[-1]):
        acc = acc + pext[HIST - k:HIST - k + tm, :]
        if k + 1 in POOL_WINDOWS:
            sums.append(acc)
    wsum = jnp.where(grp == 0, sums[0], jnp.where(grp == 1, sums[1], jnp.where(grp == 2, sums[2], sums[3])))
    win = jnp.where(grp == 0, POOL_WINDOWS[0], jnp.where(grp == 1, POOL_WINDOWS[1],
                                                         jnp.where(grp == 2, POOL_WINDOWS[2], POOL_WINDOWS[3])))
    cnt = jnp.minimum(pos + 1, win).astype(F32)
    dlt = (wsum / cnt - pin).astype(BF16)
    yc = _dot(dlt, poolw_ref[...]) * pools_ref[...]
    y_out[:, GROUP_WIDTH:2 * GROUP_WIDTH] = _rms(yc, outg[1:2, :]).astype(BF16)

    bg = proj(1536, 1792)
    z = proj(1792, 2048) * proj(2048, 2304)
    zext[HIST:, :] = z
    cw = convw_ref[...]
    conv = cw[2:3, :] * z + cw[1:2, :] * zext[HIST - 1:HIST - 1 + tm, :] + cw[0:1, :] * zext[HIST - 2:HIST - 2 + tm, :]
    y_out[:, 2 * GROUP_WIDTH:3 * GROUP_WIDTH] = _rms(bg * conv, outg[2:3, :]).astype(BF16)


def _proj_call(x, g, w, bd, cos, sa, sb, qg, kg, sgug, sguw, sgub, poolw, pools, convw, outg, *, seq, tm):
    n = x.shape[0]
    tps = seq // tm
    const = lambda shape: pl.BlockSpec(shape, lambda i: (0, 0))
    rows = lambda width: pl.BlockSpec((tm, width), lambda i: (i, 0))
    cols = lambda height: pl.BlockSpec((height, tm), lambda i: (0, i))
    tab = pl.BlockSpec((tm, LANES), lambda i: (i % tps, 0))
    return pl.pallas_call(
        functools.partial(_proj_kernel, tm=tm, tiles_per_seq=tps),
        grid=(n // tm,),
        in_specs=[rows(D_MODEL), const((1, D_MODEL)), const((D_MODEL, D_IN_PAD)), const((LANES, LANES)),
                  tab, tab, tab, const((1, LANES)), const((2, LANES)),
                  const((1, GROUP_WIDTH)), const((N_HEADS * SGU_CHUNK, SGU_CHUNK)), const((SGU_CHUNK, GROUP_WIDTH)),
                  const((GROUP_WIDTH, GROUP_WIDTH)), const((1, GROUP_WIDTH)), const((CONV_WIDTH, GROUP_WIDTH)),
                  const((3, GROUP_WIDTH))],
        out_specs=[cols(N_HEADS * LANES), pl.BlockSpec((tm // CMP_STRIDE, CMP_STRIDE * LANES), lambda i: (i, 0)),
                   rows(LANES), cols(LANES), rows(LANES), cols(LANES),
                   cols(GATE_ROWS), rows(3 * GROUP_WIDTH)],
        out_shape=[jax.ShapeDtypeStruct((N_HEADS * LANES, n), BF16),
                   jax.ShapeDtypeStruct((n // CMP_STRIDE, CMP_STRIDE * LANES), F32),
                   jax.ShapeDtypeStruct((n, LANES), BF16), jax.ShapeDtypeStruct((LANES, n), BF16),
                   jax.ShapeDtypeStruct((n, LANES), BF16), jax.ShapeDtypeStruct((LANES, n), BF16),
                   jax.ShapeDtypeStruct((GATE_ROWS, n), F32), jax.ShapeDtypeStruct((n, 3 * GROUP_WIDTH), BF16)],
        scratch_shapes=[pltpu.VMEM((tm + HIST, GROUP_WIDTH), F32), pltpu.VMEM((tm + HIST, GROUP_WIDTH), F32),
                        pltpu.VMEM((tm, D_IN_PAD), F32), pltpu.VMEM((tm, LANES), F32)],
        compiler_params=pltpu.CompilerParams(dimension_semantics=("arbitrary",), vmem_limit_bytes=VMEM_LIMIT),
        name="proj_mixers",
    )(x, g, w, bd, cos, sa, sb, qg, kg, sgug, sguw, sgub, poolw, pools, convw, outg)


def _compress_kernel(t_ref, pe_ref, w1a_ref, w1b_ref, w2_ref, kg_ref, bd_ref, cos_ref, sa_ref, sb_ref, kv_ref, vt_ref):
    t = t_ref[0]
    ncp = t.shape[0]
    a = _dot((t + pe_ref[0:1, :]).astype(BF16), w1a_ref[...])
    b = _dot((t + pe_ref[1:2, :]).astype(BF16), w1b_ref[...])
    hid = jax.nn.gelu(a + pltpu.roll(b, ncp - 1, 0), approximate=True)
    kv = _dot(hid.astype(BF16), w2_ref[...])
    first_half = lax.broadcasted_iota(jnp.int32, (1, LANES), 1) < HEAD_DIM
    kn = _head_norm_rope(kv, kg_ref[...], cos_ref[...], sa_ref[...], sb_ref[...], bd_ref[...])
    kv_ref[0] = jnp.where(first_half, kn, kv).astype(BF16)
    vt_ref[0] = jnp.where(first_half, 1.0, kv).T.astype(BF16)


def _compress_call(t2, pe, w1a, w1b, w2, kg, bd, cos, sa, sb):
    b, ncp, width = t2.shape
    const = lambda shape: pl.BlockSpec(shape, lambda i: (0,) * len(shape))
    return pl.pallas_call(
        _compress_kernel,
        grid=(b,),
        in_specs=[pl.BlockSpec((1, ncp, width), lambda i: (i, 0, 0)), const(pe.shape), const(w1a.shape),
                  const(w1b.shape), const(w2.shape), const(kg.shape), const(bd.shape),
                  const(cos.shape), const(sa.shape), const(sb.shape)],
        out_specs=[pl.BlockSpec((1, ncp, LANES), lambda i: (i, 0, 0)), pl.BlockSpec((1, LANES, ncp), lambda i: (i, 0, 0))],
        out_shape=[jax.ShapeDtypeStruct((b, ncp, LANES), BF16), jax.ShapeDtypeStruct((b, LANES, ncp), BF16)],
        compiler_params=pltpu.CompilerParams(dimension_semantics=("arbitrary",), vmem_limit_bytes=VMEM_LIMIT),
        name="nsa_compress",
    )(t2, pe, w1a, w1b, w2, kg, bd, cos, sa, sb)


def _attn_kernel(qt_ref, gatet_ref, kvc_ref, vtc_ref, kvs_ref, vts_ref, kvw_ref, vtw_ref, ovt_ref, og_ref, o_ref,
                 bias_ref, sa_ref, sb_ref, m_ref, acc_ref, oc_ref, imp_ref, sw_ref, ow_ref,
                 *, tq, tk, n_sel):
    s0 = pl.program_id(1) * tq
    nc = N_HEADS * tq
    qs = jnp.concatenate([qt_ref[h * LANES:(h + 1) * LANES, :] for h in range(N_HEADS)], axis=1)
    t_col = s0 + (lax.broadcasted_iota(jnp.int32, (1, nc), 1) & (tq - 1))
    t_q = s0 + lax.broadcasted_iota(jnp.int32, (1, tq), 1)

    ncp = kvc_ref.shape[1]
    part = CMP_PART if ncp % CMP_PART == 0 else ncp

    def cmp_branch(rows):
        cmp_end = lax.broadcasted_iota(jnp.int32, (rows, 1), 0) * CMP_STRIDE + (CMP_LEN - 1)
        sc = jnp.where(cmp_end <= t_col, _dot(kvc_ref[0, 0:rows, :], qs), NEG_INF)
        e = jnp.exp2(sc - jnp.max(sc, axis=0, keepdims=True))
        p = e * jnp.where(t_col >= CMP_LEN - 1, 1.0 / jnp.sum(e, axis=0, keepdims=True), 0.0)
        oc_ref[...] = _dot(vtc_ref[0, :, 0:rows], p.astype(BF16))[HEAD_DIM:, :]
        psum = p[:, 0:tq] + p[:, tq:2 * tq] + p[:, 2 * tq:3 * tq] + p[:, 3 * tq:4 * tq]
        hi, lo = _split(psum)
        imp_ref[...] = _dot(ovt_ref[:, 0:rows], hi) + _dot(ovt_ref[:, 0:rows], lo)

    n_parts = (s0 + tq + CMP_STRIDE * part - 1) // (CMP_STRIDE * part)
    for k in range(1, ncp // part + 1):
        pl.when(n_parts == k)(functools.partial(cmp_branch, k * part))
    o_c = oc_ref[...]
    imp = imp_ref[...]

    wk = WINDOW + tq
    w0 = pl.multiple_of(jnp.maximum(s0 - WINDOW, 0), tq)
    sw_ref[...] = _dot(kvw_ref[pl.ds(w0, wk), :], qs)

    nbp = imp.shape[0]
    blk = lax.broadcasted_iota(jnp.int32, (nbp, 1), 0)
    blk_f = blk.astype(F32)
    cur = t_q // SEL_BLOCK
    valid_b = blk <= cur
    forced = valid_b & ((blk == 0) | (blk == cur) | (blk == cur - 1))
    score = jnp.where(valid_b & jnp.logical_not(forced), imp, -1.0)
    sel = jnp.where(forced, 1.0, 0.0)
    for _ in range(n_sel - N_FORCED):
        top = jnp.max(score, axis=0, keepdims=True)
        idx = jnp.min(jnp.where(score == top, blk_f, float(nbp)), axis=0, keepdims=True)
        hit = blk_f == idx
        sel = jnp.where(hit & (top >= 0.0), 1.0, sel)
        score = jnp.where(hit, -2.0, score)
    bias = jnp.where(sel > 0.5, 0.0, NEG_INF)
    bias_ref[0:nbp, :] = jnp.concatenate([bias] * N_HEADS, axis=1)
    bias_ref[nbp:, :] = jnp.zeros((BIAS_PAD, nc), F32)

    kp = w0 + lax.broadcasted_iota(jnp.int32, (wk, 1), 0)
    sw = jnp.where((kp <= t_col) & (kp > t_col - WINDOW), sw_ref[...], NEG_INF)
    pw = jnp.exp2(sw - jnp.max(sw, axis=0, keepdims=True)).astype(BF16)
    acc_w = _dot(vtw_ref[:, pl.ds(w0, wk)], pw)
    ow_ref[...] = acc_w[HEAD_DIM:, :] / acc_w[0:1, :]

    n_blk = tk // SEL_BLOCK
    q_top = qs[0:HEAD_DIM, :]
    q_rest = jnp.zeros((LANES - HEAD_DIM - BIAS_PAD, nc), BF16)

    def scores(j):
        tile_bias = bias_ref[pl.ds(pl.multiple_of(j * n_blk, n_blk), BIAS_PAD), :].astype(BF16)
        qb = jnp.concatenate([q_top, tile_bias, q_rest], axis=0)
        return _dot(kvs_ref[pl.ds(pl.multiple_of(j * tk, tk), tk), :], qb)

    def consume(s_ref, j, causal):
        k0 = pl.multiple_of(j * tk, tk)
        s = s_ref[...]
        if causal:
            kpos = k0 + lax.broadcasted_iota(jnp.int32, (tk, 1), 0)
            s = jnp.where(kpos <= t_col, s, NEG_INF)
        m = m_ref[...]
        m_new = jnp.maximum(m, jnp.max(s, axis=0, keepdims=True))
        pe = jnp.exp2(s - m_new).astype(BF16)
        acc_ref[...] = jnp.exp2(m - m_new) * acc_ref[...] + _dot(vts_ref[:, pl.ds(k0, tk)], pe)
        m_ref[...] = m_new

    n_full = (s0 + tq + tk - 1) // tk - 1
    m_ref[...] = jnp.full((1, nc), NEG_INF, F32)
    acc_ref[...] = jnp.zeros((LANES, nc), F32)
    sa_ref[...] = scores(0)

    def pair(i, carry):
        j = 2 * i
        sb_ref[...] = scores(j + 1)
        consume(sa_ref, j, False)
        sa_ref[...] = scores(jnp.minimum(j + 2, n_full))
        consume(sb_ref, j + 1, False)
        return carry

    lax.fori_loop(0, n_full // 2, pair, 0)

    @pl.when(n_full % 2 == 1)
    def _():
        sb_ref[...] = scores(n_full)
        consume(sa_ref, n_full - 1, False)
        consume(sb_ref, n_full, True)

    @pl.when(n_full % 2 == 0)
    def _():
        consume(sa_ref, n_full, True)

    acc_s = acc_ref[...]
    o_s = acc_s[HEAD_DIM:, :] / acc_s[0:1, :]

    g = gatet_ref[...]
    heads = []
    for h in range(N_HEADS):
        c = slice(h * tq, (h + 1) * tq)
        r = h * N_BRANCH
        heads.append(g[r:r + 1, :] * o_c[:, c] + g[r + 1:r + 2, :] * o_s[:, c] + g[r + 2:r + 3, :] * ow_ref[:, c])
    ya = jnp.concatenate(heads, axis=0).T
    o_ref[...] = _rms(ya, og_ref[...]).astype(BF16)


def _attn_call(qt, gatet, kvc, vtc, kvs, vts, kvw, vtw, ovt, og, *, seq, tq, tk):
    n = qt.shape[1]
    b = n // seq
    nq = seq // tq
    ncp = kvc.shape[1]
    nbp = ovt.shape[0]
    n_sel = min(SEL_TOPK, seq // SEL_BLOCK)
    cols = lambda height: pl.BlockSpec((height, tq), lambda bi, qi: (0, bi * nq + qi))
    per_b = lambda shape: pl.BlockSpec((1,) + shape, lambda bi, qi: (bi, 0, 0))
    const = lambda shape: pl.BlockSpec(shape, lambda bi, qi: (0, 0))
    seq_rows = pl.BlockSpec((seq, LANES), lambda bi, qi: (bi, 0))
    seq_cols = pl.BlockSpec((LANES, seq), lambda bi, qi: (0, bi))
    return pl.pallas_call(
        functools.partial(_attn_kernel, tq=tq, tk=tk, n_sel=n_sel),
        grid=(b, nq),
        in_specs=[cols(N_HEADS * LANES), cols(GATE_ROWS), per_b((ncp, LANES)), per_b((LANES, ncp)),
                  seq_rows, seq_cols, seq_rows, seq_cols, const(ovt.shape), const((1, GROUP_WIDTH))],
        out_specs=pl.BlockSpec((tq, GROUP_WIDTH), lambda bi, qi: (bi * nq + qi, 0)),
        out_shape=jax.ShapeDtypeStruct((n, GROUP_WIDTH), BF16),
        scratch_shapes=[pltpu.VMEM((nbp + BIAS_PAD, N_HEADS * tq), F32), pltpu.VMEM((tk, N_HEADS * tq), F32),
                        pltpu.VMEM((tk, N_HEADS * tq), F32), pltpu.VMEM((1, N_HEADS * tq), F32),
                        pltpu.VMEM((LANES, N_HEADS * tq), F32), pltpu.VMEM((HEAD_DIM, N_HEADS * tq), F32),
                        pltpu.VMEM((nbp, tq), F32), pltpu.VMEM((WINDOW + tq, N_HEADS * tq), F32),
                        pltpu.VMEM((HEAD_DIM, N_HEADS * tq), F32)],
        compiler_params=pltpu.CompilerParams(dimension_semantics=("arbitrary", "arbitrary"),
                                             vmem_limit_bytes=VMEM_LIMIT),
        name="nsa_attention",
    )(qt, gatet, kvc, vtc, kvs, vts, kvw, vtw, ovt, og)


def _outproj_kernel(x_ref, ya_ref, yb_ref, w_ref, g_ref, *rest, moe):
    if moe:
        rw_ref, x1_ref, hn_ref, gate_ref = rest
    else:
        x1_ref, hn_ref = rest
    x1 = x_ref[...] + _dot(ya_ref[...], w_ref[0:GROUP_WIDTH, :]) + _dot(yb_ref[...], w_ref[GROUP_WIDTH:, :])
    x1_ref[...] = x1
    hn = _rms(x1, g_ref[...])
    hn_ref[...] = hn.astype(BF16)
    if moe:
        hi, lo = _split(hn)
        logits = _dot(hi, rw_ref[0]) + _dot(lo, rw_ref[0]) + _dot(hi, rw_ref[1])
        lane = lax.broadcasted_iota(jnp.int32, (1, LANES), 1)
        lane_f = lane.astype(F32)
        logits = jnp.where(lane < N_EXPERTS, logits, NEG_INF)
        m1 = jnp.max(logits, axis=-1, keepdims=True)
        i1 = jnp.min(jnp.where(logits == m1, lane_f, float(LANES)), axis=-1, keepdims=True)
        rest_l = jnp.where(lane_f == i1, 2 * NEG_INF, logits)
        m2 = jnp.max(rest_l, axis=-1, keepdims=True)
        i2 = jnp.min(jnp.where(rest_l == m2, lane_f, float(LANES)), axis=-1, keepdims=True)
        e2 = jnp.exp(m2 - m1)
        den = 1.0 + e2
        gate_ref[...] = jnp.where(lane_f == i1, 1.0 / den, jnp.where(lane_f == i2, e2 / den, 0.0))


def _outproj_call(x, ya, yb, w, g, rw, *, tm):
    n = x.shape[0]
    moe = rw is not None
    rows = lambda width: pl.BlockSpec((tm, width), lambda i: (i, 0))
    const = lambda shape: pl.BlockSpec(shape, lambda i: (0,) * len(shape))
    in_specs = [rows(D_MODEL), rows(GROUP_WIDTH), rows(3 * GROUP_WIDTH), const((D_MODEL, D_MODEL)), const((1, D_MODEL))]
    out_specs = [rows(D_MODEL), rows(D_MODEL)]
    out_shape = [jax.ShapeDtypeStruct((n, D_MODEL), F32), jax.ShapeDtypeStruct((n, D_MODEL), BF16)]
    args = [x, ya, yb, w, g]
    if moe:
        in_specs.append(const(rw.shape))
        out_specs.append(rows(LANES))
        out_shape.append(jax.ShapeDtypeStruct((n, LANES), F32))
        args.append(rw)
    return pl.pallas_call(
        functools.partial(_outproj_kernel, moe=moe),
        grid=(n // tm,),
        in_specs=in_specs, out_specs=out_specs, out_shape=out_shape,
        compiler_params=pltpu.CompilerParams(dimension_semantics=("arbitrary",), vmem_limit_bytes=VMEM_LIMIT),
        name="outproj_router" if moe else "outproj",
    )(*args)


def _ffn_kernel(h_ref, x_ref, w1_ref, w3_ref, w2_ref, o_ref, acc_ref):
    f = pl.program_id(1)

    @pl.when(f == 0)
    def _():
        acc_ref[...] = jnp.zeros_like(acc_ref)

    def rows(c, carry):
        r = pl.ds(pl.multiple_of(c * FFN_ROWS, FFN_ROWS), FFN_ROWS)
        h = h_ref[r, :]
        a = _dot(h, w1_ref[...])
        t = (a * jax.nn.sigmoid(a) * _dot(h, w3_ref[...])).astype(BF16)
        acc_ref[r, :] += _dot(t, w2_ref[...])
        return carry

    lax.fori_loop(0, h_ref.shape[0] // FFN_ROWS, rows, 0)

    @pl.when(f == pl.num_programs(1) - 1)
    def _():
        o_ref[...] = x_ref[...] + acc_ref[...]


def _ffn_call(h, x, w1, w3, w2, *, tm, tf):
    n = h.shape[0]
    rows = pl.BlockSpec((tm, D_MODEL), lambda i, f: (i, 0))
    up = pl.BlockSpec((D_MODEL, tf), lambda i, f: (0, f))
    return pl.pallas_call(
        _ffn_kernel,
        grid=(n // tm, D_FF // tf),
        in_specs=[rows, rows, up, up, pl.BlockSpec((tf, D_MODEL), lambda i, f: (f, 0))],
        out_specs=rows,
        out_shape=jax.ShapeDtypeStruct((n, D_MODEL), F32),
        scratch_shapes=[pltpu.VMEM((tm, D_MODEL), F32)],
        compiler_params=pltpu.CompilerParams(dimension_semantics=("arbitrary", "arbitrary"),
                                             vmem_limit_bytes=VMEM_LIMIT),
        name="ffn_swiglu",
    )(h, x, w1, w3, w2)


def _moe_kernel(h_ref, x_hbm, gate_ref, w1_ref, w3_ref, w2_ref, o_ref,
                xc_ref, yc_ref, rank_ref, gate_t_ref, rank_t_ref, start_ref, x_sem, *, n_sub):
    e = pl.program_id(1)
    f = pl.program_id(2)
    rb = MOE_ROWS
    sub = MOE_SUB

    def for_chunks(c_lo, c_hi, fn):
        def pair(c, carry):
            fn(pl.multiple_of((c_lo + 2 * c) * rb, rb), 2 * rb)
            return carry

        lax.fori_loop(0, (c_hi - c_lo) // 2, pair, 0)

        @pl.when((c_hi - c_lo) % 2 == 1)
        def _():
            fn(pl.multiple_of((c_hi - 1) * rb, rb), rb)

    def sub_chunks(s):
        return start_ref[s] // rb, (start_ref[s + 1] + rb - 1) // rb

    @pl.when((e == 0) & (f == 0))
    def _():
        tm = n_sub * sub
        x_copy = pltpu.make_async_copy(x_hbm.at[pl.ds(pl.multiple_of(pl.program_id(0) * tm, tm), tm), :], o_ref, x_sem)
        x_copy.start()
        before = jnp.where(lax.broadcasted_iota(jnp.int32, (sub, sub), 1)
                           < lax.broadcasted_iota(jnp.int32, (sub, sub), 0), 1.0, 0.0).astype(BF16)
        routed_before = jnp.zeros((1, LANES), F32)
        for s in range(n_sub):
            span = slice(s * sub, (s + 1) * sub)
            gate = gate_ref[span, :]
            routed = jnp.where(gate > 0.0, 1.0, 0.0)
            rank = _dot(before, routed.astype(BF16)) + routed_before
            rank_ref[span, :] = rank
            gate_t_ref[:, span] = gate.T[0:N_EXPERTS]
            rank_t_ref[:, span] = rank.T[0:N_EXPERTS]
            routed_before = routed_before + jnp.sum(routed, axis=0, keepdims=True)
        x_copy.wait()

    @pl.when(f == 0)
    def _():
        start_ref[0] = 0
        for s in range(n_sub):
            g_row = gate_t_ref[pl.ds(e, 1), s * sub:(s + 1) * sub]
            start_ref[s + 1] = start_ref[s] + jnp.sum(jnp.where(g_row > 0.0, 1.0, 0.0)).astype(jnp.int32)

        def clear(r0, rows):
            xc_ref[pl.ds(r0, rows), :] = jnp.zeros((rows, D_MODEL), BF16)
            yc_ref[pl.ds(r0, rows), :] = jnp.zeros((rows, D_MODEL), F32)

        for_chunks(0, (start_ref[n_sub] + rb - 1) // rb, clear)

        for s in range(n_sub):
            span = slice(s * sub, (s + 1) * sub)
            g_row = gate_t_ref[pl.ds(e, 1), span]
            key = jnp.where(g_row > 0.0, rank_t_ref[pl.ds(e, 1), span], -1.0)

            def gather(r0, rows, span=span, key=key):
                slot = (r0 + lax.broadcasted_iota(jnp.int32, (rows, 1), 0)).astype(F32)
                onehot = jnp.where(key == slot, 1.0, 0.0).astype(BF16)
                dst = pl.ds(r0, rows)
                xc_ref[dst, :] = (xc_ref[dst, :].astype(F32) + _dot(onehot, h_ref[span, :])).astype(BF16)

            for_chunks(*sub_chunks(s), gather)

    def swiglu(r0, rows):
        src = pl.ds(r0, rows)
        xc = xc_ref[src, :]
        a = _dot(xc, w1_ref[...])
        t = (a * jax.nn.sigmoid(a) * _dot(xc, w3_ref[...])).astype(BF16)
        yc_ref[src, :] += _dot(t, w2_ref[...])

    for_chunks(0, (start_ref[n_sub] + rb - 1) // rb, swiglu)

    @pl.when(f == pl.num_programs(2) - 1)
    def _():
        lane = lax.broadcasted_iota(jnp.int32, (1, LANES), 1)
        for s in range(n_sub):
            span = slice(s * sub, (s + 1) * sub)
            g_col = jnp.sum(jnp.where(lane == e, gate_ref[span, :], 0.0), axis=-1, keepdims=True)
            r_col = jnp.sum(jnp.where(lane == e, rank_ref[span, :], 0.0), axis=-1, keepdims=True)
            key = jnp.where(g_col > 0.0, r_col, -1.0)

            def scatter(r0, rows, span=span, key=key, g_col=g_col):
                slot = (r0 + lax.broadcasted_iota(jnp.int32, (1, rows), 1)).astype(F32)
                onehot = jnp.where(key == slot, 1.0, 0.0).astype(BF16)
                o_ref[span, :] += g_col * _dot(onehot, yc_ref[pl.ds(r0, rows), :].astype(BF16))

            for_chunks(*sub_chunks(s), scatter)


def _moe_call(h, x, gate, w1, w3, w2, *, tm, tf):
    n = h.shape[0]
    n_sub = tm // MOE_SUB
    once = pl.Buffered(1)
    rows = lambda width, mode=None: pl.BlockSpec((tm, width), lambda i, e, f: (i, 0), pipeline_mode=mode)
    return pl.pallas_call(
        functools.partial(_moe_kernel, n_sub=n_sub),
        grid=(n // tm, N_EXPERTS, D_FF // tf),
        in_specs=[rows(D_MODEL, once), pl.BlockSpec(memory_space=pl.ANY), rows(LANES),
                  pl.BlockSpec((None, D_MODEL, tf), lambda i, e, f: (e, 0, f)),
                  pl.BlockSpec((None, D_MODEL, tf), lambda i, e, f: (e, 0, f)),
                  pl.BlockSpec((None, tf, D_MODEL), lambda i, e, f: (e, f, 0))],
        out_specs=rows(D_MODEL, once),
        out_shape=jax.ShapeDtypeStruct((n, D_MODEL), F32),
        scratch_shapes=[pltpu.VMEM((tm, D_MODEL), BF16), pltpu.VMEM((tm, D_MODEL), F32), pltpu.VMEM((tm, LANES), F32),
                        pltpu.VMEM((N_EXPERTS, tm), F32), pltpu.VMEM((N_EXPERTS, tm), F32),
                        pltpu.SMEM((n_sub + 1,), jnp.int32), pltpu.SemaphoreType.DMA(())],
        compiler_params=pltpu.CompilerParams(dimension_semantics=("arbitrary", "arbitrary", "arbitrary"),
                                             vmem_limit_bytes=VMEM_LIMIT),
        name="moe_swiglu",
    )(h, x, gate, w1, w3, w2)


def _rope_tables(pos):
    half = ROPE_DIM // 2
    inv_freq = ROPE_THETA ** (-np.arange(half, dtype=np.float64) / half)
    ang = pos.astype(np.float64)[..., None] * inv_freq
    cos, sin = np.cos(ang).astype(np.float32), np.sin(ang).astype(np.float32)
    n = pos.shape[0]
    rest = HEAD_DIM - ROPE_DIM
    cos64 = np.concatenate([cos, cos, np.ones((n, rest), np.float32)], axis=-1)
    sa64 = np.concatenate([-sin, np.zeros((n, rest + half), np.float32)], axis=-1)
    sb64 = np.concatenate([np.zeros((n, half), np.float32), sin, np.zeros((n, rest), np.float32)], axis=-1)
    return tuple(jnp.asarray(np.tile(t, (1, 2)), F32) for t in (cos64, sa64, sb64))


def _block_diag(blocks):
    n = len(blocks)
    rows = []
    for i, blk in enumerate(blocks):
        rows.append(jnp.concatenate([blk if j == i else jnp.zeros((blk.shape[0], blocks[j].shape[1]), blk.dtype)
                                     for j in range(n)], axis=1))
    return jnp.concatenate(rows, axis=0)


def _layout_w_in(w):
    n_gate = N_BRANCH * N_HEADS
    g0 = GROUP_WIDTH + 6 * HEAD_DIM
    pad = jnp.zeros(w.shape[:-1] + (LANES - n_gate,), w.dtype)
    return jnp.concatenate([w[..., :g0 + n_gate], pad, w[..., g0 + n_gate:]], axis=-1)


def _layout_compress(pe, w1, w2):
    half = CMP_LEN // 2
    pe2 = jnp.transpose(pe, (1, 0, 2)).reshape(2, half * LANES)
    w1r = w1.reshape(2, 2, half, HEAD_DIM, CMP_HIDDEN)
    zeros = jnp.zeros((half, HEAD_DIM, CMP_HIDDEN), w1.dtype)

    def big(part):
        k_rows = jnp.concatenate([w1r[0, part], zeros], axis=-1)
        v_rows = jnp.concatenate([zeros, w1r[1, part]], axis=-1)
        return jnp.concatenate([k_rows, v_rows], axis=1).reshape(half * LANES, 2 * CMP_HIDDEN)

    return pe2, big(0), big(1), _block_diag([w2[0], w2[1]])


def _overlap_matrix(seq, ncp, nbp):
    cs = jnp.arange(ncp) * CMP_STRIDE
    bs = jnp.arange(nbp) * SEL_BLOCK
    ov = jnp.clip(jnp.minimum(cs[:, None] + CMP_LEN, bs[None, :] + SEL_BLOCK)
                  - jnp.maximum(cs[:, None], bs[None, :]), 0, None).astype(F32) / CMP_LEN
    keep = (jnp.arange(ncp)[:, None] < seq // CMP_STRIDE - 1) & (jnp.arange(nbp)[None, :] < seq // SEL_BLOCK)
    return jnp.where(keep, ov, 0.0)


def kernel(x, attn_norm_g, w_in, q_norm_g, k_norm_g, cmp_pe, cmp_w1, cmp_w2, sgu_norm_g, sgu_w, sgu_b, pool_w,
           pool_scale, conv_w, mix_out_norm_g, w_out, ffn_norm_g, ffn_w1, ffn_w3, ffn_w2, router_w, expert_w1,
           expert_w3, expert_w2):
    b, seq, _ = x.shape
    n = b * seq
    depth = w_in.shape[0]
    tm = TOKEN_TILE
    ncp = seq // CMP_STRIDE
    nbp = max(LANES, seq // SEL_BLOCK)

    cos, sa, sb = _rope_tables(np.arange(seq))
    cos_c, sa_c, sb_c = _rope_tables(np.arange(ncp) * CMP_STRIDE + CMP_LEN - 1)
    bd = _block_diag([jnp.full((HEAD_DIM, HEAD_DIM), 1.0 / HEAD_DIM, F32)] * 2).astype(BF16)
    overlap_t = _overlap_matrix(seq, ncp, nbp).T.astype(BF16)
    row = lambda v: v.reshape(1, -1)
    two = lambda v: jnp.tile(v, 2).reshape(1, LANES)

    w_in_b = _layout_w_in(w_in.astype(BF16))
    xf = x.reshape(n, D_MODEL)
    for layer in range(depth):
        kg = k_norm_g[layer]
        sgub = jnp.repeat(sgu_b[layer].T, HEAD_DIM, axis=1)
        qt, kvc, kvs, vts, kvw, vtw, gatet, ybcd = _proj_call(
            xf, row(attn_norm_g[layer]), w_in_b[layer], bd, cos, sa, sb,
            two(q_norm_g[layer]), jnp.stack([jnp.tile(kg[1], 2), jnp.tile(kg[2], 2)]),
            row(sgu_norm_g[layer]), sgu_w[layer].reshape(N_HEADS * SGU_CHUNK, SGU_CHUNK), sgub,
            _block_diag([pool_w[layer, i] for i in range(len(POOL_WINDOWS))]).astype(BF16),
            row(pool_scale[layer]), conv_w[layer], mix_out_norm_g[layer].reshape(4, GROUP_WIDTH)[1:],
            seq=seq, tm=tm)
        pe2, w1a, w1b, w2c = _layout_compress(cmp_pe[layer], cmp_w1[layer], cmp_w2[layer])
        kvcc, vtc = _compress_call(kvc.reshape(b, ncp, CMP_STRIDE * LANES), pe2, w1a.astype(BF16), w1b.astype(BF16),
                                   w2c.astype(BF16), two(kg[0]), bd, cos_c, sa_c, sb_c)
        ya = _attn_call(qt, gatet, kvcc, vtc, kvs, vts, kvw, vtw, overlap_t,
                        row(mix_out_norm_g[layer, :GROUP_WIDTH]), seq=seq, tq=ATTN_Q_TILE, tk=SEL_KEY_TILE)
        i = layer // 2
        if layer % 2 == 0:
            x1, hn = _outproj_call(xf, ya, ybcd, w_out[layer].astype(BF16), row(ffn_norm_g[layer]), None, tm=tm)
            xf = _ffn_call(hn, x1, ffn_w1[i].astype(BF16), ffn_w3[i].astype(BF16), ffn_w2[i].astype(BF16),
                           tm=FFN_TOKEN_TILE, tf=FFN_FF_TILE)
        else:
            r = jnp.pad(router_w[i], ((0, 0), (0, LANES - N_EXPERTS)))
            r_hi = r.astype(BF16)
            rw = jnp.stack([r_hi, (r - r_hi.astype(F32)).astype(BF16)])
            x1, hn, gate = _outproj_call(xf, ya, ybcd, w_out[layer].astype(BF16), row(ffn_norm_g[layer]), rw, tm=tm)
            xf = _moe_call(hn, x1, gate, expert_w1[i].astype(BF16), expert_w3[i].astype(BF16),
                           expert_w2[i].astype(BF16), tm=2 * MOE_SUB, tf=MOE_FF_TILE)
    return xf.reshape(b, seq, D_MODEL)
```

```python
import functools

import numpy as np
import jax
import jax.numpy as jnp
from jax import lax
from jax.experimental import pallas as pl
from jax.experimental.pallas import tpu as pltpu

D_MODEL = 1024
HEAD_DIM = 64
N_HEADS = 4
GROUP_WIDTH = 256
CMP_LEN = 32
CMP_STRIDE = 16
CMP_HIDDEN = 128
SEL_BLOCK = 64
SEL_TOPK = 16
WINDOW = 512
N_BRANCH = 3
ROPE_THETA = 500000.0
ROPE_DIM = 16
SGU_CHUNK = 128
POOL_WINDOWS = (2, 4, 8, 16)
CONV_WIDTH = 3
D_FF = 3584
N_EXPERTS = 8
EPS = 1e-6
NEG_INF = -1e30
Q_SCALE = HEAD_DIM ** -0.5 * 1.4426950408889634
CMP_PART = 128
N_FORCED = 3
SEL_KEY_TILE = 512
BIAS_PAD = 16
GATE_ROWS = 16

LANES = 128
MOE_ROWS = 128
MOE_FF_TILE = 1792
MOE_SWIGLU_GROUP = 4
MOE_SUB = 1024
HIST = 16
D_IN_PAD = 2304
TOKEN_TILE = 512
ATTN_Q_TILE = 256
FFN_TOKEN_TILE = 512
FFN_FF_TILE = 1792
VMEM_LIMIT = 56 * 1024 * 1024

F32 = jnp.float32
BF16 = jnp.bfloat16


def _dot(a, b):
    return jnp.dot(a, b, preferred_element_type=F32)


def _split(a):
    hi = a.astype(BF16)
    lo = (a - hi.astype(F32)).astype(BF16)
    return hi, lo


def _rms(y, g):
    return y * lax.rsqrt(jnp.mean(y * y, axis=-1, keepdims=True) + EPS) * g


def _head_norm_rope(c, g, cos, sa, sb, bd):
    hi, lo = _split(c * c)
    msq = _dot(hi, bd) + _dot(lo, bd)
    cn = c * lax.rsqrt(msq + EPS) * g
    return cn * cos + pltpu.roll(cn, LANES - ROPE_DIM // 2, 1) * sa + pltpu.roll(cn, ROPE_DIM // 2, 1) * sb


def _proj_kernel(x_ref, g_ref, w_ref, bd_ref, cos_ref, sa_ref, sb_ref, qg_ref, kg_ref,
                 sgug_ref, sguw_ref, sgub_ref, poolw_ref, pools_ref, convw_ref, outg_ref,
                 qt_out, kvc_out, kvs_out, vts_out, kvw_out, vtw_out, gatet_out, y_out, pext, zext, p_ref, kc_ref,
                 *, tm, tiles_per_seq):
    i = pl.program_id(0)
    x = x_ref[...]
    hn = _rms(x, g_ref[...]).astype(BF16)
    bd = bd_ref[...]
    cos, sa, sb = cos_ref[...], sa_ref[...], sb_ref[...]
    lane = lax.broadcasted_iota(jnp.int32, (1, LANES), 1)
    first_half = lane < HEAD_DIM

    for c0 in range(0, D_IN_PAD, 2 * LANES):
        p_ref[:, c0:c0 + 2 * LANES] = _dot(hn, w_ref[:, c0:c0 + 2 * LANES])

    def proj(c0, c1):
        return p_ref[:, c0:c1]

    pq = proj(0, 256)
    for c in range(2):
        qc = _head_norm_rope(pq[:, c * LANES:(c + 1) * LANES], qg_ref[...], cos, sa, sb, bd) * Q_SCALE
        qt_out[(2 * c) * LANES:(2 * c + 1) * LANES, :] = jnp.where(first_half, qc, 0.0).T.astype(BF16)
        qt_out[(2 * c + 1) * LANES:(2 * c + 2) * LANES, :] = jnp.where(
            first_half, pltpu.roll(qc, HEAD_DIM, 1), 0.0).T.astype(BF16)

    pk = proj(256, 512)
    kc_ref[...] = pk[:, 0:LANES]
    for r in range(CMP_STRIDE):
        kvc_out[:, r * LANES:(r + 1) * LANES] = kc_ref[pl.ds(r, tm // CMP_STRIDE, stride=CMP_STRIDE), :]
    ks = pk[:, LANES:2 * LANES]
    tok = (i % tiles_per_seq) * tm + lax.broadcasted_iota(jnp.int32, (tm, 1), 0)
    blk_in_tile = (tok // SEL_BLOCK) % (SEL_KEY_TILE // SEL_BLOCK)
    kvs_out[...] = jnp.where(first_half, _head_norm_rope(ks, kg_ref[0:1, :], cos, sa, sb, bd),
                             jnp.where(lane == HEAD_DIM + blk_in_tile, 1.0, 0.0)).astype(BF16)
    vts_out[...] = jnp.where(first_half, 1.0, ks).T.astype(BF16)
    pk = proj(512, 768)
    kw = pk[:, 0:LANES]
    kvw_out[...] = jnp.where(first_half, _head_norm_rope(kw, kg_ref[1:2, :], cos, sa, sb, bd), kw).astype(BF16)
    vtw_out[...] = jnp.where(first_half, 1.0, kw).T.astype(BF16)
    gatet_out[...] = jax.nn.sigmoid(pk[:, LANES:2 * LANES]).T[0:GATE_ROWS, :]

    outg = outg_ref[...]
    lane2 = lax.broadcasted_iota(jnp.int32, (1, GROUP_WIDTH), 1)
    grp = lane2 // HEAD_DIM

    u = proj(768, 1024)
    vn = _rms(proj(1024, 1280), sgug_ref[...]).astype(BF16)
    r_i = lax.broadcasted_iota(jnp.int32, (N_HEADS * SGU_CHUNK, SGU_CHUNK), 0) & (SGU_CHUNK - 1)
    c_i = lax.broadcasted_iota(jnp.int32, (N_HEADS * SGU_CHUNK, SGU_CHUNK), 1)
    wm = jnp.where(c_i <= r_i, sguw_ref[...], 0.0).astype(BF16)
    for c in range(tm // SGU_CHUNK):
        rows = slice(c * SGU_CHUNK, (c + 1) * SGU_CHUNK)
        r = _dot(wm, vn[rows, :])
        mixed = sgub_ref[...]
        for gi in range(N_HEADS):
            mixed = mixed + jnp.where(grp == gi, r[gi * SGU_CHUNK:(gi + 1) * SGU_CHUNK, :], 0.0)
        y_out[rows, 0:GROUP_WIDTH] = _rms(u[rows, :] * mixed, outg[0:1, :]).astype(BF16)

    first = (i % tiles_per_seq) == 0

    @pl.when(first)
    def _():
        pext[0:HIST, :] = jnp.zeros((HIST, GROUP_WIDTH), F32)
        zext[0:HIST, :] = jnp.zeros((HIST, GROUP_WIDTH), F32)

    @pl.when(jnp.logical_not(first))
    def _():
        pext[0:HIST, :] = pext[tm:tm + HIST, :]
        zext[0:HIST, :] = zext[tm:tm + HIST, :]

    pin = proj(1280, 1536)
    pext[HIST:, :] = pin
    pos = (i % tiles_per_seq) * tm + lax.broadcasted_iota(jnp.int32, (tm, 1), 0)
    acc = pin
    sums = []
    for k in range(1, POOL_WINDOWS[-1]):
        acc = acc + pext[HIST - k:HIST - k + tm, :]
        if k + 1 in POOL_WINDOWS:
            sums.append(acc)
    wsum = jnp.where(grp == 0, sums[0], jnp.where(grp == 1, sums[1], jnp.where(grp == 2, sums[2], sums[3])))
    win = jnp.where(grp == 0, POOL_WINDOWS[0], jnp.where(grp == 1, POOL_WINDOWS[1],
                                                         jnp.where(grp == 2, POOL_WINDOWS[2], POOL_WINDOWS[3])))
    cnt = jnp.minimum(pos + 1, win).astype(F32)
    dlt = (wsum / cnt - pin).astype(BF16)
    yc = _dot(dlt, poolw_ref[...]) * pools_ref[...]
    y_out[:, GROUP_WIDTH:2 * GROUP_WIDTH] = _rms(yc, outg[1:2, :]).astype(BF16)

    bg = proj(1536, 1792)
    z = proj(1792, 2048) * proj(2048, 2304)
    zext[HIST:, :] = z
    cw = convw_ref[...]
    conv = cw[2:3, :] * z + cw[1:2, :] * zext[HIST - 1:HIST - 1 + tm, :] + cw[0:1, :] * zext[HIST - 2:HIST - 2 + tm, :]
    y_out[:, 2 * GROUP_WIDTH:3 * GROUP_WIDTH] = _rms(bg * conv, outg[2:3, :]).astype(BF16)


def _proj_call(x, g, w, bd, cos, sa, sb, qg, kg, sgug, sguw, sgub, poolw, pools, convw, outg, *, seq, tm):
    n = x.shape[0]
    tps = seq // tm
    const = lambda shape: pl.BlockSpec(shape, lambda i: (0, 0))
    rows = lambda width: pl.BlockSpec((tm, width), lambda i: (i, 0))
    cols = lambda height: pl.BlockSpec((height, tm), lambda i: (0, i))
    tab = pl.BlockSpec((tm, LANES), lambda i: (i % tps, 0))
    return pl.pallas_call(
        functools.partial(_proj_kernel, tm=tm, tiles_per_seq=tps),
        grid=(n // tm,),
        in_specs=[rows(D_MODEL), const((1, D_MODEL)), const((D_MODEL, D_IN_PAD)), const((LANES, LANES)),
                  tab, tab, tab, const((1, LANES)), const((2, LANES)),
                  const((1, GROUP_WIDTH)), const((N_HEADS * SGU_CHUNK, SGU_CHUNK)), const((SGU_CHUNK, GROUP_WIDTH)),
                  const((GROUP_WIDTH, GROUP_WIDTH)), const((1, GROUP_WIDTH)), const((CONV_WIDTH, GROUP_WIDTH)),
                  const((3, GROUP_WIDTH))],
        out_specs=[cols(N_HEADS * LANES), pl.BlockSpec((tm // CMP_STRIDE, CMP_STRIDE * LANES), lambda i: (i, 0)),
                   rows(LANES), cols(LANES), rows(LANES), cols(LANES),
                   cols(GATE_ROWS), rows(3 * GROUP_WIDTH)],
        out_shape=[jax.ShapeDtypeStruct((N_HEADS * LANES, n), BF16),
                   jax.ShapeDtypeStruct((n // CMP_STRIDE, CMP_STRIDE * LANES), F32),
                   jax.ShapeDtypeStruct((n, LANES), BF16), jax.ShapeDtypeStruct((LANES, n), BF16),
                   jax.ShapeDtypeStruct((n, LANES), BF16), jax.ShapeDtypeStruct((LANES, n), BF16),
                   jax.ShapeDtypeStruct((GATE_ROWS, n), F32), jax.ShapeDtypeStruct((n, 3 * GROUP_WIDTH), BF16)],
        scratch_shapes=[pltpu.VMEM((tm + HIST, GROUP_WIDTH), F32), pltpu.VMEM((tm + HIST, GROUP_WIDTH), F32),
                        pltpu.VMEM((tm, D_IN_PAD), F32), pltpu.VMEM((tm, LANES), F32)],
        compiler_params=pltpu.CompilerParams(dimension_semantics=("arbitrary",), vmem_limit_bytes=VMEM_LIMIT),
        name="proj_mixers",
    )(x, g, w, bd, cos, sa, sb, qg, kg, sgug, sguw, sgub, poolw, pools, convw, outg)


def _compress_kernel(t_ref, pe_ref, w1a_ref, w1b_ref, w2_ref, kg_ref, bd_ref, cos_ref, sa_ref, sb_ref, kv_ref, vt_ref):
    t = t_ref[0]
    ncp = t.shape[0]
    a = _dot((t + pe_ref[0:1, :]).astype(BF16), w1a_ref[...])
    b = _dot((t + pe_ref[1:2, :]).astype(BF16), w1b_ref[...])
    hid = jax.nn.gelu(a + pltpu.roll(b, ncp - 1, 0), approximate=True)
    kv = _dot(hid.astype(BF16), w2_ref[...])
    first_half = lax.broadcasted_iota(jnp.int32, (1, LANES), 1) < HEAD_DIM
    kn = _head_norm_rope(kv, kg_ref[...], cos_ref[...], sa_ref[...], sb_ref[...], bd_ref[...])
    kv_ref[0] = jnp.where(first_half, kn, kv).astype(BF16)
    vt_ref[0] = jnp.where(first_half, 1.0, kv).T.astype(BF16)


def _compress_call(t2, pe, w1a, w1b, w2, kg, bd, cos, sa, sb):
    b, ncp, width = t2.shape
    const = lambda shape: pl.BlockSpec(shape, lambda i: (0,) * len(shape))
    return pl.pallas_call(
        _compress_kernel,
        grid=(b,),
        in_specs=[pl.BlockSpec((1, ncp, width), lambda i: (i, 0, 0)), const(pe.shape), const(w1a.shape),
                  const(w1b.shape), const(w2.shape), const(kg.shape), const(bd.shape),
                  const(cos.shape), const(sa.shape), const(sb.shape)],
        out_specs=[pl.BlockSpec((1, ncp, LANES), lambda i: (i, 0, 0)), pl.BlockSpec((1, LANES, ncp), lambda i: (i, 0, 0))],
        out_shape=[jax.ShapeDtypeStruct((b, ncp, LANES), BF16), jax.ShapeDtypeStruct((b, LANES, ncp), BF16)],
        compiler_params=pltpu.CompilerParams(dimension_semantics=("arbitrary",), vmem_limit_bytes=VMEM_LIMIT),
        name="nsa_compress",
    )(t2, pe, w1a, w1b, w2, kg, bd, cos, sa, sb)


def _attn_kernel(qt_ref, gatet_ref, kvc_ref, vtc_ref, kvs_ref, vts_ref, kvw_ref, vtw_ref, ovt_ref, og_ref, o_ref,
                 bias_ref, sa_ref, sb_ref, m_ref, acc_ref, oc_ref, imp_ref, sw_ref, ow_ref,
                 *, tq, tk, n_sel):
    s0 = pl.program_id(1) * tq
    nc = N_HEADS * tq
    qs = jnp.concatenate([qt_ref[h * LANES:(h + 1) * LANES, :] for h in range(N_HEADS)], axis=1)
    t_col = s0 + (lax.broadcasted_iota(jnp.int32, (1, nc), 1) & (tq - 1))
    t_q = s0 + lax.broadcasted_iota(jnp.int32, (1, tq), 1)

    ncp = kvc_ref.shape[1]
    part = CMP_PART if ncp % CMP_PART == 0 else ncp

    def cmp_branch(rows):
        cmp_end = lax.broadcasted_iota(jnp.int32, (rows, 1), 0) * CMP_STRIDE + (CMP_LEN - 1)
        sc = jnp.where(cmp_end <= t_col, _dot(kvc_ref[0, 0:rows, :], qs), NEG_INF)
        e = jnp.exp2(sc - jnp.max(sc, axis=0, keepdims=True))
        p = e * jnp.where(t_col >= CMP_LEN - 1, 1.0 / jnp.sum(e, axis=0, keepdims=True), 0.0)
        oc_ref[...] = _dot(vtc_ref[0, :, 0:rows], p.astype(BF16))[HEAD_DIM:, :]
        psum = p[:, 0:tq] + p[:, tq:2 * tq] + p[:, 2 * tq:3 * tq] + p[:, 3 * tq:4 * tq]
        hi, lo = _split(psum)
        imp_ref[...] = _dot(ovt_ref[:, 0:rows], hi) + _dot(ovt_ref[:, 0:rows], lo)

    n_parts = (s0 + tq + CMP_STRIDE * part - 1) // (CMP_STRIDE * part)
    for k in range(1, ncp // part + 1):
        pl.when(n_parts == k)(functools.partial(cmp_branch, k * part))
    o_c = oc_ref[...]
    imp = imp_ref[...]

    wk = WINDOW + tq
    w0 = pl.multiple_of(jnp.maximum(s0 - WINDOW, 0), tq)
    sw_ref[...] = _dot(kvw_ref[pl.ds(w0, wk), :], qs)

    nbp = imp.shape[0]
    blk = lax.broadcasted_iota(jnp.int32, (nbp, 1), 0)
    blk_f = blk.astype(F32)
    cur = t_q // SEL_BLOCK
    valid_b = blk <= cur
    forced = valid_b & ((blk == 0) | (blk == cur) | (blk == cur - 1))
    score = jnp.where(valid_b & jnp.logical_not(forced), imp, -1.0)
    sel = jnp.where(forced, 1.0, 0.0)
    for _ in range(n_sel - N_FORCED):
        top = jnp.max(score, axis=0, keepdims=True)
        idx = jnp.min(jnp.where(score == top, blk_f, float(nbp)), axis=0, keepdims=True)
        hit = blk_f == idx
        sel = jnp.where(hit & (top >= 0.0), 1.0, sel)
        score = jnp.where(hit, -2.0, score)
    bias = jnp.where(sel > 0.5, 0.0, NEG_INF)
    bias_ref[0:nbp, :] = jnp.concatenate([bias] * N_HEADS, axis=1)
    bias_ref[nbp:, :] = jnp.zeros((BIAS_PAD, nc), F32)

    kp = w0 + lax.broadcasted_iota(jnp.int32, (wk, 1), 0)
    sw = jnp.where((kp <= t_col) & (kp > t_col - WINDOW), sw_ref[...], NEG_INF)
    pw = jnp.exp2(sw - jnp.max(sw, axis=0, keepdims=True)).astype(BF16)
    acc_w = _dot(vtw_ref[:, pl.ds(w0, wk)], pw)
    ow_ref[...] = acc_w[HEAD_DIM:, :] / acc_w[0:1, :]

    n_blk = tk // SEL_BLOCK
    q_top = qs[0:HEAD_DIM, :]
    q_rest = jnp.zeros((LANES - HEAD_DIM - BIAS_PAD, nc), BF16)

    def scores(j):
        tile_bias = bias_ref[pl.ds(pl.multiple_of(j * n_blk, n_blk), BIAS_PAD), :].astype(BF16)
        qb = jnp.concatenate([q_top, tile_bias, q_rest], axis=0)
        return _dot(kvs_ref[pl.ds(pl.multiple_of(j * tk, tk), tk), :], qb)

    def consume(s_ref, j, causal):
        k0 = pl.multiple_of(j * tk, tk)
        s = s_ref[...]
        if causal:
            kpos = k0 + lax.broadcasted_iota(jnp.int32, (tk, 1), 0)
            s = jnp.where(kpos <= t_col, s, NEG_INF)
        m = m_ref[...]
        m_new = jnp.maximum(m, jnp.max(s, axis=0, keepdims=True))
        pe = jnp.exp2(s - m_new).astype(BF16)
        acc_ref[...] = jnp.exp2(m - m_new) * acc_ref[...] + _dot(vts_ref[:, pl.ds(k0, tk)], pe)
        m_ref[...] = m_new

    n_full = (s0 + tq + tk - 1) // tk - 1
    m_ref[...] = jnp.full((1, nc), NEG_INF, F32)
    acc_ref[...] = jnp.zeros((LANES, nc), F32)
    sa_ref[...] = scores(0)

    def pair(i, carry):
        j = 2 * i
        sb_ref[...] = scores(j + 1)
        consume(sa_ref, j, False)
        sa_ref[...] = scores(jnp.minimum(j + 2, n_full))
        consume(sb_ref, j + 1, False)
        return carry

    lax.fori_loop(0, n_full // 2, pair, 0)

    @pl.when(n_full % 2 == 1)
    def _():
        sb_ref[...] = scores(n_full)
        consume(sa_ref, n_full - 1, False)
        consume(sb_ref, n_full, True)

    @pl.when(n_full % 2 == 0)
    def _():
        consume(sa_ref, n_full, True)

    acc_s = acc_ref[...]
    o_s = acc_s[HEAD_DIM:, :] / acc_s[0:1, :]

    g = gatet_ref[...]
    heads = []
    for h in range(N_HEADS):
        c = slice(h * tq, (h + 1) * tq)
        r = h * N_BRANCH
        heads.append(g[r:r + 1, :] * o_c[:, c] + g[r + 1:r + 2, :] * o_s[:, c] + g[r + 2:r + 3, :] * ow_ref[:, c])
    ya = jnp.concatenate(heads, axis=0).T
    o_ref[...] = _rms(ya, og_ref[...]).astype(BF16)


def _attn_call(qt, gatet, kvc, vtc, kvs, vts, kvw, vtw, ovt, og, *, seq, tq, tk):
    n = qt.shape[1]
    b = n // seq
    nq = seq // tq
    ncp = kvc.shape[1]
    nbp = ovt.shape[0]
    n_sel = min(SEL_TOPK, seq // SEL_BLOCK)
    cols = lambda height: pl.BlockSpec((height, tq), lambda bi, qi: (0, bi * nq + qi))
    per_b = lambda shape: pl.BlockSpec((1,) + shape, lambda bi, qi: (bi, 0, 0))
    const = lambda shape: pl.BlockSpec(shape, lambda bi, qi: (0, 0))
    seq_rows = pl.BlockSpec((seq, LANES), lambda bi, qi: (bi, 0))
    seq_cols = pl.BlockSpec((LANES, seq), lambda bi, qi: (0, bi))
    return pl.pallas_call(
        functools.partial(_attn_kernel, tq=tq, tk=tk, n_sel=n_sel),
        grid=(b, nq),
        in_specs=[cols(N_HEADS * LANES), cols(GATE_ROWS), per_b((ncp, LANES)), per_b((LANES, ncp)),
                  seq_rows, seq_cols, seq_rows, seq_cols, const(ovt.shape), const((1, GROUP_WIDTH))],
        out_specs=pl.BlockSpec((tq, GROUP_WIDTH), lambda bi, qi: (bi * nq + qi, 0)),
        out_shape=jax.ShapeDtypeStruct((n, GROUP_WIDTH), BF16),
        scratch_shapes=[pltpu.VMEM((nbp + BIAS_PAD, N_HEADS * tq), F32), pltpu.VMEM((tk, N_HEADS * tq), F32),
                        pltpu.VMEM((tk, N_HEADS * tq), F32), pltpu.VMEM((1, N_HEADS * tq), F32),
                        pltpu.VMEM((LANES, N_HEADS * tq), F32), pltpu.VMEM((HEAD_DIM, N_HEADS * tq), F32),
                        pltpu.VMEM((nbp, tq), F32), pltpu.VMEM((WINDOW + tq, N_HEADS * tq), F32),
                        pltpu.VMEM((HEAD_DIM, N_HEADS * tq), F32)],
        compiler_params=pltpu.CompilerParams(dimension_semantics=("arbitrary", "arbitrary"),
                                             vmem_limit_bytes=VMEM_LIMIT),
        name="nsa_attention",
    )(qt, gatet, kvc, vtc, kvs, vts, kvw, vtw, ovt, og)


def _outproj_kernel(x_ref, ya_ref, yb_ref, w_ref, g_ref, *rest, moe):
    if moe:
        rw_ref, x1_ref, hn_ref, gate_ref = rest
    else:
        x1_ref, hn_ref = rest
    x1 = x_ref[...] + _dot(ya_ref[...], w_ref[0:GROUP_WIDTH, :]) + _dot(yb_ref[...], w_ref[GROUP_WIDTH:, :])
    x1_ref[...] = x1
    hn = _rms(x1, g_ref[...])
    hn_ref[...] = hn.astype(BF16)
    if moe:
        hi, lo = _split(hn)
        logits = _dot(hi, rw_ref[0]) + _dot(lo, rw_ref[0]) + _dot(hi, rw_ref[1])
        lane = lax.broadcasted_iota(jnp.int32, (1, LANES), 1)
        lane_f = lane.astype(F32)
        logits = jnp.where(lane < N_EXPERTS, logits, NEG_INF)
        m1 = jnp.max(logits, axis=-1, keepdims=True)
        i1 = jnp.min(jnp.where(logits == m1, lane_f, float(LANES)), axis=-1, keepdims=True)
        rest_l = jnp.where(lane_f == i1, 2 * NEG_INF, logits)
        m2 = jnp.max(rest_l, axis=-1, keepdims=True)
        i2 = jnp.min(jnp.where(rest_l == m2, lane_f, float(LANES)), axis=-1, keepdims=True)
        e2 = jnp.exp(m2 - m1)
        den = 1.0 + e2
        gate_ref[...] = jnp.where(lane_f == i1, 1.0 / den, jnp.where(lane_f == i2, e2 / den, 0.0))


def _outproj_call(x, ya, yb, w, g, rw, *, tm):
    n = x.shape[0]
    moe = rw is not None
    rows = lambda width: pl.BlockSpec((tm, width), lambda i: (i, 0))
    const = lambda shape: pl.BlockSpec(shape, lambda i: (0,) * len(shape))
    in_specs = [rows(D_MODEL), rows(GROUP_WIDTH), rows(3 * GROUP_WIDTH), const((D_MODEL, D_MODEL)), const((1, D_MODEL))]
    out_specs = [rows(D_MODEL), rows(D_MODEL)]
    out_shape = [jax.ShapeDtypeStruct((n, D_MODEL), F32), jax.ShapeDtypeStruct((n, D_MODEL), BF16)]
    args = [x, ya, yb, w, g]
    if moe:
        in_specs.append(const(rw.shape))
        out_specs.append(rows(LANES))
        out_shape.append(jax.ShapeDtypeStruct((n, LANES), F32))
        args.append(rw)
    return pl.pallas_call(
        functools.partial(_outproj_kernel, moe=moe),
        grid=(n // tm,),
        in_specs=in_specs, out_specs=out_specs, out_shape=out_shape,
        compiler_params=pltpu.CompilerParams(dimension_semantics=("arbitrary",), vmem_limit_bytes=VMEM_LIMIT),
        name="outproj_router" if moe else "outproj",
    )(*args)


def _ffn_kernel(x_ref, ya_ref, yb_ref, wo_ref, g_ref, w1_ref, w3_ref, w2_ref, o_ref, acc_ref, hn_ref):
    f = pl.program_id(1)

    @pl.when(f == 0)
    def _():
        x1 = x_ref[...] + _dot(ya_ref[...], wo_ref[0:GROUP_WIDTH, :]) + _dot(yb_ref[...], wo_ref[GROUP_WIDTH:, :])
        acc_ref[...] = x1
        hn_ref[...] = _rms(x1, g_ref[...]).astype(BF16)

    h = hn_ref[...]
    a = _dot(h, w1_ref[...])
    t = (a * jax.nn.sigmoid(a) * _dot(h, w3_ref[...])).astype(BF16)
    acc_ref[...] += _dot(t, w2_ref[...])

    @pl.when(f == pl.num_programs(1) - 1)
    def _():
        o_ref[...] = acc_ref[...]


def _ffn_call(x, ya, yb, wo, g, w1, w3, w2, *, tm, tf):
    n = x.shape[0]
    rows = lambda width: pl.BlockSpec((tm, width), lambda i, f: (i, 0))
    const = lambda shape: pl.BlockSpec(shape, lambda i, f: (0, 0))
    up = pl.BlockSpec((D_MODEL, tf), lambda i, f: (0, f))
    return pl.pallas_call(
        _ffn_kernel,
        grid=(n // tm, D_FF // tf),
        in_specs=[rows(D_MODEL), rows(GROUP_WIDTH), rows(3 * GROUP_WIDTH), const((D_MODEL, D_MODEL)), const((1, D_MODEL)),
                  up, up, pl.BlockSpec((tf, D_MODEL), lambda i, f: (f, 0))],
        out_specs=rows(D_MODEL),
        out_shape=jax.ShapeDtypeStruct((n, D_MODEL), F32),
        scratch_shapes=[pltpu.VMEM((tm, D_MODEL), F32), pltpu.VMEM((tm, D_MODEL), BF16)],
        compiler_params=pltpu.CompilerParams(dimension_semantics=("arbitrary", "arbitrary"),
                                             vmem_limit_bytes=VMEM_LIMIT),
        name="outproj_ffn_swiglu",
    )(x, ya, yb, wo, g, w1, w3, w2)


def _moe_kernel(h_ref, x_hbm, gate_ref, w1_ref, w3_ref, w2_ref, o_ref,
                xc_ref, yc_ref, rank_ref, gate_t_ref, rank_t_ref, start_ref, x_sem, *, n_sub):
    e = pl.program_id(1)
    f = pl.program_id(2)
    rb = MOE_ROWS
    sub = MOE_SUB

    def for_chunks(c_lo, c_hi, fn, group=2):
        n = c_hi - c_lo

        def full(c, carry):
            fn(pl.multiple_of((c_lo + group * c) * rb, rb), group * rb)
            return carry

        lax.fori_loop(0, n // group, full, 0)
        done = c_lo + n - n % group
        size = group // 2
        while size >= 1:
            taken = n % (2 * size) >= size
            pl.when(taken)(functools.partial(fn, pl.multiple_of(done * rb, rb), size * rb))
            done = done + jnp.where(taken, size, 0)
            size //= 2

    def sub_chunks(s):
        return start_ref[s] // rb, (start_ref[s + 1] + rb - 1) // rb

    @pl.when((e == 0) & (f == 0))
    def _():
        tm = n_sub * sub
        x_copy = pltpu.make_async_copy(x_hbm.at[pl.ds(pl.multiple_of(pl.program_id(0) * tm, tm), tm), :], o_ref, x_sem)
        x_copy.start()
        before = jnp.where(lax.broadcasted_iota(jnp.int32, (sub, sub), 1)
                           < lax.broadcasted_iota(jnp.int32, (sub, sub), 0), 1.0, 0.0).astype(BF16)
        routed_before = jnp.zeros((1, LANES), F32)
        for s in range(n_sub):
            span = slice(s * sub, (s + 1) * sub)
            gate = gate_ref[span, :]
            routed = jnp.where(gate > 0.0, 1.0, 0.0)
            rank = _dot(before, routed.astype(BF16)) + routed_before
            rank_ref[span, :] = rank
            gate_t_ref[:, span] = gate.T[0:N_EXPERTS]
            rank_t_ref[:, span] = rank.T[0:N_EXPERTS]
            routed_before = routed_before + jnp.sum(routed, axis=0, keepdims=True)
        x_copy.wait()

    @pl.when(f == 0)
    def _():
        start_ref[0] = 0
        for s in range(n_sub):
            g_row = gate_t_ref[pl.ds(e, 1), s * sub:(s + 1) * sub]
            start_ref[s + 1] = start_ref[s] + jnp.sum(jnp.where(g_row > 0.0, 1.0, 0.0)).astype(jnp.int32)

        def clear(r0, rows):
            xc_ref[pl.ds(r0, rows), :] = jnp.zeros((rows, D_MODEL), BF16)
            yc_ref[pl.ds(r0, rows), :] = jnp.zeros((rows, D_MODEL), F32)

        for_chunks(0, (start_ref[n_sub] + rb - 1) // rb, clear)

        for s in range(n_sub):
            span = slice(s * sub, (s + 1) * sub)
            g_row = gate_t_ref[pl.ds(e, 1), span]
            key = jnp.where(g_row > 0.0, rank_t_ref[pl.ds(e, 1), span], -1.0)

            def gather(r0, rows, span=span, key=key):
                slot = (r0 + lax.broadcasted_iota(jnp.int32, (rows, 1), 0)).astype(F32)
                onehot = jnp.where(key == slot, 1.0, 0.0).astype(BF16)
                dst = pl.ds(r0, rows)
                xc_ref[dst, :] = (xc_ref[dst, :].astype(F32) + _dot(onehot, h_ref[span, :])).astype(BF16)

            for_chunks(*sub_chunks(s), gather)

    def swiglu(r0, rows):
        src = pl.ds(r0, rows)
        xc = xc_ref[src, :]
        a = _dot(xc, w1_ref[...])
        t = (a * jax.nn.sigmoid(a) * _dot(xc, w3_ref[...])).astype(BF16)
        yc_ref[src, :] += _dot(t, w2_ref[...])

    for_chunks(0, (start_ref[n_sub] + rb - 1) // rb, swiglu, group=MOE_SWIGLU_GROUP)

    @pl.when(f == pl.num_programs(2) - 1)
    def _():
        lane = lax.broadcasted_iota(jnp.int32, (1, LANES), 1)
        for s in range(n_sub):
            span = slice(s * sub, (s + 1) * sub)
            g_col = jnp.sum(jnp.where(lane == e, gate_ref[span, :], 0.0), axis=-1, keepdims=True)
            r_col = jnp.sum(jnp.where(lane == e, rank_ref[span, :], 0.0), axis=-1, keepdims=True)
            key = jnp.where(g_col > 0.0, r_col, -1.0)

            def scatter(r0, rows, span=span, key=key, g_col=g_col):
                slot = (r0 + lax.broadcasted_iota(jnp.int32, (1, rows), 1)).astype(F32)
                onehot = jnp.where(key == slot, 1.0, 0.0).astype(BF16)
                o_ref[span, :] += g_col * _dot(onehot, yc_ref[pl.ds(r0, rows), :].astype(BF16))

            for_chunks(*sub_chunks(s), scatter)


def _moe_call(h, x, gate, w1, w3, w2, *, tm, tf):
    n = h.shape[0]
    n_sub = tm // MOE_SUB
    once = pl.Buffered(1)
    rows = lambda width, mode=None: pl.BlockSpec((tm, width), lambda i, e, f: (i, 0), pipeline_mode=mode)
    return pl.pallas_call(
        functools.partial(_moe_kernel, n_sub=n_sub),
        grid=(n // tm, N_EXPERTS, D_FF // tf),
        in_specs=[rows(D_MODEL, once), pl.BlockSpec(memory_space=pl.ANY), rows(LANES),
                  pl.BlockSpec((None, D_MODEL, tf), lambda i, e, f: (e, 0, f)),
                  pl.BlockSpec((None, D_MODEL, tf), lambda i, e, f: (e, 0, f)),
                  pl.BlockSpec((None, tf, D_MODEL), lambda i, e, f: (e, f, 0))],
        out_specs=rows(D_MODEL, once),
        out_shape=jax.ShapeDtypeStruct((n, D_MODEL), F32),
        scratch_shapes=[pltpu.VMEM((tm, D_MODEL), BF16), pltpu.VMEM((tm, D_MODEL), F32), pltpu.VMEM((tm, LANES), F32),
                        pltpu.VMEM((N_EXPERTS, tm), F32), pltpu.VMEM((N_EXPERTS, tm), F32),
                        pltpu.SMEM((n_sub + 1,), jnp.int32), pltpu.SemaphoreType.DMA(())],
        compiler_params=pltpu.CompilerParams(dimension_semantics=("arbitrary", "arbitrary", "arbitrary"),
                                             vmem_limit_bytes=VMEM_LIMIT),
        name="moe_swiglu",
    )(h, x, gate, w1, w3, w2)


def _rope_tables(pos):
    half = ROPE_DIM // 2
    inv_freq = ROPE_THETA ** (-np.arange(half, dtype=np.float64) / half)
    ang = pos.astype(np.float64)[..., None] * inv_freq
    cos, sin = np.cos(ang).astype(np.float32), np.sin(ang).astype(np.float32)
    n = pos.shape[0]
    rest = HEAD_DIM - ROPE_DIM
    cos64 = np.concatenate([cos, cos, np.ones((n, rest), np.float32)], axis=-1)
    sa64 = np.concatenate([-sin, np.zeros((n, rest + half), np.float32)], axis=-1)
    sb64 = np.concatenate([np.zeros((n, half), np.float32), sin, np.zeros((n, rest), np.float32)], axis=-1)
    return tuple(jnp.asarray(np.tile(t, (1, 2)), F32) for t in (cos64, sa64, sb64))


def _block_diag(blocks):
    n = len(blocks)
    rows = []
    for i, blk in enumerate(blocks):
        rows.append(jnp.concatenate([blk if j == i else jnp.zeros((blk.shape[0], blocks[j].shape[1]), blk.dtype)
                                     for j in range(n)], axis=1))
    return jnp.concatenate(rows, axis=0)


def _layout_w_in(w):
    n_gate = N_BRANCH * N_HEADS
    g0 = GROUP_WIDTH + 6 * HEAD_DIM
    pad = jnp.zeros(w.shape[:-1] + (LANES - n_gate,), w.dtype)
    return jnp.concatenate([w[..., :g0 + n_gate], pad, w[..., g0 + n_gate:]], axis=-1)


def _layout_compress(pe, w1, w2):
    half = CMP_LEN // 2
    pe2 = jnp.transpose(pe, (1, 0, 2)).reshape(2, half * LANES)
    w1r = w1.reshape(2, 2, half, HEAD_DIM, CMP_HIDDEN)
    zeros = jnp.zeros((half, HEAD_DIM, CMP_HIDDEN), w1.dtype)

    def big(part):
        k_rows = jnp.concatenate([w1r[0, part], zeros], axis=-1)
        v_rows = jnp.concatenate([zeros, w1r[1, part]], axis=-1)
        return jnp.concatenate([k_rows, v_rows], axis=1).reshape(half * LANES, 2 * CMP_HIDDEN)

    return pe2, big(0), big(1), _block_diag([w2[0], w2[1]])


def _overlap_matrix(seq, ncp, nbp):
    cs = jnp.arange(ncp) * CMP_STRIDE
    bs = jnp.arange(nbp) * SEL_BLOCK
    ov = jnp.clip(jnp.minimum(cs[:, None] + CMP_LEN, bs[None, :] + SEL_BLOCK)
                  - jnp.maximum(cs[:, None], bs[None, :]), 0, None).astype(F32) / CMP_LEN
    keep = (jnp.arange(ncp)[:, None] < seq // CMP_STRIDE - 1) & (jnp.arange(nbp)[None, :] < seq // SEL_BLOCK)
    return jnp.where(keep, ov, 0.0)


def kernel(x, attn_norm_g, w_in, q_norm_g, k_norm_g, cmp_pe, cmp_w1, cmp_w2, sgu_norm_g, sgu_w, sgu_b, pool_w,
           pool_scale, conv_w, mix_out_norm_g, w_out, ffn_norm_g, ffn_w1, ffn_w3, ffn_w2, router_w, expert_w1,
           expert_w3, expert_w2):
    b, seq, _ = x.shape
    n = b * seq
    depth = w_in.shape[0]
    tm = TOKEN_TILE
    ncp = seq // CMP_STRIDE
    nbp = max(LANES, seq // SEL_BLOCK)

    cos, sa, sb = _rope_tables(np.arange(seq))
    cos_c, sa_c, sb_c = _rope_tables(np.arange(ncp) * CMP_STRIDE + CMP_LEN - 1)
    bd = _block_diag([jnp.full((HEAD_DIM, HEAD_DIM), 1.0 / HEAD_DIM, F32)] * 2).astype(BF16)
    overlap_t = _overlap_matrix(seq, ncp, nbp).T.astype(BF16)
    row = lambda v: v.reshape(1, -1)
    two = lambda v: jnp.tile(v, 2).reshape(1, LANES)

    w_in_b = _layout_w_in(w_in.astype(BF16))
    xf = x.reshape(n, D_MODEL)
    for layer in range(depth):
        kg = k_norm_g[layer]
        sgub = jnp.repeat(sgu_b[layer].T, HEAD_DIM, axis=1)
        qt, kvc, kvs, vts, kvw, vtw, gatet, ybcd = _proj_call(
            xf, row(attn_norm_g[layer]), w_in_b[layer], bd, cos, sa, sb,
            two(q_norm_g[layer]), jnp.stack([jnp.tile(kg[1], 2), jnp.tile(kg[2], 2)]),
            row(sgu_norm_g[layer]), sgu_w[layer].reshape(N_HEADS * SGU_CHUNK, SGU_CHUNK), sgub,
            _block_diag([pool_w[layer, i] for i in range(len(POOL_WINDOWS))]).astype(BF16),
            row(pool_scale[layer]), conv_w[layer], mix_out_norm_g[layer].reshape(4, GROUP_WIDTH)[1:],
            seq=seq, tm=tm)
        pe2, w1a, w1b, w2c = _layout_compress(cmp_pe[layer], cmp_w1[layer], cmp_w2[layer])
        kvcc, vtc = _compress_call(kvc.reshape(b, ncp, CMP_STRIDE * LANES), pe2, w1a.astype(BF16), w1b.astype(BF16),
                                   w2c.astype(BF16), two(kg[0]), bd, cos_c, sa_c, sb_c)
        ya = _attn_call(qt, gatet, kvcc, vtc, kvs, vts, kvw, vtw, overlap_t,
                        row(mix_out_norm_g[layer, :GROUP_WIDTH]), seq=seq, tq=ATTN_Q_TILE, tk=SEL_KEY_TILE)
        i = layer // 2
        if layer % 2 == 0:
            xf = _ffn_call(xf, ya, ybcd, w_out[layer].astype(BF16), row(ffn_norm_g[layer]), ffn_w1[i].astype(BF16),
                           ffn_w3[i].astype(BF16), ffn_w2[i].astype(BF16), tm=FFN_TOKEN_TILE, tf=FFN_FF_TILE)
        else:
            r = jnp.pad(router_w[i], ((0, 0), (0, LANES - N_EXPERTS)))
            r_hi = r.astype(BF16)
            rw = jnp.stack([r_hi, (r - r_hi.astype(F32)).astype(BF16)])
            x1, hn, gate = _outproj_call(xf, ya, ybcd, w_out[layer].astype(BF16), row(ffn_norm_g[layer]), rw, tm=tm)
            xf = _moe_call(hn, x1, gate, expert_w1[i].astype(BF16), expert_w3[i].astype(BF16),
                           expert_w2[i].astype(BF16), tm=2 * MOE_SUB, tf=MOE_FF_TILE)
    return xf.reshape(b, seq, D_MODEL)
```

```python
import functools

import numpy as np
import jax
import jax.numpy as jnp
from jax import lax
from jax.experimental import pallas as pl
from jax.experimental.pallas import tpu as pltpu

D_MODEL = 1024
HEAD_DIM = 64
N_HEADS = 4
GROUP_WIDTH = 256
CMP_LEN = 32
CMP_STRIDE = 16
CMP_HIDDEN = 128
SEL_BLOCK = 64
SEL_TOPK = 16
WINDOW = 512
N_BRANCH = 3
ROPE_THETA = 500000.0
ROPE_DIM = 16
SGU_CHUNK = 128
POOL_WINDOWS = (2, 4, 8, 16)
CONV_WIDTH = 3
D_FF = 3584
N_EXPERTS = 8
EPS = 1e-6
NEG_INF = -1e30
Q_SCALE = HEAD_DIM ** -0.5 * 1.4426950408889634
CMP_PART = 128
N_FORCED = 3
SEL_KEY_TILE = 512
BIAS_PAD = 16
GATE_ROWS = 16

LANES = 128
MOE_ROWS = 128
MOE_FF_TILE = 1792
MOE_SWIGLU_GROUP = 4
MOE_SUB = 1024
HIST = 16
D_IN_PAD = 2304
TOKEN_TILE = 512
ATTN_Q_TILE = 256
FFN_TOKEN_TILE = 512
FFN_FF_TILE = 1792
VMEM_LIMIT = 56 * 1024 * 1024

F32 = jnp.float32
BF16 = jnp.bfloat16


def _dot(a, b):
    return jnp.dot(a, b, preferred_element_type=F32)


def _split(a):
    hi = a.astype(BF16)
    lo = (a - hi.astype(F32)).astype(BF16)
    return hi, lo


def _rms(y, g):
    return y * lax.rsqrt(jnp.mean(y * y, axis=-1, keepdims=True) + EPS) * g


def _head_norm_rope(c, g, cos, sa, sb, bd):
    width = c.shape[1]
    if width > LANES:
        cos, sa, sb = (jnp.concatenate([t] * (width // LANES), axis=1) for t in (cos, sa, sb))
    hi, lo = _split(c * c)
    msq = _dot(hi, bd[0:width, 0:width]) + _dot(lo, bd[0:width, 0:width])
    cn = c * lax.rsqrt(msq + EPS) * g
    return cn * cos + pltpu.roll(cn, width - ROPE_DIM // 2, 1) * sa + pltpu.roll(cn, ROPE_DIM // 2, 1) * sb


def _proj_kernel(x_ref, g_ref, w_ref, bd_ref, cos_ref, sa_ref, sb_ref, qg_ref, kg_ref,
                 sgug_ref, sguw_ref, sgub_ref, poolw_ref, pools_ref, convw_ref, outg_ref,
                 qt_out, kvc_out, kvs_out, vts_out, kvw_out, vtw_out, gatet_out, y_out, pext, zext, p_ref, kc_ref,
                 *, tm, tiles_per_seq):
    i = pl.program_id(0)
    x = x_ref[...]
    hn = _rms(x, g_ref[...]).astype(BF16)
    bd = bd_ref[...]
    cos, sa, sb = cos_ref[...], sa_ref[...], sb_ref[...]
    lane = lax.broadcasted_iota(jnp.int32, (1, LANES), 1)
    first_half = lane < HEAD_DIM

    for c0 in range(0, D_IN_PAD, 2 * LANES):
        p_ref[:, c0:c0 + 2 * LANES] = _dot(hn, w_ref[:, c0:c0 + 2 * LANES])

    def proj(c0, c1):
        return p_ref[:, c0:c1]

    qn = _head_norm_rope(proj(0, 256), qg_ref[...], cos, sa, sb, bd) * Q_SCALE
    for c in range(2):
        qc = qn[:, c * LANES:(c + 1) * LANES]
        qt_out[(2 * c) * LANES:(2 * c + 1) * LANES, :] = jnp.where(first_half, qc, 0.0).T.astype(BF16)
        qt_out[(2 * c + 1) * LANES:(2 * c + 2) * LANES, :] = jnp.where(
            first_half, pltpu.roll(qc, HEAD_DIM, 1), 0.0).T.astype(BF16)

    kc_ref[...] = proj(256, 384)
    for r in range(CMP_STRIDE):
        kvc_out[:, r * LANES:(r + 1) * LANES] = kc_ref[pl.ds(r, tm // CMP_STRIDE, stride=CMP_STRIDE), :]

    kk = proj(384, 640)
    kn = _head_norm_rope(kk, kg_ref[...], cos, sa, sb, bd)
    ks, kw = kk[:, 0:LANES], kk[:, LANES:2 * LANES]
    tok = (i % tiles_per_seq) * tm + lax.broadcasted_iota(jnp.int32, (tm, 1), 0)
    blk_in_tile = (tok // SEL_BLOCK) % (SEL_KEY_TILE // SEL_BLOCK)
    kvs_out[...] = jnp.where(first_half, kn[:, 0:LANES], jnp.where(lane == HEAD_DIM + blk_in_tile, 1.0, 0.0)).astype(BF16)
    vts_out[...] = jnp.where(first_half, 1.0, ks).T.astype(BF16)
    kvw_out[...] = jnp.where(first_half, kn[:, LANES:2 * LANES], kw).astype(BF16)
    vtw_out[...] = jnp.where(first_half, 1.0, kw).T.astype(BF16)
    gatet_out[...] = jax.nn.sigmoid(proj(640, 768)).T[0:GATE_ROWS, :]

    outg = outg_ref[...]
    lane2 = lax.broadcasted_iota(jnp.int32, (1, GROUP_WIDTH), 1)
    grp = lane2 // HEAD_DIM

    u = proj(768, 1024)
    vn = _rms(proj(1024, 1280), sgug_ref[...])
    t_i = lax.broadcasted_iota(jnp.int32, (SGU_CHUNK, N_HEADS * SGU_CHUNK), 0)
    s_i = lax.broadcasted_iota(jnp.int32, (SGU_CHUNK, N_HEADS * SGU_CHUNK), 1) & (SGU_CHUNK - 1)
    wm = jnp.where(s_i <= t_i, sguw_ref[...], 0.0).astype(BF16)
    vg = [jnp.where(grp == gi, vn, 0.0).astype(BF16) for gi in range(N_HEADS)]
    for c in range(tm // SGU_CHUNK):
        rows = slice(c * SGU_CHUNK, (c + 1) * SGU_CHUNK)
        mixed = _dot(wm, jnp.concatenate([v[rows, :] for v in vg], axis=0)) + sgub_ref[...]
        y_out[rows, 0:GROUP_WIDTH] = _rms(u[rows, :] * mixed, outg[0:1, :]).astype(BF16)

    first = (i % tiles_per_seq) == 0

    @pl.when(first)
    def _():
        pext[0:HIST, :] = jnp.zeros((HIST, GROUP_WIDTH), F32)
        zext[0:HIST, :] = jnp.zeros((HIST, GROUP_WIDTH), F32)

    @pl.when(jnp.logical_not(first))
    def _():
        pext[0:HIST, :] = pext[tm:tm + HIST, :]
        zext[0:HIST, :] = zext[tm:tm + HIST, :]

    pin = proj(1280, 1536)
    pext[HIST:, :] = pin
    pos = (i % tiles_per_seq) * tm + lax.broadcasted_iota(jnp.int32, (tm, 1), 0)
    run = pext[...]
    sums = []
    width = 1
    while width < POOL_WINDOWS[-1]:
        run = run + pltpu.roll(run, width, 0)
        width *= 2
        if width in POOL_WINDOWS:
            sums.append(run[HIST:, :])
    wsum = jnp.where(grp == 0, sums[0], jnp.where(grp == 1, sums[1], jnp.where(grp == 2, sums[2], sums[3])))
    win = jnp.where(grp == 0, POOL_WINDOWS[0], jnp.where(grp == 1, POOL_WINDOWS[1],
                                                         jnp.where(grp == 2, POOL_WINDOWS[2], POOL_WINDOWS[3])))
    cnt = jnp.minimum(pos + 1, win).astype(F32)
    dlt = (wsum / cnt - pin).astype(BF16)
    yc = _dot(dlt, poolw_ref[...]) * pools_ref[...]
    y_out[:, GROUP_WIDTH:2 * GROUP_WIDTH] = _rms(yc, outg[1:2, :]).astype(BF16)

    bg = proj(1536, 1792)
    z = proj(1792, 2048) * proj(2048, 2304)
    zext[HIST:, :] = z
    cw = convw_ref[...]
    conv = cw[2:3, :] * z + cw[1:2, :] * zext[HIST - 1:HIST - 1 + tm, :] + cw[0:1, :] * zext[HIST - 2:HIST - 2 + tm, :]
    y_out[:, 2 * GROUP_WIDTH:3 * GROUP_WIDTH] = _rms(bg * conv, outg[2:3, :]).astype(BF16)


def _proj_call(x, g, w, bd, cos, sa, sb, qg, kg, sgug, sguw, sgub, poolw, pools, convw, outg, *, seq, tm):
    n = x.shape[0]
    tps = seq // tm
    const = lambda shape: pl.BlockSpec(shape, lambda i: (0, 0))
    rows = lambda width: pl.BlockSpec((tm, width), lambda i: (i, 0))
    cols = lambda height: pl.BlockSpec((height, tm), lambda i: (0, i))
    tab = pl.BlockSpec((tm, LANES), lambda i: (i % tps, 0))
    return pl.pallas_call(
        functools.partial(_proj_kernel, tm=tm, tiles_per_seq=tps),
        grid=(n // tm,),
        in_specs=[rows(D_MODEL), const((1, D_MODEL)), const((D_MODEL, D_IN_PAD)), const((2 * LANES, 2 * LANES)),
                  tab, tab, tab, const((1, 2 * LANES)), const((1, 2 * LANES)),
                  const((1, GROUP_WIDTH)), const((SGU_CHUNK, N_HEADS * SGU_CHUNK)), const((SGU_CHUNK, GROUP_WIDTH)),
                  const((GROUP_WIDTH, GROUP_WIDTH)), const((1, GROUP_WIDTH)), const((CONV_WIDTH, GROUP_WIDTH)),
                  const((3, GROUP_WIDTH))],
        out_specs=[cols(N_HEADS * LANES), pl.BlockSpec((tm // CMP_STRIDE, CMP_STRIDE * LANES), lambda i: (i, 0)),
                   rows(LANES), cols(LANES), rows(LANES), cols(LANES),
                   cols(GATE_ROWS), rows(3 * GROUP_WIDTH)],
        out_shape=[jax.ShapeDtypeStruct((N_HEADS * LANES, n), BF16),
                   jax.ShapeDtypeStruct((n // CMP_STRIDE, CMP_STRIDE * LANES), F32),
                   jax.ShapeDtypeStruct((n, LANES), BF16), jax.ShapeDtypeStruct((LANES, n), BF16),
                   jax.ShapeDtypeStruct((n, LANES), BF16), jax.ShapeDtypeStruct((LANES, n), BF16),
                   jax.ShapeDtypeStruct((GATE_ROWS, n), F32), jax.ShapeDtypeStruct((n, 3 * GROUP_WIDTH), BF16)],
        scratch_shapes=[pltpu.VMEM((tm + HIST, GROUP_WIDTH), F32), pltpu.VMEM((tm + HIST, GROUP_WIDTH), F32),
                        pltpu.VMEM((tm, D_IN_PAD), F32), pltpu.VMEM((tm, LANES), F32)],
        compiler_params=pltpu.CompilerParams(dimension_semantics=("arbitrary",), vmem_limit_bytes=VMEM_LIMIT),
        name="proj_mixers",
    )(x, g, w, bd, cos, sa, sb, qg, kg, sgug, sguw, sgub, poolw, pools, convw, outg)


def _compress_kernel(t_ref, pe_ref, w1a_ref, w1b_ref, w2_ref, kg_ref, bd_ref, cos_ref, sa_ref, sb_ref, kv_ref, vt_ref):
    t = t_ref[0]
    ncp = t.shape[0]
    a = _dot((t + pe_ref[0:1, :]).astype(BF16), w1a_ref[...])
    b = _dot((t + pe_ref[1:2, :]).astype(BF16), w1b_ref[...])
    hid = jax.nn.gelu(a + pltpu.roll(b, ncp - 1, 0), approximate=True)
    kv = _dot(hid.astype(BF16), w2_ref[...])
    first_half = lax.broadcasted_iota(jnp.int32, (1, LANES), 1) < HEAD_DIM
    kn = _head_norm_rope(kv, kg_ref[...], cos_ref[...], sa_ref[...], sb_ref[...], bd_ref[...])
    kv_ref[0] = jnp.where(first_half, kn, kv).astype(BF16)
    vt_ref[0] = jnp.where(first_half, 1.0, kv).T.astype(BF16)


def _compress_call(t2, pe, w1a, w1b, w2, kg, bd, cos, sa, sb):
    b, ncp, width = t2.shape
    const = lambda shape: pl.BlockSpec(shape, lambda i: (0,) * len(shape))
    return pl.pallas_call(
        _compress_kernel,
        grid=(b,),
        in_specs=[pl.BlockSpec((1, ncp, width), lambda i: (i, 0, 0)), const(pe.shape), const(w1a.shape),
                  const(w1b.shape), const(w2.shape), const(kg.shape), const(bd.shape),
                  const(cos.shape), const(sa.shape), const(sb.shape)],
        out_specs=[pl.BlockSpec((1, ncp, LANES), lambda i: (i, 0, 0)), pl.BlockSpec((1, LANES, ncp), lambda i: (i, 0, 0))],
        out_shape=[jax.ShapeDtypeStruct((b, ncp, LANES), BF16), jax.ShapeDtypeStruct((b, LANES, ncp), BF16)],
        compiler_params=pltpu.CompilerParams(dimension_semantics=("arbitrary",), vmem_limit_bytes=VMEM_LIMIT),
        name="nsa_compress",
    )(t2, pe, w1a, w1b, w2, kg, bd, cos, sa, sb)


def _attn_kernel(qt_ref, gatet_ref, kvc_ref, vtc_ref, kvs_ref, vts_ref, kvw_ref, vtw_ref, ovt_ref, og_ref, o_ref,
                 bias_ref, sa_ref, sb_ref, m_ref, acc_ref, oc_ref, imp_ref, sw_ref, ow_ref,
                 *, tq, tk, n_sel):
    s0 = pl.program_id(1) * tq
    nc = N_HEADS * tq
    qs = jnp.concatenate([qt_ref[h * LANES:(h + 1) * LANES, :] for h in range(N_HEADS)], axis=1)
    t_col = s0 + (lax.broadcasted_iota(jnp.int32, (1, nc), 1) & (tq - 1))
    t_q = s0 + lax.broadcasted_iota(jnp.int32, (1, tq), 1)

    ncp = kvc_ref.shape[1]
    part = CMP_PART if ncp % CMP_PART == 0 else ncp

    def cmp_branch(rows):
        cmp_end = lax.broadcasted_iota(jnp.int32, (rows, 1), 0) * CMP_STRIDE + (CMP_LEN - 1)
        sc = jnp.where(cmp_end <= t_col, _dot(kvc_ref[0, 0:rows, :], qs), NEG_INF)
        e = jnp.exp2(sc - jnp.max(sc, axis=0, keepdims=True))
        p = e * jnp.where(t_col >= CMP_LEN - 1, 1.0 / jnp.sum(e, axis=0, keepdims=True), 0.0)
        oc_ref[...] = _dot(vtc_ref[0, :, 0:rows], p.astype(BF16))[HEAD_DIM:, :]
        psum = p[:, 0:tq] + p[:, tq:2 * tq] + p[:, 2 * tq:3 * tq] + p[:, 3 * tq:4 * tq]
        hi, lo = _split(psum)
        imp_ref[...] = _dot(ovt_ref[:, 0:rows], hi) + _dot(ovt_ref[:, 0:rows], lo)

    n_parts = (s0 + tq + CMP_STRIDE * part - 1) // (CMP_STRIDE * part)
    for k in range(1, ncp // part + 1):
        pl.when(n_parts == k)(functools.partial(cmp_branch, k * part))
    o_c = oc_ref[...]
    imp = imp_ref[...]

    n_seg = WINDOW // tq
    seg_start = [pl.multiple_of(jnp.maximum(s0 - k * tq, 0), tq) for k in range(n_seg + 1)]
    for k in range(n_seg + 1):
        sw_ref[k * tq:(k + 1) * tq, :] = _dot(kvw_ref[pl.ds(seg_start[k], tq), :], qs)

    nbp = imp.shape[0]
    blk = lax.broadcasted_iota(jnp.int32, (nbp, 1), 0)
    blk_f = blk.astype(F32)
    cur = t_q // SEL_BLOCK
    valid_b = blk <= cur
    forced = valid_b & ((blk == 0) | (blk == cur) | (blk == cur - 1))
    score = jnp.where(valid_b & jnp.logical_not(forced), imp, -1.0)
    sel = jnp.where(forced, 1.0, 0.0)
    for _ in range(n_sel - N_FORCED):
        top = jnp.max(score, axis=0, keepdims=True)
        idx = jnp.min(jnp.where(score == top, blk_f, float(nbp)), axis=0, keepdims=True)
        hit = blk_f == idx
        sel = jnp.where(hit & (top >= 0.0), 1.0, sel)
        score = jnp.where(hit, -2.0, score)
    bias = jnp.where(sel > 0.5, 0.0, NEG_INF)
    bias_ref[0:nbp, :] = jnp.concatenate([bias] * N_HEADS, axis=1)
    bias_ref[nbp:, :] = jnp.zeros((BIAS_PAD, nc), F32)

    off = lambda k: jnp.where(s0 >= k * tq, 0.0, NEG_INF)
    lower = lax.broadcasted_iota(jnp.int32, (tq, 1), 0) <= (lax.broadcasted_iota(jnp.int32, (1, nc), 1) & (tq - 1))
    s_tri = jnp.where(lower, sw_ref[0:tq, :], sw_ref[n_seg * tq:(n_seg + 1) * tq, :] + off(n_seg))
    s_mid = [sw_ref[k * tq:(k + 1) * tq, :] + off(k) for k in range(1, n_seg)]
    m_w = jnp.max(s_tri, axis=0, keepdims=True)
    for s_k in s_mid:
        m_w = jnp.maximum(m_w, jnp.max(s_k, axis=0, keepdims=True))
    p_tri = jnp.exp2(s_tri - m_w)
    acc_w = (_dot(vtw_ref[:, pl.ds(seg_start[0], tq)], jnp.where(lower, p_tri, 0.0).astype(BF16))
             + _dot(vtw_ref[:, pl.ds(seg_start[n_seg], tq)], jnp.where(lower, 0.0, p_tri).astype(BF16)))
    for k, s_k in zip(range(1, n_seg), s_mid):
        acc_w = acc_w + _dot(vtw_ref[:, pl.ds(seg_start[k], tq)], jnp.exp2(s_k - m_w).astype(BF16))
    ow_ref[...] = acc_w[HEAD_DIM:, :] / acc_w[0:1, :]

    n_blk = tk // SEL_BLOCK
    q_top = qs[0:HEAD_DIM, :]
    q_rest = jnp.zeros((LANES - HEAD_DIM - BIAS_PAD, nc), BF16)

    def scores(j):
        tile_bias = bias_ref[pl.ds(pl.multiple_of(j * n_blk, n_blk), BIAS_PAD), :].astype(BF16)
        qb = jnp.concatenate([q_top, tile_bias, q_rest], axis=0)
        return _dot(kvs_ref[pl.ds(pl.multiple_of(j * tk, tk), tk), :], qb)

    def consume(s_ref, j, causal):
        k0 = pl.multiple_of(j * tk, tk)
        s = s_ref[...]
        if causal:
            kpos = k0 + lax.broadcasted_iota(jnp.int32, (tk, 1), 0)
            s = jnp.where(kpos <= t_col, s, NEG_INF)
        m = m_ref[...]
        m_new = jnp.maximum(m, jnp.max(s, axis=0, keepdims=True))
        pe = jnp.exp2(s - m_new).astype(BF16)
        acc_ref[...] = jnp.exp2(m - m_new) * acc_ref[...] + _dot(vts_ref[:, pl.ds(k0, tk)], pe)
        m_ref[...] = m_new

    n_full = (s0 + tq + tk - 1) // tk - 1
    m_ref[...] = jnp.full((1, nc), NEG_INF, F32)
    acc_ref[...] = jnp.zeros((LANES, nc), F32)
    sa_ref[...] = scores(0)

    def pair(i, carry):
        j = 2 * i
        sb_ref[...] = scores(j + 1)
        consume(sa_ref, j, False)
        sa_ref[...] = scores(jnp.minimum(j + 2, n_full))
        consume(sb_ref, j + 1, False)
        return carry

    lax.fori_loop(0, n_full // 2, pair, 0)

    @pl.when(n_full % 2 == 1)
    def _():
        sb_ref[...] = scores(n_full)
        consume(sa_ref, n_full - 1, False)
        consume(sb_ref, n_full, True)

    @pl.when(n_full % 2 == 0)
    def _():
        consume(sa_ref, n_full, True)

    acc_s = acc_ref[...]
    o_s = acc_s[HEAD_DIM:, :] / acc_s[0:1, :]

    g = gatet_ref[...]
    heads = []
    for h in range(N_HEADS):
        c = slice(h * tq, (h + 1) * tq)
        r = h * N_BRANCH
        heads.append(g[r:r + 1, :] * o_c[:, c] + g[r + 1:r + 2, :] * o_s[:, c] + g[r + 2:r + 3, :] * ow_ref[:, c])
    ya = jnp.concatenate(heads, axis=0).T
    o_ref[...] = _rms(ya, og_ref[...]).astype(BF16)


def _attn_call(qt, gatet, kvc, vtc, kvs, vts, kvw, vtw, ovt, og, *, seq, tq, tk):
    n = qt.shape[1]
    b = n // seq
    nq = seq // tq
    ncp = kvc.shape[1]
    nbp = ovt.shape[0]
    n_sel = min(SEL_TOPK, seq // SEL_BLOCK)
    cols = lambda height: pl.BlockSpec((height, tq), lambda bi, qi: (0, bi * nq + qi))
    per_b = lambda shape: pl.BlockSpec((1,) + shape, lambda bi, qi: (bi, 0, 0))
    const = lambda shape: pl.BlockSpec(shape, lambda bi, qi: (0, 0))
    seq_rows = pl.BlockSpec((seq, LANES), lambda bi, qi: (bi, 0))
    seq_cols = pl.BlockSpec((LANES, seq), lambda bi, qi: (0, bi))
    return pl.pallas_call(
        functools.partial(_attn_kernel, tq=tq, tk=tk, n_sel=n_sel),
        grid=(b, nq),
        in_specs=[cols(N_HEADS * LANES), cols(GATE_ROWS), per_b((ncp, LANES)), per_b((LANES, ncp)),
                  seq_rows, seq_cols, seq_rows, seq_cols, const(ovt.shape), const((1, GROUP_WIDTH))],
        out_specs=pl.BlockSpec((tq, GROUP_WIDTH), lambda bi, qi: (bi * nq + qi, 0)),
        out_shape=jax.ShapeDtypeStruct((n, GROUP_WIDTH), BF16),
        scratch_shapes=[pltpu.VMEM((nbp + BIAS_PAD, N_HEADS * tq), F32), pltpu.VMEM((tk, N_HEADS * tq), F32),
                        pltpu.VMEM((tk, N_HEADS * tq), F32), pltpu.VMEM((1, N_HEADS * tq), F32),
                        pltpu.VMEM((LANES, N_HEADS * tq), F32), pltpu.VMEM((HEAD_DIM, N_HEADS * tq), F32),
                        pltpu.VMEM((nbp, tq), F32), pltpu.VMEM((WINDOW + tq, N_HEADS * tq), F32),
                        pltpu.VMEM((HEAD_DIM, N_HEADS * tq), F32)],
        compiler_params=pltpu.CompilerParams(dimension_semantics=("arbitrary", "arbitrary"),
                                             vmem_limit_bytes=VMEM_LIMIT),
        name="nsa_attention",
    )(qt, gatet, kvc, vtc, kvs, vts, kvw, vtw, ovt, og)


def _outproj_kernel(x_ref, ya_ref, yb_ref, w_ref, g_ref, *rest, moe):
    if moe:
        rw_ref, x1_ref, hn_ref, gate_ref = rest
    else:
        x1_ref, hn_ref = rest
    x1 = x_ref[...] + _dot(ya_ref[...], w_ref[0:GROUP_WIDTH, :]) + _dot(yb_ref[...], w_ref[GROUP_WIDTH:, :])
    x1_ref[...] = x1
    hn = _rms(x1, g_ref[...])
    hn_ref[...] = hn.astype(BF16)
    if moe:
        hi, lo = _split(hn)
        both = _dot(hi, rw_ref[...])
        logits = both[:, 0:LANES] + both[:, LANES:] + _dot(lo, rw_ref[:, 0:LANES])
        lane = lax.broadcasted_iota(jnp.int32, (1, LANES), 1)
        lane_f = lane.astype(F32)
        logits = jnp.where(lane < N_EXPERTS, logits, NEG_INF)
        m1 = jnp.max(logits, axis=-1, keepdims=True)
        i1 = jnp.min(jnp.where(logits == m1, lane_f, float(LANES)), axis=-1, keepdims=True)
        rest_l = jnp.where(lane_f == i1, 2 * NEG_INF, logits)
        m2 = jnp.max(rest_l, axis=-1, keepdims=True)
        i2 = jnp.min(jnp.where(rest_l == m2, lane_f, float(LANES)), axis=-1, keepdims=True)
        e2 = jnp.exp(m2 - m1)
        den = 1.0 + e2
        gate_ref[...] = jnp.where(lane_f == i1, 1.0 / den, jnp.where(lane_f == i2, e2 / den, 0.0))


def _outproj_call(x, ya, yb, w, g, rw, *, tm):
    n = x.shape[0]
    moe = rw is not None
    rows = lambda width: pl.BlockSpec((tm, width), lambda i: (i, 0))
    const = lambda shape: pl.BlockSpec(shape, lambda i: (0,) * len(shape))
    in_specs = [rows(D_MODEL), rows(GROUP_WIDTH), rows(3 * GROUP_WIDTH), const((D_MODEL, D_MODEL)), const((1, D_MODEL))]
    out_specs = [rows(D_MODEL), rows(D_MODEL)]
    out_shape = [jax.ShapeDtypeStruct((n, D_MODEL), F32), jax.ShapeDtypeStruct((n, D_MODEL), BF16)]
    args = [x, ya, yb, w, g]
    if moe:
        in_specs.append(const(rw.shape))
        out_specs.append(rows(LANES))
        out_shape.append(jax.ShapeDtypeStruct((n, LANES), F32))
        args.append(rw)
    return pl.pallas_call(
        functools.partial(_outproj_kernel, moe=moe),
        grid=(n // tm,),
        in_specs=in_specs, out_specs=out_specs, out_shape=out_shape,
        compiler_params=pltpu.CompilerParams(dimension_semantics=("arbitrary",), vmem_limit_bytes=VMEM_LIMIT),
        name="outproj_router" if moe else "outproj",
    )(*args)


def _ffn_kernel(x_ref, ya_ref, yb_ref, wo_ref, g_ref, w1_ref, w3_ref, w2_ref, o_ref, acc_ref, hn_ref):
    f = pl.program_id(1)

    @pl.when(f == 0)
    def _():
        x1 = x_ref[...] + _dot(ya_ref[...], wo_ref[0:GROUP_WIDTH, :]) + _dot(yb_ref[...], wo_ref[GROUP_WIDTH:, :])
        acc_ref[...] = x1
        hn_ref[...] = _rms(x1, g_ref[...]).astype(BF16)

    h = hn_ref[...]
    a = _dot(h, w1_ref[...])
    t = (a * jax.nn.sigmoid(a) * _dot(h, w3_ref[...])).astype(BF16)
    acc_ref[...] += _dot(t, w2_ref[...])

    @pl.when(f == pl.num_programs(1) - 1)
    def _():
        o_ref[...] = acc_ref[...]


def _ffn_call(x, ya, yb, wo, g, w1, w3, w2, *, tm, tf):
    n = x.shape[0]
    rows = lambda width: pl.BlockSpec((tm, width), lambda i, f: (i, 0))
    const = lambda shape: pl.BlockSpec(shape, lambda i, f: (0, 0))
    up = pl.BlockSpec((D_MODEL, tf), lambda i, f: (0, f))
    return pl.pallas_call(
        _ffn_kernel,
        grid=(n // tm, D_FF // tf),
        in_specs=[rows(D_MODEL), rows(GROUP_WIDTH), rows(3 * GROUP_WIDTH), const((D_MODEL, D_MODEL)), const((1, D_MODEL)),
                  up, up, pl.BlockSpec((tf, D_MODEL), lambda i, f: (f, 0))],
        out_specs=rows(D_MODEL),
        out_shape=jax.ShapeDtypeStruct((n, D_MODEL), F32),
        scratch_shapes=[pltpu.VMEM((tm, D_MODEL), F32), pltpu.VMEM((tm, D_MODEL), BF16)],
        compiler_params=pltpu.CompilerParams(dimension_semantics=("arbitrary", "arbitrary"),
                                             vmem_limit_bytes=VMEM_LIMIT),
        name="outproj_ffn_swiglu",
    )(x, ya, yb, wo, g, w1, w3, w2)


def _moe_kernel(h_ref, x_hbm, gate_ref, w1_ref, w3_ref, w2_ref, o_ref,
                xc_ref, yc_ref, rank_ref, gate_t_ref, rank_t_ref, start_ref, x_sem, *, n_sub):
    e = pl.program_id(1)
    f = pl.program_id(2)
    rb = MOE_ROWS
    sub = MOE_SUB

    def for_chunks(c_lo, c_hi, fn, group=2):
        n = c_hi - c_lo

        def full(c, carry):
            fn(pl.multiple_of((c_lo + group * c) * rb, rb), group * rb)
            return carry

        lax.fori_loop(0, n // group, full, 0)
        done = c_lo + n - n % group
        size = group // 2
        while size >= 1:
            taken = n % (2 * size) >= size
            pl.when(taken)(functools.partial(fn, pl.multiple_of(done * rb, rb), size * rb))
            done = done + jnp.where(taken, size, 0)
            size //= 2

    def sub_chunks(s):
        return start_ref[s] // rb, (start_ref[s + 1] + rb - 1) // rb

    @pl.when((e == 0) & (f == 0))
    def _():
        tm = n_sub * sub
        x_copy = pltpu.make_async_copy(x_hbm.at[pl.ds(pl.multiple_of(pl.program_id(0) * tm, tm), tm), :], o_ref, x_sem)
        x_copy.start()
        before = jnp.where(lax.broadcasted_iota(jnp.int32, (sub, sub), 1)
                           < lax.broadcasted_iota(jnp.int32, (sub, sub), 0), 1.0, 0.0).astype(BF16)
        routed_before = jnp.zeros((1, LANES), F32)
        for s in range(n_sub):
            span = slice(s * sub, (s + 1) * sub)
            gate = gate_ref[span, :]
            routed = jnp.where(gate > 0.0, 1.0, 0.0)
            rank = _dot(before, routed.astype(BF16)) + routed_before
            rank_ref[span, :] = rank
            gate_t_ref[:, span] = gate.T[0:N_EXPERTS]
            rank_t_ref[:, span] = rank.T[0:N_EXPERTS]
            routed_before = routed_before + jnp.sum(routed, axis=0, keepdims=True)
        x_copy.wait()

    @pl.when(f == 0)
    def _():
        start_ref[0] = 0
        for s in range(n_sub):
            g_row = gate_t_ref[pl.ds(e, 1), s * sub:(s + 1) * sub]
            start_ref[s + 1] = start_ref[s] + jnp.sum(jnp.where(g_row > 0.0, 1.0, 0.0)).astype(jnp.int32)

        def clear(r0, rows):
            xc_ref[pl.ds(r0, rows), :] = jnp.zeros((rows, D_MODEL), BF16)
            yc_ref[pl.ds(r0, rows), :] = jnp.zeros((rows, D_MODEL), F32)

        for_chunks(0, (start_ref[n_sub] + rb - 1) // rb, clear)

        for s in range(n_sub):
            span = slice(s * sub, (s + 1) * sub)
            g_row = gate_t_ref[pl.ds(e, 1), span]
            key = jnp.where(g_row > 0.0, rank_t_ref[pl.ds(e, 1), span], -1.0)

            def gather(r0, rows, span=span, key=key):
                slot = (r0 + lax.broadcasted_iota(jnp.int32, (rows, 1), 0)).astype(F32)
                onehot = jnp.where(key == slot, 1.0, 0.0).astype(BF16)
                dst = pl.ds(r0, rows)
                xc_ref[dst, :] = (xc_ref[dst, :].astype(F32) + _dot(onehot, h_ref[span, :])).astype(BF16)

            for_chunks(*sub_chunks(s), gather)

    def swiglu(r0, rows):
        src = pl.ds(r0, rows)
        xc = xc_ref[src, :]
        a = _dot(xc, w1_ref[...])
        t = (a * jax.nn.sigmoid(a) * _dot(xc, w3_ref[...])).astype(BF16)
        yc_ref[src, :] += _dot(t, w2_ref[...])

    for_chunks(0, (start_ref[n_sub] + rb - 1) // rb, swiglu, group=MOE_SWIGLU_GROUP)

    @pl.when(f == pl.num_programs(2) - 1)
    def _():
        lane = lax.broadcasted_iota(jnp.int32, (1, LANES), 1)
        for s in range(n_sub):
            span = slice(s * sub, (s + 1) * sub)
            g_col = jnp.sum(jnp.where(lane == e, gate_ref[span, :], 0.0), axis=-1, keepdims=True)
            r_col = jnp.sum(jnp.where(lane == e, rank_ref[span, :], 0.0), axis=-1, keepdims=True)
            key = jnp.where(g_col > 0.0, r_col, -1.0)

            def scatter(r0, rows, span=span, key=key, g_col=g_col):
                slot = (r0 + lax.broadcasted_iota(jnp.int32, (1, rows), 1)).astype(F32)
                onehot = jnp.where(key == slot, 1.0, 0.0).astype(BF16)
                o_ref[span, :] += g_col * _dot(onehot, yc_ref[pl.ds(r0, rows), :].astype(BF16))

            for_chunks(*sub_chunks(s), scatter)


def _moe_call(h, x, gate, w1, w3, w2, *, tm, tf):
    n = h.shape[0]
    n_sub = tm // MOE_SUB
    once = pl.Buffered(1)
    rows = lambda width, mode=None: pl.BlockSpec((tm, width), lambda i, e, f: (i, 0), pipeline_mode=mode)
    return pl.pallas_call(
        functools.partial(_moe_kernel, n_sub=n_sub),
        grid=(n // tm, N_EXPERTS, D_FF // tf),
        in_specs=[rows(D_MODEL, once), pl.BlockSpec(memory_space=pl.ANY), rows(LANES),
                  pl.BlockSpec((None, D_MODEL, tf), lambda i, e, f: (e, 0, f)),
                  pl.BlockSpec((None, D_MODEL, tf), lambda i, e, f: (e, 0, f)),
                  pl.BlockSpec((None, tf, D_MODEL), lambda i, e, f: (e, f, 0))],
        out_specs=rows(D_MODEL, once),
        out_shape=jax.ShapeDtypeStruct((n, D_MODEL), F32),
        scratch_shapes=[pltpu.VMEM((tm, D_MODEL), BF16), pltpu.VMEM((tm, D_MODEL), F32), pltpu.VMEM((tm, LANES), F32),
                        pltpu.VMEM((N_EXPERTS, tm), F32), pltpu.VMEM((N_EXPERTS, tm), F32),
                        pltpu.SMEM((n_sub + 1,), jnp.int32), pltpu.SemaphoreType.DMA(())],
        compiler_params=pltpu.CompilerParams(dimension_semantics=("arbitrary", "arbitrary", "arbitrary"),
                                             vmem_limit_bytes=VMEM_LIMIT),
        name="moe_swiglu",
    )(h, x, gate, w1, w3, w2)


def _rope_tables(pos):
    half = ROPE_DIM // 2
    inv_freq = ROPE_THETA ** (-np.arange(half, dtype=np.float64) / half)
    ang = pos.astype(np.float64)[..., None] * inv_freq
    cos, sin = np.cos(ang).astype(np.float32), np.sin(ang).astype(np.float32)
    n = pos.shape[0]
    rest = HEAD_DIM - ROPE_DIM
    cos64 = np.concatenate([cos, cos, np.ones((n, rest), np.float32)], axis=-1)
    sa64 = np.concatenate([-sin, np.zeros((n, rest + half), np.float32)], axis=-1)
    sb64 = np.concatenate([np.zeros((n, half), np.float32), sin, np.zeros((n, rest), np.float32)], axis=-1)
    return tuple(jnp.asarray(np.tile(t, (1, 2)), F32) for t in (cos64, sa64, sb64))


def _block_diag(blocks):
    n = len(blocks)
    rows = []
    for i, blk in enumerate(blocks):
        rows.append(jnp.concatenate([blk if j == i else jnp.zeros((blk.shape[0], blocks[j].shape[1]), blk.dtype)
                                     for j in range(n)], axis=1))
    return jnp.concatenate(rows, axis=0)


def _layout_w_in(w):
    n_gate = N_BRANCH * N_HEADS
    g0 = GROUP_WIDTH + 6 * HEAD_DIM
    pad = jnp.zeros(w.shape[:-1] + (LANES - n_gate,), w.dtype)
    return jnp.concatenate([w[..., :g0 + n_gate], pad, w[..., g0 + n_gate:]], axis=-1)


def _layout_compress(pe, w1, w2):
    half = CMP_LEN // 2
    pe2 = jnp.transpose(pe, (1, 0, 2)).reshape(2, half * LANES)
    w1r = w1.reshape(2, 2, half, HEAD_DIM, CMP_HIDDEN)
    zeros = jnp.zeros((half, HEAD_DIM, CMP_HIDDEN), w1.dtype)

    def big(part):
        k_rows = jnp.concatenate([w1r[0, part], zeros], axis=-1)
        v_rows = jnp.concatenate([zeros, w1r[1, part]], axis=-1)
        return jnp.concatenate([k_rows, v_rows], axis=1).reshape(half * LANES, 2 * CMP_HIDDEN)

    return pe2, big(0), big(1), _block_diag([w2[0], w2[1]])


def _overlap_matrix(seq, ncp, nbp):
    cs = jnp.arange(ncp) * CMP_STRIDE
    bs = jnp.arange(nbp) * SEL_BLOCK
    ov = jnp.clip(jnp.minimum(cs[:, None] + CMP_LEN, bs[None, :] + SEL_BLOCK)
                  - jnp.maximum(cs[:, None], bs[None, :]), 0, None).astype(F32) / CMP_LEN
    keep = (jnp.arange(ncp)[:, None] < seq // CMP_STRIDE - 1) & (jnp.arange(nbp)[None, :] < seq // SEL_BLOCK)
    return jnp.where(keep, ov, 0.0)


def kernel(x, attn_norm_g, w_in, q_norm_g, k_norm_g, cmp_pe, cmp_w1, cmp_w2, sgu_norm_g, sgu_w, sgu_b, pool_w,
           pool_scale, conv_w, mix_out_norm_g, w_out, ffn_norm_g, ffn_w1, ffn_w3, ffn_w2, router_w, expert_w1,
           expert_w3, expert_w2):
    b, seq, _ = x.shape
    n = b * seq
    depth = w_in.shape[0]
    tm = TOKEN_TILE
    ncp = seq // CMP_STRIDE
    nbp = max(LANES, seq // SEL_BLOCK)

    cos, sa, sb = _rope_tables(np.arange(seq))
    cos_c, sa_c, sb_c = _rope_tables(np.arange(ncp) * CMP_STRIDE + CMP_LEN - 1)
    bd = _block_diag([jnp.full((HEAD_DIM, HEAD_DIM), 1.0 / HEAD_DIM, F32)] * 4).astype(BF16)
    overlap_t = _overlap_matrix(seq, ncp, nbp).T.astype(BF16)
    row = lambda v: v.reshape(1, -1)
    two = lambda v: jnp.tile(v, 2).reshape(1, LANES)

    w_in_b = _layout_w_in(w_in.astype(BF16))
    xf = x.reshape(n, D_MODEL)
    for layer in range(depth):
        kg = k_norm_g[layer]
        sgub = jnp.repeat(sgu_b[layer].T, HEAD_DIM, axis=1)
        qt, kvc, kvs, vts, kvw, vtw, gatet, ybcd = _proj_call(
            xf, row(attn_norm_g[layer]), w_in_b[layer], bd, cos, sa, sb,
            jnp.tile(q_norm_g[layer], 4).reshape(1, 2 * LANES), jnp.tile(kg[1:3], (1, 2)).reshape(1, 2 * LANES),
            row(sgu_norm_g[layer]), sgu_w[layer].transpose(1, 0, 2).reshape(SGU_CHUNK, N_HEADS * SGU_CHUNK), sgub,
            _block_diag([pool_w[layer, i] for i in range(len(POOL_WINDOWS))]).astype(BF16),
            row(pool_scale[layer]), conv_w[layer], mix_out_norm_g[layer].reshape(4, GROUP_WIDTH)[1:],
            seq=seq, tm=tm)
        pe2, w1a, w1b, w2c = _layout_compress(cmp_pe[layer], cmp_w1[layer], cmp_w2[layer])
        kvcc, vtc = _compress_call(kvc.reshape(b, ncp, CMP_STRIDE * LANES), pe2, w1a.astype(BF16), w1b.astype(BF16),
                                   w2c.astype(BF16), two(kg[0]), bd, cos_c, sa_c, sb_c)
        ya = _attn_call(qt, gatet, kvcc, vtc, kvs, vts, kvw, vtw, overlap_t,
                        row(mix_out_norm_g[layer, :GROUP_WIDTH]), seq=seq, tq=ATTN_Q_TILE, tk=SEL_KEY_TILE)
        i = layer // 2
        if layer % 2 == 0:
            xf = _ffn_call(xf, ya, ybcd, w_out[layer].astype(BF16), row(ffn_norm_g[layer]), ffn_w1[i].astype(BF16),
                           ffn_w3[i].astype(BF16), ffn_w2[i].astype(BF16), tm=FFN_TOKEN_TILE, tf=FFN_FF_TILE)
        else:
            r = jnp.pad(router_w[i], ((0, 0), (0, LANES - N_EXPERTS)))
            r_hi = r.astype(BF16)
            rw = jnp.concatenate([r_hi, (r - r_hi.astype(F32)).astype(BF16)], axis=1)
            x1, hn, gate = _outproj_call(xf, ya, ybcd, w_out[layer].astype(BF16), row(ffn_norm_g[layer]), rw, tm=tm)
            xf = _moe_call(hn, x1, gate, expert_w1[i].astype(BF16), expert_w3[i].astype(BF16),
                           expert_w2[i].astype(BF16), tm=2 * MOE_SUB, tf=MOE_FF_TILE)
    return xf.reshape(b, seq, D_MODEL)
```

```python
import functools

import numpy as np
import jax
import jax.numpy as jnp
from jax import lax
from jax.experimental import pallas as pl
from jax.experimental.pallas import tpu as pltpu

D_MODEL = 1024
HEAD_DIM = 64
N_HEADS = 4
GROUP_WIDTH = 256
CMP_LEN = 32
CMP_STRIDE = 16
CMP_HIDDEN = 128
SEL_BLOCK = 64
SEL_TOPK = 16
WINDOW = 512
N_BRANCH = 3
ROPE_THETA = 500000.0
ROPE_DIM = 16
SGU_CHUNK = 128
POOL_WINDOWS = (2, 4, 8, 16)
CONV_WIDTH = 3
D_FF = 3584
N_EXPERTS = 8
EPS = 1e-6
NEG_INF = -1e30
Q_SCALE = HEAD_DIM ** -0.5 * 1.4426950408889634
CMP_PART = 128
N_FORCED = 3
SEL_KEY_TILE = 512
BIAS_PAD = 16
GATE_ROWS = 16

LANES = 128
MOE_ROWS = 128
MOE_FF_TILE = 1792
MOE_SWIGLU_GROUP = 4
MOE_SUB = 1024
HIST = 16
D_IN_PAD = 2304
TOKEN_TILE = 512
ATTN_Q_TILE = 512
FFN_TOKEN_TILE = 512
FFN_FF_TILE = 1792
VMEM_LIMIT = 56 * 1024 * 1024

F32 = jnp.float32
BF16 = jnp.bfloat16


def _dot(a, b):
    return jnp.dot(a, b, preferred_element_type=F32)


def _split(a):
    hi = a.astype(BF16)
    lo = (a - hi.astype(F32)).astype(BF16)
    return hi, lo


def _rms(y, g):
    return y * lax.rsqrt(jnp.mean(y * y, axis=-1, keepdims=True) + EPS) * g


def _head_norm_rope(c, g, cos, sa, sb, bd):
    width = c.shape[1]
    if width > LANES:
        cos, sa, sb = (jnp.concatenate([t] * (width // LANES), axis=1) for t in (cos, sa, sb))
    hi, lo = _split(c * c)
    msq = _dot(hi, bd[0:width, 0:width]) + _dot(lo, bd[0:width, 0:width])
    cn = c * lax.rsqrt(msq + EPS) * g
    return cn * cos + pltpu.roll(cn, width - ROPE_DIM // 2, 1) * sa + pltpu.roll(cn, ROPE_DIM // 2, 1) * sb


def _proj_kernel(x_ref, g_ref, w_ref, bd_ref, cos_ref, sa_ref, sb_ref, qg_ref, kg_ref,
                 sgug_ref, sguw_ref, sgub_ref, poolw_ref, pools_ref, convw_ref, outg_ref,
                 qt_out, kvc_out, kvs_out, vts_out, kvw_out, vtw_out, gatet_out, y_out, pext, zext, p_ref, kc_ref,
                 *, tm, tiles_per_seq):
    i = pl.program_id(0)
    x = x_ref[...]
    hn = _rms(x, g_ref[...]).astype(BF16)
    bd = bd_ref[...]
    cos, sa, sb = cos_ref[...], sa_ref[...], sb_ref[...]
    lane = lax.broadcasted_iota(jnp.int32, (1, LANES), 1)
    first_half = lane < HEAD_DIM

    for c0 in range(0, D_IN_PAD, 2 * LANES):
        p_ref[:, c0:c0 + 2 * LANES] = _dot(hn, w_ref[:, c0:c0 + 2 * LANES])

    def proj(c0, c1):
        return p_ref[:, c0:c1]

    qn = _head_norm_rope(proj(0, 256), qg_ref[...], cos, sa, sb, bd) * Q_SCALE
    for c in range(2):
        qc = qn[:, c * LANES:(c + 1) * LANES]
        qt_out[(2 * c) * LANES:(2 * c + 1) * LANES, :] = jnp.where(first_half, qc, 0.0).T.astype(BF16)
        qt_out[(2 * c + 1) * LANES:(2 * c + 2) * LANES, :] = jnp.where(
            first_half, pltpu.roll(qc, HEAD_DIM, 1), 0.0).T.astype(BF16)

    kc_ref[...] = proj(256, 384)
    for r in range(CMP_STRIDE):
        kvc_out[:, r * LANES:(r + 1) * LANES] = kc_ref[pl.ds(r, tm // CMP_STRIDE, stride=CMP_STRIDE), :]

    kk = proj(384, 640)
    kn = _head_norm_rope(kk, kg_ref[...], cos, sa, sb, bd)
    ks, kw = kk[:, 0:LANES], kk[:, LANES:2 * LANES]
    tok = (i % tiles_per_seq) * tm + lax.broadcasted_iota(jnp.int32, (tm, 1), 0)
    blk_in_tile = (tok // SEL_BLOCK) % (SEL_KEY_TILE // SEL_BLOCK)
    kvs_out[...] = jnp.where(first_half, kn[:, 0:LANES], jnp.where(lane == HEAD_DIM + blk_in_tile, 1.0, 0.0)).astype(BF16)
    vts_out[...] = jnp.where(first_half, 1.0, ks).T.astype(BF16)
    kvw_out[...] = jnp.where(first_half, kn[:, LANES:2 * LANES], kw).astype(BF16)
    vtw_out[...] = jnp.where(first_half, 1.0, kw).T.astype(BF16)
    gatet_out[...] = jax.nn.sigmoid(proj(640, 768)).T[0:GATE_ROWS, :]

    outg = outg_ref[...]
    lane2 = lax.broadcasted_iota(jnp.int32, (1, GROUP_WIDTH), 1)
    grp = lane2 // HEAD_DIM

    u = proj(768, 1024)
    vn = _rms(proj(1024, 1280), sgug_ref[...])
    t_i = lax.broadcasted_iota(jnp.int32, (SGU_CHUNK, N_HEADS * SGU_CHUNK), 0)
    s_i = lax.broadcasted_iota(jnp.int32, (SGU_CHUNK, N_HEADS * SGU_CHUNK), 1) & (SGU_CHUNK - 1)
    wm = jnp.where(s_i <= t_i, sguw_ref[...], 0.0).astype(BF16)
    vg = [jnp.where(grp == gi, vn, 0.0).astype(BF16) for gi in range(N_HEADS)]
    for c in range(tm // SGU_CHUNK):
        rows = slice(c * SGU_CHUNK, (c + 1) * SGU_CHUNK)
        mixed = _dot(wm, jnp.concatenate([v[rows, :] for v in vg], axis=0)) + sgub_ref[...]
        y_out[rows, 0:GROUP_WIDTH] = _rms(u[rows, :] * mixed, outg[0:1, :]).astype(BF16)

    first = (i % tiles_per_seq) == 0

    @pl.when(first)
    def _():
        pext[0:HIST, :] = jnp.zeros((HIST, GROUP_WIDTH), F32)
        zext[0:HIST, :] = jnp.zeros((HIST, GROUP_WIDTH), F32)

    @pl.when(jnp.logical_not(first))
    def _():
        pext[0:HIST, :] = pext[tm:tm + HIST, :]
        zext[0:HIST, :] = zext[tm:tm + HIST, :]

    pin = proj(1280, 1536)
    pext[HIST:, :] = pin
    pos = (i % tiles_per_seq) * tm + lax.broadcasted_iota(jnp.int32, (tm, 1), 0)
    run = pext[...]
    sums = []
    width = 1
    while width < POOL_WINDOWS[-1]:
        run = run + pltpu.roll(run, width, 0)
        width *= 2
        if width in POOL_WINDOWS:
            sums.append(run[HIST:, :])
    wsum = jnp.where(grp == 0, sums[0], jnp.where(grp == 1, sums[1], jnp.where(grp == 2, sums[2], sums[3])))
    win = jnp.where(grp == 0, POOL_WINDOWS[0], jnp.where(grp == 1, POOL_WINDOWS[1],
                                                         jnp.where(grp == 2, POOL_WINDOWS[2], POOL_WINDOWS[3])))
    cnt = jnp.minimum(pos + 1, win).astype(F32)
    dlt = (wsum / cnt - pin).astype(BF16)
    yc = _dot(dlt, poolw_ref[...]) * pools_ref[...]
    y_out[:, GROUP_WIDTH:2 * GROUP_WIDTH] = _rms(yc, outg[1:2, :]).astype(BF16)

    bg = proj(1536, 1792)
    z = proj(1792, 2048) * proj(2048, 2304)
    zext[HIST:, :] = z
    cw = convw_ref[...]
    conv = cw[2:3, :] * z + cw[1:2, :] * zext[HIST - 1:HIST - 1 + tm, :] + cw[0:1, :] * zext[HIST - 2:HIST - 2 + tm, :]
    y_out[:, 2 * GROUP_WIDTH:3 * GROUP_WIDTH] = _rms(bg * conv, outg[2:3, :]).astype(BF16)


def _proj_call(x, g, w, bd, cos, sa, sb, qg, kg, sgug, sguw, sgub, poolw, pools, convw, outg, *, seq, tm):
    n = x.shape[0]
    tps = seq // tm
    const = lambda shape: pl.BlockSpec(shape, lambda i: (0, 0))
    rows = lambda width: pl.BlockSpec((tm, width), lambda i: (i, 0))
    cols = lambda height: pl.BlockSpec((height, tm), lambda i: (0, i))
    tab = pl.BlockSpec((tm, LANES), lambda i: (i % tps, 0))
    return pl.pallas_call(
        functools.partial(_proj_kernel, tm=tm, tiles_per_seq=tps),
        grid=(n // tm,),
        in_specs=[rows(D_MODEL), const((1, D_MODEL)), const((D_MODEL, D_IN_PAD)), const((2 * LANES, 2 * LANES)),
                  tab, tab, tab, const((1, 2 * LANES)), const((1, 2 * LANES)),
                  const((1, GROUP_WIDTH)), const((SGU_CHUNK, N_HEADS * SGU_CHUNK)), const((SGU_CHUNK, GROUP_WIDTH)),
                  const((GROUP_WIDTH, GROUP_WIDTH)), const((1, GROUP_WIDTH)), const((CONV_WIDTH, GROUP_WIDTH)),
                  const((3, GROUP_WIDTH))],
        out_specs=[cols(N_HEADS * LANES), pl.BlockSpec((tm // CMP_STRIDE, CMP_STRIDE * LANES), lambda i: (i, 0)),
                   rows(LANES), cols(LANES), rows(LANES), cols(LANES),
                   cols(GATE_ROWS), rows(3 * GROUP_WIDTH)],
        out_shape=[jax.ShapeDtypeStruct((N_HEADS * LANES, n), BF16),
                   jax.ShapeDtypeStruct((n // CMP_STRIDE, CMP_STRIDE * LANES), F32),
                   jax.ShapeDtypeStruct((n, LANES), BF16), jax.ShapeDtypeStruct((LANES, n), BF16),
                   jax.ShapeDtypeStruct((n, LANES), BF16), jax.ShapeDtypeStruct((LANES, n), BF16),
                   jax.ShapeDtypeStruct((GATE_ROWS, n), F32), jax.ShapeDtypeStruct((n, 3 * GROUP_WIDTH), BF16)],
        scratch_shapes=[pltpu.VMEM((tm + HIST, GROUP_WIDTH), F32), pltpu.VMEM((tm + HIST, GROUP_WIDTH), F32),
                        pltpu.VMEM((tm, D_IN_PAD), F32), pltpu.VMEM((tm, LANES), F32)],
        compiler_params=pltpu.CompilerParams(dimension_semantics=("arbitrary",), vmem_limit_bytes=VMEM_LIMIT),
        name="proj_mixers",
    )(x, g, w, bd, cos, sa, sb, qg, kg, sgug, sguw, sgub, poolw, pools, convw, outg)


def _compress_kernel(t_ref, pe_ref, w1a_ref, w1b_ref, w2_ref, kg_ref, bd_ref, cos_ref, sa_ref, sb_ref, kv_ref, vt_ref):
    t = t_ref[0]
    ncp = t.shape[0]
    a = _dot((t + pe_ref[0:1, :]).astype(BF16), w1a_ref[...])
    b = _dot((t + pe_ref[1:2, :]).astype(BF16), w1b_ref[...])
    hid = jax.nn.gelu(a + pltpu.roll(b, ncp - 1, 0), approximate=True)
    kv = _dot(hid.astype(BF16), w2_ref[...])
    first_half = lax.broadcasted_iota(jnp.int32, (1, LANES), 1) < HEAD_DIM
    kn = _head_norm_rope(kv, kg_ref[...], cos_ref[...], sa_ref[...], sb_ref[...], bd_ref[...])
    kv_ref[0] = jnp.where(first_half, kn, kv).astype(BF16)
    vt_ref[0] = jnp.where(first_half, 1.0, kv).T.astype(BF16)


def _compress_call(t2, pe, w1a, w1b, w2, kg, bd, cos, sa, sb):
    b, ncp, width = t2.shape
    const = lambda shape: pl.BlockSpec(shape, lambda i: (0,) * len(shape))
    return pl.pallas_call(
        _compress_kernel,
        grid=(b,),
        in_specs=[pl.BlockSpec((1, ncp, width), lambda i: (i, 0, 0)), const(pe.shape), const(w1a.shape),
                  const(w1b.shape), const(w2.shape), const(kg.shape), const(bd.shape),
                  const(cos.shape), const(sa.shape), const(sb.shape)],
        out_specs=[pl.BlockSpec((1, ncp, LANES), lambda i: (i, 0, 0)), pl.BlockSpec((1, LANES, ncp), lambda i: (i, 0, 0))],
        out_shape=[jax.ShapeDtypeStruct((b, ncp, LANES), BF16), jax.ShapeDtypeStruct((b, LANES, ncp), BF16)],
        compiler_params=pltpu.CompilerParams(dimension_semantics=("arbitrary",), vmem_limit_bytes=VMEM_LIMIT),
        name="nsa_compress",
    )(t2, pe, w1a, w1b, w2, kg, bd, cos, sa, sb)


def _attn_kernel(qt_ref, gatet_ref, kvc_ref, vtc_ref, kvs_ref, vts_ref, kvw_ref, vtw_ref, ovt_ref, og_ref, o_ref,
                 bias_ref, sa_ref, sb_ref, m_ref, acc_ref, oc_ref, imp_ref, sw_ref, ow_ref,
                 *, tq, tk, n_sel):
    s0 = pl.program_id(1) * tq
    nc = N_HEADS * tq
    qs = jnp.concatenate([qt_ref[h * LANES:(h + 1) * LANES, :] for h in range(N_HEADS)], axis=1)
    t_col = s0 + (lax.broadcasted_iota(jnp.int32, (1, nc), 1) & (tq - 1))
    t_q = s0 + lax.broadcasted_iota(jnp.int32, (1, tq), 1)

    ncp = kvc_ref.shape[1]
    part = CMP_PART if ncp % CMP_PART == 0 else ncp

    def cmp_branch(rows):
        cmp_end = lax.broadcasted_iota(jnp.int32, (rows, 1), 0) * CMP_STRIDE + (CMP_LEN - 1)
        sc = jnp.where(cmp_end <= t_col, _dot(kvc_ref[0, 0:rows, :], qs), NEG_INF)
        e = jnp.exp2(sc - jnp.max(sc, axis=0, keepdims=True))
        p = e * jnp.where(t_col >= CMP_LEN - 1, 1.0 / jnp.sum(e, axis=0, keepdims=True), 0.0)
        oc_ref[...] = _dot(vtc_ref[0, :, 0:rows], p.astype(BF16))[HEAD_DIM:, :]
        psum = p[:, 0:tq] + p[:, tq:2 * tq] + p[:, 2 * tq:3 * tq] + p[:, 3 * tq:4 * tq]
        hi, lo = _split(psum)
        imp_ref[...] = _dot(ovt_ref[:, 0:rows], hi) + _dot(ovt_ref[:, 0:rows], lo)

    n_parts = (s0 + tq + CMP_STRIDE * part - 1) // (CMP_STRIDE * part)
    for k in range(1, ncp // part + 1):
        pl.when(n_parts == k)(functools.partial(cmp_branch, k * part))
    o_c = oc_ref[...]
    imp = imp_ref[...]

    n_seg = WINDOW // tq
    seg_start = [pl.multiple_of(jnp.maximum(s0 - k * tq, 0), tq) for k in range(n_seg + 1)]
    for k in range(n_seg + 1):
        sw_ref[k * tq:(k + 1) * tq, :] = _dot(kvw_ref[pl.ds(seg_start[k], tq), :], qs)

    nbp = imp.shape[0]
    blk = lax.broadcasted_iota(jnp.int32, (nbp, 1), 0)
    blk_f = blk.astype(F32)
    cur = t_q // SEL_BLOCK
    valid_b = blk <= cur
    forced = valid_b & ((blk == 0) | (blk == cur) | (blk == cur - 1))
    score = jnp.where(valid_b & jnp.logical_not(forced), imp, -1.0)
    sel = jnp.where(forced, 1.0, 0.0)
    for _ in range(n_sel - N_FORCED):
        top = jnp.max(score, axis=0, keepdims=True)
        idx = jnp.min(jnp.where(score == top, blk_f, float(nbp)), axis=0, keepdims=True)
        hit = blk_f == idx
        sel = jnp.where(hit & (top >= 0.0), 1.0, sel)
        score = jnp.where(hit, -2.0, score)
    bias = jnp.where(sel > 0.5, 0.0, NEG_INF)
    bias_ref[0:nbp, :] = jnp.concatenate([bias] * N_HEADS, axis=1)
    bias_ref[nbp:, :] = jnp.zeros((BIAS_PAD, nc), F32)

    off = lambda k: jnp.where(s0 >= k * tq, 0.0, NEG_INF)
    lower = lax.broadcasted_iota(jnp.int32, (tq, 1), 0) <= (lax.broadcasted_iota(jnp.int32, (1, nc), 1) & (tq - 1))
    s_tri = jnp.where(lower, sw_ref[0:tq, :], sw_ref[n_seg * tq:(n_seg + 1) * tq, :] + off(n_seg))
    s_mid = [sw_ref[k * tq:(k + 1) * tq, :] + off(k) for k in range(1, n_seg)]
    m_w = jnp.max(s_tri, axis=0, keepdims=True)
    for s_k in s_mid:
        m_w = jnp.maximum(m_w, jnp.max(s_k, axis=0, keepdims=True))
    p_tri = jnp.exp2(s_tri - m_w)
    acc_w = (_dot(vtw_ref[:, pl.ds(seg_start[0], tq)], jnp.where(lower, p_tri, 0.0).astype(BF16))
             + _dot(vtw_ref[:, pl.ds(seg_start[n_seg], tq)], jnp.where(lower, 0.0, p_tri).astype(BF16)))
    for k, s_k in zip(range(1, n_seg), s_mid):
        acc_w = acc_w + _dot(vtw_ref[:, pl.ds(seg_start[k], tq)], jnp.exp2(s_k - m_w).astype(BF16))
    ow_ref[...] = acc_w[HEAD_DIM:, :] / acc_w[0:1, :]

    n_blk = tk // SEL_BLOCK
    q_top = qs[0:HEAD_DIM, :]
    q_rest = jnp.zeros((LANES - HEAD_DIM - BIAS_PAD, nc), BF16)

    def scores(j):
        tile_bias = bias_ref[pl.ds(pl.multiple_of(j * n_blk, n_blk), BIAS_PAD), :].astype(BF16)
        qb = jnp.concatenate([q_top, tile_bias, q_rest], axis=0)
        return _dot(kvs_ref[pl.ds(pl.multiple_of(j * tk, tk), tk), :], qb)

    def consume(s_ref, j, causal):
        k0 = pl.multiple_of(j * tk, tk)
        s = s_ref[...]
        if causal:
            kpos = k0 + lax.broadcasted_iota(jnp.int32, (tk, 1), 0)
            s = jnp.where(kpos <= t_col, s, NEG_INF)
        m = m_ref[...]
        m_new = jnp.maximum(m, jnp.max(s, axis=0, keepdims=True))
        pe = jnp.exp2(s - m_new).astype(BF16)
        acc_ref[...] = jnp.exp2(m - m_new) * acc_ref[...] + _dot(vts_ref[:, pl.ds(k0, tk)], pe)
        m_ref[...] = m_new

    n_full = (s0 + tq + tk - 1) // tk - 1
    m_ref[...] = jnp.full((1, nc), NEG_INF, F32)
    acc_ref[...] = jnp.zeros((LANES, nc), F32)
    sa_ref[...] = scores(0)

    def pair(i, carry):
        j = 2 * i
        sb_ref[...] = scores(j + 1)
        consume(sa_ref, j, False)
        sa_ref[...] = scores(jnp.minimum(j + 2, n_full))
        consume(sb_ref, j + 1, False)
        return carry

    lax.fori_loop(0, n_full // 2, pair, 0)

    @pl.when(n_full % 2 == 1)
    def _():
        sb_ref[...] = scores(n_full)
        consume(sa_ref, n_full - 1, False)
        consume(sb_ref, n_full, True)

    @pl.when(n_full % 2 == 0)
    def _():
        consume(sa_ref, n_full, True)

    acc_s = acc_ref[...]
    o_s = acc_s[HEAD_DIM:, :] / acc_s[0:1, :]

    g = gatet_ref[...]
    heads = []
    for h in range(N_HEADS):
        c = slice(h * tq, (h + 1) * tq)
        r = h * N_BRANCH
        heads.append(g[r:r + 1, :] * o_c[:, c] + g[r + 1:r + 2, :] * o_s[:, c] + g[r + 2:r + 3, :] * ow_ref[:, c])
    ya = jnp.concatenate(heads, axis=0).T
    o_ref[...] = _rms(ya, og_ref[...]).astype(BF16)


def _attn_call(qt, gatet, kvc, vtc, kvs, vts, kvw, vtw, ovt, og, *, seq, tq, tk):
    n = qt.shape[1]
    b = n // seq
    nq = seq // tq
    ncp = kvc.shape[1]
    nbp = ovt.shape[0]
    n_sel = min(SEL_TOPK, seq // SEL_BLOCK)
    cols = lambda height: pl.BlockSpec((height, tq), lambda bi, qi: (0, bi * nq + qi))
    per_b = lambda shape: pl.BlockSpec((1,) + shape, lambda bi, qi: (bi, 0, 0))
    const = lambda shape: pl.BlockSpec(shape, lambda bi, qi: (0, 0))
    seq_rows = pl.BlockSpec((seq, LANES), lambda bi, qi: (bi, 0))
    seq_cols = pl.BlockSpec((LANES, seq), lambda bi, qi: (0, bi))
    return pl.pallas_call(
        functools.partial(_attn_kernel, tq=tq, tk=tk, n_sel=n_sel),
        grid=(b, nq),
        in_specs=[cols(N_HEADS * LANES), cols(GATE_ROWS), per_b((ncp, LANES)), per_b((LANES, ncp)),
                  seq_rows, seq_cols, seq_rows, seq_cols, const(ovt.shape), const((1, GROUP_WIDTH))],
        out_specs=pl.BlockSpec((tq, GROUP_WIDTH), lambda bi, qi: (bi * nq + qi, 0)),
        out_shape=jax.ShapeDtypeStruct((n, GROUP_WIDTH), BF16),
        scratch_shapes=[pltpu.VMEM((nbp + BIAS_PAD, N_HEADS * tq), F32), pltpu.VMEM((tk, N_HEADS * tq), F32),
                        pltpu.VMEM((tk, N_HEADS * tq), F32), pltpu.VMEM((1, N_HEADS * tq), F32),
                        pltpu.VMEM((LANES, N_HEADS * tq), F32), pltpu.VMEM((HEAD_DIM, N_HEADS * tq), F32),
                        pltpu.VMEM((nbp, tq), F32), pltpu.VMEM((WINDOW + tq, N_HEADS * tq), F32),
                        pltpu.VMEM((HEAD_DIM, N_HEADS * tq), F32)],
        compiler_params=pltpu.CompilerParams(dimension_semantics=("arbitrary", "arbitrary"),
                                             vmem_limit_bytes=VMEM_LIMIT),
        name="nsa_attention",
    )(qt, gatet, kvc, vtc, kvs, vts, kvw, vtw, ovt, og)


def _outproj_kernel(x_ref, ya_ref, yb_ref, w_ref, g_ref, *rest, moe):
    if moe:
        rw_ref, x1_ref, hn_ref, gate_ref = rest
    else:
        x1_ref, hn_ref = rest
    x1 = x_ref[...] + _dot(ya_ref[...], w_ref[0:GROUP_WIDTH, :]) + _dot(yb_ref[...], w_ref[GROUP_WIDTH:, :])
    x1_ref[...] = x1
    hn = _rms(x1, g_ref[...])
    hn_ref[...] = hn.astype(BF16)
    if moe:
        hi, lo = _split(hn)
        both = _dot(hi, rw_ref[...])
        logits = both[:, 0:LANES] + both[:, LANES:] + _dot(lo, rw_ref[:, 0:LANES])
        lane = lax.broadcasted_iota(jnp.int32, (1, LANES), 1)
        lane_f = lane.astype(F32)
        logits = jnp.where(lane < N_EXPERTS, logits, NEG_INF)
        m1 = jnp.max(logits, axis=-1, keepdims=True)
        i1 = jnp.min(jnp.where(logits == m1, lane_f, float(LANES)), axis=-1, keepdims=True)
        rest_l = jnp.where(lane_f == i1, 2 * NEG_INF, logits)
        m2 = jnp.max(rest_l, axis=-1, keepdims=True)
        i2 = jnp.min(jnp.where(rest_l == m2, lane_f, float(LANES)), axis=-1, keepdims=True)
        e2 = jnp.exp(m2 - m1)
        den = 1.0 + e2
        gate_ref[...] = jnp.where(lane_f == i1, 1.0 / den, jnp.where(lane_f == i2, e2 / den, 0.0))


def _outproj_call(x, ya, yb, w, g, rw, *, tm):
    n = x.shape[0]
    moe = rw is not None
    rows = lambda width: pl.BlockSpec((tm, width), lambda i: (i, 0))
    const = lambda shape: pl.BlockSpec(shape, lambda i: (0,) * len(shape))
    in_specs = [rows(D_MODEL), rows(GROUP_WIDTH), rows(3 * GROUP_WIDTH), const((D_MODEL, D_MODEL)), const((1, D_MODEL))]
    out_specs = [rows(D_MODEL), rows(D_MODEL)]
    out_shape = [jax.ShapeDtypeStruct((n, D_MODEL), F32), jax.ShapeDtypeStruct((n, D_MODEL), BF16)]
    args = [x, ya, yb, w, g]
    if moe:
        in_specs.append(const(rw.shape))
        out_specs.append(rows(LANES))
        out_shape.append(jax.ShapeDtypeStruct((n, LANES), F32))
        args.append(rw)
    return pl.pallas_call(
        functools.partial(_outproj_kernel, moe=moe),
        grid=(n // tm,),
        in_specs=in_specs, out_specs=out_specs, out_shape=out_shape,
        compiler_params=pltpu.CompilerParams(dimension_semantics=("arbitrary",), vmem_limit_bytes=VMEM_LIMIT),
        name="outproj_router" if moe else "outproj",
    )(*args)


def _ffn_kernel(x_ref, ya_ref, yb_ref, wo_ref, g_ref, w1_ref, w3_ref, w2_ref, o_ref, acc_ref, hn_ref):
    f = pl.program_id(1)

    @pl.when(f == 0)
    def _():
        x1 = x_ref[...] + _dot(ya_ref[...], wo_ref[0:GROUP_WIDTH, :]) + _dot(yb_ref[...], wo_ref[GROUP_WIDTH:, :])
        acc_ref[...] = x1
        hn_ref[...] = _rms(x1, g_ref[...]).astype(BF16)

    h = hn_ref[...]
    a = _dot(h, w1_ref[...])
    t = (a * jax.nn.sigmoid(a) * _dot(h, w3_ref[...])).astype(BF16)
    acc_ref[...] += _dot(t, w2_ref[...])

    @pl.when(f == pl.num_programs(1) - 1)
    def _():
        o_ref[...] = acc_ref[...]


def _ffn_call(x, ya, yb, wo, g, w1, w3, w2, *, tm, tf):
    n = x.shape[0]
    rows = lambda width: pl.BlockSpec((tm, width), lambda i, f: (i, 0))
    const = lambda shape: pl.BlockSpec(shape, lambda i, f: (0, 0))
    up = pl.BlockSpec((D_MODEL, tf), lambda i, f: (0, f))
    return pl.pallas_call(
        _ffn_kernel,
        grid=(n // tm, D_FF // tf),
        in_specs=[rows(D_MODEL), rows(GROUP_WIDTH), rows(3 * GROUP_WIDTH), const((D_MODEL, D_MODEL)), const((1, D_MODEL)),
                  up, up, pl.BlockSpec((tf, D_MODEL), lambda i, f: (f, 0))],
        out_specs=rows(D_MODEL),
        out_shape=jax.ShapeDtypeStruct((n, D_MODEL), F32),
        scratch_shapes=[pltpu.VMEM((tm, D_MODEL), F32), pltpu.VMEM((tm, D_MODEL), BF16)],
        compiler_params=pltpu.CompilerParams(dimension_semantics=("arbitrary", "arbitrary"),
                                             vmem_limit_bytes=VMEM_LIMIT),
        name="outproj_ffn_swiglu",
    )(x, ya, yb, wo, g, w1, w3, w2)


def _moe_kernel(h_ref, x_hbm, gate_ref, w1_ref, w3_ref, w2_ref, o_ref,
                xc_ref, yc_ref, rank_ref, gate_t_ref, rank_t_ref, start_ref, x_sem, *, n_sub):
    e = pl.program_id(1)
    f = pl.program_id(2)
    rb = MOE_ROWS
    sub = MOE_SUB

    def for_chunks(c_lo, c_hi, fn, group=2):
        n = c_hi - c_lo

        def full(c, carry):
            fn(pl.multiple_of((c_lo + group * c) * rb, rb), group * rb)
            return carry

        lax.fori_loop(0, n // group, full, 0)
        done = c_lo + n - n % group
        size = group // 2
        while size >= 1:
            taken = n % (2 * size) >= size
            pl.when(taken)(functools.partial(fn, pl.multiple_of(done * rb, rb), size * rb))
            done = done + jnp.where(taken, size, 0)
            size //= 2

    def sub_chunks(s):
        return start_ref[s] // rb, (start_ref[s + 1] + rb - 1) // rb

    @pl.when((e == 0) & (f == 0))
    def _():
        tm = n_sub * sub
        x_copy = pltpu.make_async_copy(x_hbm.at[pl.ds(pl.multiple_of(pl.program_id(0) * tm, tm), tm), :], o_ref, x_sem)
        x_copy.start()
        before = jnp.where(lax.broadcasted_iota(jnp.int32, (sub, sub), 1)
                           < lax.broadcasted_iota(jnp.int32, (sub, sub), 0), 1.0, 0.0).astype(BF16)
        routed_before = jnp.zeros((1, LANES), F32)
        for s in range(n_sub):
            span = slice(s * sub, (s + 1) * sub)
            gate = gate_ref[span, :]
            routed = jnp.where(gate > 0.0, 1.0, 0.0)
            rank = _dot(before, routed.astype(BF16)) + routed_before
            rank_ref[span, :] = rank
            gate_t_ref[:, span] = gate.T[0:N_EXPERTS]
            rank_t_ref[:, span] = rank.T[0:N_EXPERTS]
            routed_before = routed_before + jnp.sum(routed, axis=0, keepdims=True)
        x_copy.wait()

    @pl.when(f == 0)
    def _():
        start_ref[0] = 0
        for s in range(n_sub):
            g_row = gate_t_ref[pl.ds(e, 1), s * sub:(s + 1) * sub]
            start_ref[s + 1] = start_ref[s] + jnp.sum(jnp.where(g_row > 0.0, 1.0, 0.0)).astype(jnp.int32)

        def clear(r0, rows):
            xc_ref[pl.ds(r0, rows), :] = jnp.zeros((rows, D_MODEL), BF16)
            yc_ref[pl.ds(r0, rows), :] = jnp.zeros((rows, D_MODEL), F32)

        for_chunks(0, (start_ref[n_sub] + rb - 1) // rb, clear)

        for s in range(n_sub):
            span = slice(s * sub, (s + 1) * sub)
            g_row = gate_t_ref[pl.ds(e, 1), span]
            key = jnp.where(g_row > 0.0, rank_t_ref[pl.ds(e, 1), span], -1.0)

            def gather(r0, rows, span=span, key=key):
                slot = (r0 + lax.broadcasted_iota(jnp.int32, (rows, 1), 0)).astype(F32)
                onehot = jnp.where(key == slot, 1.0, 0.0).astype(BF16)
                dst = pl.ds(r0, rows)
                xc_ref[dst, :] = (xc_ref[dst, :].astype(F32) + _dot(onehot, h_ref[span, :])).astype(BF16)

            for_chunks(*sub_chunks(s), gather)

    def swiglu(r0, rows):
        src = pl.ds(r0, rows)
        xc = xc_ref[src, :]
        a = _dot(xc, w1_ref[...])
        t = (a * jax.nn.sigmoid(a) * _dot(xc, w3_ref[...])).astype(BF16)
        yc_ref[src, :] += _dot(t, w2_ref[...])

    for_chunks(0, (start_ref[n_sub] + rb - 1) // rb, swiglu, group=MOE_SWIGLU_GROUP)

    @pl.when(f == pl.num_programs(2) - 1)
    def _():
        lane = lax.broadcasted_iota(jnp.int32, (1, LANES), 1)
        for s in range(n_sub):
            span = slice(s * sub, (s + 1) * sub)
            g_col = jnp.sum(jnp.where(lane == e, gate_ref[span, :], 0.0), axis=-1, keepdims=True)
            r_col = jnp.sum(jnp.where(lane == e, rank_ref[span, :], 0.0), axis=-1, keepdims=True)
            key = jnp.where(g_col > 0.0, r_col, -1.0)

            def scatter(r0, rows, span=span, key=key, g_col=g_col):
                slot = (r0 + lax.broadcasted_iota(jnp.int32, (1, rows), 1)).astype(F32)
                onehot = jnp.where(key == slot, 1.0, 0.0).astype(BF16)
                o_ref[span, :] += g_col * _dot(onehot, yc_ref[pl.ds(r0, rows), :].astype(BF16))

            for_chunks(*sub_chunks(s), scatter)


def _moe_call(h, x, gate, w1, w3, w2, *, tm, tf):
    n = h.shape[0]
    n_sub = tm // MOE_SUB
    once = pl.Buffered(1)
    rows = lambda width, mode=None: pl.BlockSpec((tm, width), lambda i, e, f: (i, 0), pipeline_mode=mode)
    return pl.pallas_call(
        functools.partial(_moe_kernel, n_sub=n_sub),
        grid=(n // tm, N_EXPERTS, D_FF // tf),
        in_specs=[rows(D_MODEL, once), pl.BlockSpec(memory_space=pl.ANY), rows(LANES),
                  pl.BlockSpec((None, D_MODEL, tf), lambda i, e, f: (e, 0, f)),
                  pl.BlockSpec((None, D_MODEL, tf), lambda i, e, f: (e, 0, f)),
                  pl.BlockSpec((None, tf, D_MODEL), lambda i, e, f: (e, f, 0))],
        out_specs=rows(D_MODEL, once),
        out_shape=jax.ShapeDtypeStruct((n, D_MODEL), F32),
        scratch_shapes=[pltpu.VMEM((tm, D_MODEL), BF16), pltpu.VMEM((tm, D_MODEL), F32), pltpu.VMEM((tm, LANES), F32),
                        pltpu.VMEM((N_EXPERTS, tm), F32), pltpu.VMEM((N_EXPERTS, tm), F32),
                        pltpu.SMEM((n_sub + 1,), jnp.int32), pltpu.SemaphoreType.DMA(())],
        compiler_params=pltpu.CompilerParams(dimension_semantics=("arbitrary", "arbitrary", "arbitrary"),
                                             vmem_limit_bytes=VMEM_LIMIT),
        name="moe_swiglu",
    )(h, x, gate, w1, w3, w2)


def _rope_tables(pos):
    half = ROPE_DIM // 2
    inv_freq = ROPE_THETA ** (-np.arange(half, dtype=np.float64) / half)
    ang = pos.astype(np.float64)[..., None] * inv_freq
    cos, sin = np.cos(ang).astype(np.float32), np.sin(ang).astype(np.float32)
    n = pos.shape[0]
    rest = HEAD_DIM - ROPE_DIM
    cos64 = np.concatenate([cos, cos, np.ones((n, rest), np.float32)], axis=-1)
    sa64 = np.concatenate([-sin, np.zeros((n, rest + half), np.float32)], axis=-1)
    sb64 = np.concatenate([np.zeros((n, half), np.float32), sin, np.zeros((n, rest), np.float32)], axis=-1)
    return tuple(jnp.asarray(np.tile(t, (1, 2)), F32) for t in (cos64, sa64, sb64))


def _block_diag(blocks):
    n = len(blocks)
    rows = []
    for i, blk in enumerate(blocks):
        rows.append(jnp.concatenate([blk if j == i else jnp.zeros((blk.shape[0], blocks[j].shape[1]), blk.dtype)
                                     for j in range(n)], axis=1))
    return jnp.concatenate(rows, axis=0)


def _layout_w_in(w):
    n_gate = N_BRANCH * N_HEADS
    g0 = GROUP_WIDTH + 6 * HEAD_DIM
    pad = jnp.zeros(w.shape[:-1] + (LANES - n_gate,), w.dtype)
    return jnp.concatenate([w[..., :g0 + n_gate], pad, w[..., g0 + n_gate:]], axis=-1)


def _layout_compress(pe, w1, w2):
    half = CMP_LEN // 2
    pe2 = jnp.transpose(pe, (1, 0, 2)).reshape(2, half * LANES)
    w1r = w1.reshape(2, 2, half, HEAD_DIM, CMP_HIDDEN)
    zeros = jnp.zeros((half, HEAD_DIM, CMP_HIDDEN), w1.dtype)

    def big(part):
        k_rows = jnp.concatenate([w1r[0, part], zeros], axis=-1)
        v_rows = jnp.concatenate([zeros, w1r[1, part]], axis=-1)
        return jnp.concatenate([k_rows, v_rows], axis=1).reshape(half * LANES, 2 * CMP_HIDDEN)

    return pe2, big(0), big(1), _block_diag([w2[0], w2[1]])


def _overlap_matrix(seq, ncp, nbp):
    cs = jnp.arange(ncp) * CMP_STRIDE
    bs = jnp.arange(nbp) * SEL_BLOCK
    ov = jnp.clip(jnp.minimum(cs[:, None] + CMP_LEN, bs[None, :] + SEL_BLOCK)
                  - jnp.maximum(cs[:, None], bs[None, :]), 0, None).astype(F32) / CMP_LEN
    keep = (jnp.arange(ncp)[:, None] < seq // CMP_STRIDE - 1) & (jnp.arange(nbp)[None, :] < seq // SEL_BLOCK)
    return jnp.where(keep, ov, 0.0)


def kernel(x, attn_norm_g, w_in, q_norm_g, k_norm_g, cmp_pe, cmp_w1, cmp_w2, sgu_norm_g, sgu_w, sgu_b, pool_w,
           pool_scale, conv_w, mix_out_norm_g, w_out, ffn_norm_g, ffn_w1, ffn_w3, ffn_w2, router_w, expert_w1,
           expert_w3, expert_w2):
    b, seq, _ = x.shape
    n = b * seq
    depth = w_in.shape[0]
    tm = TOKEN_TILE
    ncp = seq // CMP_STRIDE
    nbp = max(LANES, seq // SEL_BLOCK)

    cos, sa, sb = _rope_tables(np.arange(seq))
    cos_c, sa_c, sb_c = _rope_tables(np.arange(ncp) * CMP_STRIDE + CMP_LEN - 1)
    bd = _block_diag([jnp.full((HEAD_DIM, HEAD_DIM), 1.0 / HEAD_DIM, F32)] * 4).astype(BF16)
    overlap_t = _overlap_matrix(seq, ncp, nbp).T.astype(BF16)
    row = lambda v: v.reshape(1, -1)
    two = lambda v: jnp.tile(v, 2).reshape(1, LANES)

    w_in_b = _layout_w_in(w_in.astype(BF16))
    xf = x.reshape(n, D_MODEL)
    for layer in range(depth):
        kg = k_norm_g[layer]
        sgub = jnp.repeat(sgu_b[layer].T, HEAD_DIM, axis=1)
        qt, kvc, kvs, vts, kvw, vtw, gatet, ybcd = _proj_call(
            xf, row(attn_norm_g[layer]), w_in_b[layer], bd, cos, sa, sb,
            jnp.tile(q_norm_g[layer], 4).reshape(1, 2 * LANES), jnp.tile(kg[1:3], (1, 2)).reshape(1, 2 * LANES),
            row(sgu_norm_g[layer]), sgu_w[layer].transpose(1, 0, 2).reshape(SGU_CHUNK, N_HEADS * SGU_CHUNK), sgub,
            _block_diag([pool_w[layer, i] for i in range(len(POOL_WINDOWS))]).astype(BF16),
            row(pool_scale[layer]), conv_w[layer], mix_out_norm_g[layer].reshape(4, GROUP_WIDTH)[1:],
            seq=seq, tm=tm)
        pe2, w1a, w1b, w2c = _layout_compress(cmp_pe[layer], cmp_w1[layer], cmp_w2[layer])
        kvcc, vtc = _compress_call(kvc.reshape(b, ncp, CMP_STRIDE * LANES), pe2, w1a.astype(BF16), w1b.astype(BF16),
                                   w2c.astype(BF16), two(kg[0]), bd, cos_c, sa_c, sb_c)
        ya = _attn_call(qt, gatet, kvcc, vtc, kvs, vts, kvw, vtw, overlap_t,
                        row(mix_out_norm_g[layer, :GROUP_WIDTH]), seq=seq, tq=ATTN_Q_TILE, tk=SEL_KEY_TILE)
        i = layer // 2
        if layer % 2 == 0:
            xf = _ffn_call(xf, ya, ybcd, w_out[layer].astype(BF16), row(ffn_norm_g[layer]), ffn_w1[i].astype(BF16),
                           ffn_w3[i].astype(BF16), ffn_w2[i].astype(BF16), tm=FFN_TOKEN_TILE, tf=FFN_FF_TILE)
        else:
            r = jnp.pad(router_w[i], ((0, 0), (0, LANES - N_EXPERTS)))
            r_hi = r.astype(BF16)
            rw = jnp.concatenate([r_hi, (r - r_hi.astype(F32)).astype(BF16)], axis=1)
            x1, hn, gate = _outproj_call(xf, ya, ybcd, w_out[layer].astype(BF16), row(ffn_norm_g[layer]), rw, tm=tm)
            xf = _moe_call(hn, x1, gate, expert_w1[i].astype(BF16), expert_w3[i].astype(BF16),
                           expert_w2[i].astype(BF16), tm=2 * MOE_SUB, tf=MOE_FF_TILE)
    return xf.reshape(b, seq, D_MODEL)
```

```python
import functools

import numpy as np
import jax
import jax.numpy as jnp
from jax import lax
from jax.experimental import pallas as pl
from jax.experimental.pallas import tpu as pltpu

D_MODEL = 1024
HEAD_DIM = 64
N_HEADS = 4
GROUP_WIDTH = 256
CMP_LEN = 32
CMP_STRIDE = 16
CMP_HIDDEN = 128
SEL_BLOCK = 64
SEL_TOPK = 16
WINDOW = 512
N_BRANCH = 3
ROPE_THETA = 500000.0
ROPE_DIM = 16
SGU_CHUNK = 128
POOL_WINDOWS = (2, 4, 8, 16)
CONV_WIDTH = 3
D_FF = 3584
N_EXPERTS = 8
EPS = 1e-6
NEG_INF = -1e30
Q_SCALE = HEAD_DIM ** -0.5 * 1.4426950408889634
CMP_PART = 128
N_FORCED = 3
SEL_KEY_TILE = 512
BIAS_PAD = 16
GATE_ROWS = 16

LANES = 128
MOE_ROWS = 128
MOE_FF_TILE = 1792
MOE_SWIGLU_GROUP = 4
MOE_TOKEN_TILE = 2048
MOE_SUB = 512
HIST = 16
D_IN_PAD = 2304
TOKEN_TILE = 512
ATTN_Q_TILE = 512
FFN_TOKEN_TILE = 512
FFN_FF_TILE = 1792
VMEM_LIMIT = 56 * 1024 * 1024

F32 = jnp.float32
BF16 = jnp.bfloat16


def _dot(a, b):
    return jnp.dot(a, b, preferred_element_type=F32)


def _split(a):
    hi = a.astype(BF16)
    lo = (a - hi.astype(F32)).astype(BF16)
    return hi, lo


def _rms(y, g):
    return y * lax.rsqrt(jnp.mean(y * y, axis=-1, keepdims=True) + EPS) * g


def _head_norm_rope(c, g, cos, sa, sb, bd):
    width = c.shape[1]
    if width > LANES:
        cos, sa, sb = (jnp.concatenate([t] * (width // LANES), axis=1) for t in (cos, sa, sb))
    hi, lo = _split(c * c)
    msq = _dot(hi, bd[0:width, 0:width]) + _dot(lo, bd[0:width, 0:width])
    cn = c * lax.rsqrt(msq + EPS) * g
    return cn * cos + pltpu.roll(cn, width - ROPE_DIM // 2, 1) * sa + pltpu.roll(cn, ROPE_DIM // 2, 1) * sb


def _proj_kernel(x_ref, g_ref, w_ref, bd_ref, cos_ref, sa_ref, sb_ref, qg_ref, kg_ref,
                 sgug_ref, sguw_ref, sgub_ref, poolw_ref, pools_ref, convw_ref, outg_ref,
                 qt_out, kvc_out, kvs_out, vts_out, kvw_out, vtw_out, gatet_out, y_out, pext, zext, p_ref, kc_ref,
                 *, tm, tiles_per_seq):
    i = pl.program_id(0)
    x = x_ref[...]
    hn = _rms(x, g_ref[...]).astype(BF16)
    bd = bd_ref[...]
    cos, sa, sb = cos_ref[...], sa_ref[...], sb_ref[...]
    lane = lax.broadcasted_iota(jnp.int32, (1, LANES), 1)
    first_half = lane < HEAD_DIM

    for c0 in range(0, D_IN_PAD, 2 * LANES):
        p_ref[:, c0:c0 + 2 * LANES] = _dot(hn, w_ref[:, c0:c0 + 2 * LANES])

    def proj(c0, c1):
        return p_ref[:, c0:c1]

    qn = _head_norm_rope(proj(0, 256), qg_ref[...], cos, sa, sb, bd) * Q_SCALE
    for c in range(2):
        qc = qn[:, c * LANES:(c + 1) * LANES]
        qt_out[(2 * c) * LANES:(2 * c + 1) * LANES, :] = jnp.where(first_half, qc, 0.0).T.astype(BF16)
        qt_out[(2 * c + 1) * LANES:(2 * c + 2) * LANES, :] = jnp.where(
            first_half, pltpu.roll(qc, HEAD_DIM, 1), 0.0).T.astype(BF16)

    kc_ref[...] = proj(256, 384)
    for r in range(CMP_STRIDE):
        kvc_out[:, r * LANES:(r + 1) * LANES] = kc_ref[pl.ds(r, tm // CMP_STRIDE, stride=CMP_STRIDE), :]

    kk = proj(384, 640)
    kn = _head_norm_rope(kk, kg_ref[...], cos, sa, sb, bd)
    ks, kw = kk[:, 0:LANES], kk[:, LANES:2 * LANES]
    tok = (i % tiles_per_seq) * tm + lax.broadcasted_iota(jnp.int32, (tm, 1), 0)
    blk_in_tile = (tok // SEL_BLOCK) % (SEL_KEY_TILE // SEL_BLOCK)
    kvs_out[...] = jnp.where(first_half, kn[:, 0:LANES], jnp.where(lane == HEAD_DIM + blk_in_tile, 1.0, 0.0)).astype(BF16)
    vts_out[...] = jnp.where(first_half, 1.0, ks).T.astype(BF16)
    kvw_out[...] = jnp.where(first_half, kn[:, LANES:2 * LANES], kw).astype(BF16)
    vtw_out[...] = jnp.where(first_half, 1.0, kw).T.astype(BF16)
    gatet_out[...] = jax.nn.sigmoid(proj(640, 768)).T[0:GATE_ROWS, :]

    outg = outg_ref[...]
    lane2 = lax.broadcasted_iota(jnp.int32, (1, GROUP_WIDTH), 1)
    grp = lane2 // HEAD_DIM

    u = proj(768, 1024)
    vn = _rms(proj(1024, 1280), sgug_ref[...])
    t_i = lax.broadcasted_iota(jnp.int32, (SGU_CHUNK, N_HEADS * SGU_CHUNK), 0)
    s_i = lax.broadcasted_iota(jnp.int32, (SGU_CHUNK, N_HEADS * SGU_CHUNK), 1) & (SGU_CHUNK - 1)
    wm = jnp.where(s_i <= t_i, sguw_ref[...], 0.0).astype(BF16)
    vg = [jnp.where(grp == gi, vn, 0.0).astype(BF16) for gi in range(N_HEADS)]
    for c in range(tm // SGU_CHUNK):
        rows = slice(c * SGU_CHUNK, (c + 1) * SGU_CHUNK)
        mixed = _dot(wm, jnp.concatenate([v[rows, :] for v in vg], axis=0)) + sgub_ref[...]
        y_out[rows, 0:GROUP_WIDTH] = _rms(u[rows, :] * mixed, outg[0:1, :]).astype(BF16)

    first = (i % tiles_per_seq) == 0

    @pl.when(first)
    def _():
        pext[0:HIST, :] = jnp.zeros((HIST, GROUP_WIDTH), F32)
        zext[0:HIST, :] = jnp.zeros((HIST, GROUP_WIDTH), F32)

    @pl.when(jnp.logical_not(first))
    def _():
        pext[0:HIST, :] = pext[tm:tm + HIST, :]
        zext[0:HIST, :] = zext[tm:tm + HIST, :]

    pin = proj(1280, 1536)
    pext[HIST:, :] = pin
    pos = (i % tiles_per_seq) * tm + lax.broadcasted_iota(jnp.int32, (tm, 1), 0)
    run = pext[...]
    sums = []
    width = 1
    while width < POOL_WINDOWS[-1]:
        run = run + pltpu.roll(run, width, 0)
        width *= 2
        if width in POOL_WINDOWS:
            sums.append(run[HIST:, :])
    wsum = jnp.where(grp == 0, sums[0], jnp.where(grp == 1, sums[1], jnp.where(grp == 2, sums[2], sums[3])))
    win = jnp.where(grp == 0, POOL_WINDOWS[0], jnp.where(grp == 1, POOL_WINDOWS[1],
                                                         jnp.where(grp == 2, POOL_WINDOWS[2], POOL_WINDOWS[3])))
    cnt = jnp.minimum(pos + 1, win).astype(F32)
    dlt = (wsum / cnt - pin).astype(BF16)
    yc = _dot(dlt, poolw_ref[...]) * pools_ref[...]
    y_out[:, GROUP_WIDTH:2 * GROUP_WIDTH] = _rms(yc, outg[1:2, :]).astype(BF16)

    bg = proj(1536, 1792)
    z = proj(1792, 2048) * proj(2048, 2304)
    zext[HIST:, :] = z
    cw = convw_ref[...]
    conv = cw[2:3, :] * z + cw[1:2, :] * zext[HIST - 1:HIST - 1 + tm, :] + cw[0:1, :] * zext[HIST - 2:HIST - 2 + tm, :]
    y_out[:, 2 * GROUP_WIDTH:3 * GROUP_WIDTH] = _rms(bg * conv, outg[2:3, :]).astype(BF16)


def _proj_call(x, g, w, bd, cos, sa, sb, qg, kg, sgug, sguw, sgub, poolw, pools, convw, outg, *, layer, seq, tm):
    n = x.shape[0]
    tps = seq // tm
    const = lambda shape: pl.BlockSpec(shape, lambda i: (0, 0))
    rows = lambda width: pl.BlockSpec((tm, width), lambda i: (i, 0))
    cols = lambda height: pl.BlockSpec((height, tm), lambda i: (0, i))
    tab = pl.BlockSpec((tm, LANES), lambda i: (i % tps, 0))
    return pl.pallas_call(
        functools.partial(_proj_kernel, tm=tm, tiles_per_seq=tps),
        grid=(n // tm,),
        in_specs=[rows(D_MODEL), const((1, D_MODEL)),
                  pl.BlockSpec((None, D_MODEL, D_IN_PAD), lambda i: (layer, 0, 0)), const((2 * LANES, 2 * LANES)),
                  tab, tab, tab, const((1, 2 * LANES)), const((1, 2 * LANES)),
                  const((1, GROUP_WIDTH)), const((SGU_CHUNK, N_HEADS * SGU_CHUNK)), const((SGU_CHUNK, GROUP_WIDTH)),
                  const((GROUP_WIDTH, GROUP_WIDTH)), const((1, GROUP_WIDTH)), const((CONV_WIDTH, GROUP_WIDTH)),
                  const((3, GROUP_WIDTH))],
        out_specs=[cols(N_HEADS * LANES), pl.BlockSpec((tm // CMP_STRIDE, CMP_STRIDE * LANES), lambda i: (i, 0)),
                   rows(LANES), cols(LANES), rows(LANES), cols(LANES),
                   cols(GATE_ROWS), rows(3 * GROUP_WIDTH)],
        out_shape=[jax.ShapeDtypeStruct((N_HEADS * LANES, n), BF16),
                   jax.ShapeDtypeStruct((n // CMP_STRIDE, CMP_STRIDE * LANES), F32),
                   jax.ShapeDtypeStruct((n, LANES), BF16), jax.ShapeDtypeStruct((LANES, n), BF16),
                   jax.ShapeDtypeStruct((n, LANES), BF16), jax.ShapeDtypeStruct((LANES, n), BF16),
                   jax.ShapeDtypeStruct((GATE_ROWS, n), F32), jax.ShapeDtypeStruct((n, 3 * GROUP_WIDTH), BF16)],
        scratch_shapes=[pltpu.VMEM((tm + HIST, GROUP_WIDTH), F32), pltpu.VMEM((tm + HIST, GROUP_WIDTH), F32),
                        pltpu.VMEM((tm, D_IN_PAD), F32), pltpu.VMEM((tm, LANES), F32)],
        compiler_params=pltpu.CompilerParams(dimension_semantics=("arbitrary",), vmem_limit_bytes=VMEM_LIMIT),
        name="proj_mixers",
    )(x, g, w, bd, cos, sa, sb, qg, kg, sgug, sguw, sgub, poolw, pools, convw, outg)


def _compress_kernel(t_ref, pe_ref, w1a_ref, w1b_ref, w2_ref, kg_ref, bd_ref, cos_ref, sa_ref, sb_ref, kv_ref, vt_ref):
    t = t_ref[0]
    ncp = t.shape[0]
    a = _dot((t + pe_ref[0:1, :]).astype(BF16), w1a_ref[...])
    b = _dot((t + pe_ref[1:2, :]).astype(BF16), w1b_ref[...])
    hid = jax.nn.gelu(a + pltpu.roll(b, ncp - 1, 0), approximate=True)
    kv = _dot(hid.astype(BF16), w2_ref[...])
    first_half = lax.broadcasted_iota(jnp.int32, (1, LANES), 1) < HEAD_DIM
    kn = _head_norm_rope(kv, kg_ref[...], cos_ref[...], sa_ref[...], sb_ref[...], bd_ref[...])
    kv_ref[0] = jnp.where(first_half, kn, kv).astype(BF16)
    vt_ref[0] = jnp.where(first_half, 1.0, kv).T.astype(BF16)


def _compress_call(t2, pe, w1a, w1b, w2, kg, bd, cos, sa, sb):
    b, ncp, width = t2.shape
    const = lambda shape: pl.BlockSpec(shape, lambda i: (0,) * len(shape))
    return pl.pallas_call(
        _compress_kernel,
        grid=(b,),
        in_specs=[pl.BlockSpec((1, ncp, width), lambda i: (i, 0, 0)), const(pe.shape), const(w1a.shape),
                  const(w1b.shape), const(w2.shape), const(kg.shape), const(bd.shape),
                  const(cos.shape), const(sa.shape), const(sb.shape)],
        out_specs=[pl.BlockSpec((1, ncp, LANES), lambda i: (i, 0, 0)), pl.BlockSpec((1, LANES, ncp), lambda i: (i, 0, 0))],
        out_shape=[jax.ShapeDtypeStruct((b, ncp, LANES), BF16), jax.ShapeDtypeStruct((b, LANES, ncp), BF16)],
        compiler_params=pltpu.CompilerParams(dimension_semantics=("arbitrary",), vmem_limit_bytes=VMEM_LIMIT),
        name="nsa_compress",
    )(t2, pe, w1a, w1b, w2, kg, bd, cos, sa, sb)


def _attn_kernel(qt_ref, gatet_ref, kvc_ref, vtc_ref, kvs_ref, vts_ref, kvw_ref, vtw_ref, ovt_ref, og_ref, o_ref,
                 bias_ref, sa_ref, sb_ref, m_ref, acc_ref, oc_ref, imp_ref, sw_ref, ow_ref,
                 *, tq, tk, n_sel):
    s0 = pl.program_id(1) * tq
    nc = N_HEADS * tq
    qs = jnp.concatenate([qt_ref[h * LANES:(h + 1) * LANES, :] for h in range(N_HEADS)], axis=1)
    t_col = s0 + (lax.broadcasted_iota(jnp.int32, (1, nc), 1) & (tq - 1))
    t_q = s0 + lax.broadcasted_iota(jnp.int32, (1, tq), 1)

    ncp = kvc_ref.shape[1]
    part = CMP_PART if ncp % CMP_PART == 0 else ncp

    def cmp_branch(rows):
        cmp_end = lax.broadcasted_iota(jnp.int32, (rows, 1), 0) * CMP_STRIDE + (CMP_LEN - 1)
        sc = jnp.where(cmp_end <= t_col, _dot(kvc_ref[0, 0:rows, :], qs), NEG_INF)
        e = jnp.exp2(sc - jnp.max(sc, axis=0, keepdims=True))
        p = e * jnp.where(t_col >= CMP_LEN - 1, 1.0 / jnp.sum(e, axis=0, keepdims=True), 0.0)
        oc_ref[...] = _dot(vtc_ref[0, :, 0:rows], p.astype(BF16))[HEAD_DIM:, :]
        psum = p[:, 0:tq] + p[:, tq:2 * tq] + p[:, 2 * tq:3 * tq] + p[:, 3 * tq:4 * tq]
        hi, lo = _split(psum)
        imp_ref[...] = _dot(ovt_ref[:, 0:rows], hi) + _dot(ovt_ref[:, 0:rows], lo)

    n_parts = (s0 + tq + CMP_STRIDE * part - 1) // (CMP_STRIDE * part)
    for k in range(1, ncp // part + 1):
        pl.when(n_parts == k)(functools.partial(cmp_branch, k * part))
    o_c = oc_ref[...]
    imp = imp_ref[...]

    n_seg = WINDOW // tq
    seg_start = [pl.multiple_of(jnp.maximum(s0 - k * tq, 0), tq) for k in range(n_seg + 1)]
    for k in range(n_seg + 1):
        sw_ref[k * tq:(k + 1) * tq, :] = _dot(kvw_ref[pl.ds(seg_start[k], tq), :], qs)

    nbp = imp.shape[0]
    blk = lax.broadcasted_iota(jnp.int32, (nbp, 1), 0)
    blk_f = blk.astype(F32)
    cur = t_q // SEL_BLOCK
    valid_b = blk <= cur
    forced = valid_b & ((blk == 0) | (blk == cur) | (blk == cur - 1))
    score = jnp.where(valid_b & jnp.logical_not(forced), imp, -1.0)
    sel = jnp.where(forced, 1.0, 0.0)
    for _ in range(n_sel - N_FORCED):
        top = jnp.max(score, axis=0, keepdims=True)
        idx = jnp.min(jnp.where(score == top, blk_f, float(nbp)), axis=0, keepdims=True)
        hit = blk_f == idx
        sel = jnp.where(hit & (top >= 0.0), 1.0, sel)
        score = jnp.where(hit, -2.0, score)
    bias = jnp.where(sel > 0.5, 0.0, NEG_INF)
    bias_ref[0:nbp, :] = jnp.concatenate([bias] * N_HEADS, axis=1)
    bias_ref[nbp:, :] = jnp.zeros((BIAS_PAD, nc), F32)

    off = lambda k: jnp.where(s0 >= k * tq, 0.0, NEG_INF)
    lower = lax.broadcasted_iota(jnp.int32, (tq, 1), 0) <= (lax.broadcasted_iota(jnp.int32, (1, nc), 1) & (tq - 1))
    s_tri = jnp.where(lower, sw_ref[0:tq, :], sw_ref[n_seg * tq:(n_seg + 1) * tq, :] + off(n_seg))
    s_mid = [sw_ref[k * tq:(k + 1) * tq, :] + off(k) for k in range(1, n_seg)]
    m_w = jnp.max(s_tri, axis=0, keepdims=True)
    for s_k in s_mid:
        m_w = jnp.maximum(m_w, jnp.max(s_k, axis=0, keepdims=True))
    p_tri = jnp.exp2(s_tri - m_w)
    acc_w = (_dot(vtw_ref[:, pl.ds(seg_start[0], tq)], jnp.where(lower, p_tri, 0.0).astype(BF16))
             + _dot(vtw_ref[:, pl.ds(seg_start[n_seg], tq)], jnp.where(lower, 0.0, p_tri).astype(BF16)))
    for k, s_k in zip(range(1, n_seg), s_mid):
        acc_w = acc_w + _dot(vtw_ref[:, pl.ds(seg_start[k], tq)], jnp.exp2(s_k - m_w).astype(BF16))
    ow_ref[...] = acc_w[HEAD_DIM:, :] / acc_w[0:1, :]

    n_blk = tk // SEL_BLOCK
    q_top = qs[0:HEAD_DIM, :]
    q_rest = jnp.zeros((LANES - HEAD_DIM - BIAS_PAD, nc), BF16)

    def scores(j):
        tile_bias = bias_ref[pl.ds(pl.multiple_of(j * n_blk, n_blk), BIAS_PAD), :].astype(BF16)
        qb = jnp.concatenate([q_top, tile_bias, q_rest], axis=0)
        return _dot(kvs_ref[pl.ds(pl.multiple_of(j * tk, tk), tk), :], qb)

    def consume(s_ref, j, causal):
        k0 = pl.multiple_of(j * tk, tk)
        s = s_ref[...]
        if causal:
            kpos = k0 + lax.broadcasted_iota(jnp.int32, (tk, 1), 0)
            s = jnp.where(kpos <= t_col, s, NEG_INF)
        m = m_ref[...]
        m_new = jnp.maximum(m, jnp.max(s, axis=0, keepdims=True))
        pe = jnp.exp2(s - m_new).astype(BF16)
        acc_ref[...] = jnp.exp2(m - m_new) * acc_ref[...] + _dot(vts_ref[:, pl.ds(k0, tk)], pe)
        m_ref[...] = m_new

    n_full = (s0 + tq + tk - 1) // tk - 1
    m_ref[...] = jnp.full((1, nc), NEG_INF, F32)
    acc_ref[...] = jnp.zeros((LANES, nc), F32)
    sa_ref[...] = scores(0)

    def pair(i, carry):
        j = 2 * i
        sb_ref[...] = scores(j + 1)
        consume(sa_ref, j, False)
        sa_ref[...] = scores(jnp.minimum(j + 2, n_full))
        consume(sb_ref, j + 1, False)
        return carry

    lax.fori_loop(0, n_full // 2, pair, 0)

    @pl.when(n_full % 2 == 1)
    def _():
        sb_ref[...] = scores(n_full)
        consume(sa_ref, n_full - 1, False)
        consume(sb_ref, n_full, True)

    @pl.when(n_full % 2 == 0)
    def _():
        consume(sa_ref, n_full, True)

    acc_s = acc_ref[...]
    o_s = acc_s[HEAD_DIM:, :] / acc_s[0:1, :]

    g = gatet_ref[...]
    heads = []
    for h in range(N_HEADS):
        c = slice(h * tq, (h + 1) * tq)
        r = h * N_BRANCH
        heads.append(g[r:r + 1, :] * o_c[:, c] + g[r + 1:r + 2, :] * o_s[:, c] + g[r + 2:r + 3, :] * ow_ref[:, c])
    ya = jnp.concatenate(heads, axis=0).T
    o_ref[...] = _rms(ya, og_ref[...]).astype(BF16)


def _attn_call(qt, gatet, kvc, vtc, kvs, vts, kvw, vtw, ovt, og, *, seq, tq, tk):
    n = qt.shape[1]
    b = n // seq
    nq = seq // tq
    ncp = kvc.shape[1]
    nbp = ovt.shape[0]
    n_sel = min(SEL_TOPK, seq // SEL_BLOCK)
    cols = lambda height: pl.BlockSpec((height, tq), lambda bi, qi: (0, bi * nq + qi))
    per_b = lambda shape: pl.BlockSpec((1,) + shape, lambda bi, qi: (bi, 0, 0))
    const = lambda shape: pl.BlockSpec(shape, lambda bi, qi: (0, 0))
    seq_rows = pl.BlockSpec((seq, LANES), lambda bi, qi: (bi, 0))
    seq_cols = pl.BlockSpec((LANES, seq), lambda bi, qi: (0, bi))
    return pl.pallas_call(
        functools.partial(_attn_kernel, tq=tq, tk=tk, n_sel=n_sel),
        grid=(b, nq),
        in_specs=[cols(N_HEADS * LANES), cols(GATE_ROWS), per_b((ncp, LANES)), per_b((LANES, ncp)),
                  seq_rows, seq_cols, seq_rows, seq_cols, const(ovt.shape), const((1, GROUP_WIDTH))],
        out_specs=pl.BlockSpec((tq, GROUP_WIDTH), lambda bi, qi: (bi * nq + qi, 0)),
        out_shape=jax.ShapeDtypeStruct((n, GROUP_WIDTH), BF16),
        scratch_shapes=[pltpu.VMEM((nbp + BIAS_PAD, N_HEADS * tq), F32), pltpu.VMEM((tk, N_HEADS * tq), F32),
                        pltpu.VMEM((tk, N_HEADS * tq), F32), pltpu.VMEM((1, N_HEADS * tq), F32),
                        pltpu.VMEM((LANES, N_HEADS * tq), F32), pltpu.VMEM((HEAD_DIM, N_HEADS * tq), F32),
                        pltpu.VMEM((nbp, tq), F32), pltpu.VMEM((WINDOW + tq, N_HEADS * tq), F32),
                        pltpu.VMEM((HEAD_DIM, N_HEADS * tq), F32)],
        compiler_params=pltpu.CompilerParams(dimension_semantics=("arbitrary", "arbitrary"),
                                             vmem_limit_bytes=VMEM_LIMIT),
        name="nsa_attention",
    )(qt, gatet, kvc, vtc, kvs, vts, kvw, vtw, ovt, og)


def _outproj_kernel(x_ref, ya_ref, yb_ref, w_ref, g_ref, rw_ref, x1_ref, hn_ref, gate_ref):
    x1 = x_ref[...] + _dot(ya_ref[...], w_ref[0:GROUP_WIDTH, :]) + _dot(yb_ref[...], w_ref[GROUP_WIDTH:, :])
    x1_ref[...] = x1
    hn = _rms(x1, g_ref[...])
    hn_ref[...] = hn.astype(BF16)
    hi, lo = _split(hn)
    both = _dot(hi, rw_ref[...])
    logits = both[:, 0:LANES] + both[:, LANES:] + _dot(lo, rw_ref[:, 0:LANES])
    lane = lax.broadcasted_iota(jnp.int32, (1, LANES), 1)
    lane_f = lane.astype(F32)
    logits = jnp.where(lane < N_EXPERTS, logits, NEG_INF)
    m1 = jnp.max(logits, axis=-1, keepdims=True)
    i1 = jnp.min(jnp.where(logits == m1, lane_f, float(LANES)), axis=-1, keepdims=True)
    rest_l = jnp.where(lane_f == i1, 2 * NEG_INF, logits)
    m2 = jnp.max(rest_l, axis=-1, keepdims=True)
    i2 = jnp.min(jnp.where(rest_l == m2, lane_f, float(LANES)), axis=-1, keepdims=True)
    e2 = jnp.exp(m2 - m1)
    den = 1.0 + e2
    gate_ref[...] = jnp.where(lane_f == i1, 1.0 / den, jnp.where(lane_f == i2, e2 / den, 0.0))


def _outproj_call(x, ya, yb, w, g, rw, *, layer, tm):
    n = x.shape[0]
    rows = lambda width: pl.BlockSpec((tm, width), lambda i: (i, 0))
    const = lambda shape: pl.BlockSpec(shape, lambda i: (0,) * len(shape))
    return pl.pallas_call(
        _outproj_kernel,
        grid=(n // tm,),
        in_specs=[rows(D_MODEL), rows(GROUP_WIDTH), rows(3 * GROUP_WIDTH),
                  pl.BlockSpec((None, D_MODEL, D_MODEL), lambda i: (layer, 0, 0)), const((1, D_MODEL)), const(rw.shape)],
        out_specs=[rows(D_MODEL), rows(D_MODEL), rows(LANES)],
        out_shape=[jax.ShapeDtypeStruct((n, D_MODEL), F32), jax.ShapeDtypeStruct((n, D_MODEL), BF16),
                   jax.ShapeDtypeStruct((n, LANES), F32)],
        compiler_params=pltpu.CompilerParams(dimension_semantics=("arbitrary",), vmem_limit_bytes=VMEM_LIMIT),
        name="outproj_router",
    )(x, ya, yb, w, g, rw)


def _ffn_kernel(x_ref, ya_ref, yb_ref, wo_ref, g_ref, w1_ref, w3_ref, w2_ref, o_ref, acc_ref, hn_ref):
    f = pl.program_id(1)

    @pl.when(f == 0)
    def _():
        x1 = x_ref[...] + _dot(ya_ref[...], wo_ref[0:GROUP_WIDTH, :]) + _dot(yb_ref[...], wo_ref[GROUP_WIDTH:, :])
        acc_ref[...] = x1
        hn_ref[...] = _rms(x1, g_ref[...]).astype(BF16)

    h = hn_ref[...]
    a = _dot(h, w1_ref[...])
    t = (a * jax.nn.sigmoid(a) * _dot(h, w3_ref[...])).astype(BF16)
    acc_ref[...] += _dot(t, w2_ref[...])

    @pl.when(f == pl.num_programs(1) - 1)
    def _():
        o_ref[...] = acc_ref[...]


def _ffn_call(x, ya, yb, wo, g, w1, w3, w2, *, layer, tm, tf):
    n = x.shape[0]
    rows = lambda width: pl.BlockSpec((tm, width), lambda i, f: (i, 0))
    const = lambda shape: pl.BlockSpec(shape, lambda i, f: (0, 0))
    up = pl.BlockSpec((D_MODEL, tf), lambda i, f: (0, f))
    return pl.pallas_call(
        _ffn_kernel,
        grid=(n // tm, D_FF // tf),
        in_specs=[rows(D_MODEL), rows(GROUP_WIDTH), rows(3 * GROUP_WIDTH),
                  pl.BlockSpec((None, D_MODEL, D_MODEL), lambda i, f: (layer, 0, 0)), const((1, D_MODEL)),
                  up, up, pl.BlockSpec((tf, D_MODEL), lambda i, f: (f, 0))],
        out_specs=rows(D_MODEL),
        out_shape=jax.ShapeDtypeStruct((n, D_MODEL), F32),
        scratch_shapes=[pltpu.VMEM((tm, D_MODEL), F32), pltpu.VMEM((tm, D_MODEL), BF16)],
        compiler_params=pltpu.CompilerParams(dimension_semantics=("arbitrary", "arbitrary"),
                                             vmem_limit_bytes=VMEM_LIMIT),
        name="outproj_ffn_swiglu",
    )(x, ya, yb, wo, g, w1, w3, w2)


def _moe_kernel(h_ref, x_hbm, gate_ref, w1_ref, w3_ref, w2_ref, o_ref,
                xc_ref, yc_ref, rank_ref, gate_t_ref, rank_t_ref, start_ref, x_sem, *, n_sub):
    e = pl.program_id(1)
    f = pl.program_id(2)
    rb = MOE_ROWS
    sub = MOE_SUB

    def for_chunks(c_lo, c_hi, fn, group=2):
        n = c_hi - c_lo

        def full(c, carry):
            fn(pl.multiple_of((c_lo + group * c) * rb, rb), group * rb)
            return carry

        lax.fori_loop(0, n // group, full, 0)
        done = c_lo + n - n % group
        size = group // 2
        while size >= 1:
            taken = n % (2 * size) >= size
            pl.when(taken)(functools.partial(fn, pl.multiple_of(done * rb, rb), size * rb))
            done = done + jnp.where(taken, size, 0)
            size //= 2

    def sub_chunks(s):
        return start_ref[s] // rb, (start_ref[s + 1] + rb - 1) // rb

    @pl.when((e == 0) & (f == 0))
    def _():
        tm = n_sub * sub
        x_copy = pltpu.make_async_copy(x_hbm.at[pl.ds(pl.multiple_of(pl.program_id(0) * tm, tm), tm), :], o_ref, x_sem)
        x_copy.start()
        before = jnp.where(lax.broadcasted_iota(jnp.int32, (sub, sub), 1)
                           < lax.broadcasted_iota(jnp.int32, (sub, sub), 0), 1.0, 0.0).astype(BF16)
        routed_before = jnp.zeros((1, LANES), F32)
        for s in range(n_sub):
            span = slice(s * sub, (s + 1) * sub)
            gate = gate_ref[span, :]
            routed = jnp.where(gate > 0.0, 1.0, 0.0)
            rank = _dot(before, routed.astype(BF16)) + routed_before
            rank_ref[span, :] = rank
            gate_t_ref[:, span] = gate.T[0:N_EXPERTS]
            rank_t_ref[:, span] = rank.T[0:N_EXPERTS]
            routed_before = routed_before + jnp.sum(routed, axis=0, keepdims=True)
        x_copy.wait()

    @pl.when(f == 0)
    def _():
        start_ref[0] = 0
        for s in range(n_sub):
            g_row = gate_t_ref[pl.ds(e, 1), s * sub:(s + 1) * sub]
            start_ref[s + 1] = start_ref[s] + jnp.sum(jnp.where(g_row > 0.0, 1.0, 0.0)).astype(jnp.int32)

        def clear(r0, rows):
            xc_ref[pl.ds(r0, rows), :] = jnp.zeros((rows, D_MODEL), BF16)
            yc_ref[pl.ds(r0, rows), :] = jnp.zeros((rows, D_MODEL), F32)

        for_chunks(0, (start_ref[n_sub] + rb - 1) // rb, clear)

        for s in range(n_sub):
            span = slice(s * sub, (s + 1) * sub)
            g_row = gate_t_ref[pl.ds(e, 1), span]
            key = jnp.where(g_row > 0.0, rank_t_ref[pl.ds(e, 1), span], -1.0)

            def gather(r0, rows, span=span, key=key):
                slot = (r0 + lax.broadcasted_iota(jnp.int32, (rows, 1), 0)).astype(F32)
                onehot = jnp.where(key == slot, 1.0, 0.0).astype(BF16)
                dst = pl.ds(r0, rows)
                xc_ref[dst, :] = (xc_ref[dst, :].astype(F32) + _dot(onehot, h_ref[span, :])).astype(BF16)

            for_chunks(*sub_chunks(s), gather)

    def swiglu(r0, rows):
        src = pl.ds(r0, rows)
        xc = xc_ref[src, :]
        a = _dot(xc, w1_ref[...])
        t = (a * jax.nn.sigmoid(a) * _dot(xc, w3_ref[...])).astype(BF16)
        yc_ref[src, :] += _dot(t, w2_ref[...])

    for_chunks(0, (start_ref[n_sub] + rb - 1) // rb, swiglu, group=MOE_SWIGLU_GROUP)

    @pl.when(f == pl.num_programs(2) - 1)
    def _():
        lane = lax.broadcasted_iota(jnp.int32, (1, LANES), 1)
        for s in range(n_sub):
            span = slice(s * sub, (s + 1) * sub)
            g_col = jnp.sum(jnp.where(lane == e, gate_ref[span, :], 0.0), axis=-1, keepdims=True)
            r_col = jnp.sum(jnp.where(lane == e, rank_ref[span, :], 0.0), axis=-1, keepdims=True)
            key = jnp.where(g_col > 0.0, r_col, -1.0)

            def scatter(r0, rows, span=span, key=key, g_col=g_col):
                slot = (r0 + lax.broadcasted_iota(jnp.int32, (1, rows), 1)).astype(F32)
                onehot = jnp.where(key == slot, 1.0, 0.0).astype(BF16)
                o_ref[span, :] += g_col * _dot(onehot, yc_ref[pl.ds(r0, rows), :].astype(BF16))

            for_chunks(*sub_chunks(s), scatter)


def _moe_call(h, x, gate, w1, w3, w2, *, tm, tf):
    n = h.shape[0]
    n_sub = tm // MOE_SUB
    once = pl.Buffered(1)
    rows = lambda width, mode=None: pl.BlockSpec((tm, width), lambda i, e, f: (i, 0), pipeline_mode=mode)
    return pl.pallas_call(
        functools.partial(_moe_kernel, n_sub=n_sub),
        grid=(n // tm, N_EXPERTS, D_FF // tf),
        in_specs=[rows(D_MODEL, once), pl.BlockSpec(memory_space=pl.ANY), rows(LANES),
                  pl.BlockSpec((None, D_MODEL, tf), lambda i, e, f: (e, 0, f)),
                  pl.BlockSpec((None, D_MODEL, tf), lambda i, e, f: (e, 0, f)),
                  pl.BlockSpec((None, tf, D_MODEL), lambda i, e, f: (e, f, 0))],
        out_specs=rows(D_MODEL, once),
        out_shape=jax.ShapeDtypeStruct((n, D_MODEL), F32),
        scratch_shapes=[pltpu.VMEM((tm, D_MODEL), BF16), pltpu.VMEM((tm, D_MODEL), F32), pltpu.VMEM((tm, LANES), F32),
                        pltpu.VMEM((N_EXPERTS, tm), F32), pltpu.VMEM((N_EXPERTS, tm), F32),
                        pltpu.SMEM((n_sub + 1,), jnp.int32), pltpu.SemaphoreType.DMA(())],
        compiler_params=pltpu.CompilerParams(dimension_semantics=("arbitrary", "arbitrary", "arbitrary"),
                                             vmem_limit_bytes=VMEM_LIMIT),
        name="moe_swiglu",
    )(h, x, gate, w1, w3, w2)


def _rope_tables(pos):
    half = ROPE_DIM // 2
    inv_freq = ROPE_THETA ** (-np.arange(half, dtype=np.float64) / half)
    ang = pos.astype(np.float64)[..., None] * inv_freq
    cos, sin = np.cos(ang).astype(np.float32), np.sin(ang).astype(np.float32)
    n = pos.shape[0]
    rest = HEAD_DIM - ROPE_DIM
    cos64 = np.concatenate([cos, cos, np.ones((n, rest), np.float32)], axis=-1)
    sa64 = np.concatenate([-sin, np.zeros((n, rest + half), np.float32)], axis=-1)
    sb64 = np.concatenate([np.zeros((n, half), np.float32), sin, np.zeros((n, rest), np.float32)], axis=-1)
    return tuple(jnp.asarray(np.tile(t, (1, 2)), F32) for t in (cos64, sa64, sb64))


def _block_diag(blocks):
    n = len(blocks)
    rows = []
    for i, blk in enumerate(blocks):
        rows.append(jnp.concatenate([blk if j == i else jnp.zeros((blk.shape[0], blocks[j].shape[1]), blk.dtype)
                                     for j in range(n)], axis=1))
    return jnp.concatenate(rows, axis=0)


def _layout_w_in(w):
    n_gate = N_BRANCH * N_HEADS
    g0 = GROUP_WIDTH + 6 * HEAD_DIM
    pad = jnp.zeros(w.shape[:-1] + (LANES - n_gate,), w.dtype)
    return jnp.concatenate([w[..., :g0 + n_gate], pad, w[..., g0 + n_gate:]], axis=-1)


def _layout_compress(pe, w1, w2):
    half = CMP_LEN // 2
    pe2 = jnp.transpose(pe, (1, 0, 2)).reshape(2, half * LANES)
    w1r = w1.reshape(2, 2, half, HEAD_DIM, CMP_HIDDEN)
    zeros = jnp.zeros((half, HEAD_DIM, CMP_HIDDEN), w1.dtype)

    def big(part):
        k_rows = jnp.concatenate([w1r[0, part], zeros], axis=-1)
        v_rows = jnp.concatenate([zeros, w1r[1, part]], axis=-1)
        return jnp.concatenate([k_rows, v_rows], axis=1).reshape(half * LANES, 2 * CMP_HIDDEN)

    return pe2, big(0), big(1), _block_diag([w2[0], w2[1]])


def _overlap_matrix(seq, ncp, nbp):
    cs = jnp.arange(ncp) * CMP_STRIDE
    bs = jnp.arange(nbp) * SEL_BLOCK
    ov = jnp.clip(jnp.minimum(cs[:, None] + CMP_LEN, bs[None, :] + SEL_BLOCK)
                  - jnp.maximum(cs[:, None], bs[None, :]), 0, None).astype(F32) / CMP_LEN
    keep = (jnp.arange(ncp)[:, None] < seq // CMP_STRIDE - 1) & (jnp.arange(nbp)[None, :] < seq // SEL_BLOCK)
    return jnp.where(keep, ov, 0.0)


def kernel(x, attn_norm_g, w_in, q_norm_g, k_norm_g, cmp_pe, cmp_w1, cmp_w2, sgu_norm_g, sgu_w, sgu_b, pool_w,
           pool_scale, conv_w, mix_out_norm_g, w_out, ffn_norm_g, ffn_w1, ffn_w3, ffn_w2, router_w, expert_w1,
           expert_w3, expert_w2):
    b, seq, _ = x.shape
    n = b * seq
    depth = w_in.shape[0]
    tm = TOKEN_TILE
    ncp = seq // CMP_STRIDE
    nbp = max(LANES, seq // SEL_BLOCK)

    cos, sa, sb = _rope_tables(np.arange(seq))
    cos_c, sa_c, sb_c = _rope_tables(np.arange(ncp) * CMP_STRIDE + CMP_LEN - 1)
    bd = _block_diag([jnp.full((HEAD_DIM, HEAD_DIM), 1.0 / HEAD_DIM, F32)] * 4).astype(BF16)
    overlap_t = _overlap_matrix(seq, ncp, nbp).T.astype(BF16)
    row = lambda v: v.reshape(1, -1)
    two = lambda v: jnp.tile(v, 2).reshape(1, LANES)

    w_in_b = _layout_w_in(w_in.astype(BF16))
    w_out_b = w_out.astype(BF16)
    xf = x.reshape(n, D_MODEL)
    for layer in range(depth):
        kg = k_norm_g[layer]
        sgub = jnp.repeat(sgu_b[layer].T, HEAD_DIM, axis=1)
        qt, kvc, kvs, vts, kvw, vtw, gatet, ybcd = _proj_call(
            xf, row(attn_norm_g[layer]), w_in_b, bd, cos, sa, sb,
            jnp.tile(q_norm_g[layer], 4).reshape(1, 2 * LANES), jnp.tile(kg[1:3], (1, 2)).reshape(1, 2 * LANES),
            row(sgu_norm_g[layer]), sgu_w[layer].transpose(1, 0, 2).reshape(SGU_CHUNK, N_HEADS * SGU_CHUNK), sgub,
            _block_diag([pool_w[layer, i] for i in range(len(POOL_WINDOWS))]).astype(BF16),
            row(pool_scale[layer]), conv_w[layer], mix_out_norm_g[layer].reshape(4, GROUP_WIDTH)[1:],
            layer=layer, seq=seq, tm=tm)
        pe2, w1a, w1b, w2c = _layout_compress(cmp_pe[layer], cmp_w1[layer], cmp_w2[layer])
        kvcc, vtc = _compress_call(kvc.reshape(b, ncp, CMP_STRIDE * LANES), pe2, w1a.astype(BF16), w1b.astype(BF16),
                                   w2c.astype(BF16), two(kg[0]), bd, cos_c, sa_c, sb_c)
        ya = _attn_call(qt, gatet, kvcc, vtc, kvs, vts, kvw, vtw, overlap_t,
                        row(mix_out_norm_g[layer, :GROUP_WIDTH]), seq=seq, tq=ATTN_Q_TILE, tk=SEL_KEY_TILE)
        i = layer // 2
        if layer % 2 == 0:
            xf = _ffn_call(xf, ya, ybcd, w_out_b, row(ffn_norm_g[layer]), ffn_w1[i].astype(BF16),
                           ffn_w3[i].astype(BF16), ffn_w2[i].astype(BF16), layer=layer, tm=FFN_TOKEN_TILE, tf=FFN_FF_TILE)
        else:
            r = jnp.pad(router_w[i], ((0, 0), (0, LANES - N_EXPERTS)))
            r_hi = r.astype(BF16)
            rw = jnp.concatenate([r_hi, (r - r_hi.astype(F32)).astype(BF16)], axis=1)
            x1, hn, gate = _outproj_call(xf, ya, ybcd, w_out_b, row(ffn_norm_g[layer]), rw, layer=layer, tm=tm)
            xf = _moe_call(hn, x1, gate, expert_w1[i].astype(BF16), expert_w3[i].astype(BF16),
                           expert_w2[i].astype(BF16), tm=MOE_TOKEN_TILE, tf=MOE_FF_TILE)
    return xf.reshape(b, seq, D_MODEL)
```

```python
import functools

import numpy as np
import jax
import jax.numpy as jnp
from jax import lax
from jax.experimental import pallas as pl
from jax.experimental.pallas import tpu as pltpu

D_MODEL = 1024
HEAD_DIM = 64
N_HEADS = 4
GROUP_WIDTH = 256
CMP_LEN = 32
CMP_STRIDE = 16
CMP_HIDDEN = 128
SEL_BLOCK = 64
SEL_TOPK = 16
WINDOW = 512
N_BRANCH = 3
ROPE_THETA = 500000.0
ROPE_DIM = 16
SGU_CHUNK = 128
POOL_WINDOWS = (2, 4, 8, 16)
CONV_WIDTH = 3
D_FF = 3584
N_EXPERTS = 8
EPS = 1e-6
NEG_INF = -1e30
Q_SCALE = HEAD_DIM ** -0.5 * 1.4426950408889634
CMP_PART = 128
N_FORCED = 3
SEL_KEY_TILE = 512
BIAS_PAD = 16
GATE_ROWS = 16

LANES = 128
MOE_ROWS = 128
MOE_FF_TILE = 1792
MOE_SWIGLU_GROUP = 4
MOE_TOKEN_TILE = 2048
MOE_SUB = 512
HIST = 16
D_IN_PAD = 2304
TOKEN_TILE = 512
ATTN_Q_TILE = 512
FFN_TOKEN_TILE = 512
FFN_FF_TILE = 1792
VMEM_LIMIT = 56 * 1024 * 1024

F32 = jnp.float32
BF16 = jnp.bfloat16


def _dot(a, b):
    return jnp.dot(a, b, preferred_element_type=F32)


def _split(a):
    hi = a.astype(BF16)
    lo = (a - hi.astype(F32)).astype(BF16)
    return hi, lo


def _rms(y, g):
    return y * lax.rsqrt(jnp.mean(y * y, axis=-1, keepdims=True) + EPS) * g


def _head_norm_rope(c, g, cos, sa, sb, bd):
    width = c.shape[1]
    if width > LANES:
        cos, sa, sb = (jnp.concatenate([t] * (width // LANES), axis=1) for t in (cos, sa, sb))
    hi, lo = _split(c * c)
    msq = _dot(hi, bd[0:width, 0:width]) + _dot(lo, bd[0:width, 0:width])
    cn = c * lax.rsqrt(msq + EPS) * g
    return cn * cos + pltpu.roll(cn, width - ROPE_DIM // 2, 1) * sa + pltpu.roll(cn, ROPE_DIM // 2, 1) * sb


def _proj_kernel(x_ref, g_ref, w_ref, bd_ref, cos_ref, sa_ref, sb_ref, qg_ref, kg_ref,
                 sgug_ref, sguw_ref, sgub_ref, poolw_ref, pools_ref, convw_ref, outg_ref,
                 qt_out, kvc_out, kvs_out, vts_out, kvw_out, vtw_out, gatet_out, y_out, pext, zext, p_ref, kc_ref,
                 *, tm, tiles_per_seq):
    i = pl.program_id(0)
    x = x_ref[...]
    hn = _rms(x, g_ref[...]).astype(BF16)
    bd = bd_ref[...]
    cos, sa, sb = cos_ref[...], sa_ref[...], sb_ref[...]
    lane = lax.broadcasted_iota(jnp.int32, (1, LANES), 1)
    first_half = lane < HEAD_DIM

    for c0 in range(0, D_IN_PAD, 2 * LANES):
        p_ref[:, c0:c0 + 2 * LANES] = _dot(hn, w_ref[:, c0:c0 + 2 * LANES])

    def proj(c0, c1):
        return p_ref[:, c0:c1]

    qn = _head_norm_rope(proj(0, 256), qg_ref[...], cos, sa, sb, bd) * Q_SCALE
    for c in range(2):
        qc = qn[:, c * LANES:(c + 1) * LANES]
        qt_out[(2 * c) * LANES:(2 * c + 1) * LANES, :] = jnp.where(first_half, qc, 0.0).T.astype(BF16)
        qt_out[(2 * c + 1) * LANES:(2 * c + 2) * LANES, :] = jnp.where(
            first_half, pltpu.roll(qc, HEAD_DIM, 1), 0.0).T.astype(BF16)

    kc_ref[...] = proj(256, 384)
    for r in range(CMP_STRIDE):
        kvc_out[:, r * LANES:(r + 1) * LANES] = kc_ref[pl.ds(r, tm // CMP_STRIDE, stride=CMP_STRIDE), :]

    kk = proj(384, 640)
    kn = _head_norm_rope(kk, kg_ref[...], cos, sa, sb, bd)
    ks, kw = kk[:, 0:LANES], kk[:, LANES:2 * LANES]
    tok = (i % tiles_per_seq) * tm + lax.broadcasted_iota(jnp.int32, (tm, 1), 0)
    blk_in_tile = (tok // SEL_BLOCK) % (SEL_KEY_TILE // SEL_BLOCK)
    kvs_out[...] = jnp.where(first_half, kn[:, 0:LANES], jnp.where(lane == HEAD_DIM + blk_in_tile, 1.0, 0.0)).astype(BF16)
    vts_out[...] = jnp.where(first_half, 1.0, ks).T.astype(BF16)
    kvw_out[...] = jnp.where(first_half, kn[:, LANES:2 * LANES], kw).astype(BF16)
    vtw_out[...] = jnp.where(first_half, 1.0, kw).T.astype(BF16)
    gatet_out[...] = jax.nn.sigmoid(proj(640, 768)).T[0:GATE_ROWS, :]

    outg = outg_ref[...]
    lane2 = lax.broadcasted_iota(jnp.int32, (1, GROUP_WIDTH), 1)
    grp = lane2 // HEAD_DIM

    u = proj(768, 1024)
    vn = _rms(proj(1024, 1280), sgug_ref[...])
    t_i = lax.broadcasted_iota(jnp.int32, (SGU_CHUNK, N_HEADS * SGU_CHUNK), 0)
    s_i = lax.broadcasted_iota(jnp.int32, (SGU_CHUNK, N_HEADS * SGU_CHUNK), 1) & (SGU_CHUNK - 1)
    wm = jnp.where(s_i <= t_i, sguw_ref[...], 0.0).astype(BF16)
    vg = [jnp.where(grp == gi, vn, 0.0).astype(BF16) for gi in range(N_HEADS)]
    for c in range(tm // SGU_CHUNK):
        rows = slice(c * SGU_CHUNK, (c + 1) * SGU_CHUNK)
        mixed = _dot(wm, jnp.concatenate([v[rows, :] for v in vg], axis=0)) + sgub_ref[...]
        y_out[rows, 0:GROUP_WIDTH] = _rms(u[rows, :] * mixed, outg[0:1, :]).astype(BF16)

    first = (i % tiles_per_seq) == 0

    @pl.when(first)
    def _():
        pext[0:HIST, :] = jnp.zeros((HIST, GROUP_WIDTH), F32)
        zext[0:HIST, :] = jnp.zeros((HIST, GROUP_WIDTH), F32)

    @pl.when(jnp.logical_not(first))
    def _():
        pext[0:HIST, :] = pext[tm:tm + HIST, :]
        zext[0:HIST, :] = zext[tm:tm + HIST, :]

    pin = proj(1280, 1536)
    pext[HIST:, :] = pin
    pos = (i % tiles_per_seq) * tm + lax.broadcasted_iota(jnp.int32, (tm, 1), 0)
    run = pext[...]
    sums = []
    width = 1
    while width < POOL_WINDOWS[-1]:
        run = run + pltpu.roll(run, width, 0)
        width *= 2
        if width in POOL_WINDOWS:
            sums.append(run[HIST:, :])
    wsum = jnp.where(grp == 0, sums[0], jnp.where(grp == 1, sums[1], jnp.where(grp == 2, sums[2], sums[3])))
    win = jnp.where(grp == 0, POOL_WINDOWS[0], jnp.where(grp == 1, POOL_WINDOWS[1],
                                                         jnp.where(grp == 2, POOL_WINDOWS[2], POOL_WINDOWS[3])))
    cnt = jnp.minimum(pos + 1, win).astype(F32)
    dlt = (wsum / cnt - pin).astype(BF16)
    yc = _dot(dlt, poolw_ref[...]) * pools_ref[...]
    y_out[:, GROUP_WIDTH:2 * GROUP_WIDTH] = _rms(yc, outg[1:2, :]).astype(BF16)

    bg = proj(1536, 1792)
    z = proj(1792, 2048) * proj(2048, 2304)
    zext[HIST:, :] = z
    cw = convw_ref[...]
    conv = cw[2:3, :] * z + cw[1:2, :] * zext[HIST - 1:HIST - 1 + tm, :] + cw[0:1, :] * zext[HIST - 2:HIST - 2 + tm, :]
    y_out[:, 2 * GROUP_WIDTH:3 * GROUP_WIDTH] = _rms(bg * conv, outg[2:3, :]).astype(BF16)


def _proj_call(x, g, w, bd, cos, sa, sb, qg, kg, sgug, sguw, sgub, poolw, pools, convw, outg, *, layer, seq, tm):
    n = x.shape[0]
    tps = seq // tm
    const = lambda shape: pl.BlockSpec(shape, lambda i: (0, 0))
    rows = lambda width: pl.BlockSpec((tm, width), lambda i: (i, 0))
    cols = lambda height: pl.BlockSpec((height, tm), lambda i: (0, i))
    tab = pl.BlockSpec((tm, LANES), lambda i: (i % tps, 0))
    return pl.pallas_call(
        functools.partial(_proj_kernel, tm=tm, tiles_per_seq=tps),
        grid=(n // tm,),
        in_specs=[rows(D_MODEL), const((1, D_MODEL)),
                  pl.BlockSpec((None, D_MODEL, D_IN_PAD), lambda i: (layer, 0, 0)), const((2 * LANES, 2 * LANES)),
                  tab, tab, tab, const((1, 2 * LANES)), const((1, 2 * LANES)),
                  const((1, GROUP_WIDTH)), const((SGU_CHUNK, N_HEADS * SGU_CHUNK)), const((SGU_CHUNK, GROUP_WIDTH)),
                  const((GROUP_WIDTH, GROUP_WIDTH)), const((1, GROUP_WIDTH)), const((CONV_WIDTH, GROUP_WIDTH)),
                  const((3, GROUP_WIDTH))],
        out_specs=[cols(N_HEADS * LANES), pl.BlockSpec((tm // CMP_STRIDE, CMP_STRIDE * LANES), lambda i: (i, 0)),
                   rows(LANES), cols(LANES), rows(LANES), cols(LANES),
                   cols(GATE_ROWS), rows(3 * GROUP_WIDTH)],
        out_shape=[jax.ShapeDtypeStruct((N_HEADS * LANES, n), BF16),
                   jax.ShapeDtypeStruct((n // CMP_STRIDE, CMP_STRIDE * LANES), F32),
                   jax.ShapeDtypeStruct((n, LANES), BF16), jax.ShapeDtypeStruct((LANES, n), BF16),
                   jax.ShapeDtypeStruct((n, LANES), BF16), jax.ShapeDtypeStruct((LANES, n), BF16),
                   jax.ShapeDtypeStruct((GATE_ROWS, n), F32), jax.ShapeDtypeStruct((n, 3 * GROUP_WIDTH), BF16)],
        scratch_shapes=[pltpu.VMEM((tm + HIST, GROUP_WIDTH), F32), pltpu.VMEM((tm + HIST, GROUP_WIDTH), F32),
                        pltpu.VMEM((tm, D_IN_PAD), F32), pltpu.VMEM((tm, LANES), F32)],
        compiler_params=pltpu.CompilerParams(dimension_semantics=("arbitrary",), vmem_limit_bytes=VMEM_LIMIT),
        name="proj_mixers",
    )(x, g, w, bd, cos, sa, sb, qg, kg, sgug, sguw, sgub, poolw, pools, convw, outg)


def _compress_kernel(t_ref, pe_ref, w1a_ref, w1b_ref, w2_ref, kg_ref, bd_ref, cos_ref, sa_ref, sb_ref, kv_ref, vt_ref):
    t = t_ref[0]
    ncp = t.shape[0]
    a = _dot((t + pe_ref[0:1, :]).astype(BF16), w1a_ref[...])
    b = _dot((t + pe_ref[1:2, :]).astype(BF16), w1b_ref[...])
    hid = jax.nn.gelu(a + pltpu.roll(b, ncp - 1, 0), approximate=True)
    kv = _dot(hid.astype(BF16), w2_ref[...])
    first_half = lax.broadcasted_iota(jnp.int32, (1, LANES), 1) < HEAD_DIM
    kn = _head_norm_rope(kv, kg_ref[...], cos_ref[...], sa_ref[...], sb_ref[...], bd_ref[...])
    kv_ref[0] = jnp.where(first_half, kn, kv).astype(BF16)
    vt_ref[0] = jnp.where(first_half, 1.0, kv).T.astype(BF16)


def _compress_call(t2, pe, w1a, w1b, w2, kg, bd, cos, sa, sb):
    b, ncp, width = t2.shape
    const = lambda shape: pl.BlockSpec(shape, lambda i: (0,) * len(shape))
    return pl.pallas_call(
        _compress_kernel,
        grid=(b,),
        in_specs=[pl.BlockSpec((1, ncp, width), lambda i: (i, 0, 0)), const(pe.shape), const(w1a.shape),
                  const(w1b.shape), const(w2.shape), const(kg.shape), const(bd.shape),
                  const(cos.shape), const(sa.shape), const(sb.shape)],
        out_specs=[pl.BlockSpec((1, ncp, LANES), lambda i: (i, 0, 0)), pl.BlockSpec((1, LANES, ncp), lambda i: (i, 0, 0))],
        out_shape=[jax.ShapeDtypeStruct((b, ncp, LANES), BF16), jax.ShapeDtypeStruct((b, LANES, ncp), BF16)],
        compiler_params=pltpu.CompilerParams(dimension_semantics=("arbitrary",), vmem_limit_bytes=VMEM_LIMIT),
        name="nsa_compress",
    )(t2, pe, w1a, w1b, w2, kg, bd, cos, sa, sb)


def _attn_kernel(qt_ref, gatet_ref, kvc_ref, vtc_ref, kvs_ref, vts_ref, kvw_ref, vtw_ref, ovt_ref, og_ref, o_ref,
                 bias_ref, sa_ref, sb_ref, m_ref, acc_ref, oc_ref, imp_ref, sw_ref, ow_ref,
                 *, tq, tk, n_sel):
    s0 = pl.program_id(1) * tq
    nc = N_HEADS * tq
    qs = jnp.concatenate([qt_ref[h * LANES:(h + 1) * LANES, :] for h in range(N_HEADS)], axis=1)
    t_col = s0 + (lax.broadcasted_iota(jnp.int32, (1, nc), 1) & (tq - 1))
    t_q = s0 + lax.broadcasted_iota(jnp.int32, (1, tq), 1)

    ncp = kvc_ref.shape[1]
    part = CMP_PART if ncp % CMP_PART == 0 else ncp

    def cmp_branch(rows):
        cmp_end = lax.broadcasted_iota(jnp.int32, (rows, 1), 0) * CMP_STRIDE + (CMP_LEN - 1)
        sc = jnp.where(cmp_end <= t_col, _dot(kvc_ref[0, 0:rows, :], qs), NEG_INF)
        e = jnp.exp2(sc - jnp.max(sc, axis=0, keepdims=True))
        p = e * jnp.where(t_col >= CMP_LEN - 1, 1.0 / jnp.sum(e, axis=0, keepdims=True), 0.0)
        oc_ref[...] = _dot(vtc_ref[0, :, 0:rows], p.astype(BF16))[HEAD_DIM:, :]
        psum = p[:, 0:tq] + p[:, tq:2 * tq] + p[:, 2 * tq:3 * tq] + p[:, 3 * tq:4 * tq]
        hi, lo = _split(psum)
        imp_ref[...] = _dot(ovt_ref[:, 0:rows], hi) + _dot(ovt_ref[:, 0:rows], lo)

    n_parts = (s0 + tq + CMP_STRIDE * part - 1) // (CMP_STRIDE * part)
    for k in range(1, ncp // part + 1):
        pl.when(n_parts == k)(functools.partial(cmp_branch, k * part))
    o_c = oc_ref[...]
    imp = imp_ref[...]

    n_seg = WINDOW // tq
    seg_start = [pl.multiple_of(jnp.maximum(s0 - k * tq, 0), tq) for k in range(n_seg + 1)]
    for k in range(n_seg + 1):
        sw_ref[k * tq:(k + 1) * tq, :] = _dot(kvw_ref[pl.ds(seg_start[k], tq), :], qs)

    nbp = imp.shape[0]
    blk = lax.broadcasted_iota(jnp.int32, (nbp, 1), 0)
    blk_f = blk.astype(F32)
    cur = t_q // SEL_BLOCK
    valid_b = blk <= cur
    forced = valid_b & ((blk == 0) | (blk == cur) | (blk == cur - 1))
    score = jnp.where(valid_b & jnp.logical_not(forced), imp, -1.0)
    sel = jnp.where(forced, 1.0, 0.0)
    for _ in range(n_sel - N_FORCED):
        top = jnp.max(score, axis=0, keepdims=True)
        idx = jnp.min(jnp.where(score == top, blk_f, float(nbp)), axis=0, keepdims=True)
        hit = blk_f == idx
        sel = jnp.where(hit & (top >= 0.0), 1.0, sel)
        score = jnp.where(hit, -2.0, score)
    bias = jnp.where(sel > 0.5, 0.0, NEG_INF)
    bias_ref[0:nbp, :] = jnp.concatenate([bias] * N_HEADS, axis=1)
    bias_ref[nbp:, :] = jnp.zeros((BIAS_PAD, nc), F32)

    off = lambda k: jnp.where(s0 >= k * tq, 0.0, NEG_INF)
    lower = lax.broadcasted_iota(jnp.int32, (tq, 1), 0) <= (lax.broadcasted_iota(jnp.int32, (1, nc), 1) & (tq - 1))
    s_tri = jnp.where(lower, sw_ref[0:tq, :], sw_ref[n_seg * tq:(n_seg + 1) * tq, :] + off(n_seg))
    s_mid = [sw_ref[k * tq:(k + 1) * tq, :] + off(k) for k in range(1, n_seg)]
    m_w = jnp.max(s_tri, axis=0, keepdims=True)
    for s_k in s_mid:
        m_w = jnp.maximum(m_w, jnp.max(s_k, axis=0, keepdims=True))
    p_tri = jnp.exp2(s_tri - m_w)
    acc_w = (_dot(vtw_ref[:, pl.ds(seg_start[0], tq)], jnp.where(lower, p_tri, 0.0).astype(BF16))
             + _dot(vtw_ref[:, pl.ds(seg_start[n_seg], tq)], jnp.where(lower, 0.0, p_tri).astype(BF16)))
    for k, s_k in zip(range(1, n_seg), s_mid):
        acc_w = acc_w + _dot(vtw_ref[:, pl.ds(seg_start[k], tq)], jnp.exp2(s_k - m_w).astype(BF16))
    ow_ref[...] = acc_w[HEAD_DIM:, :] / acc_w[0:1, :]

    n_blk = tk // SEL_BLOCK
    q_top = qs[0:HEAD_DIM, :]
    q_rest = jnp.zeros((LANES - HEAD_DIM - BIAS_PAD, nc), BF16)

    def scores(j):
        tile_bias = bias_ref[pl.ds(pl.multiple_of(j * n_blk, n_blk), BIAS_PAD), :].astype(BF16)
        qb = jnp.concatenate([q_top, tile_bias, q_rest], axis=0)
        return _dot(kvs_ref[pl.ds(pl.multiple_of(j * tk, tk), tk), :], qb)

    def consume(s_ref, j, causal):
        k0 = pl.multiple_of(j * tk, tk)
        s = s_ref[...]
        if causal:
            kpos = k0 + lax.broadcasted_iota(jnp.int32, (tk, 1), 0)
            s = jnp.where(kpos <= t_col, s, NEG_INF)
        m = m_ref[...]
        m_new = jnp.maximum(m, jnp.max(s, axis=0, keepdims=True))
        pe = jnp.exp2(s - m_new).astype(BF16)
        acc_ref[...] = jnp.exp2(m - m_new) * acc_ref[...] + _dot(vts_ref[:, pl.ds(k0, tk)], pe)
        m_ref[...] = m_new

    n_full = (s0 + tq + tk - 1) // tk - 1
    m_ref[...] = jnp.full((1, nc), NEG_INF, F32)
    acc_ref[...] = jnp.zeros((LANES, nc), F32)
    sa_ref[...] = scores(0)

    def pair(i, carry):
        j = 2 * i
        sb_ref[...] = scores(j + 1)
        consume(sa_ref, j, False)
        sa_ref[...] = scores(jnp.minimum(j + 2, n_full))
        consume(sb_ref, j + 1, False)
        return carry

    lax.fori_loop(0, n_full // 2, pair, 0)

    @pl.when(n_full % 2 == 1)
    def _():
        sb_ref[...] = scores(n_full)
        consume(sa_ref, n_full - 1, False)
        consume(sb_ref, n_full, True)

    @pl.when(n_full % 2 == 0)
    def _():
        consume(sa_ref, n_full, True)

    acc_s = acc_ref[...]
    o_s = acc_s[HEAD_DIM:, :] / acc_s[0:1, :]

    g = gatet_ref[...]
    heads = []
    for h in range(N_HEADS):
        c = slice(h * tq, (h + 1) * tq)
        r = h * N_BRANCH
        heads.append(g[r:r + 1, :] * o_c[:, c] + g[r + 1:r + 2, :] * o_s[:, c] + g[r + 2:r + 3, :] * ow_ref[:, c])
    ya = jnp.concatenate(heads, axis=0).T
    o_ref[...] = _rms(ya, og_ref[...]).astype(BF16)


def _attn_call(qt, gatet, kvc, vtc, kvs, vts, kvw, vtw, ovt, og, *, seq, tq, tk):
    n = qt.shape[1]
    b = n // seq
    nq = seq // tq
    ncp = kvc.shape[1]
    nbp = ovt.shape[0]
    n_sel = min(SEL_TOPK, seq // SEL_BLOCK)
    cols = lambda height: pl.BlockSpec((height, tq), lambda bi, qi: (0, bi * nq + qi))
    per_b = lambda shape: pl.BlockSpec((1,) + shape, lambda bi, qi: (bi, 0, 0))
    const = lambda shape: pl.BlockSpec(shape, lambda bi, qi: (0, 0))
    seq_rows = pl.BlockSpec((seq, LANES), lambda bi, qi: (bi, 0))
    seq_cols = pl.BlockSpec((LANES, seq), lambda bi, qi: (0, bi))
    return pl.pallas_call(
        functools.partial(_attn_kernel, tq=tq, tk=tk, n_sel=n_sel),
        grid=(b, nq),
        in_specs=[cols(N_HEADS * LANES), cols(GATE_ROWS), per_b((ncp, LANES)), per_b((LANES, ncp)),
                  seq_rows, seq_cols, seq_rows, seq_cols, const(ovt.shape), const((1, GROUP_WIDTH))],
        out_specs=pl.BlockSpec((tq, GROUP_WIDTH), lambda bi, qi: (bi * nq + qi, 0)),
        out_shape=jax.ShapeDtypeStruct((n, GROUP_WIDTH), BF16),
        scratch_shapes=[pltpu.VMEM((nbp + BIAS_PAD, N_HEADS * tq), F32), pltpu.VMEM((tk, N_HEADS * tq), F32),
                        pltpu.VMEM((tk, N_HEADS * tq), F32), pltpu.VMEM((1, N_HEADS * tq), F32),
                        pltpu.VMEM((LANES, N_HEADS * tq), F32), pltpu.VMEM((HEAD_DIM, N_HEADS * tq), F32),
                        pltpu.VMEM((nbp, tq), F32), pltpu.VMEM((WINDOW + tq, N_HEADS * tq), F32),
                        pltpu.VMEM((HEAD_DIM, N_HEADS * tq), F32)],
        compiler_params=pltpu.CompilerParams(dimension_semantics=("arbitrary", "arbitrary"),
                                             vmem_limit_bytes=VMEM_LIMIT),
        name="nsa_attention",
    )(qt, gatet, kvc, vtc, kvs, vts, kvw, vtw, ovt, og)


def _outproj_kernel(x_ref, ya_ref, yb_ref, w_ref, g_ref, rw_ref, x1_ref, hn_ref, gate_ref):
    x1 = x_ref[...] + _dot(ya_ref[...], w_ref[0:GROUP_WIDTH, :]) + _dot(yb_ref[...], w_ref[GROUP_WIDTH:, :])
    x1_ref[...] = x1
    hn = _rms(x1, g_ref[...])
    hn_ref[...] = hn.astype(BF16)
    hi, lo = _split(hn)
    both = _dot(hi, rw_ref[...])
    logits = both[:, 0:LANES] + both[:, LANES:] + _dot(lo, rw_ref[:, 0:LANES])
    lane = lax.broadcasted_iota(jnp.int32, (1, LANES), 1)
    lane_f = lane.astype(F32)
    logits = jnp.where(lane < N_EXPERTS, logits, NEG_INF)
    m1 = jnp.max(logits, axis=-1, keepdims=True)
    i1 = jnp.min(jnp.where(logits == m1, lane_f, float(LANES)), axis=-1, keepdims=True)
    rest_l = jnp.where(lane_f == i1, 2 * NEG_INF, logits)
    m2 = jnp.max(rest_l, axis=-1, keepdims=True)
    i2 = jnp.min(jnp.where(rest_l == m2, lane_f, float(LANES)), axis=-1, keepdims=True)
    e2 = jnp.exp(m2 - m1)
    den = 1.0 + e2
    gate_ref[...] = jnp.where(lane_f == i1, 1.0 / den, jnp.where(lane_f == i2, e2 / den, 0.0))


def _outproj_call(x, ya, yb, w, g, rw, *, layer, tm):
    n = x.shape[0]
    rows = lambda width: pl.BlockSpec((tm, width), lambda i: (i, 0))
    const = lambda shape: pl.BlockSpec(shape, lambda i: (0,) * len(shape))
    return pl.pallas_call(
        _outproj_kernel,
        grid=(n // tm,),
        in_specs=[rows(D_MODEL), rows(GROUP_WIDTH), rows(3 * GROUP_WIDTH),
                  pl.BlockSpec((None, D_MODEL, D_MODEL), lambda i: (layer, 0, 0)), const((1, D_MODEL)), const(rw.shape)],
        out_specs=[rows(D_MODEL), rows(D_MODEL), rows(LANES)],
        out_shape=[jax.ShapeDtypeStruct((n, D_MODEL), F32), jax.ShapeDtypeStruct((n, D_MODEL), BF16),
                   jax.ShapeDtypeStruct((n, LANES), F32)],
        compiler_params=pltpu.CompilerParams(dimension_semantics=("arbitrary",), vmem_limit_bytes=VMEM_LIMIT),
        name="outproj_router",
    )(x, ya, yb, w, g, rw)


def _ffn_kernel(x_ref, ya_ref, yb_ref, wo_ref, g_ref, w1_ref, w3_ref, w2_ref, o_ref, acc_ref, hn_ref):
    f = pl.program_id(1)

    @pl.when(f == 0)
    def _():
        x1 = x_ref[...] + _dot(ya_ref[...], wo_ref[0:GROUP_WIDTH, :]) + _dot(yb_ref[...], wo_ref[GROUP_WIDTH:, :])
        acc_ref[...] = x1
        hn_ref[...] = _rms(x1, g_ref[...]).astype(BF16)

    h = hn_ref[...]
    a = _dot(h, w1_ref[...])
    t = (a * jax.nn.sigmoid(a) * _dot(h, w3_ref[...])).astype(BF16)
    acc_ref[...] += _dot(t, w2_ref[...])

    @pl.when(f == pl.num_programs(1) - 1)
    def _():
        o_ref[...] = acc_ref[...]


def _ffn_call(x, ya, yb, wo, g, w1, w3, w2, *, layer, tm, tf):
    n = x.shape[0]
    rows = lambda width: pl.BlockSpec((tm, width), lambda i, f: (i, 0))
    const = lambda shape: pl.BlockSpec(shape, lambda i, f: (0, 0))
    up = pl.BlockSpec((D_MODEL, tf), lambda i, f: (0, f))
    return pl.pallas_call(
        _ffn_kernel,
        grid=(n // tm, D_FF // tf),
        in_specs=[rows(D_MODEL), rows(GROUP_WIDTH), rows(3 * GROUP_WIDTH),
                  pl.BlockSpec((None, D_MODEL, D_MODEL), lambda i, f: (layer, 0, 0)), const((1, D_MODEL)),
                  up, up, pl.BlockSpec((tf, D_MODEL), lambda i, f: (f, 0))],
        out_specs=rows(D_MODEL),
        out_shape=jax.ShapeDtypeStruct((n, D_MODEL), F32),
        scratch_shapes=[pltpu.VMEM((tm, D_MODEL), F32), pltpu.VMEM((tm, D_MODEL), BF16)],
        compiler_params=pltpu.CompilerParams(dimension_semantics=("arbitrary", "arbitrary"),
                                             vmem_limit_bytes=VMEM_LIMIT),
        name="outproj_ffn_swiglu",
    )(x, ya, yb, wo, g, w1, w3, w2)


def _moe_kernel(h_ref, x_hbm, gate_ref, w1a_ref, w1b_ref, w3a_ref, w3b_ref, w2a_ref, w2b_ref, o_ref,
                xc_ref, yc_ref, rank_ref, gate_t_ref, rank_t_ref, start_ref, x_sem, *, n_sub):
    e = pl.program_id(1)
    f = pl.program_id(2)
    rb = MOE_ROWS
    sub = MOE_SUB

    def for_chunks(c_lo, c_hi, fn, group=2):
        n = c_hi - c_lo

        def full(c, carry):
            fn(pl.multiple_of((c_lo + group * c) * rb, rb), group * rb)
            return carry

        lax.fori_loop(0, n // group, full, 0)
        done = c_lo + n - n % group
        size = group // 2
        while size >= 1:
            taken = n % (2 * size) >= size
            pl.when(taken)(functools.partial(fn, pl.multiple_of(done * rb, rb), size * rb))
            done = done + jnp.where(taken, size, 0)
            size //= 2

    def sub_chunks(s):
        return start_ref[s] // rb, (start_ref[s + 1] + rb - 1) // rb

    @pl.when((e == 0) & (f == 0))
    def _():
        tm = n_sub * sub
        x_copy = pltpu.make_async_copy(x_hbm.at[pl.ds(pl.multiple_of(pl.program_id(0) * tm, tm), tm), :], o_ref, x_sem)
        x_copy.start()
        before = jnp.where(lax.broadcasted_iota(jnp.int32, (sub, sub), 1)
                           < lax.broadcasted_iota(jnp.int32, (sub, sub), 0), 1.0, 0.0).astype(BF16)
        routed_before = jnp.zeros((1, LANES), F32)
        for s in range(n_sub):
            span = slice(s * sub, (s + 1) * sub)
            gate = gate_ref[span, :]
            routed = jnp.where(gate > 0.0, 1.0, 0.0)
            rank = _dot(before, routed.astype(BF16)) + routed_before
            rank_ref[span, :] = rank
            gate_t_ref[:, span] = gate.T[0:N_EXPERTS]
            rank_t_ref[:, span] = rank.T[0:N_EXPERTS]
            routed_before = routed_before + jnp.sum(routed, axis=0, keepdims=True)
        x_copy.wait()

    @pl.when(f == 0)
    def _():
        start_ref[0] = 0
        for s in range(n_sub):
            g_row = gate_t_ref[pl.ds(e, 1), s * sub:(s + 1) * sub]
            start_ref[s + 1] = start_ref[s] + jnp.sum(jnp.where(g_row > 0.0, 1.0, 0.0)).astype(jnp.int32)

        def clear(r0, rows):
            xc_ref[pl.ds(r0, rows), :] = jnp.zeros((rows, D_MODEL), BF16)
            yc_ref[pl.ds(r0, rows), :] = jnp.zeros((rows, D_MODEL), F32)

        for_chunks(0, (start_ref[n_sub] + rb - 1) // rb, clear)

        for s in range(n_sub):
            span = slice(s * sub, (s + 1) * sub)
            g_row = gate_t_ref[pl.ds(e, 1), span]
            key = jnp.where(g_row > 0.0, rank_t_ref[pl.ds(e, 1), span], -1.0)

            def gather(r0, rows, span=span, key=key):
                slot = (r0 + lax.broadcasted_iota(jnp.int32, (rows, 1), 0)).astype(F32)
                onehot = jnp.where(key == slot, 1.0, 0.0).astype(BF16)
                dst = pl.ds(r0, rows)
                xc_ref[dst, :] = (xc_ref[dst, :].astype(F32) + _dot(onehot, h_ref[span, :])).astype(BF16)

            for_chunks(*sub_chunks(s), gather)

    def swiglu(r0, rows):
        src = pl.ds(r0, rows)
        xc = xc_ref[src, :]
        y = None
        for w1_ref, w3_ref, w2_ref in ((w1a_ref, w3a_ref, w2a_ref), (w1b_ref, w3b_ref, w2b_ref)):
            a = _dot(xc, w1_ref[...])
            t = (a * jax.nn.sigmoid(a) * _dot(xc, w3_ref[...])).astype(BF16)
            y = _dot(t, w2_ref[...]) if y is None else y + _dot(t, w2_ref[...])
        yc_ref[src, :] += y

    for_chunks(0, (start_ref[n_sub] + rb - 1) // rb, swiglu, group=MOE_SWIGLU_GROUP)

    @pl.when(f == pl.num_programs(2) - 1)
    def _():
        lane = lax.broadcasted_iota(jnp.int32, (1, LANES), 1)
        for s in range(n_sub):
            span = slice(s * sub, (s + 1) * sub)
            g_col = jnp.sum(jnp.where(lane == e, gate_ref[span, :], 0.0), axis=-1, keepdims=True)
            r_col = jnp.sum(jnp.where(lane == e, rank_ref[span, :], 0.0), axis=-1, keepdims=True)
            key = jnp.where(g_col > 0.0, r_col, -1.0)

            def scatter(r0, rows, span=span, key=key, g_col=g_col):
                slot = (r0 + lax.broadcasted_iota(jnp.int32, (1, rows), 1)).astype(F32)
                onehot = jnp.where(key == slot, 1.0, 0.0).astype(BF16)
                o_ref[span, :] += g_col * _dot(onehot, yc_ref[pl.ds(r0, rows), :].astype(BF16))

            for_chunks(*sub_chunks(s), scatter)


def _moe_call(h, x, gate, w1, w3, w2, *, tm, tf):
    n = h.shape[0]
    n_sub = tm // MOE_SUB
    once = pl.Buffered(1)
    rows = lambda width, mode=None: pl.BlockSpec((tm, width), lambda i, e, f: (i, 0), pipeline_mode=mode)
    up = lambda half: pl.BlockSpec((None, D_MODEL, tf // 2), lambda i, e, f: (e, 0, 2 * f + half))
    down = lambda half: pl.BlockSpec((None, tf // 2, D_MODEL), lambda i, e, f: (e, 2 * f + half, 0))
    return pl.pallas_call(
        functools.partial(_moe_kernel, n_sub=n_sub),
        grid=(n // tm, N_EXPERTS, D_FF // tf),
        in_specs=[rows(D_MODEL, once), pl.BlockSpec(memory_space=pl.ANY), rows(LANES),
                  up(0), up(1), up(0), up(1), down(0), down(1)],
        out_specs=rows(D_MODEL, once),
        out_shape=jax.ShapeDtypeStruct((n, D_MODEL), F32),
        scratch_shapes=[pltpu.VMEM((tm, D_MODEL), BF16), pltpu.VMEM((tm, D_MODEL), F32), pltpu.VMEM((tm, LANES), F32),
                        pltpu.VMEM((N_EXPERTS, tm), F32), pltpu.VMEM((N_EXPERTS, tm), F32),
                        pltpu.SMEM((n_sub + 1,), jnp.int32), pltpu.SemaphoreType.DMA(())],
        compiler_params=pltpu.CompilerParams(dimension_semantics=("arbitrary", "arbitrary", "arbitrary"),
                                             vmem_limit_bytes=VMEM_LIMIT),
        name="moe_swiglu",
    )(h, x, gate, w1, w1, w3, w3, w2, w2)


def _rope_tables(pos):
    half = ROPE_DIM // 2
    inv_freq = ROPE_THETA ** (-np.arange(half, dtype=np.float64) / half)
    ang = pos.astype(np.float64)[..., None] * inv_freq
    cos, sin = np.cos(ang).astype(np.float32), np.sin(ang).astype(np.float32)
    n = pos.shape[0]
    rest = HEAD_DIM - ROPE_DIM
    cos64 = np.concatenate([cos, cos, np.ones((n, rest), np.float32)], axis=-1)
    sa64 = np.concatenate([-sin, np.zeros((n, rest + half), np.float32)], axis=-1)
    sb64 = np.concatenate([np.zeros((n, half), np.float32), sin, np.zeros((n, rest), np.float32)], axis=-1)
    return tuple(jnp.asarray(np.tile(t, (1, 2)), F32) for t in (cos64, sa64, sb64))


def _block_diag(blocks):
    n = len(blocks)
    rows = []
    for i, blk in enumerate(blocks):
        rows.append(jnp.concatenate([blk if j == i else jnp.zeros((blk.shape[0], blocks[j].shape[1]), blk.dtype)
                                     for j in range(n)], axis=1))
    return jnp.concatenate(rows, axis=0)


def _layout_w_in(w):
    n_gate = N_BRANCH * N_HEADS
    g0 = GROUP_WIDTH + 6 * HEAD_DIM
    pad = jnp.zeros(w.shape[:-1] + (LANES - n_gate,), w.dtype)
    return jnp.concatenate([w[..., :g0 + n_gate], pad, w[..., g0 + n_gate:]], axis=-1)


def _layout_compress(pe, w1, w2):
    half = CMP_LEN // 2
    pe2 = jnp.transpose(pe, (1, 0, 2)).reshape(2, half * LANES)
    w1r = w1.reshape(2, 2, half, HEAD_DIM, CMP_HIDDEN)
    zeros = jnp.zeros((half, HEAD_DIM, CMP_HIDDEN), w1.dtype)

    def big(part):
        k_rows = jnp.concatenate([w1r[0, part], zeros], axis=-1)
        v_rows = jnp.concatenate([zeros, w1r[1, part]], axis=-1)
        return jnp.concatenate([k_rows, v_rows], axis=1).reshape(half * LANES, 2 * CMP_HIDDEN)

    return pe2, big(0), big(1), _block_diag([w2[0], w2[1]])


def _overlap_matrix(seq, ncp, nbp):
    cs = jnp.arange(ncp) * CMP_STRIDE
    bs = jnp.arange(nbp) * SEL_BLOCK
    ov = jnp.clip(jnp.minimum(cs[:, None] + CMP_LEN, bs[None, :] + SEL_BLOCK)
                  - jnp.maximum(cs[:, None], bs[None, :]), 0, None).astype(F32) / CMP_LEN
    keep = (jnp.arange(ncp)[:, None] < seq // CMP_STRIDE - 1) & (jnp.arange(nbp)[None, :] < seq // SEL_BLOCK)
    return jnp.where(keep, ov, 0.0)


def kernel(x, attn_norm_g, w_in, q_norm_g, k_norm_g, cmp_pe, cmp_w1, cmp_w2, sgu_norm_g, sgu_w, sgu_b, pool_w,
           pool_scale, conv_w, mix_out_norm_g, w_out, ffn_norm_g, ffn_w1, ffn_w3, ffn_w2, router_w, expert_w1,
           expert_w3, expert_w2):
    b, seq, _ = x.shape
    n = b * seq
    depth = w_in.shape[0]
    tm = TOKEN_TILE
    ncp = seq // CMP_STRIDE
    nbp = max(LANES, seq // SEL_BLOCK)

    cos, sa, sb = _rope_tables(np.arange(seq))
    cos_c, sa_c, sb_c = _rope_tables(np.arange(ncp) * CMP_STRIDE + CMP_LEN - 1)
    bd = _block_diag([jnp.full((HEAD_DIM, HEAD_DIM), 1.0 / HEAD_DIM, F32)] * 4).astype(BF16)
    overlap_t = _overlap_matrix(seq, ncp, nbp).T.astype(BF16)
    row = lambda v: v.reshape(1, -1)
    two = lambda v: jnp.tile(v, 2).reshape(1, LANES)

    w_in_b = _layout_w_in(w_in.astype(BF16))
    w_out_b = w_out.astype(BF16)
    xf = x.reshape(n, D_MODEL)
    for layer in range(depth):
        kg = k_norm_g[layer]
        sgub = jnp.repeat(sgu_b[layer].T, HEAD_DIM, axis=1)
        qt, kvc, kvs, vts, kvw, vtw, gatet, ybcd = _proj_call(
            xf, row(attn_norm_g[layer]), w_in_b, bd, cos, sa, sb,
            jnp.tile(q_norm_g[layer], 4).reshape(1, 2 * LANES), jnp.tile(kg[1:3], (1, 2)).reshape(1, 2 * LANES),
            row(sgu_norm_g[layer]), sgu_w[layer].transpose(1, 0, 2).reshape(SGU_CHUNK, N_HEADS * SGU_CHUNK), sgub,
            _block_diag([pool_w[layer, i] for i in range(len(POOL_WINDOWS))]).astype(BF16),
            row(pool_scale[layer]), conv_w[layer], mix_out_norm_g[layer].reshape(4, GROUP_WIDTH)[1:],
            layer=layer, seq=seq, tm=tm)
        pe2, w1a, w1b, w2c = _layout_compress(cmp_pe[layer], cmp_w1[layer], cmp_w2[layer])
        kvcc, vtc = _compress_call(kvc.reshape(b, ncp, CMP_STRIDE * LANES), pe2, w1a.astype(BF16), w1b.astype(BF16),
                                   w2c.astype(BF16), two(kg[0]), bd, cos_c, sa_c, sb_c)
        ya = _attn_call(qt, gatet, kvcc, vtc, kvs, vts, kvw, vtw, overlap_t,
                        row(mix_out_norm_g[layer, :GROUP_WIDTH]), seq=seq, tq=ATTN_Q_TILE, tk=SEL_KEY_TILE)
        i = layer // 2
        if layer % 2 == 0:
            xf = _ffn_call(xf, ya, ybcd, w_out_b, row(ffn_norm_g[layer]), ffn_w1[i].astype(BF16),
                           ffn_w3[i].astype(BF16), ffn_w2[i].astype(BF16), layer=layer, tm=FFN_TOKEN_TILE, tf=FFN_FF_TILE)
        else:
            r = jnp.pad(router_w[i], ((0, 0), (0, LANES - N_EXPERTS)))
            r_hi = r.astype(BF16)
            rw = jnp.concatenate([r_hi, (r - r_hi.astype(F32)).astype(BF16)], axis=1)
            x1, hn, gate = _outproj_call(xf, ya, ybcd, w_out_b, row(ffn_norm_g[layer]), rw, layer=layer, tm=tm)
            xf = _moe_call(hn, x1, gate, expert_w1[i].astype(BF16), expert_w3[i].astype(BF16),
                           expert_w2[i].astype(BF16), tm=MOE_TOKEN_TILE, tf=MOE_FF_TILE)
    return xf.reshape(b, seq, D_MODEL)
```

```python
import functools

import numpy as np
import jax
import jax.numpy as jnp
from jax import lax
from jax.experimental import pallas as pl
from jax.experimental.pallas import tpu as pltpu

D_MODEL = 1024
HEAD_DIM = 64
N_HEADS = 4
GROUP_WIDTH = 256
CMP_LEN = 32
CMP_STRIDE = 16
CMP_HIDDEN = 128
SEL_BLOCK = 64
SEL_TOPK = 16
WINDOW = 512
N_BRANCH = 3
ROPE_THETA = 500000.0
ROPE_DIM = 16
SGU_CHUNK = 128
POOL_WINDOWS = (2, 4, 8, 16)
CONV_WIDTH = 3
D_FF = 3584
N_EXPERTS = 8
EPS = 1e-6
NEG_INF = -1e30
Q_SCALE = HEAD_DIM ** -0.5 * 1.4426950408889634
CMP_PART = 128
N_FORCED = 3
SEL_KEY_TILE = 512
BIAS_PAD = 16
GATE_ROWS = 16

LANES = 128
MOE_ROWS = 128
MOE_FF_TILE = 1792
MOE_SWIGLU_GROUP = 2
MOE_TOKEN_TILE = 2048
MOE_SUB = 512
HIST = 16
D_IN_PAD = 2304
TOKEN_TILE = 512
ATTN_Q_TILE = 512
FFN_TOKEN_TILE = 512
FFN_FF_TILE = 1792
VMEM_LIMIT = 56 * 1024 * 1024

F32 = jnp.float32
BF16 = jnp.bfloat16


def _dot(a, b):
    return jnp.dot(a, b, preferred_element_type=F32)


def _split(a):
    hi = a.astype(BF16)
    lo = (a - hi.astype(F32)).astype(BF16)
    return hi, lo


def _rms(y, g):
    return y * lax.rsqrt(jnp.mean(y * y, axis=-1, keepdims=True) + EPS) * g


def _head_norm_rope(c, g, cos, sa, sb, bd):
    width = c.shape[1]
    if width > LANES:
        cos, sa, sb = (jnp.concatenate([t] * (width // LANES), axis=1) for t in (cos, sa, sb))
    hi, lo = _split(c * c)
    msq = _dot(hi, bd[0:width, 0:width]) + _dot(lo, bd[0:width, 0:width])
    cn = c * lax.rsqrt(msq + EPS) * g
    return cn * cos + pltpu.roll(cn, width - ROPE_DIM // 2, 1) * sa + pltpu.roll(cn, ROPE_DIM // 2, 1) * sb


def _proj_kernel(x_ref, g_ref, w_ref, bd_ref, cos_ref, sa_ref, sb_ref, qg_ref, kg_ref,
                 sgug_ref, sguw_ref, sgub_ref, poolw_ref, pools_ref, convw_ref, outg_ref,
                 qt_out, kvc_out, kvs_out, vts_out, kvw_out, vtw_out, gatet_out, y_out, pext, zext, p_ref, kc_ref,
                 *, tm, tiles_per_seq):
    i = pl.program_id(0)
    x = x_ref[...]
    hn = _rms(x, g_ref[...]).astype(BF16)
    bd = bd_ref[...]
    cos, sa, sb = cos_ref[...], sa_ref[...], sb_ref[...]
    lane = lax.broadcasted_iota(jnp.int32, (1, LANES), 1)
    first_half = lane < HEAD_DIM

    for c0 in range(0, D_IN_PAD, 2 * LANES):
        p_ref[:, c0:c0 + 2 * LANES] = _dot(hn, w_ref[:, c0:c0 + 2 * LANES])

    def proj(c0, c1):
        return p_ref[:, c0:c1]

    qn = _head_norm_rope(proj(0, 256), qg_ref[...], cos, sa, sb, bd) * Q_SCALE
    for c in range(2):
        qc = qn[:, c * LANES:(c + 1) * LANES]
        qt_out[(2 * c) * LANES:(2 * c + 1) * LANES, :] = jnp.where(first_half, qc, 0.0).T.astype(BF16)
        qt_out[(2 * c + 1) * LANES:(2 * c + 2) * LANES, :] = jnp.where(
            first_half, pltpu.roll(qc, HEAD_DIM, 1), 0.0).T.astype(BF16)

    kc_ref[...] = proj(256, 384)
    for r in range(CMP_STRIDE):
        kvc_out[:, r * LANES:(r + 1) * LANES] = kc_ref[pl.ds(r, tm // CMP_STRIDE, stride=CMP_STRIDE), :]

    kk = proj(384, 640)
    kn = _head_norm_rope(kk, kg_ref[...], cos, sa, sb, bd)
    ks, kw = kk[:, 0:LANES], kk[:, LANES:2 * LANES]
    tok = (i % tiles_per_seq) * tm + lax.broadcasted_iota(jnp.int32, (tm, 1), 0)
    blk_in_tile = (tok // SEL_BLOCK) % (SEL_KEY_TILE // SEL_BLOCK)
    kvs_out[...] = jnp.where(first_half, kn[:, 0:LANES], jnp.where(lane == HEAD_DIM + blk_in_tile, 1.0, 0.0)).astype(BF16)
    vts_out[...] = jnp.where(first_half, 1.0, ks).T.astype(BF16)
    kvw_out[...] = jnp.where(first_half, kn[:, LANES:2 * LANES], kw).astype(BF16)
    vtw_out[...] = jnp.where(first_half, 1.0, kw).T.astype(BF16)
    gatet_out[...] = jax.nn.sigmoid(proj(640, 768)).T[0:GATE_ROWS, :]

    outg = outg_ref[...]
    lane2 = lax.broadcasted_iota(jnp.int32, (1, GROUP_WIDTH), 1)
    grp = lane2 // HEAD_DIM

    u = proj(768, 1024)
    vn = _rms(proj(1024, 1280), sgug_ref[...])
    t_i = lax.broadcasted_iota(jnp.int32, (SGU_CHUNK, N_HEADS * SGU_CHUNK), 0)
    s_i = lax.broadcasted_iota(jnp.int32, (SGU_CHUNK, N_HEADS * SGU_CHUNK), 1) & (SGU_CHUNK - 1)
    wm = jnp.where(s_i <= t_i, sguw_ref[...], 0.0).astype(BF16)
    vg = [jnp.where(grp == gi, vn, 0.0).astype(BF16) for gi in range(N_HEADS)]
    for c in range(tm // SGU_CHUNK):
        rows = slice(c * SGU_CHUNK, (c + 1) * SGU_CHUNK)
        mixed = _dot(wm, jnp.concatenate([v[rows, :] for v in vg], axis=0)) + sgub_ref[...]
        y_out[rows, 0:GROUP_WIDTH] = _rms(u[rows, :] * mixed, outg[0:1, :]).astype(BF16)

    first = (i % tiles_per_seq) == 0

    @pl.when(first)
    def _():
        pext[0:HIST, :] = jnp.zeros((HIST, GROUP_WIDTH), F32)
        zext[0:HIST, :] = jnp.zeros((HIST, GROUP_WIDTH), F32)

    @pl.when(jnp.logical_not(first))
    def _():
        pext[0:HIST, :] = pext[tm:tm + HIST, :]
        zext[0:HIST, :] = zext[tm:tm + HIST, :]

    pin = proj(1280, 1536)
    pext[HIST:, :] = pin
    pos = (i % tiles_per_seq) * tm + lax.broadcasted_iota(jnp.int32, (tm, 1), 0)
    run = pext[...]
    sums = []
    width = 1
    while width < POOL_WINDOWS[-1]:
        run = run + pltpu.roll(run, width, 0)
        width *= 2
        if width in POOL_WINDOWS:
            sums.append(run[HIST:, :])
    wsum = jnp.where(grp == 0, sums[0], jnp.where(grp == 1, sums[1], jnp.where(grp == 2, sums[2], sums[3])))
    win = jnp.where(grp == 0, POOL_WINDOWS[0], jnp.where(grp == 1, POOL_WINDOWS[1],
                                                         jnp.where(grp == 2, POOL_WINDOWS[2], POOL_WINDOWS[3])))
    cnt = jnp.minimum(pos + 1, win).astype(F32)
    dlt = (wsum / cnt - pin).astype(BF16)
    yc = _dot(dlt, poolw_ref[...]) * pools_ref[...]
    y_out[:, GROUP_WIDTH:2 * GROUP_WIDTH] = _rms(yc, outg[1:2, :]).astype(BF16)

    bg = proj(1536, 1792)
    z = proj(1792, 2048) * proj(2048, 2304)
    zext[HIST:, :] = z
    cw = convw_ref[...]
    conv = cw[2:3, :] * z + cw[1:2, :] * zext[HIST - 1:HIST - 1 + tm, :] + cw[0:1, :] * zext[HIST - 2:HIST - 2 + tm, :]
    y_out[:, 2 * GROUP_WIDTH:3 * GROUP_WIDTH] = _rms(bg * conv, outg[2:3, :]).astype(BF16)


def _proj_call(x, g, w, bd, cos, sa, sb, qg, kg, sgug, sguw, sgub, poolw, pools, convw, outg, *, layer, seq, tm):
    n = x.shape[0]
    tps = seq // tm
    const = lambda shape: pl.BlockSpec(shape, lambda i: (0, 0))
    rows = lambda width: pl.BlockSpec((tm, width), lambda i: (i, 0))
    cols = lambda height: pl.BlockSpec((height, tm), lambda i: (0, i))
    tab = pl.BlockSpec((tm, LANES), lambda i: (i % tps, 0))
    return pl.pallas_call(
        functools.partial(_proj_kernel, tm=tm, tiles_per_seq=tps),
        grid=(n // tm,),
        in_specs=[rows(D_MODEL), const((1, D_MODEL)),
                  pl.BlockSpec((None, D_MODEL, D_IN_PAD), lambda i: (layer, 0, 0)), const((2 * LANES, 2 * LANES)),
                  tab, tab, tab, const((1, 2 * LANES)), const((1, 2 * LANES)),
                  const((1, GROUP_WIDTH)), const((SGU_CHUNK, N_HEADS * SGU_CHUNK)), const((SGU_CHUNK, GROUP_WIDTH)),
                  const((GROUP_WIDTH, GROUP_WIDTH)), const((1, GROUP_WIDTH)), const((CONV_WIDTH, GROUP_WIDTH)),
                  const((3, GROUP_WIDTH))],
        out_specs=[cols(N_HEADS * LANES), pl.BlockSpec((tm // CMP_STRIDE, CMP_STRIDE * LANES), lambda i: (i, 0)),
                   rows(LANES), cols(LANES), rows(LANES), cols(LANES),
                   cols(GATE_ROWS), rows(3 * GROUP_WIDTH)],
        out_shape=[jax.ShapeDtypeStruct((N_HEADS * LANES, n), BF16),
                   jax.ShapeDtypeStruct((n // CMP_STRIDE, CMP_STRIDE * LANES), F32),
                   jax.ShapeDtypeStruct((n, LANES), BF16), jax.ShapeDtypeStruct((LANES, n), BF16),
                   jax.ShapeDtypeStruct((n, LANES), BF16), jax.ShapeDtypeStruct((LANES, n), BF16),
                   jax.ShapeDtypeStruct((GATE_ROWS, n), F32), jax.ShapeDtypeStruct((n, 3 * GROUP_WIDTH), BF16)],
        scratch_shapes=[pltpu.VMEM((tm + HIST, GROUP_WIDTH), F32), pltpu.VMEM((tm + HIST, GROUP_WIDTH), F32),
                        pltpu.VMEM((tm, D_IN_PAD), F32), pltpu.VMEM((tm, LANES), F32)],
        compiler_params=pltpu.CompilerParams(dimension_semantics=("arbitrary",), vmem_limit_bytes=VMEM_LIMIT),
        name="proj_mixers",
    )(x, g, w, bd, cos, sa, sb, qg, kg, sgug, sguw, sgub, poolw, pools, convw, outg)


def _compress_kernel(t_ref, pe_ref, w1a_ref, w1b_ref, w2_ref, kg_ref, bd_ref, cos_ref, sa_ref, sb_ref, kv_ref, vt_ref):
    t = t_ref[0]
    ncp = t.shape[0]
    a = _dot((t + pe_ref[0:1, :]).astype(BF16), w1a_ref[...])
    b = _dot((t + pe_ref[1:2, :]).astype(BF16), w1b_ref[...])
    hid = jax.nn.gelu(a + pltpu.roll(b, ncp - 1, 0), approximate=True)
    kv = _dot(hid.astype(BF16), w2_ref[...])
    first_half = lax.broadcasted_iota(jnp.int32, (1, LANES), 1) < HEAD_DIM
    kn = _head_norm_rope(kv, kg_ref[...], cos_ref[...], sa_ref[...], sb_ref[...], bd_ref[...])
    kv_ref[0] = jnp.where(first_half, kn, kv).astype(BF16)
    vt_ref[0] = jnp.where(first_half, 1.0, kv).T.astype(BF16)


def _compress_call(t2, pe, w1a, w1b, w2, kg, bd, cos, sa, sb):
    b, ncp, width = t2.shape
    const = lambda shape: pl.BlockSpec(shape, lambda i: (0,) * len(shape))
    return pl.pallas_call(
        _compress_kernel,
        grid=(b,),
        in_specs=[pl.BlockSpec((1, ncp, width), lambda i: (i, 0, 0)), const(pe.shape), const(w1a.shape),
                  const(w1b.shape), const(w2.shape), const(kg.shape), const(bd.shape),
                  const(cos.shape), const(sa.shape), const(sb.shape)],
        out_specs=[pl.BlockSpec((1, ncp, LANES), lambda i: (i, 0, 0)), pl.BlockSpec((1, LANES, ncp), lambda i: (i, 0, 0))],
        out_shape=[jax.ShapeDtypeStruct((b, ncp, LANES), BF16), jax.ShapeDtypeStruct((b, LANES, ncp), BF16)],
        compiler_params=pltpu.CompilerParams(dimension_semantics=("arbitrary",), vmem_limit_bytes=VMEM_LIMIT),
        name="nsa_compress",
    )(t2, pe, w1a, w1b, w2, kg, bd, cos, sa, sb)


def _attn_kernel(qt_ref, gatet_ref, kvc_ref, vtc_ref, kvs_ref, vts_ref, kvw_ref, vtw_ref, ovt_ref, og_ref, o_ref,
                 bias_ref, sa_ref, sb_ref, m_ref, acc_ref, oc_ref, imp_ref, sw_ref, ow_ref,
                 *, tq, tk, n_sel):
    s0 = pl.program_id(1) * tq
    nc = N_HEADS * tq
    qs = jnp.concatenate([qt_ref[h * LANES:(h + 1) * LANES, :] for h in range(N_HEADS)], axis=1)
    t_col = s0 + (lax.broadcasted_iota(jnp.int32, (1, nc), 1) & (tq - 1))
    t_q = s0 + lax.broadcasted_iota(jnp.int32, (1, tq), 1)

    ncp = kvc_ref.shape[1]
    part = CMP_PART if ncp % CMP_PART == 0 else ncp

    def cmp_branch(rows):
        cmp_end = lax.broadcasted_iota(jnp.int32, (rows, 1), 0) * CMP_STRIDE + (CMP_LEN - 1)
        sc = jnp.where(cmp_end <= t_col, _dot(kvc_ref[0, 0:rows, :], qs), NEG_INF)
        e = jnp.exp2(sc - jnp.max(sc, axis=0, keepdims=True))
        p = e * jnp.where(t_col >= CMP_LEN - 1, 1.0 / jnp.sum(e, axis=0, keepdims=True), 0.0)
        oc_ref[...] = _dot(vtc_ref[0, :, 0:rows], p.astype(BF16))[HEAD_DIM:, :]
        psum = p[:, 0:tq] + p[:, tq:2 * tq] + p[:, 2 * tq:3 * tq] + p[:, 3 * tq:4 * tq]
        hi, lo = _split(psum)
        imp_ref[...] = _dot(ovt_ref[:, 0:rows], hi) + _dot(ovt_ref[:, 0:rows], lo)

    n_parts = (s0 + tq + CMP_STRIDE * part - 1) // (CMP_STRIDE * part)
    for k in range(1, ncp // part + 1):
        pl.when(n_parts == k)(functools.partial(cmp_branch, k * part))
    o_c = oc_ref[...]
    imp = imp_ref[...]

    n_seg = WINDOW // tq
    seg_start = [pl.multiple_of(jnp.maximum(s0 - k * tq, 0), tq) for k in range(n_seg + 1)]
    for k in range(n_seg + 1):
        sw_ref[k * tq:(k + 1) * tq, :] = _dot(kvw_ref[pl.ds(seg_start[k], tq), :], qs)

    nbp = imp.shape[0]
    blk = lax.broadcasted_iota(jnp.int32, (nbp, 1), 0)
    blk_f = blk.astype(F32)
    cur = t_q // SEL_BLOCK
    valid_b = blk <= cur
    forced = valid_b & ((blk == 0) | (blk == cur) | (blk == cur - 1))
    score = jnp.where(valid_b & jnp.logical_not(forced), imp, -1.0)
    sel = jnp.where(forced, 1.0, 0.0)
    for _ in range(n_sel - N_FORCED):
        top = jnp.max(score, axis=0, keepdims=True)
        idx = jnp.min(jnp.where(score == top, blk_f, float(nbp)), axis=0, keepdims=True)
        hit = blk_f == idx
        sel = jnp.where(hit & (top >= 0.0), 1.0, sel)
        score = jnp.where(hit, -2.0, score)
    bias = jnp.where(sel > 0.5, 0.0, NEG_INF)
    bias_ref[0:nbp, :] = jnp.concatenate([bias] * N_HEADS, axis=1)
    bias_ref[nbp:, :] = jnp.zeros((BIAS_PAD, nc), F32)

    off = lambda k: jnp.where(s0 >= k * tq, 0.0, NEG_INF)
    lower = lax.broadcasted_iota(jnp.int32, (tq, 1), 0) <= (lax.broadcasted_iota(jnp.int32, (1, nc), 1) & (tq - 1))
    s_tri = jnp.where(lower, sw_ref[0:tq, :], sw_ref[n_seg * tq:(n_seg + 1) * tq, :] + off(n_seg))
    s_mid = [sw_ref[k * tq:(k + 1) * tq, :] + off(k) for k in range(1, n_seg)]
    m_w = jnp.max(s_tri, axis=0, keepdims=True)
    for s_k in s_mid:
        m_w = jnp.maximum(m_w, jnp.max(s_k, axis=0, keepdims=True))
    p_tri = jnp.exp2(s_tri - m_w)
    acc_w = (_dot(vtw_ref[:, pl.ds(seg_start[0], tq)], jnp.where(lower, p_tri, 0.0).astype(BF16))
             + _dot(vtw_ref[:, pl.ds(seg_start[n_seg], tq)], jnp.where(lower, 0.0, p_tri).astype(BF16)))
    for k, s_k in zip(range(1, n_seg), s_mid):
        acc_w = acc_w + _dot(vtw_ref[:, pl.ds(seg_start[k], tq)], jnp.exp2(s_k - m_w).astype(BF16))
    ow_ref[...] = acc_w[HEAD_DIM:, :] / acc_w[0:1, :]

    n_blk = tk // SEL_BLOCK
    q_top = qs[0:HEAD_DIM, :]
    q_rest = jnp.zeros((LANES - HEAD_DIM - BIAS_PAD, nc), BF16)

    def scores(j):
        tile_bias = bias_ref[pl.ds(pl.multiple_of(j * n_blk, n_blk), BIAS_PAD), :].astype(BF16)
        qb = jnp.concatenate([q_top, tile_bias, q_rest], axis=0)
        return _dot(kvs_ref[pl.ds(pl.multiple_of(j * tk, tk), tk), :], qb)

    def consume(s_ref, j, causal):
        k0 = pl.multiple_of(j * tk, tk)
        s = s_ref[...]
        if causal:
            kpos = k0 + lax.broadcasted_iota(jnp.int32, (tk, 1), 0)
            s = jnp.where(kpos <= t_col, s, NEG_INF)
        m = m_ref[...]
        m_new = jnp.maximum(m, jnp.max(s, axis=0, keepdims=True))
        pe = jnp.exp2(s - m_new).astype(BF16)
        acc_ref[...] = jnp.exp2(m - m_new) * acc_ref[...] + _dot(vts_ref[:, pl.ds(k0, tk)], pe)
        m_ref[...] = m_new

    n_full = (s0 + tq + tk - 1) // tk - 1
    m_ref[...] = jnp.full((1, nc), NEG_INF, F32)
    acc_ref[...] = jnp.zeros((LANES, nc), F32)
    sa_ref[...] = scores(0)

    def pair(i, carry):
        j = 2 * i
        sb_ref[...] = scores(j + 1)
        consume(sa_ref, j, False)
        sa_ref[...] = scores(jnp.minimum(j + 2, n_full))
        consume(sb_ref, j + 1, False)
        return carry

    lax.fori_loop(0, n_full // 2, pair, 0)

    @pl.when(n_full % 2 == 1)
    def _():
        sb_ref[...] = scores(n_full)
        consume(sa_ref, n_full - 1, False)
        consume(sb_ref, n_full, True)

    @pl.when(n_full % 2 == 0)
    def _():
        consume(sa_ref, n_full, True)

    acc_s = acc_ref[...]
    o_s = acc_s[HEAD_DIM:, :] / acc_s[0:1, :]

    g = gatet_ref[...]
    heads = []
    for h in range(N_HEADS):
        c = slice(h * tq, (h + 1) * tq)
        r = h * N_BRANCH
        heads.append(g[r:r + 1, :] * o_c[:, c] + g[r + 1:r + 2, :] * o_s[:, c] + g[r + 2:r + 3, :] * ow_ref[:, c])
    ya = jnp.concatenate(heads, axis=0).T
    o_ref[...] = _rms(ya, og_ref[...]).astype(BF16)


def _attn_call(qt, gatet, kvc, vtc, kvs, vts, kvw, vtw, ovt, og, *, seq, tq, tk):
    n = qt.shape[1]
    b = n // seq
    nq = seq // tq
    ncp = kvc.shape[1]
    nbp = ovt.shape[0]
    n_sel = min(SEL_TOPK, seq // SEL_BLOCK)
    cols = lambda height: pl.BlockSpec((height, tq), lambda bi, qi: (0, bi * nq + qi))
    per_b = lambda shape: pl.BlockSpec((1,) + shape, lambda bi, qi: (bi, 0, 0))
    const = lambda shape: pl.BlockSpec(shape, lambda bi, qi: (0, 0))
    seq_rows = pl.BlockSpec((seq, LANES), lambda bi, qi: (bi, 0))
    seq_cols = pl.BlockSpec((LANES, seq), lambda bi, qi: (0, bi))
    return pl.pallas_call(
        functools.partial(_attn_kernel, tq=tq, tk=tk, n_sel=n_sel),
        grid=(b, nq),
        in_specs=[cols(N_HEADS * LANES), cols(GATE_ROWS), per_b((ncp, LANES)), per_b((LANES, ncp)),
                  seq_rows, seq_cols, seq_rows, seq_cols, const(ovt.shape), const((1, GROUP_WIDTH))],
        out_specs=pl.BlockSpec((tq, GROUP_WIDTH), lambda bi, qi: (bi * nq + qi, 0)),
        out_shape=jax.ShapeDtypeStruct((n, GROUP_WIDTH), BF16),
        scratch_shapes=[pltpu.VMEM((nbp + BIAS_PAD, N_HEADS * tq), F32), pltpu.VMEM((tk, N_HEADS * tq), F32),
                        pltpu.VMEM((tk, N_HEADS * tq), F32), pltpu.VMEM((1, N_HEADS * tq), F32),
                        pltpu.VMEM((LANES, N_HEADS * tq), F32), pltpu.VMEM((HEAD_DIM, N_HEADS * tq), F32),
                        pltpu.VMEM((nbp, tq), F32), pltpu.VMEM((WINDOW + tq, N_HEADS * tq), F32),
                        pltpu.VMEM((HEAD_DIM, N_HEADS * tq), F32)],
        compiler_params=pltpu.CompilerParams(dimension_semantics=("arbitrary", "arbitrary"),
                                             vmem_limit_bytes=VMEM_LIMIT),
        name="nsa_attention",
    )(qt, gatet, kvc, vtc, kvs, vts, kvw, vtw, ovt, og)


def _outproj_kernel(x_ref, ya_ref, yb_ref, w_ref, g_ref, rw_ref, x1_ref, hn_ref, gate_ref):
    x1 = x_ref[...] + _dot(ya_ref[...], w_ref[0:GROUP_WIDTH, :]) + _dot(yb_ref[...], w_ref[GROUP_WIDTH:, :])
    x1_ref[...] = x1
    hn = _rms(x1, g_ref[...])
    hn_ref[...] = hn.astype(BF16)
    hi, lo = _split(hn)
    both = _dot(hi, rw_ref[...])
    logits = both[:, 0:LANES] + both[:, LANES:] + _dot(lo, rw_ref[:, 0:LANES])
    lane = lax.broadcasted_iota(jnp.int32, (1, LANES), 1)
    lane_f = lane.astype(F32)
    logits = jnp.where(lane < N_EXPERTS, logits, NEG_INF)
    m1 = jnp.max(logits, axis=-1, keepdims=True)
    i1 = jnp.min(jnp.where(logits == m1, lane_f, float(LANES)), axis=-1, keepdims=True)
    rest_l = jnp.where(lane_f == i1, 2 * NEG_INF, logits)
    m2 = jnp.max(rest_l, axis=-1, keepdims=True)
    i2 = jnp.min(jnp.where(rest_l == m2, lane_f, float(LANES)), axis=-1, keepdims=True)
    e2 = jnp.exp(m2 - m1)
    den = 1.0 + e2
    gate_ref[...] = jnp.where(lane_f == i1, 1.0 / den, jnp.where(lane_f == i2, e2 / den, 0.0))


def _outproj_call(x, ya, yb, w, g, rw, *, layer, tm):
    n = x.shape[0]
    rows = lambda width: pl.BlockSpec((tm, width), lambda i: (i, 0))
    const = lambda shape: pl.BlockSpec(shape, lambda i: (0,) * len(shape))
    return pl.pallas_call(
        _outproj_kernel,
        grid=(n // tm,),
        in_specs=[rows(D_MODEL), rows(GROUP_WIDTH), rows(3 * GROUP_WIDTH),
                  pl.BlockSpec((None, D_MODEL, D_MODEL), lambda i: (layer, 0, 0)), const((1, D_MODEL)), const(rw.shape)],
        out_specs=[rows(D_MODEL), rows(D_MODEL), rows(LANES)],
        out_shape=[jax.ShapeDtypeStruct((n, D_MODEL), F32), jax.ShapeDtypeStruct((n, D_MODEL), BF16),
                   jax.ShapeDtypeStruct((n, LANES), F32)],
        compiler_params=pltpu.CompilerParams(dimension_semantics=("arbitrary",), vmem_limit_bytes=VMEM_LIMIT),
        name="outproj_router",
    )(x, ya, yb, w, g, rw)


def _ffn_kernel(x_ref, ya_ref, yb_ref, wo_ref, g_ref, w1_ref, w3_ref, w2_ref, o_ref, acc_ref, hn_ref):
    f = pl.program_id(1)

    @pl.when(f == 0)
    def _():
        x1 = x_ref[...] + _dot(ya_ref[...], wo_ref[0:GROUP_WIDTH, :]) + _dot(yb_ref[...], wo_ref[GROUP_WIDTH:, :])
        acc_ref[...] = x1
        hn_ref[...] = _rms(x1, g_ref[...]).astype(BF16)

    h = hn_ref[...]
    a = _dot(h, w1_ref[...])
    t = (a * jax.nn.sigmoid(a) * _dot(h, w3_ref[...])).astype(BF16)
    acc_ref[...] += _dot(t, w2_ref[...])

    @pl.when(f == pl.num_programs(1) - 1)
    def _():
        o_ref[...] = acc_ref[...]


def _ffn_call(x, ya, yb, wo, g, w1, w3, w2, *, layer, tm, tf):
    n = x.shape[0]
    rows = lambda width: pl.BlockSpec((tm, width), lambda i, f: (i, 0))
    const = lambda shape: pl.BlockSpec(shape, lambda i, f: (0, 0))
    up = pl.BlockSpec((D_MODEL, tf), lambda i, f: (0, f))
    return pl.pallas_call(
        _ffn_kernel,
        grid=(n // tm, D_FF // tf),
        in_specs=[rows(D_MODEL), rows(GROUP_WIDTH), rows(3 * GROUP_WIDTH),
                  pl.BlockSpec((None, D_MODEL, D_MODEL), lambda i, f: (layer, 0, 0)), const((1, D_MODEL)),
                  up, up, pl.BlockSpec((tf, D_MODEL), lambda i, f: (f, 0))],
        out_specs=rows(D_MODEL),
        out_shape=jax.ShapeDtypeStruct((n, D_MODEL), F32),
        scratch_shapes=[pltpu.VMEM((tm, D_MODEL), F32), pltpu.VMEM((tm, D_MODEL), BF16)],
        compiler_params=pltpu.CompilerParams(dimension_semantics=("arbitrary", "arbitrary"),
                                             vmem_limit_bytes=VMEM_LIMIT),
        name="outproj_ffn_swiglu",
    )(x, ya, yb, wo, g, w1, w3, w2)


def _moe_kernel(h_ref, x_hbm, gate_ref, w1_ref, w3_ref, w2_ref, o_ref,
                xc_ref, yc_ref, rank_ref, gate_t_ref, rank_t_ref, start_ref, x_sem, *, n_sub):
    e = pl.program_id(1)
    f = pl.program_id(2)
    rb = MOE_ROWS
    sub = MOE_SUB

    def for_chunks(c_lo, c_hi, fn, group=2):
        n = c_hi - c_lo

        def full(c, carry):
            fn(pl.multiple_of((c_lo + group * c) * rb, rb), group * rb)
            return carry

        lax.fori_loop(0, n // group, full, 0)
        done = c_lo + n - n % group
        size = group // 2
        while size >= 1:
            taken = n % (2 * size) >= size
            pl.when(taken)(functools.partial(fn, pl.multiple_of(done * rb, rb), size * rb))
            done = done + jnp.where(taken, size, 0)
            size //= 2

    def sub_chunks(s):
        return start_ref[s] // rb, (start_ref[s + 1] + rb - 1) // rb

    def sub_span(s):
        return pl.ds(pl.multiple_of(s * sub, sub), sub)

    @pl.when((e == 0) & (f == 0))
    def _():
        tm = n_sub * sub
        x_copy = pltpu.make_async_copy(x_hbm.at[pl.ds(pl.multiple_of(pl.program_id(0) * tm, tm), tm), :], o_ref, x_sem)
        x_copy.start()
        before = jnp.where(lax.broadcasted_iota(jnp.int32, (sub, sub), 1)
                           < lax.broadcasted_iota(jnp.int32, (sub, sub), 0), 1.0, 0.0).astype(BF16)

        def rank_sub(s, routed_before):
            span = sub_span(s)
            gate = gate_ref[span, :]
            routed = jnp.where(gate > 0.0, 1.0, 0.0)
            rank = _dot(before, routed.astype(BF16)) + routed_before
            rank_ref[span, :] = rank
            gate_t_ref[:, span] = gate.T[0:N_EXPERTS]
            rank_t_ref[:, span] = rank.T[0:N_EXPERTS]
            return routed_before + jnp.sum(routed, axis=0, keepdims=True)

        lax.fori_loop(0, n_sub, rank_sub, jnp.zeros((1, LANES), F32))
        x_copy.wait()

    @pl.when(f == 0)
    def _():
        start_ref[0] = 0

        def count_sub(s, carry):
            g_row = gate_t_ref[pl.ds(e, 1), sub_span(s)]
            start_ref[s + 1] = start_ref[s] + jnp.sum(jnp.where(g_row > 0.0, 1.0, 0.0)).astype(jnp.int32)
            return carry

        lax.fori_loop(0, n_sub, count_sub, 0)

        def clear(r0, rows):
            xc_ref[pl.ds(r0, rows), :] = jnp.zeros((rows, D_MODEL), BF16)
            yc_ref[pl.ds(r0, rows), :] = jnp.zeros((rows, D_MODEL), F32)

        for_chunks(0, (start_ref[n_sub] + rb - 1) // rb, clear)

        def gather_sub(s, carry):
            span = sub_span(s)
            g_row = gate_t_ref[pl.ds(e, 1), span]
            key = jnp.where(g_row > 0.0, rank_t_ref[pl.ds(e, 1), span], -1.0)

            def gather(r0, rows):
                slot = (r0 + lax.broadcasted_iota(jnp.int32, (rows, 1), 0)).astype(F32)
                onehot = jnp.where(key == slot, 1.0, 0.0).astype(BF16)
                dst = pl.ds(r0, rows)
                xc_ref[dst, :] = (xc_ref[dst, :].astype(F32) + _dot(onehot, h_ref[span, :])).astype(BF16)

            for_chunks(*sub_chunks(s), gather)
            return carry

        lax.fori_loop(0, n_sub, gather_sub, 0)

    def swiglu(r0, rows):
        src = pl.ds(r0, rows)
        xc = xc_ref[src, :]
        a = _dot(xc, w1_ref[...])
        t = (a * jax.nn.sigmoid(a) * _dot(xc, w3_ref[...])).astype(BF16)
        yc_ref[src, :] += _dot(t, w2_ref[...])

    for_chunks(0, (start_ref[n_sub] + rb - 1) // rb, swiglu, group=MOE_SWIGLU_GROUP)

    @pl.when(f == pl.num_programs(2) - 1)
    def _():
        lane = lax.broadcasted_iota(jnp.int32, (1, LANES), 1)

        def scatter_sub(s, carry):
            span = sub_span(s)
            g_col = jnp.sum(jnp.where(lane == e, gate_ref[span, :], 0.0), axis=-1, keepdims=True)
            r_col = jnp.sum(jnp.where(lane == e, rank_ref[span, :], 0.0), axis=-1, keepdims=True)
            key = jnp.where(g_col > 0.0, r_col, -1.0)

            def scatter(r0, rows):
                slot = (r0 + lax.broadcasted_iota(jnp.int32, (1, rows), 1)).astype(F32)
                onehot = jnp.where(key == slot, 1.0, 0.0).astype(BF16)
                o_ref[span, :] += g_col * _dot(onehot, yc_ref[pl.ds(r0, rows), :].astype(BF16))

            for_chunks(*sub_chunks(s), scatter)
            return carry

        lax.fori_loop(0, n_sub, scatter_sub, 0)


def _moe_call(h, x, gate, w1, w3, w2, *, tm, tf):
    n = h.shape[0]
    n_sub = tm // MOE_SUB
    once = pl.Buffered(1)
    rows = lambda width, mode=None: pl.BlockSpec((tm, width), lambda i, e, f: (i, 0), pipeline_mode=mode)
    return pl.pallas_call(
        functools.partial(_moe_kernel, n_sub=n_sub),
        grid=(n // tm, N_EXPERTS, D_FF // tf),
        in_specs=[rows(D_MODEL, once), pl.BlockSpec(memory_space=pl.ANY), rows(LANES),
                  pl.BlockSpec((None, D_MODEL, tf), lambda i, e, f: (e, 0, f)),
                  pl.BlockSpec((None, D_MODEL, tf), lambda i, e, f: (e, 0, f)),
                  pl.BlockSpec((None, tf, D_MODEL), lambda i, e, f: (e, f, 0))],
        out_specs=rows(D_MODEL, once),
        out_shape=jax.ShapeDtypeStruct((n, D_MODEL), F32),
        scratch_shapes=[pltpu.VMEM((tm, D_MODEL), BF16), pltpu.VMEM((tm, D_MODEL), F32), pltpu.VMEM((tm, LANES), F32),
                        pltpu.VMEM((N_EXPERTS, tm), F32), pltpu.VMEM((N_EXPERTS, tm), F32),
                        pltpu.SMEM((n_sub + 1,), jnp.int32), pltpu.SemaphoreType.DMA(())],
        compiler_params=pltpu.CompilerParams(dimension_semantics=("arbitrary", "arbitrary", "arbitrary"),
                                             vmem_limit_bytes=VMEM_LIMIT),
        name="moe_swiglu",
    )(h, x, gate, w1, w3, w2)


def _rope_tables(pos):
    half = ROPE_DIM // 2
    inv_freq = ROPE_THETA ** (-np.arange(half, dtype=np.float64) / half)
    ang = pos.astype(np.float64)[..., None] * inv_freq
    cos, sin = np.cos(ang).astype(np.float32), np.sin(ang).astype(np.float32)
    n = pos.shape[0]
    rest = HEAD_DIM - ROPE_DIM
    cos64 = np.concatenate([cos, cos, np.ones((n, rest), np.float32)], axis=-1)
    sa64 = np.concatenate([-sin, np.zeros((n, rest + half), np.float32)], axis=-1)
    sb64 = np.concatenate([np.zeros((n, half), np.float32), sin, np.zeros((n, rest), np.float32)], axis=-1)
    return tuple(jnp.asarray(np.tile(t, (1, 2)), F32) for t in (cos64, sa64, sb64))


def _block_diag(blocks):
    n = len(blocks)
    rows = []
    for i, blk in enumerate(blocks):
        rows.append(jnp.concatenate([blk if j == i else jnp.zeros((blk.shape[0], blocks[j].shape[1]), blk.dtype)
                                     for j in range(n)], axis=1))
    return jnp.concatenate(rows, axis=0)


def _layout_w_in(w):
    n_gate = N_BRANCH * N_HEADS
    g0 = GROUP_WIDTH + 6 * HEAD_DIM
    pad = jnp.zeros(w.shape[:-1] + (LANES - n_gate,), w.dtype)
    return jnp.concatenate([w[..., :g0 + n_gate], pad, w[..., g0 + n_gate:]], axis=-1)


def _layout_compress(pe, w1, w2):
    half = CMP_LEN // 2
    pe2 = jnp.transpose(pe, (1, 0, 2)).reshape(2, half * LANES)
    w1r = w1.reshape(2, 2, half, HEAD_DIM, CMP_HIDDEN)
    zeros = jnp.zeros((half, HEAD_DIM, CMP_HIDDEN), w1.dtype)

    def big(part):
        k_rows = jnp.concatenate([w1r[0, part], zeros], axis=-1)
        v_rows = jnp.concatenate([zeros, w1r[1, part]], axis=-1)
        return jnp.concatenate([k_rows, v_rows], axis=1).reshape(half * LANES, 2 * CMP_HIDDEN)

    return pe2, big(0), big(1), _block_diag([w2[0], w2[1]])


def _overlap_matrix(seq, ncp, nbp):
    cs = jnp.arange(ncp) * CMP_STRIDE
    bs = jnp.arange(nbp) * SEL_BLOCK
    ov = jnp.clip(jnp.minimum(cs[:, None] + CMP_LEN, bs[None, :] + SEL_BLOCK)
                  - jnp.maximum(cs[:, None], bs[None, :]), 0, None).astype(F32) / CMP_LEN
    keep = (jnp.arange(ncp)[:, None] < seq // CMP_STRIDE - 1) & (jnp.arange(nbp)[None, :] < seq // SEL_BLOCK)
    return jnp.where(keep, ov, 0.0)


def kernel(x, attn_norm_g, w_in, q_norm_g, k_norm_g, cmp_pe, cmp_w1, cmp_w2, sgu_norm_g, sgu_w, sgu_b, pool_w,
           pool_scale, conv_w, mix_out_norm_g, w_out, ffn_norm_g, ffn_w1, ffn_w3, ffn_w2, router_w, expert_w1,
           expert_w3, expert_w2):
    b, seq, _ = x.shape
    n = b * seq
    depth = w_in.shape[0]
    tm = TOKEN_TILE
    ncp = seq // CMP_STRIDE
    nbp = max(LANES, seq // SEL_BLOCK)

    cos, sa, sb = _rope_tables(np.arange(seq))
    cos_c, sa_c, sb_c = _rope_tables(np.arange(ncp) * CMP_STRIDE + CMP_LEN - 1)
    bd = _block_diag([jnp.full((HEAD_DIM, HEAD_DIM), 1.0 / HEAD_DIM, F32)] * 4).astype(BF16)
    overlap_t = _overlap_matrix(seq, ncp, nbp).T.astype(BF16)
    row = lambda v: v.reshape(1, -1)
    two = lambda v: jnp.tile(v, 2).reshape(1, LANES)

    w_in_b = _layout_w_in(w_in.astype(BF16))
    w_out_b = w_out.astype(BF16)
    xf = x.reshape(n, D_MODEL)
    for layer in range(depth):
        kg = k_norm_g[layer]
        sgub = jnp.repeat(sgu_b[layer].T, HEAD_DIM, axis=1)
        qt, kvc, kvs, vts, kvw, vtw, gatet, ybcd = _proj_call(
            xf, row(attn_norm_g[layer]), w_in_b, bd, cos, sa, sb,
            jnp.tile(q_norm_g[layer], 4).reshape(1, 2 * LANES), jnp.tile(kg[1:3], (1, 2)).reshape(1, 2 * LANES),
            row(sgu_norm_g[layer]), sgu_w[layer].transpose(1, 0, 2).reshape(SGU_CHUNK, N_HEADS * SGU_CHUNK), sgub,
            _block_diag([pool_w[layer, i] for i in range(len(POOL_WINDOWS))]).astype(BF16),
            row(pool_scale[layer]), conv_w[layer], mix_out_norm_g[layer].reshape(4, GROUP_WIDTH)[1:],
            layer=layer, seq=seq, tm=tm)
        pe2, w1a, w1b, w2c = _layout_compress(cmp_pe[layer], cmp_w1[layer], cmp_w2[layer])
        kvcc, vtc = _compress_call(kvc.reshape(b, ncp, CMP_STRIDE * LANES), pe2, w1a.astype(BF16), w1b.astype(BF16),
                                   w2c.astype(BF16), two(kg[0]), bd, cos_c, sa_c, sb_c)
        ya = _attn_call(qt, gatet, kvcc, vtc, kvs, vts, kvw, vtw, overlap_t,
                        row(mix_out_norm_g[layer, :GROUP_WIDTH]), seq=seq, tq=ATTN_Q_TILE, tk=SEL_KEY_TILE)
        i = layer // 2
        if layer % 2 == 0:
            xf = _ffn_call(xf, ya, ybcd, w_out_b, row(ffn_norm_g[layer]), ffn_w1[i].astype(BF16),
                           ffn_w3[i].astype(BF16), ffn_w2[i].astype(BF16), layer=layer, tm=FFN_TOKEN_TILE, tf=FFN_FF_TILE)
        else:
            r = jnp.pad(router_w[i], ((0, 0), (0, LANES - N_EXPERTS)))
            r_hi = r.astype(BF16)
            rw = jnp.concatenate([r_hi, (r - r_hi.astype(F32)).astype(BF16)], axis=1)
            x1, hn, gate = _outproj_call(xf, ya, ybcd, w_out_b, row(ffn_norm_g[layer]), rw, layer=layer, tm=tm)
            xf = _moe_call(hn, x1, gate, expert_w1[i].astype(BF16), expert_w3[i].astype(BF16),
                           expert_w2[i].astype(BF16), tm=MOE_TOKEN_TILE, tf=MOE_FF_TILE)
    return xf.reshape(b, seq, D_MODEL)
```

```python
import functools

import numpy as np
import jax
import jax.numpy as jnp
from jax import lax
from jax.experimental import pallas as pl
from jax.experimental.pallas import tpu as pltpu

D_MODEL = 1024
HEAD_DIM = 64
N_HEADS = 4
GROUP_WIDTH = 256
CMP_LEN = 32
CMP_STRIDE = 16
CMP_HIDDEN = 128
SEL_BLOCK = 64
SEL_TOPK = 16
WINDOW = 512
N_BRANCH = 3
ROPE_THETA = 500000.0
ROPE_DIM = 16
SGU_CHUNK = 128
POOL_WINDOWS = (2, 4, 8, 16)
CONV_WIDTH = 3
D_FF = 3584
N_EXPERTS = 8
EPS = 1e-6
NEG_INF = -1e30
Q_SCALE = HEAD_DIM ** -0.5 * 1.4426950408889634
CMP_PART = 128
N_FORCED = 3
SEL_KEY_TILE = 512
BIAS_PAD = 16
GATE_ROWS = 16

LANES = 128
MOE_ROWS = 128
MOE_FF_TILE = 1792
MOE_SWIGLU_GROUP = 4
MOE_TOKEN_TILE = 2048
MOE_SUB = 512
HIST = 16
D_IN_PAD = 2304
TOKEN_TILE = 1024
ATTN_Q_TILE = 512
FFN_TOKEN_TILE = 512
FFN_FF_TILE = 1792
VMEM_LIMIT = 56 * 1024 * 1024

F32 = jnp.float32
BF16 = jnp.bfloat16


def _dot(a, b):
    return jnp.dot(a, b, preferred_element_type=F32)


def _split(a):
    hi = a.astype(BF16)
    lo = (a - hi.astype(F32)).astype(BF16)
    return hi, lo


def _rms(y, g):
    return y * lax.rsqrt(jnp.mean(y * y, axis=-1, keepdims=True) + EPS) * g


def _head_norm_rope(c, g, cos, sa, sb, bd):
    width = c.shape[1]
    if width > LANES:
        cos, sa, sb = (jnp.concatenate([t] * (width // LANES), axis=1) for t in (cos, sa, sb))
    hi, lo = _split(c * c)
    msq = _dot(hi, bd[0:width, 0:width]) + _dot(lo, bd[0:width, 0:width])
    cn = c * lax.rsqrt(msq + EPS) * g
    return cn * cos + pltpu.roll(cn, width - ROPE_DIM // 2, 1) * sa + pltpu.roll(cn, ROPE_DIM // 2, 1) * sb


def _proj_kernel(x_ref, g_ref, w_ref, bd_ref, cos_ref, sa_ref, sb_ref, qg_ref, kg_ref,
                 sgug_ref, sguw_ref, sgub_ref, poolw_ref, pools_ref, convw_ref, outg_ref,
                 qt_out, kvc_out, kvs_out, vts_out, kvw_out, vtw_out, gatet_out, y_out, pext, zext, p_ref, kc_ref,
                 *, tm, tiles_per_seq):
    i = pl.program_id(0)
    x = x_ref[...]
    hn = _rms(x, g_ref[...]).astype(BF16)
    bd = bd_ref[...]
    cos, sa, sb = cos_ref[...], sa_ref[...], sb_ref[...]
    lane = lax.broadcasted_iota(jnp.int32, (1, LANES), 1)
    first_half = lane < HEAD_DIM

    for c0 in range(0, D_IN_PAD, 2 * LANES):
        p_ref[:, c0:c0 + 2 * LANES] = _dot(hn, w_ref[:, c0:c0 + 2 * LANES])

    def proj(c0, c1):
        return p_ref[:, c0:c1]

    qn = _head_norm_rope(proj(0, 256), qg_ref[...], cos, sa, sb, bd) * Q_SCALE
    for c in range(2):
        qc = qn[:, c * LANES:(c + 1) * LANES]
        qt_out[(2 * c) * LANES:(2 * c + 1) * LANES, :] = jnp.where(first_half, qc, 0.0).T.astype(BF16)
        qt_out[(2 * c + 1) * LANES:(2 * c + 2) * LANES, :] = jnp.where(
            first_half, pltpu.roll(qc, HEAD_DIM, 1), 0.0).T.astype(BF16)

    kc_ref[...] = proj(256, 384)
    for r in range(CMP_STRIDE):
        kvc_out[:, r * LANES:(r + 1) * LANES] = kc_ref[pl.ds(r, tm // CMP_STRIDE, stride=CMP_STRIDE), :]

    kk = proj(384, 640)
    kn = _head_norm_rope(kk, kg_ref[...], cos, sa, sb, bd)
    ks, kw = kk[:, 0:LANES], kk[:, LANES:2 * LANES]
    tok = (i % tiles_per_seq) * tm + lax.broadcasted_iota(jnp.int32, (tm, 1), 0)
    blk_in_tile = (tok // SEL_BLOCK) % (SEL_KEY_TILE // SEL_BLOCK)
    kvs_out[...] = jnp.where(first_half, kn[:, 0:LANES], jnp.where(lane == HEAD_DIM + blk_in_tile, 1.0, 0.0)).astype(BF16)
    vts_out[...] = jnp.where(first_half, 1.0, ks).T.astype(BF16)
    kvw_out[...] = jnp.where(first_half, kn[:, LANES:2 * LANES], kw).astype(BF16)
    vtw_out[...] = jnp.where(first_half, 1.0, kw).T.astype(BF16)
    gatet_out[...] = jax.nn.sigmoid(proj(640, 768)).T[0:GATE_ROWS, :]

    outg = outg_ref[...]
    lane2 = lax.broadcasted_iota(jnp.int32, (1, GROUP_WIDTH), 1)
    grp = lane2 // HEAD_DIM

    u = proj(768, 1024)
    vn = _rms(proj(1024, 1280), sgug_ref[...])
    t_i = lax.broadcasted_iota(jnp.int32, (SGU_CHUNK, N_HEADS * SGU_CHUNK), 0)
    s_i = lax.broadcasted_iota(jnp.int32, (SGU_CHUNK, N_HEADS * SGU_CHUNK), 1) & (SGU_CHUNK - 1)
    wm = jnp.where(s_i <= t_i, sguw_ref[...], 0.0).astype(BF16)
    vg = [jnp.where(grp == gi, vn, 0.0).astype(BF16) for gi in range(N_HEADS)]
    for c in range(tm // SGU_CHUNK):
        rows = slice(c * SGU_CHUNK, (c + 1) * SGU_CHUNK)
        mixed = _dot(wm, jnp.concatenate([v[rows, :] for v in vg], axis=0)) + sgub_ref[...]
        y_out[rows, 0:GROUP_WIDTH] = _rms(u[rows, :] * mixed, outg[0:1, :]).astype(BF16)

    first = (i % tiles_per_seq) == 0

    @pl.when(first)
    def _():
        pext[0:HIST, :] = jnp.zeros((HIST, GROUP_WIDTH), F32)
        zext[0:HIST, :] = jnp.zeros((HIST, GROUP_WIDTH), F32)

    @pl.when(jnp.logical_not(first))
    def _():
        pext[0:HIST, :] = pext[tm:tm + HIST, :]
        zext[0:HIST, :] = zext[tm:tm + HIST, :]

    pin = proj(1280, 1536)
    pext[HIST:, :] = pin
    pos = (i % tiles_per_seq) * tm + lax.broadcasted_iota(jnp.int32, (tm, 1), 0)
    run = pext[...]
    sums = []
    width = 1
    while width < POOL_WINDOWS[-1]:
        run = run + pltpu.roll(run, width, 0)
        width *= 2
        if width in POOL_WINDOWS:
            sums.append(run[HIST:, :])
    wsum = jnp.where(grp == 0, sums[0], jnp.where(grp == 1, sums[1], jnp.where(grp == 2, sums[2], sums[3])))
    win = jnp.where(grp == 0, POOL_WINDOWS[0], jnp.where(grp == 1, POOL_WINDOWS[1],
                                                         jnp.where(grp == 2, POOL_WINDOWS[2], POOL_WINDOWS[3])))
    cnt = jnp.minimum(pos + 1, win).astype(F32)
    dlt = (wsum / cnt - pin).astype(BF16)
    yc = _dot(dlt, poolw_ref[...]) * pools_ref[...]
    y_out[:, GROUP_WIDTH:2 * GROUP_WIDTH] = _rms(yc, outg[1:2, :]).astype(BF16)

    bg = proj(1536, 1792)
    z = proj(1792, 2048) * proj(2048, 2304)
    zext[HIST:, :] = z
    cw = convw_ref[...]
    conv = cw[2:3, :] * z + cw[1:2, :] * zext[HIST - 1:HIST - 1 + tm, :] + cw[0:1, :] * zext[HIST - 2:HIST - 2 + tm, :]
    y_out[:, 2 * GROUP_WIDTH:3 * GROUP_WIDTH] = _rms(bg * conv, outg[2:3, :]).astype(BF16)


def _proj_call(x, g, w, bd, cos, sa, sb, qg, kg, sgug, sguw, sgub, poolw, pools, convw, outg, *, layer, seq, tm):
    n = x.shape[0]
    tps = seq // tm
    const = lambda shape: pl.BlockSpec(shape, lambda i: (0, 0))
    rows = lambda width: pl.BlockSpec((tm, width), lambda i: (i, 0))
    cols = lambda height: pl.BlockSpec((height, tm), lambda i: (0, i))
    tab = pl.BlockSpec((tm, LANES), lambda i: (i % tps, 0))
    return pl.pallas_call(
        functools.partial(_proj_kernel, tm=tm, tiles_per_seq=tps),
        grid=(n // tm,),
        in_specs=[rows(D_MODEL), const((1, D_MODEL)),
                  pl.BlockSpec((None, D_MODEL, D_IN_PAD), lambda i: (layer, 0, 0)), const((2 * LANES, 2 * LANES)),
                  tab, tab, tab, const((1, 2 * LANES)), const((1, 2 * LANES)),
                  const((1, GROUP_WIDTH)), const((SGU_CHUNK, N_HEADS * SGU_CHUNK)), const((SGU_CHUNK, GROUP_WIDTH)),
                  const((GROUP_WIDTH, GROUP_WIDTH)), const((1, GROUP_WIDTH)), const((CONV_WIDTH, GROUP_WIDTH)),
                  const((3, GROUP_WIDTH))],
        out_specs=[cols(N_HEADS * LANES), pl.BlockSpec((tm // CMP_STRIDE, CMP_STRIDE * LANES), lambda i: (i, 0)),
                   rows(LANES), cols(LANES), rows(LANES), cols(LANES),
                   cols(GATE_ROWS), rows(3 * GROUP_WIDTH)],
        out_shape=[jax.ShapeDtypeStruct((N_HEADS * LANES, n), BF16),
                   jax.ShapeDtypeStruct((n // CMP_STRIDE, CMP_STRIDE * LANES), F32),
                   jax.ShapeDtypeStruct((n, LANES), BF16), jax.ShapeDtypeStruct((LANES, n), BF16),
                   jax.ShapeDtypeStruct((n, LANES), BF16), jax.ShapeDtypeStruct((LANES, n), BF16),
                   jax.ShapeDtypeStruct((GATE_ROWS, n), F32), jax.ShapeDtypeStruct((n, 3 * GROUP_WIDTH), BF16)],
        scratch_shapes=[pltpu.VMEM((tm + HIST, GROUP_WIDTH), F32), pltpu.VMEM((tm + HIST, GROUP_WIDTH), F32),
                        pltpu.VMEM((tm, D_IN_PAD), F32), pltpu.VMEM((tm, LANES), F32)],
        compiler_params=pltpu.CompilerParams(dimension_semantics=("arbitrary",), vmem_limit_bytes=VMEM_LIMIT),
        name="proj_mixers",
    )(x, g, w, bd, cos, sa, sb, qg, kg, sgug, sguw, sgub, poolw, pools, convw, outg)


def _compress_kernel(t_ref, pe_ref, w1a_ref, w1b_ref, w2_ref, kg_ref, bd_ref, cos_ref, sa_ref, sb_ref, kv_ref, vt_ref):
    t = t_ref[0]
    ncp = t.shape[0]
    a = _dot((t + pe_ref[0:1, :]).astype(BF16), w1a_ref[...])
    b = _dot((t + pe_ref[1:2, :]).astype(BF16), w1b_ref[...])
    hid = jax.nn.gelu(a + pltpu.roll(b, ncp - 1, 0), approximate=True)
    kv = _dot(hid.astype(BF16), w2_ref[...])
    first_half = lax.broadcasted_iota(jnp.int32, (1, LANES), 1) < HEAD_DIM
    kn = _head_norm_rope(kv, kg_ref[...], cos_ref[...], sa_ref[...], sb_ref[...], bd_ref[...])
    kv_ref[0] = jnp.where(first_half, kn, kv).astype(BF16)
    vt_ref[0] = jnp.where(first_half, 1.0, kv).T.astype(BF16)


def _compress_call(t2, pe, w1a, w1b, w2, kg, bd, cos, sa, sb):
    b, ncp, width = t2.shape
    const = lambda shape: pl.BlockSpec(shape, lambda i: (0,) * len(shape))
    return pl.pallas_call(
        _compress_kernel,
        grid=(b,),
        in_specs=[pl.BlockSpec((1, ncp, width), lambda i: (i, 0, 0)), const(pe.shape), const(w1a.shape),
                  const(w1b.shape), const(w2.shape), const(kg.shape), const(bd.shape),
                  const(cos.shape), const(sa.shape), const(sb.shape)],
        out_specs=[pl.BlockSpec((1, ncp, LANES), lambda i: (i, 0, 0)), pl.BlockSpec((1, LANES, ncp), lambda i: (i, 0, 0))],
        out_shape=[jax.ShapeDtypeStruct((b, ncp, LANES), BF16), jax.ShapeDtypeStruct((b, LANES, ncp), BF16)],
        compiler_params=pltpu.CompilerParams(dimension_semantics=("arbitrary",), vmem_limit_bytes=VMEM_LIMIT),
        name="nsa_compress",
    )(t2, pe, w1a, w1b, w2, kg, bd, cos, sa, sb)


def _attn_kernel(qt_ref, gatet_ref, kvc_ref, vtc_ref, kvs_ref, vts_ref, kvw_ref, vtw_ref, ovt_ref, og_ref, o_ref,
                 bias_ref, sa_ref, sb_ref, m_ref, acc_ref, oc_ref, imp_ref, sw_ref, ow_ref,
                 *, tq, tk, n_sel):
    s0 = pl.program_id(1) * tq
    nc = N_HEADS * tq
    qs = jnp.concatenate([qt_ref[h * LANES:(h + 1) * LANES, :] for h in range(N_HEADS)], axis=1)
    t_col = s0 + (lax.broadcasted_iota(jnp.int32, (1, nc), 1) & (tq - 1))
    t_q = s0 + lax.broadcasted_iota(jnp.int32, (1, tq), 1)

    ncp = kvc_ref.shape[1]
    part = CMP_PART if ncp % CMP_PART == 0 else ncp

    def cmp_branch(rows):
        cmp_end = lax.broadcasted_iota(jnp.int32, (rows, 1), 0) * CMP_STRIDE + (CMP_LEN - 1)
        sc = jnp.where(cmp_end <= t_col, _dot(kvc_ref[0, 0:rows, :], qs), NEG_INF)
        e = jnp.exp2(sc - jnp.max(sc, axis=0, keepdims=True))
        p = e * jnp.where(t_col >= CMP_LEN - 1, 1.0 / jnp.sum(e, axis=0, keepdims=True), 0.0)
        oc_ref[...] = _dot(vtc_ref[0, :, 0:rows], p.astype(BF16))[HEAD_DIM:, :]
        psum = p[:, 0:tq] + p[:, tq:2 * tq] + p[:, 2 * tq:3 * tq] + p[:, 3 * tq:4 * tq]
        hi, lo = _split(psum)
        imp_ref[...] = _dot(ovt_ref[:, 0:rows], hi) + _dot(ovt_ref[:, 0:rows], lo)

    n_parts = (s0 + tq + CMP_STRIDE * part - 1) // (CMP_STRIDE * part)
    for k in range(1, ncp // part + 1):
        pl.when(n_parts == k)(functools.partial(cmp_branch, k * part))
    o_c = oc_ref[...]
    imp = imp_ref[...]

    n_seg = WINDOW // tq
    seg_start = [pl.multiple_of(jnp.maximum(s0 - k * tq, 0), tq) for k in range(n_seg + 1)]
    for k in range(n_seg + 1):
        sw_ref[k * tq:(k + 1) * tq, :] = _dot(kvw_ref[pl.ds(seg_start[k], tq), :], qs)

    nbp = imp.shape[0]
    blk = lax.broadcasted_iota(jnp.int32, (nbp, 1), 0)
    blk_f = blk.astype(F32)
    cur = t_q // SEL_BLOCK
    valid_b = blk <= cur
    forced = valid_b & ((blk == 0) | (blk == cur) | (blk == cur - 1))
    score = jnp.where(valid_b & jnp.logical_not(forced), imp, -1.0)
    sel = jnp.where(forced, 1.0, 0.0)
    for _ in range(n_sel - N_FORCED):
        top = jnp.max(score, axis=0, keepdims=True)
        idx = jnp.min(jnp.where(score == top, blk_f, float(nbp)), axis=0, keepdims=True)
        hit = blk_f == idx
        sel = jnp.where(hit & (top >= 0.0), 1.0, sel)
        score = jnp.where(hit, -2.0, score)
    bias = jnp.where(sel > 0.5, 0.0, NEG_INF)
    bias_ref[0:nbp, :] = jnp.concatenate([bias] * N_HEADS, axis=1)
    bias_ref[nbp:, :] = jnp.zeros((BIAS_PAD, nc), F32)

    off = lambda k: jnp.where(s0 >= k * tq, 0.0, NEG_INF)
    lower = lax.broadcasted_iota(jnp.int32, (tq, 1), 0) <= (lax.broadcasted_iota(jnp.int32, (1, nc), 1) & (tq - 1))
    s_tri = jnp.where(lower, sw_ref[0:tq, :], sw_ref[n_seg * tq:(n_seg + 1) * tq, :] + off(n_seg))
    s_mid = [sw_ref[k * tq:(k + 1) * tq, :] + off(k) for k in range(1, n_seg)]
    m_w = jnp.max(s_tri, axis=0, keepdims=True)
    for s_k in s_mid:
        m_w = jnp.maximum(m_w, jnp.max(s_k, axis=0, keepdims=True))
    p_tri = jnp.exp2(s_tri - m_w)
    acc_w = (_dot(vtw_ref[:, pl.ds(seg_start[0], tq)], jnp.where(lower, p_tri, 0.0).astype(BF16))
             + _dot(vtw_ref[:, pl.ds(seg_start[n_seg], tq)], jnp.where(lower, 0.0, p_tri).astype(BF16)))
    for k, s_k in zip(range(1, n_seg), s_mid):
        acc_w = acc_w + _dot(vtw_ref[:, pl.ds(seg_start[k], tq)], jnp.exp2(s_k - m_w).astype(BF16))
    ow_ref[...] = acc_w[HEAD_DIM:, :] / acc_w[0:1, :]

    n_blk = tk // SEL_BLOCK
    q_top = qs[0:HEAD_DIM, :]
    q_rest = jnp.zeros((LANES - HEAD_DIM - BIAS_PAD, nc), BF16)

    def scores(j):
        tile_bias = bias_ref[pl.ds(pl.multiple_of(j * n_blk, n_blk), BIAS_PAD), :].astype(BF16)
        qb = jnp.concatenate([q_top, tile_bias, q_rest], axis=0)
        return _dot(kvs_ref[pl.ds(pl.multiple_of(j * tk, tk), tk), :], qb)

    def consume(s_ref, j, causal):
        k0 = pl.multiple_of(j * tk, tk)
        s = s_ref[...]
        if causal:
            kpos = k0 + lax.broadcasted_iota(jnp.int32, (tk, 1), 0)
            s = jnp.where(kpos <= t_col, s, NEG_INF)
        m = m_ref[...]
        m_new = jnp.maximum(m, jnp.max(s, axis=0, keepdims=True))
        pe = jnp.exp2(s - m_new).astype(BF16)
        acc_ref[...] = jnp.exp2(m - m_new) * acc_ref[...] + _dot(vts_ref[:, pl.ds(k0, tk)], pe)
        m_ref[...] = m_new

    n_full = (s0 + tq + tk - 1) // tk - 1
    m_ref[...] = jnp.full((1, nc), NEG_INF, F32)
    acc_ref[...] = jnp.zeros((LANES, nc), F32)
    sa_ref[...] = scores(0)

    def pair(i, carry):
        j = 2 * i
        sb_ref[...] = scores(j + 1)
        consume(sa_ref, j, False)
        sa_ref[...] = scores(jnp.minimum(j + 2, n_full))
        consume(sb_ref, j + 1, False)
        return carry

    lax.fori_loop(0, n_full // 2, pair, 0)

    @pl.when(n_full % 2 == 1)
    def _():
        sb_ref[...] = scores(n_full)
        consume(sa_ref, n_full - 1, False)
        consume(sb_ref, n_full, True)

    @pl.when(n_full % 2 == 0)
    def _():
        consume(sa_ref, n_full, True)

    acc_s = acc_ref[...]
    o_s = acc_s[HEAD_DIM:, :] / acc_s[0:1, :]

    g = gatet_ref[...]
    heads = []
    for h in range(N_HEADS):
        c = slice(h * tq, (h + 1) * tq)
        r = h * N_BRANCH
        heads.append(g[r:r + 1, :] * o_c[:, c] + g[r + 1:r + 2, :] * o_s[:, c] + g[r + 2:r + 3, :] * ow_ref[:, c])
    ya = jnp.concatenate(heads, axis=0).T
    o_ref[...] = _rms(ya, og_ref[...]).astype(BF16)


def _attn_call(qt, gatet, kvc, vtc, kvs, vts, kvw, vtw, ovt, og, *, seq, tq, tk):
    n = qt.shape[1]
    b = n // seq
    nq = seq // tq
    ncp = kvc.shape[1]
    nbp = ovt.shape[0]
    n_sel = min(SEL_TOPK, seq // SEL_BLOCK)
    cols = lambda height: pl.BlockSpec((height, tq), lambda bi, qi: (0, bi * nq + qi))
    per_b = lambda shape: pl.BlockSpec((1,) + shape, lambda bi, qi: (bi, 0, 0))
    const = lambda shape: pl.BlockSpec(shape, lambda bi, qi: (0, 0))
    seq_rows = pl.BlockSpec((seq, LANES), lambda bi, qi: (bi, 0))
    seq_cols = pl.BlockSpec((LANES, seq), lambda bi, qi: (0, bi))
    return pl.pallas_call(
        functools.partial(_attn_kernel, tq=tq, tk=tk, n_sel=n_sel),
        grid=(b, nq),
        in_specs=[cols(N_HEADS * LANES), cols(GATE_ROWS), per_b((ncp, LANES)), per_b((LANES, ncp)),
                  seq_rows, seq_cols, seq_rows, seq_cols, const(ovt.shape), const((1, GROUP_WIDTH))],
        out_specs=pl.BlockSpec((tq, GROUP_WIDTH), lambda bi, qi: (bi * nq + qi, 0)),
        out_shape=jax.ShapeDtypeStruct((n, GROUP_WIDTH), BF16),
        scratch_shapes=[pltpu.VMEM((nbp + BIAS_PAD, N_HEADS * tq), F32), pltpu.VMEM((tk, N_HEADS * tq), F32),
                        pltpu.VMEM((tk, N_HEADS * tq), F32), pltpu.VMEM((1, N_HEADS * tq), F32),
                        pltpu.VMEM((LANES, N_HEADS * tq), F32), pltpu.VMEM((HEAD_DIM, N_HEADS * tq), F32),
                        pltpu.VMEM((nbp, tq), F32), pltpu.VMEM((WINDOW + tq, N_HEADS * tq), F32),
                        pltpu.VMEM((HEAD_DIM, N_HEADS * tq), F32)],
        compiler_params=pltpu.CompilerParams(dimension_semantics=("arbitrary", "arbitrary"),
                                             vmem_limit_bytes=VMEM_LIMIT),
        name="nsa_attention",
    )(qt, gatet, kvc, vtc, kvs, vts, kvw, vtw, ovt, og)


def _outproj_kernel(x_ref, ya_ref, yb_ref, w_ref, g_ref, rw_ref, x1_ref, hn_ref, gate_ref):
    x1 = x_ref[...] + _dot(ya_ref[...], w_ref[0:GROUP_WIDTH, :]) + _dot(yb_ref[...], w_ref[GROUP_WIDTH:, :])
    x1_ref[...] = x1
    hn = _rms(x1, g_ref[...])
    hn_ref[...] = hn.astype(BF16)
    hi, lo = _split(hn)
    both = _dot(hi, rw_ref[...])
    logits = both[:, 0:LANES] + both[:, LANES:] + _dot(lo, rw_ref[:, 0:LANES])
    lane = lax.broadcasted_iota(jnp.int32, (1, LANES), 1)
    lane_f = lane.astype(F32)
    logits = jnp.where(lane < N_EXPERTS, logits, NEG_INF)
    m1 = jnp.max(logits, axis=-1, keepdims=True)
    i1 = jnp.min(jnp.where(logits == m1, lane_f, float(LANES)), axis=-1, keepdims=True)
    rest_l = jnp.where(lane_f == i1, 2 * NEG_INF, logits)
    m2 = jnp.max(rest_l, axis=-1, keepdims=True)
    i2 = jnp.min(jnp.where(rest_l == m2, lane_f, float(LANES)), axis=-1, keepdims=True)
    e2 = jnp.exp(m2 - m1)
    den = 1.0 + e2
    gate_ref[...] = jnp.where(lane_f == i1, 1.0 / den, jnp.where(lane_f == i2, e2 / den, 0.0))


def _outproj_call(x, ya, yb, w, g, rw, *, layer, tm):
    n = x.shape[0]
    rows = lambda width: pl.BlockSpec((tm, width), lambda i: (i, 0))
    const = lambda shape: pl.BlockSpec(shape, lambda i: (0,) * len(shape))
    return pl.pallas_call(
        _outproj_kernel,
        grid=(n // tm,),
        in_specs=[rows(D_MODEL), rows(GROUP_WIDTH), rows(3 * GROUP_WIDTH),
                  pl.BlockSpec((None, D_MODEL, D_MODEL), lambda i: (layer, 0, 0)), const((1, D_MODEL)), const(rw.shape)],
        out_specs=[rows(D_MODEL), rows(D_MODEL), rows(LANES)],
        out_shape=[jax.ShapeDtypeStruct((n, D_MODEL), F32), jax.ShapeDtypeStruct((n, D_MODEL), BF16),
                   jax.ShapeDtypeStruct((n, LANES), F32)],
        compiler_params=pltpu.CompilerParams(dimension_semantics=("arbitrary",), vmem_limit_bytes=VMEM_LIMIT),
        name="outproj_router",
    )(x, ya, yb, w, g, rw)


def _ffn_kernel(x_ref, ya_ref, yb_ref, wo_ref, g_ref, w1_ref, w3_ref, w2_ref, o_ref, acc_ref, hn_ref):
    f = pl.program_id(1)

    @pl.when(f == 0)
    def _():
        x1 = x_ref[...] + _dot(ya_ref[...], wo_ref[0:GROUP_WIDTH, :]) + _dot(yb_ref[...], wo_ref[GROUP_WIDTH:, :])
        acc_ref[...] = x1
        hn_ref[...] = _rms(x1, g_ref[...]).astype(BF16)

    h = hn_ref[...]
    a = _dot(h, w1_ref[...])
    t = (a * jax.nn.sigmoid(a) * _dot(h, w3_ref[...])).astype(BF16)
    acc_ref[...] += _dot(t, w2_ref[...])

    @pl.when(f == pl.num_programs(1) - 1)
    def _():
        o_ref[...] = acc_ref[...]


def _ffn_call(x, ya, yb, wo, g, w1, w3, w2, *, layer, tm, tf):
    n = x.shape[0]
    rows = lambda width: pl.BlockSpec((tm, width), lambda i, f: (i, 0))
    const = lambda shape: pl.BlockSpec(shape, lambda i, f: (0, 0))
    up = pl.BlockSpec((D_MODEL, tf), lambda i, f: (0, f))
    return pl.pallas_call(
        _ffn_kernel,
        grid=(n // tm, D_FF // tf),
        in_specs=[rows(D_MODEL), rows(GROUP_WIDTH), rows(3 * GROUP_WIDTH),
                  pl.BlockSpec((None, D_MODEL, D_MODEL), lambda i, f: (layer, 0, 0)), const((1, D_MODEL)),
                  up, up, pl.BlockSpec((tf, D_MODEL), lambda i, f: (f, 0))],
        out_specs=rows(D_MODEL),
        out_shape=jax.ShapeDtypeStruct((n, D_MODEL), F32),
        scratch_shapes=[pltpu.VMEM((tm, D_MODEL), F32), pltpu.VMEM((tm, D_MODEL), BF16)],
        compiler_params=pltpu.CompilerParams(dimension_semantics=("arbitrary", "arbitrary"),
                                             vmem_limit_bytes=VMEM_LIMIT),
        name="outproj_ffn_swiglu",
    )(x, ya, yb, wo, g, w1, w3, w2)


def _moe_kernel(h_ref, x_hbm, gate_ref, w1_ref, w3_ref, w2_ref, o_ref,
                xc_ref, yc_ref, rank_ref, gate_t_ref, rank_t_ref, start_ref, x_sem, *, n_sub):
    e = pl.program_id(1)
    f = pl.program_id(2)
    rb = MOE_ROWS
    sub = MOE_SUB

    def for_chunks(c_lo, c_hi, fn, group=2):
        n = c_hi - c_lo

        def full(c, carry):
            fn(pl.multiple_of((c_lo + group * c) * rb, rb), group * rb)
            return carry

        lax.fori_loop(0, n // group, full, 0)
        done = c_lo + n - n % group
        size = group // 2
        while size >= 1:
            taken = n % (2 * size) >= size
            pl.when(taken)(functools.partial(fn, pl.multiple_of(done * rb, rb), size * rb))
            done = done + jnp.where(taken, size, 0)
            size //= 2

    def sub_chunks(s):
        return start_ref[s] // rb, (start_ref[s + 1] + rb - 1) // rb

    @pl.when((e == 0) & (f == 0))
    def _():
        tm = n_sub * sub
        x_copy = pltpu.make_async_copy(x_hbm.at[pl.ds(pl.multiple_of(pl.program_id(0) * tm, tm), tm), :], o_ref, x_sem)
        x_copy.start()
        before = jnp.where(lax.broadcasted_iota(jnp.int32, (sub, sub), 1)
                           < lax.broadcasted_iota(jnp.int32, (sub, sub), 0), 1.0, 0.0).astype(BF16)
        routed_before = jnp.zeros((1, LANES), F32)
        for s in range(n_sub):
            span = slice(s * sub, (s + 1) * sub)
            gate = gate_ref[span, :]
            routed = jnp.where(gate > 0.0, 1.0, 0.0)
            rank = _dot(before, routed.astype(BF16)) + routed_before
            rank_ref[span, :] = rank
            gate_t_ref[:, span] = gate.T[0:N_EXPERTS]
            rank_t_ref[:, span] = rank.T[0:N_EXPERTS]
            routed_before = routed_before + jnp.sum(routed, axis=0, keepdims=True)
        x_copy.wait()

    @pl.when(f == 0)
    def _():
        start_ref[0] = 0
        for s in range(n_sub):
            g_row = gate_t_ref[pl.ds(e, 1), s * sub:(s + 1) * sub]
            start_ref[s + 1] = start_ref[s] + jnp.sum(jnp.where(g_row > 0.0, 1.0, 0.0)).astype(jnp.int32)

        def clear(r0, rows):
            xc_ref[pl.ds(r0, rows), :] = jnp.zeros((rows, D_MODEL), BF16)
            yc_ref[pl.ds(r0, rows), :] = jnp.zeros((rows, D_MODEL), F32)

        for_chunks(0, (start_ref[n_sub] + rb - 1) // rb, clear)

        for s in range(n_sub):
            span = slice(s * sub, (s + 1) * sub)
            g_row = gate_t_ref[pl.ds(e, 1), span]
            key = jnp.where(g_row > 0.0, rank_t_ref[pl.ds(e, 1), span], -1.0)

            def gather(r0, rows, span=span, key=key):
                slot = (r0 + lax.broadcasted_iota(jnp.int32, (rows, 1), 0)).astype(F32)
                onehot = jnp.where(key == slot, 1.0, 0.0).astype(BF16)
                dst = pl.ds(r0, rows)
                xc_ref[dst, :] = (xc_ref[dst, :].astype(F32) + _dot(onehot, h_ref[span, :])).astype(BF16)

            for_chunks(*sub_chunks(s), gather)

    def swiglu(r0, rows):
        src = pl.ds(r0, rows)
        xc = xc_ref[src, :]
        a = _dot(xc, w1_ref[...])
        t = (a * jax.nn.sigmoid(a) * _dot(xc, w3_ref[...])).astype(BF16)
        yc_ref[src, :] += _dot(t, w2_ref[...])

    for_chunks(0, (start_ref[n_sub] + rb - 1) // rb, swiglu, group=MOE_SWIGLU_GROUP)

    @pl.when(f == pl.num_programs(2) - 1)
    def _():
        lane = lax.broadcasted_iota(jnp.int32, (1, LANES), 1)
        for s in range(n_sub):
            span = slice(s * sub, (s + 1) * sub)
            g_col = jnp.sum(jnp.where(lane == e, gate_ref[span, :], 0.0), axis=-1, keepdims=True)
            r_col = jnp.sum(jnp.where(lane == e, rank_ref[span, :], 0.0), axis=-1, keepdims=True)
            key = jnp.where(g_col > 0.0, r_col, -1.0)

            def scatter(r0, rows, span=span, key=key, g_col=g_col):
                slot = (r0 + lax.broadcasted_iota(jnp.int32, (1, rows), 1)).astype(F32)
                onehot = jnp.where(key == slot, 1.0, 0.0).astype(BF16)
                o_ref[span, :] += g_col * _dot(onehot, yc_ref[pl.ds(r0, rows), :].astype(BF16))

            for_chunks(*sub_chunks(s), scatter)


def _moe_call(h, x, gate, w1, w3, w2, *, tm, tf):
    n = h.shape[0]
    n_sub = tm // MOE_SUB
    once = pl.Buffered(1)
    rows = lambda width, mode=None: pl.BlockSpec((tm, width), lambda i, e, f: (i, 0), pipeline_mode=mode)
    return pl.pallas_call(
        functools.partial(_moe_kernel, n_sub=n_sub),
        grid=(n // tm, N_EXPERTS, D_FF // tf),
        in_specs=[rows(D_MODEL, once), pl.BlockSpec(memory_space=pl.ANY), rows(LANES),
                  pl.BlockSpec((None, D_MODEL, tf), lambda i, e, f: (e, 0, f)),
                  pl.BlockSpec((None, D_MODEL, tf), lambda i, e, f: (e, 0, f)),
                  pl.BlockSpec((None, tf, D_MODEL), lambda i, e, f: (e, f, 0))],
        out_specs=rows(D_MODEL, once),
        out_shape=jax.ShapeDtypeStruct((n, D_MODEL), F32),
        scratch_shapes=[pltpu.VMEM((tm, D_MODEL), BF16), pltpu.VMEM((tm, D_MODEL), F32), pltpu.VMEM((tm, LANES), F32),
                        pltpu.VMEM((N_EXPERTS, tm), F32), pltpu.VMEM((N_EXPERTS, tm), F32),
                        pltpu.SMEM((n_sub + 1,), jnp.int32), pltpu.SemaphoreType.DMA(())],
        compiler_params=pltpu.CompilerParams(dimension_semantics=("arbitrary", "arbitrary", "arbitrary"),
                                             vmem_limit_bytes=VMEM_LIMIT),
        name="moe_swiglu",
    )(h, x, gate, w1, w3, w2)


def _rope_tables(pos):
    half = ROPE_DIM // 2
    inv_freq = ROPE_THETA ** (-np.arange(half, dtype=np.float64) / half)
    ang = pos.astype(np.float64)[..., None] * inv_freq
    cos, sin = np.cos(ang).astype(np.float32), np.sin(ang).astype(np.float32)
    n = pos.shape[0]
    rest = HEAD_DIM - ROPE_DIM
    cos64 = np.concatenate([cos, cos, np.ones((n, rest), np.float32)], axis=-1)
    sa64 = np.concatenate([-sin, np.zeros((n, rest + half), np.float32)], axis=-1)
    sb64 = np.concatenate([np.zeros((n, half), np.float32), sin, np.zeros((n, rest), np.float32)], axis=-1)
    return tuple(jnp.asarray(np.tile(t, (1, 2)), F32) for t in (cos64, sa64, sb64))


def _block_diag(blocks):
    n = len(blocks)
    rows = []
    for i, blk in enumerate(blocks):
        rows.append(jnp.concatenate([blk if j == i else jnp.zeros((blk.shape[0], blocks[j].shape[1]), blk.dtype)
                                     for j in range(n)], axis=1))
    return jnp.concatenate(rows, axis=0)


def _layout_w_in(w):
    n_gate = N_BRANCH * N_HEADS
    g0 = GROUP_WIDTH + 6 * HEAD_DIM
    pad = jnp.zeros(w.shape[:-1] + (LANES - n_gate,), w.dtype)
    return jnp.concatenate([w[..., :g0 + n_gate], pad, w[..., g0 + n_gate:]], axis=-1)


def _layout_compress(pe, w1, w2):
    half = CMP_LEN // 2
    pe2 = jnp.transpose(pe, (1, 0, 2)).reshape(2, half * LANES)
    w1r = w1.reshape(2, 2, half, HEAD_DIM, CMP_HIDDEN)
    zeros = jnp.zeros((half, HEAD_DIM, CMP_HIDDEN), w1.dtype)

    def big(part):
        k_rows = jnp.concatenate([w1r[0, part], zeros], axis=-1)
        v_rows = jnp.concatenate([zeros, w1r[1, part]], axis=-1)
        return jnp.concatenate([k_rows, v_rows], axis=1).reshape(half * LANES, 2 * CMP_HIDDEN)

    return pe2, big(0), big(1), _block_diag([w2[0], w2[1]])


def _overlap_matrix(seq, ncp, nbp):
    cs = jnp.arange(ncp) * CMP_STRIDE
    bs = jnp.arange(nbp) * SEL_BLOCK
    ov = jnp.clip(jnp.minimum(cs[:, None] + CMP_LEN, bs[None, :] + SEL_BLOCK)
                  - jnp.maximum(cs[:, None], bs[None, :]), 0, None).astype(F32) / CMP_LEN
    keep = (jnp.arange(ncp)[:, None] < seq // CMP_STRIDE - 1) & (jnp.arange(nbp)[None, :] < seq // SEL_BLOCK)
    return jnp.where(keep, ov, 0.0)


def kernel(x, attn_norm_g, w_in, q_norm_g, k_norm_g, cmp_pe, cmp_w1, cmp_w2, sgu_norm_g, sgu_w, sgu_b, pool_w,
           pool_scale, conv_w, mix_out_norm_g, w_out, ffn_norm_g, ffn_w1, ffn_w3, ffn_w2, router_w, expert_w1,
           expert_w3, expert_w2):
    b, seq, _ = x.shape
    n = b * seq
    depth = w_in.shape[0]
    tm = TOKEN_TILE
    ncp = seq // CMP_STRIDE
    nbp = max(LANES, seq // SEL_BLOCK)

    cos, sa, sb = _rope_tables(np.arange(seq))
    cos_c, sa_c, sb_c = _rope_tables(np.arange(ncp) * CMP_STRIDE + CMP_LEN - 1)
    bd = _block_diag([jnp.full((HEAD_DIM, HEAD_DIM), 1.0 / HEAD_DIM, F32)] * 4).astype(BF16)
    overlap_t = _overlap_matrix(seq, ncp, nbp).T.astype(BF16)
    row = lambda v: v.reshape(1, -1)
    two = lambda v: jnp.tile(v, 2).reshape(1, LANES)

    w_in_b = _layout_w_in(w_in.astype(BF16))
    w_out_b = w_out.astype(BF16)
    xf = x.reshape(n, D_MODEL)
    for layer in range(depth):
        kg = k_norm_g[layer]
        sgub = jnp.repeat(sgu_b[layer].T, HEAD_DIM, axis=1)
        qt, kvc, kvs, vts, kvw, vtw, gatet, ybcd = _proj_call(
            xf, row(attn_norm_g[layer]), w_in_b, bd, cos, sa, sb,
            jnp.tile(q_norm_g[layer], 4).reshape(1, 2 * LANES), jnp.tile(kg[1:3], (1, 2)).reshape(1, 2 * LANES),
            row(sgu_norm_g[layer]), sgu_w[layer].transpose(1, 0, 2).reshape(SGU_CHUNK, N_HEADS * SGU_CHUNK), sgub,
            _block_diag([pool_w[layer, i] for i in range(len(POOL_WINDOWS))]).astype(BF16),
            row(pool_scale[layer]), conv_w[layer], mix_out_norm_g[layer].reshape(4, GROUP_WIDTH)[1:],
            layer=layer, seq=seq, tm=tm)
        pe2, w1a, w1b, w2c = _layout_compress(cmp_pe[layer], cmp_w1[layer], cmp_w2[layer])
        kvcc, vtc = _compress_call(kvc.reshape(b, ncp, CMP_STRIDE * LANES), pe2, w1a.astype(BF16), w1b.astype(BF16),
                                   w2c.astype(BF16), two(kg[0]), bd, cos_c, sa_c, sb_c)
        ya = _attn_call(qt, gatet, kvcc, vtc, kvs, vts, kvw, vtw, overlap_t,
                        row(mix_out_norm_g[layer, :GROUP_WIDTH]), seq=seq, tq=ATTN_Q_TILE, tk=SEL_KEY_TILE)
        i = layer // 2
        if layer % 2 == 0:
            xf = _ffn_call(xf, ya, ybcd, w_out_b, row(ffn_norm_g[layer]), ffn_w1[i].astype(BF16),
                           ffn_w3[i].astype(BF16), ffn_w2[i].astype(BF16), layer=layer, tm=FFN_TOKEN_TILE, tf=FFN_FF_TILE)
        else:
            r = jnp.pad(router_w[i], ((0, 0), (0, LANES - N_EXPERTS)))
            r_hi = r.astype(BF16)
            rw = jnp.concatenate([r_hi, (r - r_hi.astype(F32)).astype(BF16)], axis=1)
            x1, hn, gate = _outproj_call(xf, ya, ybcd, w_out_b, row(ffn_norm_g[layer]), rw, layer=layer, tm=tm)
            xf = _moe_call(hn, x1, gate, expert_w1[i].astype(BF16), expert_w3[i].astype(BF16),
                           expert_w2[i].astype(BF16), tm=MOE_TOKEN_TILE, tf=MOE_FF_TILE)
    return xf.reshape(b, seq, D_MODEL)
```
